```python
import jax, jax.numpy as jnp
from jax import lax
import numpy as np

D_MODEL = 1024
BATCH = 2
SEQ = 8192
DEPTH = 4

GRID_W = 64
CTX_LEN = 256
N_MIXERS = 3
ROPE_THETA = 10000.0
NORM_EPS = 1e-6
Q_BLOCK = 128

GQA_HEAD_DIM = 128
GQA_Q_HEADS = D_MODEL // 64
GQA_KV_HEADS = GQA_Q_HEADS // 2
GQA_GROUP = GQA_Q_HEADS // GQA_KV_HEADS

HGRN_DK = 128
HGRN_DV = 128
HGRN_HEADS = D_MODEL // HGRN_DK
HGRN_WIDTH = HGRN_HEADS * HGRN_DK
HGRN_CHUNK = 64

MLA_HEADS = D_MODEL // 64
MLA_NOPE = 64
MLA_ROPE = 32
MLA_V = 64
MLA_Q_LORA = 3 * D_MODEL // 4
MLA_KV_LORA = D_MODEL // 4

D_FF = 3 * D_MODEL
CONV_W = 3

N_GQA = (DEPTH + 2) // 3
N_HGRN = (DEPTH + 1) // 3
N_MLA = DEPTH // 3

kernel_name = 'hybrid_gqa_hgrn2_mla_convffn_prefix_dit'


def rms_norm(x, gain):
    x32 = x.astype(jnp.float32)
    y = x32 * lax.rsqrt(jnp.mean(x32 * x32, axis=-1, keepdims=True) + NORM_EPS)
    return (y * gain.astype(jnp.float32)).astype(x.dtype)


def modulate(x, shift, scale):
    return x * (1.0 + scale) + shift


def axial_rope_tables(rows, rot_dim):
    row = jnp.repeat(jnp.arange(rows, dtype=jnp.float32), GRID_W)
    col = jnp.tile(jnp.arange(GRID_W, dtype=jnp.float32), rows)
    axis_dim = rot_dim // 2
    inv_freq = jnp.power(ROPE_THETA, -jnp.arange(0, axis_dim, 2, dtype=jnp.float32) / axis_dim)
    ang = jnp.concatenate([row[:, None] * inv_freq, col[:, None] * inv_freq], axis=-1)
    return jnp.cos(ang), jnp.sin(ang)


def apply_rope(x, cos, sin):
    half = x.shape[-1] // 2
    x32 = x.astype(jnp.float32)
    x1, x2 = x32[..., :half], x32[..., half:]
    return jnp.concatenate([x1 * cos - x2 * sin, x1 * sin + x2 * cos], axis=-1).astype(x.dtype)


def block_attention(q, k, v, scale):
    B, Nq, Hk, G, Dk = q.shape
    nb = Nq // Q_BLOCK
    qb = q.reshape(B, nb, Q_BLOCK, Hk, G, Dk).swapaxes(0, 1)

    def one_block(q_blk):
        s = jnp.einsum('bqhgd,bkhd->bhgqk', q_blk, k).astype(jnp.float32) * scale
        p = jax.nn.softmax(s, axis=-1).astype(v.dtype)
        return jnp.einsum('bhgqk,bkhd->bqhgd', p, v)

    o = lax.map(one_block, qb)
    return o.swapaxes(0, 1).reshape(B, Nq, Hk, G, v.shape[-1])


def dwconv_centered(u, w, b):
    L = u.shape[1]
    pad = CONV_W // 2
    up = jnp.pad(u, ((0, 0), (pad, CONV_W - 1 - pad), (0, 0)))
    out = b
    for j in range(CONV_W):
        out = out + up[:, j:j + L] * w[j]
    return out


def conv_ffn(h, w_in, w_conv, b_conv, w_out):
    u = dwconv_centered(h @ w_in, w_conv, b_conv)
    a, val = jnp.split(u, 2, axis=-1)
    return (jax.nn.silu(a) * val) @ w_out


def gqa_mixer(h, hc, w_in, q_gain, k_gain, w_out, cos, sin, ctx_out):
    qd = GQA_Q_HEADS * GQA_HEAD_DIM
    kd = GQA_KV_HEADS * GQA_HEAD_DIM

    def project(t, rotate):
        B, L, _ = t.shape
        p = t @ w_in
        q = rms_norm(p[..., :qd].reshape(B, L, GQA_Q_HEADS, GQA_HEAD_DIM), q_gain)
        k = rms_norm(p[..., qd:qd + kd].reshape(B, L, GQA_KV_HEADS, GQA_HEAD_DIM), k_gain)
        v = p[..., qd + kd:].reshape(B, L, GQA_KV_HEADS, GQA_HEAD_DIM)
        if rotate:
            q = apply_rope(q, cos[:, None, :], sin[:, None, :])
            k = apply_rope(k, cos[:, None, :], sin[:, None, :])
        return q.reshape(B, L, GQA_KV_HEADS, GQA_GROUP, GQA_HEAD_DIM), k, v

    def merge(o):
        B, L = o.shape[:2]
        return o.reshape(B, L, qd) @ w_out

    scale = GQA_HEAD_DIM ** -0.5
    q, k, v = project(h, True)
    qc, kc, vc = project(hc, False)
    y = merge(block_attention(q, jnp.concatenate([k, kc], axis=1), jnp.concatenate([v, vc], axis=1), scale))
    yc = merge(block_attention(qc, kc, vc, scale)) if ctx_out else None
    return y, yc


def mla_mixer(h, hc, w_in, q_gain, kv_gain, w_qb, w_kvb, w_out, cos, sin, ctx_out):
    def project(t, rotate):
        B, L, _ = t.shape
        p = t @ w_in
        cq = rms_norm(p[..., :MLA_Q_LORA], q_gain)
        ckv = rms_norm(p[..., MLA_Q_LORA:MLA_Q_LORA + MLA_KV_LORA], kv_gain)
        k_rope = p[..., MLA_Q_LORA + MLA_KV_LORA:]
        q = (cq @ w_qb).reshape(B, L, MLA_HEADS, MLA_NOPE + MLA_ROPE)
        kv = (ckv @ w_kvb).reshape(B, L, MLA_HEADS, MLA_NOPE + MLA_V)
        q_nope, q_rope = q[..., :MLA_NOPE], q[..., MLA_NOPE:]
        k_nope, v = kv[..., :MLA_NOPE], kv[..., MLA_NOPE:]
        if rotate:
            q_rope = apply_rope(q_rope, cos[:, None, :], sin[:, None, :])
            k_rope = apply_rope(k_rope, cos, sin)
        q = jnp.concatenate([q_nope, q_rope], axis=-1)[:, :, :, None, :]
        k = jnp.concatenate([k_nope, jnp.broadcast_to(k_rope[:, :, None, :], (B, L, MLA_HEADS, MLA_ROPE))], axis=-1)
        return q, k, v

    def merge(o):
        B, L = o.shape[:2]
        return o.reshape(B, L, MLA_HEADS * MLA_V) @ w_out

    scale = (MLA_NOPE + MLA_ROPE) ** -0.5
    q, k, v = project(h, True)
    qc, kc, vc = project(hc, False)
    y = merge(block_attention(q, jnp.concatenate([k, kc], axis=1), jnp.concatenate([v, vc], axis=1), scale))
    yc = merge(block_attention(qc, kc, vc, scale)) if ctx_out else None
    return y, yc


def gla_chunk_scan(q, k, v, log_f, state0):
    B, L, H, _ = q.shape
    dv = v.shape[-1]
    n = L // HGRN_CHUNK

    def to_chunks(a):
        return a.reshape(B, n, HGRN_CHUNK, H, a.shape[-1]).transpose(1, 0, 3, 2, 4)

    incl = jnp.tril(jnp.ones((HGRN_CHUNK, HGRN_CHUNK), dtype=bool))[:, :, None]

    def step(state, inp):
        qc, kc, vc, gc = inp
        cum = jnp.cumsum(gc, axis=2)
        diff = cum[:, :, :, None, :] - cum[:, :, None, :, :]
        decay = jnp.where(incl, jnp.exp(jnp.where(incl, diff, 0.0)), 0.0)
        scores = jnp.einsum('bhtd,bhsd,bhtsd->bhts', qc, kc, decay)
        out = jnp.einsum('bhts,bhsv->bhtv', scores, vc) + jnp.einsum('bhtd,bhdv->bhtv', qc * jnp.exp(cum), state)
        cum_end = cum[:, :, -1, :]
        state = jnp.exp(cum_end)[..., None] * state + jnp.einsum(
            'bhsd,bhsv->bhdv', kc * jnp.exp(cum_end[:, :, None, :] - cum), vc)
        return state, out

    state, out = lax.scan(step, state0, (to_chunks(q), to_chunks(k), to_chunks(v), to_chunks(log_f)))
    return out.transpose(1, 0, 3, 2, 4).reshape(B, L, H, dv), state


def hgrn_mixer(h, hc, w_in, o_gain, w_out, lower_bound, ctx_out):
    lb = lower_bound.reshape(HGRN_HEADS, HGRN_DK)
    scale = HGRN_DK ** -0.5

    def project(t):
        B, L, _ = t.shape
        p = (t @ w_in).astype(jnp.float32).reshape(B, L, 5, HGRN_HEADS, HGRN_DK)
        q, inp, f_fwd_pre, f_bwd_pre, gate = p[:, :, 0], p[:, :, 1], p[:, :, 2], p[:, :, 3], p[:, :, 4]
        f_fwd = lb + (1.0 - lb) * jax.nn.sigmoid(f_fwd_pre)
        f_bwd = lb + (1.0 - lb) * jax.nn.sigmoid(f_bwd_pre)
        return q * scale, inp, 1.0 - f_fwd, jnp.log(f_fwd), 1.0 - f_bwd, jnp.log(f_bwd), gate

    def flip(a):
        return jnp.flip(a, axis=1)

    def readout(o, gate, dtype):
        B, L = o.shape[:2]
        return (rms_norm(o, o_gain) * jax.nn.silu(gate)).reshape(B, L, HGRN_WIDTH).astype(dtype) @ w_out

    q, i_, kf, gf, kb, gb, g = project(h)
    qc, ic, kfc, gfc, kbc, gbc, gc = project(hc)
    state0 = jnp.zeros((h.shape[0], HGRN_HEADS, HGRN_DK, HGRN_DV), jnp.float32)
    oc_f, s_f = gla_chunk_scan(qc, kfc, ic, gfc, state0)
    oc_b, s_b = gla_chunk_scan(flip(qc), flip(kbc), flip(ic), flip(gbc), state0)
    o_f, _ = gla_chunk_scan(q, kf, i_, gf, s_f)
    o_b, _ = gla_chunk_scan(flip(q), flip(kb), flip(i_), flip(gb), s_b)
    y = readout(o_f + flip(o_b), g, h.dtype)
    yc = readout(oc_f + flip(oc_b), gc, hc.dtype) if ctx_out else None
    return y, yc


def setup_inputs(seed: int = 0) -> dict:
    key = jax.random.key(seed)
    ks = jax.random.split(key, 28)

    def nrm(k, shape, s):
        return jax.random.normal(k, shape, jnp.float32) * s

    D = D_MODEL
    return {
        'x': nrm(ks[0], (BATCH, SEQ, D), 1.0),
        'c': nrm(ks[1], (BATCH, D), 1.0),
        'ctx': nrm(ks[2], (BATCH, CTX_LEN, D), 1.0),
        'c_ctx': nrm(ks[3], (D,), 1.0),
        'w_ada': nrm(ks[4], (DEPTH, D, 6 * D), 0.5 * D ** -0.5),
        'b_ada': nrm(ks[5], (DEPTH, 6 * D), 0.01),
        'norm_mix': 1.0 + nrm(ks[6], (DEPTH, D), 0.02),
        'norm_ffn': 1.0 + nrm(ks[7], (DEPTH, D), 0.02),
        'ffn_w_in': nrm(ks[8], (DEPTH, D, 2 * D_FF), D ** -0.5),
        'ffn_conv_w': nrm(ks[9], (DEPTH, CONV_W, 2 * D_FF), CONV_W ** -0.5),
        'ffn_conv_b': nrm(ks[10], (DEPTH, 2 * D_FF), 0.01),
        'ffn_w_out': nrm(ks[11], (DEPTH, D_FF, D), D_FF ** -0.5),
        'gqa_w_in': nrm(ks[12], (N_GQA, D, (GQA_Q_HEADS + 2 * GQA_KV_HEADS) * GQA_HEAD_DIM), D ** -0.5),
        'gqa_q_norm': 1.0 + nrm(ks[13], (N_GQA, GQA_HEAD_DIM), 0.02),
        'gqa_k_norm': 1.0 + nrm(ks[14], (N_GQA, GQA_HEAD_DIM), 0.02),
        'gqa_w_out': nrm(ks[15], (N_GQA, GQA_Q_HEADS * GQA_HEAD_DIM, D), (GQA_Q_HEADS * GQA_HEAD_DIM) ** -0.5),
        'hgrn_w_in': nrm(ks[16], (N_HGRN, D, 5 * HGRN_WIDTH), D ** -0.5),
        'hgrn_out_norm': 1.0 + nrm(ks[17], (N_HGRN, HGRN_DV), 0.02),
        'hgrn_w_out': nrm(ks[18], (N_HGRN, HGRN_WIDTH, D), HGRN_WIDTH ** -0.5),
        'hgrn_lower_bounds': nrm(ks[19], (DEPTH, HGRN_WIDTH), 0.5),
        'mla_w_in': nrm(ks[20], (N_MLA, D, MLA_Q_LORA + MLA_KV_LORA + MLA_ROPE), D ** -0.5),
        'mla_q_norm': 1.0 + nrm(ks[21], (N_MLA, MLA_Q_LORA), 0.02),
        'mla_kv_norm': 1.0 + nrm(ks[22], (N_MLA, MLA_KV_LORA), 0.02),
        'mla_w_qb': nrm(ks[23], (N_MLA, MLA_Q_LORA, MLA_HEADS * (MLA_NOPE + MLA_ROPE)), MLA_Q_LORA ** -0.5),
        'mla_w_kvb': nrm(ks[24], (N_MLA, MLA_KV_LORA, MLA_HEADS * (MLA_NOPE + MLA_V)), MLA_KV_LORA ** -0.5),
        'mla_w_out': nrm(ks[25], (N_MLA, MLA_HEADS * MLA_V, D), (MLA_HEADS * MLA_V) ** -0.5),
        'final_norm': 1.0 + nrm(ks[26], (D,), 0.02),
    }


def reference(x, c, ctx, c_ctx, w_ada, b_ada, norm_mix, norm_ffn, ffn_w_in, ffn_conv_w, ffn_conv_b, ffn_w_out,
              gqa_w_in, gqa_q_norm, gqa_k_norm, gqa_w_out, hgrn_w_in, hgrn_out_norm, hgrn_w_out, hgrn_lower_bounds,
              mla_w_in, mla_q_norm, mla_kv_norm, mla_w_qb, mla_w_kvb, mla_w_out, final_norm):
    rows = x.shape[1] // GRID_W
    cos_a, sin_a = axial_rope_tables(rows, GQA_HEAD_DIM)
    cos_m, sin_m = axial_rope_tables(rows, MLA_ROPE)
    lb_all = jnp.cumsum(jax.nn.softmax(hgrn_lower_bounds.astype(jnp.float32), axis=0), axis=0)
    lb_all = lb_all - lb_all[0]
    silu_c = jax.nn.silu(c)
    silu_cc = jax.nn.silu(c_ctx)

    for i in range(DEPTH):
        last = i == DEPTH - 1
        kind = i % N_MIXERS
        j = i // N_MIXERS
        mod = (silu_c @ w_ada[i] + b_ada[i])[:, None, :]
        mod_c = silu_cc @ w_ada[i] + b_ada[i]
        sh_a, sc_a, g_a, sh_f, sc_f, g_f = jnp.split(mod, 6, axis=-1)
        csh_a, csc_a, cg_a, csh_f, csc_f, cg_f = jnp.split(mod_c, 6, axis=-1)

        h = modulate(rms_norm(x, norm_mix[i]), sh_a, sc_a)
        hc = modulate(rms_norm(ctx, norm_mix[i]), csh_a, csc_a)
        if kind == 0:
            y, yc = gqa_mixer(h, hc, gqa_w_in[j], gqa_q_norm[j], gqa_k_norm[j], gqa_w_out[j],
                              cos_a, sin_a, not last)
        elif kind == 1:
            y, yc = hgrn_mixer(h, hc, hgrn_w_in[j], hgrn_out_norm[j], hgrn_w_out[j], lb_all[i], not last)
        else:
            y, yc = mla_mixer(h, hc, mla_w_in[j], mla_q_norm[j], mla_kv_norm[j], mla_w_qb[j], mla_w_kvb[j],
                              mla_w_out[j], cos_m, sin_m, not last)
        x = x + g_a * y
        x = x + g_f * conv_ffn(modulate(rms_norm(x, norm_ffn[i]), sh_f, sc_f),
                               ffn_w_in[i], ffn_conv_w[i], ffn_conv_b[i], ffn_w_out[i])
        if not last:
            ctx = ctx + cg_a * yc
            ctx = ctx + cg_f * conv_ffn(modulate(rms_norm(ctx, norm_ffn[i]), csh_f, csc_f),
                                        ffn_w_in[i], ffn_conv_w[i], ffn_conv_b[i], ffn_w_out[i])

    return rms_norm(x, final_norm)
```

```python
import functools

import numpy as np
import jax
import jax.numpy as jnp
from jax import lax
from jax.experimental import pallas as pl
from jax.experimental.pallas import tpu as pltpu

F32 = jnp.float32
BF16 = jnp.bfloat16
HIGHEST = lax.Precision.HIGHEST

GRID_W = 64
ROPE_THETA = 10000.0
NORM_EPS = 1e-6
CONV_W = 3

LANES = 128
SUBLANES = 8

GQA_HEAD_DIM = 128
GQA_GROUP = 2

HGRN_DK = 128
HGRN_CHUNK = 64
HGRN_PARTS = 5

MLA_NOPE = 64
MLA_ROPE = 32
MLA_V = 64

VMEM_LIMIT = 56 * 1024 * 1024


def _params(*sem):
    return pltpu.CompilerParams(dimension_semantics=sem, vmem_limit_bytes=VMEM_LIMIT)


def _resident(shape):
    nd = len(shape)
    return pl.BlockSpec(shape, lambda *_: (0,) * nd, pipeline_mode=pl.Buffered(1))


def _silu(x):
    return x / (1.0 + jnp.exp(-x))


def _sigmoid(x):
    return 1.0 / (1.0 + jnp.exp(-x))


def _rms(x, gain):
    return x * lax.rsqrt(jnp.mean(x * x, axis=-1, keepdims=True) + NORM_EPS) * gain


def _modnorm(x, gain, shift, scale):
    return _rms(x, gain) * (1.0 + scale) + shift


def _row_tile(n, want):
    t = min(n, want)
    assert n % t == 0, (n, t)
    return t


def _ada_kernel(cv_ref, w_ref, b_ref, o_ref):
    s = _silu(cv_ref[...])
    o_ref[0] = jnp.dot(s, w_ref[0], precision=HIGHEST, preferred_element_type=F32) + b_ref[0]


def _ada_mods(cv, w_ada, b_ada):
    depth, d, n = w_ada.shape
    tn = 1536
    return pl.pallas_call(
        _ada_kernel,
        grid=(depth, n // tn),
        in_specs=[pl.BlockSpec((SUBLANES, d), lambda i, j: (0, 0)),
                  pl.BlockSpec((1, d, tn), lambda i, j: (i, 0, j)),
                  pl.BlockSpec((1, 1, tn), lambda i, j: (i, 0, j))],
        out_specs=pl.BlockSpec((1, SUBLANES, tn), lambda i, j: (i, 0, j)),
        out_shape=jax.ShapeDtypeStruct((depth, SUBLANES, n), F32),
        compiler_params=_params("arbitrary", "arbitrary"),
        name="ada_mods",
    )(cv, w_ada, b_ada.reshape(depth, 1, n))


def _lb_kernel(x_ref, o_ref):
    depth = x_ref.shape[0]
    rows = [x_ref[i:i + 1, :] for i in range(depth)]
    m = rows[0]
    for r in rows[1:]:
        m = jnp.maximum(m, r)
    e = [jnp.exp(r - m) for r in rows]
    tot = e[0]
    for r in e[1:]:
        tot = tot + r
    p = [r / tot for r in e]
    cum = p[0]
    o_ref[0:1, :] = cum - p[0]
    for i in range(1, depth):
        cum = cum + p[i]
        o_ref[i:i + 1, :] = cum - p[0]


def _lower_bounds(lb_raw):
    return pl.pallas_call(
        _lb_kernel,
        out_shape=jax.ShapeDtypeStruct(lb_raw.shape, F32),
        name="hgrn_lower_bounds",
    )(lb_raw.astype(F32))


def _gqa_proj_kernel(x_ref, mod_ref, ng_ref, w_ref, qg_ref, kg_ref, cos_ref, sin_ref,
                     q_ref, k_ref, v_ref, *, rotate, qd, kd, chunk):
    h = _modnorm(x_ref[0], ng_ref[...], mod_ref[0, 0], mod_ref[0, 1]).astype(BF16)
    qg = qg_ref[...]
    kg = kg_ref[...]
    scale = GQA_HEAD_DIM ** -0.5
    if rotate:
        cos = cos_ref[...]
        sin = sin_ref[...]

    def head(p, gain):
        y = _rms(p, gain)
        if rotate:
            y = y * cos + pltpu.roll(y, GQA_HEAD_DIM // 2, 1) * sin
        return y

    n = w_ref.shape[1]
    for j in range(n // chunk):
        p = jnp.dot(h, w_ref[:, j * chunk:(j + 1) * chunk], preferred_element_type=F32)
        for u in range(chunk // LANES):
            col = j * chunk + u * LANES
            ph = p[:, u * LANES:(u + 1) * LANES]
            if col < qd:
                q_ref[0, :, col:col + LANES] = (head(ph, qg) * scale).astype(BF16)
            elif col < qd + kd:
                k_ref[0, :, col - qd:col - qd + LANES] = head(ph, kg).astype(BF16)
            else:
                c0 = col - qd - kd
                v_ref[0, :, c0:c0 + LANES] = ph.astype(BF16)


def _gqa_proj(x, mod, ng, w_in, qg, kg, cos, sin, rotate):
    b, l, d = x.shape
    n = w_in.shape[1]
    kd = n // 4
    qd = n - 2 * kd
    tm = _row_tile(l, 512)
    kern = functools.partial(_gqa_proj_kernel, rotate=rotate, qd=qd, kd=kd, chunk=512)
    return pl.pallas_call(
        kern,
        grid=(b, l // tm),
        in_specs=[pl.BlockSpec((1, tm, d), lambda i, t: (i, t, 0)),
                  pl.BlockSpec((1, 6, 1, d), lambda i, t: (i % mod.shape[0], 0, 0, 0)),
                  _resident((1, d)),
                  _resident((d, n)),
                  _resident((1, LANES)),
                  _resident((1, LANES)),
                  pl.BlockSpec((tm, LANES), lambda i, t: (t, 0)),
                  pl.BlockSpec((tm, LANES), lambda i, t: (t, 0))],
        out_specs=[pl.BlockSpec((1, tm, qd), lambda i, t: (i, t, 0)),
                   pl.BlockSpec((1, tm, kd), lambda i, t: (i, t, 0)),
                   pl.BlockSpec((1, tm, kd), lambda i, t: (i, t, 0))],
        out_shape=[jax.ShapeDtypeStruct((b, l, qd), BF16),
                   jax.ShapeDtypeStruct((b, l, kd), BF16),
                   jax.ShapeDtypeStruct((b, l, kd), BF16)],
        compiler_params=_params("parallel", "parallel"),
        name="gqa_proj",
    )(x, mod, ng, w_in, qg, kg, cos, sin)


def _attn_kernel(*refs, group, n_src, tk):
    q_ref = refs[0]
    kv_refs = refs[1:1 + 2 * n_src]
    o_ref = refs[1 + 2 * n_src]
    m_sc, l_sc, acc_sc = refs[2 + 2 * n_src:]
    tq = q_ref.shape[1]

    q = q_ref[0]
    if group > 1:
        q = jnp.concatenate([q[:, g * LANES:(g + 1) * LANES] for g in range(group)], axis=0)

    m_sc[...] = jnp.full(m_sc.shape, -jnp.inf, F32)
    l_sc[...] = jnp.zeros(l_sc.shape, F32)
    acc_sc[...] = jnp.zeros(acc_sc.shape, F32)

    def step(kc, vc):
        s = lax.dot_general(q, kc, (((1,), (1,)), ((), ())), preferred_element_type=F32)
        m_old = m_sc[...]
        m_new = jnp.maximum(m_old, jnp.max(s, axis=-1, keepdims=True))
        alpha = jnp.exp(m_old - m_new)
        p = jnp.exp(s - m_new)
        l_sc[...] = alpha * l_sc[...] + jnp.sum(p, axis=-1, keepdims=True)
        acc_sc[...] = alpha * acc_sc[...] + jnp.dot(p.astype(BF16), vc, preferred_element_type=F32)
        m_sc[...] = m_new

    for i in range(n_src):
        k_ref, v_ref = kv_refs[2 * i], kv_refs[2 * i + 1]
        nk = k_ref.shape[1]
        c = min(tk, nk)
        if nk // c == 1:
            step(k_ref[0], v_ref[0])
        else:
            def body(j, carry, k_ref=k_ref, v_ref=v_ref, c=c):
                off = pl.multiple_of(j * c, c)
                step(k_ref[0, pl.ds(off, c), :], v_ref[0, pl.ds(off, c), :])
                return carry
            lax.fori_loop(0, nk // c, body, 0)

    o = acc_sc[...] / l_sc[...]
    for g in range(group):
        o_ref[0, :, g * LANES:(g + 1) * LANES] = o[g * tq:(g + 1) * tq].astype(o_ref.dtype)


def _attention(q, kv_sources, group, tq):
    b, nq, hd = q.shape
    hkv = hd // (group * LANES)
    tq = _row_tile(nq, tq)
    n_src = len(kv_sources)
    in_specs = [pl.BlockSpec((1, tq, group * LANES), lambda i, h, t: (i, t, h))]
    args = [q]
    for k, v in kv_sources:
        nk = k.shape[1]
        in_specs.append(pl.BlockSpec((1, nk, LANES), lambda i, h, t: (i, 0, h)))
        in_specs.append(pl.BlockSpec((1, nk, LANES), lambda i, h, t: (i, 0, h)))
        args += [k, v]
    rows = group * tq
    return pl.pallas_call(
        functools.partial(_attn_kernel, group=group, n_src=n_src, tk=512),
        grid=(b, hkv, nq // tq),
        in_specs=in_specs,
        out_specs=pl.BlockSpec((1, tq, group * LANES), lambda i, h, t: (i, t, h)),
        out_shape=jax.ShapeDtypeStruct((b, nq, hd), BF16),
        scratch_shapes=[pltpu.VMEM((rows, 1), F32), pltpu.VMEM((rows, 1), F32),
                        pltpu.VMEM((rows, LANES), F32)],
        compiler_params=_params("parallel", "parallel", "arbitrary"),
        name="flash_attention",
    )(*args)


def _merge_kernel(x_ref, o_ref, w_ref, mod_ref, y_ref):
    y = jnp.dot(o_ref[0], w_ref[...], preferred_element_type=F32)
    y_ref[0] = x_ref[0] + mod_ref[0, 2] * y


def _merge(x, o, w_out, mod):
    b, l, d = x.shape
    ko = o.shape[2]
    tm = _row_tile(l, 512)
    return pl.pallas_call(
        _merge_kernel,
        grid=(b, l // tm),
        in_specs=[pl.BlockSpec((1, tm, d), lambda i, t: (i, t, 0)),
                  pl.BlockSpec((1, tm, ko), lambda i, t: (i, t, 0)),
                  _resident((ko, d)),
                  pl.BlockSpec((1, 6, 1, d), lambda i, t: (i % mod.shape[0], 0, 0, 0))],
        out_specs=pl.BlockSpec((1, tm, d), lambda i, t: (i, t, 0)),
        out_shape=jax.ShapeDtypeStruct((b, l, d), F32),
        compiler_params=_params("parallel", "parallel"),
        name="merge_residual",
    )(x, o, w_out, mod)


def _ffn_kernel(xp_ref, x_ref, xn_ref, mod_ref, ng_ref, win_ref, cw_ref, cb_ref, wout_ref, y_ref,
                *, d_ff, chunk):
    t = pl.program_id(1)
    nt = pl.num_programs(1)
    gain = ng_ref[...]
    shift, scale, gate = mod_ref[0, 3], mod_ref[0, 4], mod_ref[0, 5]
    x = x_ref[0]
    tm = x.shape[0]
    halo = xp_ref.shape[1]
    hp = jnp.where(t > 0, _modnorm(xp_ref[0], gain, shift, scale), 0.0)
    hn = jnp.where(t < nt - 1, _modnorm(xn_ref[0], gain, shift, scale), 0.0)
    h = jnp.concatenate([hp, _modnorm(x, gain, shift, scale), hn], axis=0).astype(BF16)
    rows = tm + 2 * halo

    def conv(p, col):
        w = cw_ref[:, col:col + chunk]
        prev = pltpu.roll(p, 1, 0)[halo:halo + tm]
        nxt = pltpu.roll(p, rows - 1, 0)[halo:halo + tm]
        cur = p[halo:halo + tm]
        return ((cb_ref[:, col:col + chunk] + prev * w[0:1]) + cur * w[1:2]) + nxt * w[2:3]

    acc = jnp.zeros((tm, x.shape[1]), F32)
    for c in range(d_ff // chunk):
        pa = jnp.dot(h, win_ref[:, c * chunk:(c + 1) * chunk], preferred_element_type=F32)
        pv = jnp.dot(h, win_ref[:, d_ff + c * chunk:d_ff + (c + 1) * chunk], preferred_element_type=F32)
        g = _silu(conv(pa, c * chunk)) * conv(pv, d_ff + c * chunk)
        acc = acc + jnp.dot(g.astype(BF16), wout_ref[c * chunk:(c + 1) * chunk, :],
                            preferred_element_type=F32)
    y_ref[0] = x + gate * acc


def _conv_ffn(x, mod, ng, w_in, conv_w, conv_b, w_out):
    b, l, d = x.shape
    d_ff = w_out.shape[0]
    tm = _row_tile(l, 512)
    halo = SUBLANES
    per = tm // halo
    last = l // halo - 1
    kern = functools.partial(_ffn_kernel, d_ff=d_ff, chunk=512)
    return pl.pallas_call(
        kern,
        grid=(b, l // tm),
        in_specs=[pl.BlockSpec((1, halo, d), lambda i, t: (i, jnp.maximum(t * per - 1, 0), 0)),
                  pl.BlockSpec((1, tm, d), lambda i, t: (i, t, 0)),
                  pl.BlockSpec((1, halo, d), lambda i, t: (i, jnp.minimum((t + 1) * per, last), 0)),
                  pl.BlockSpec((1, 6, 1, d), lambda i, t: (i % mod.shape[0], 0, 0, 0)),
                  _resident((1, d)),
                  _resident((d, 2 * d_ff)),
                  _resident((CONV_W, 2 * d_ff)),
                  _resident((1, 2 * d_ff)),
                  _resident((d_ff, d))],
        out_specs=pl.BlockSpec((1, tm, d), lambda i, t: (i, t, 0)),
        out_shape=jax.ShapeDtypeStruct((b, l, d), F32),
        compiler_params=_params("parallel", "arbitrary"),
        name="conv_ffn",
    )(x, x, x, mod, ng, w_in, conv_w, conv_b, w_out)


def _hgrn_proj_kernel(x_ref, mod_ref, ng_ref, w_ref, lb_ref, q_ref, v_ref, kf_ref, gf_ref, kb_ref, gb_ref,
                      gate_ref, *, chunk):
    h = _modnorm(x_ref[0], ng_ref[...], mod_ref[0, 0], mod_ref[0, 1]).astype(BF16)
    width = q_ref.shape[2]
    scale = HGRN_DK ** -0.5
    for part in range(HGRN_PARTS):
        for j in range(width // chunk):
            c0 = j * chunk
            p = jnp.dot(h, w_ref[:, part * width + c0:part * width + c0 + chunk], preferred_element_type=F32)
            if part == 0:
                q_ref[0, :, c0:c0 + chunk] = p * scale
            elif part == 1:
                v_ref[0, :, c0:c0 + chunk] = p
            elif part == 4:
                gate_ref[0, :, c0:c0 + chunk] = p
            else:
                lb = lb_ref[:, c0:c0 + chunk]
                f = lb + (1.0 - lb) * _sigmoid(p)
                k_out, g_out = (kf_ref, gf_ref) if part == 2 else (kb_ref, gb_ref)
                k_out[0, :, c0:c0 + chunk] = 1.0 - f
                g_out[0, :, c0:c0 + chunk] = jnp.log(f)


def _hgrn_proj(x, mod, ng, w_in, lb):
    b, l, d = x.shape
    width = w_in.shape[1] // HGRN_PARTS
    tm = _row_tile(l, 512)
    out = jax.ShapeDtypeStruct((b, l, width), F32)
    ospec = pl.BlockSpec((1, tm, width), lambda i, t: (i, t, 0))
    return pl.pallas_call(
        functools.partial(_hgrn_proj_kernel, chunk=512),
        grid=(b, l // tm),
        in_specs=[pl.BlockSpec((1, tm, d), lambda i, t: (i, t, 0)),
                  pl.BlockSpec((1, 6, 1, d), lambda i, t: (i % mod.shape[0], 0, 0, 0)),
                  _resident((1, d)),
                  _resident((d, HGRN_PARTS * width)),
                  _resident((1, width))],
        out_specs=[ospec] * 7,
        out_shape=[out] * 7,
        compiler_params=_params("parallel", "parallel"),
        name="hgrn_proj",
    )(x, mod, ng, w_in, lb)


_HGRN_LEVELS = (32, 16, 8)


def _hgrn_constants(reverse):
    c = HGRN_CHUNK
    idx = np.arange(c)
    if reverse:
        tri = (idx[None, :] >= idx[:, None]).astype(np.float32)
    else:
        tri = (idx[None, :] <= idx[:, None]).astype(np.float32)
    mats = [tri]
    for half in _HGRN_LEVELS:
        blk = idx // (2 * half)
        boundary = blk * 2 * half + (half if reverse else half - 1)
        mats.append(tri[boundary])
    mats.append(np.ones((c, c), np.float32))
    return jnp.asarray(np.concatenate(mats, axis=0))


def _hgrn_chunk(q, k, v, g, cmat, st, reverse, ones_bf):
    c = HGRN_CHUNK
    a = jnp.dot(cmat, g, precision=HIGHEST, preferred_element_type=F32)
    cum = a[0:c]
    tot = a[(len(_HGRN_LEVELS) + 1) * c:(len(_HGRN_LEVELS) + 2) * c]
    row = lax.broadcasted_iota(jnp.int32, (c, c), 0)
    col = lax.broadcasted_iota(jnp.int32, (c, c), 1)
    early, late = (col, row) if not reverse else (row, col)
    scores = jnp.zeros((c, c), F32)
    for li, half in enumerate(_HGRN_LEVELS):
        ref = a[(li + 1) * c:(li + 2) * c]
        qs = q * jnp.exp(jnp.minimum(cum - ref, 0.0))
        ks = k * jnp.exp(jnp.minimum(ref - cum, 0.0))
        s_l = lax.dot_general(qs.astype(BF16), ks.astype(BF16), (((1,), (1,)), ((), ())),
                              preferred_element_type=F32)
        same = (row // (2 * half)) == (col // (2 * half))
        early_half = (early % (2 * half)) < half
        late_half = (late % (2 * half)) >= half
        scores = jnp.where(same & early_half & late_half, s_l, scores)
    sub = _HGRN_LEVELS[-1]
    prods = [q * k]
    for delta in range(1, sub):
        shift = delta if not reverse else c - delta
        k_sh = pltpu.roll(k, shift, 0)
        cum_sh = pltpu.roll(cum, shift, 0)
        prods.append(q * k_sh * jnp.exp(jnp.minimum(cum - cum_sh, 0.0)))
    diag = jnp.dot(jnp.concatenate(prods, axis=0).astype(BF16), ones_bf, preferred_element_type=F32)
    for delta in range(sub):
        d_t = diag[delta * c:(delta + 1) * c, 0:c]
        if not reverse:
            hit = (col == row - delta) & ((row % sub) >= delta)
        else:
            hit = (col == row + delta) & ((row % sub) + delta < sub)
        scores = jnp.where(hit, d_t, scores)
    qe = (q * jnp.exp(cum)).astype(BF16)
    out = jnp.dot(scores.astype(BF16), v.astype(BF16), preferred_element_type=F32)
    out = out + lax.dot_general(qe, st.astype(BF16), (((1,), (1,)), ((), ())), preferred_element_type=F32)
    kd = (k * jnp.exp(tot - cum)).astype(BF16)
    upd = jnp.dot(v.T.astype(BF16), kd, preferred_element_type=F32)
    st_new = st * jnp.exp(tot[0:1]) + upd
    return out, st_new


def _hgrn_scan_kernel(qf_ref, vf_ref, kf_ref, gf_ref, qb_ref, vb_ref, kb_ref, gb_ref, cf_ref, cb_ref,
                      sf0_ref, sb0_ref, of_ref, ob_ref, sf_ref, sb_ref, st_sc):
    step = pl.program_id(2)
    rb = qf_ref.shape[1]
    c = HGRN_CHUNK

    @pl.when(step == 0)
    def _():
        st_sc[0] = sf0_ref[0, 0]
        st_sc[1] = sb0_ref[0, 0]

    ones_bf = jnp.ones((LANES, LANES), BF16)
    cf = cf_ref[...]
    cb = cb_ref[...]
    st_f = st_sc[0]
    st_b = st_sc[1]
    for j in range(rb // c):
        lo = j * c
        o, st_f = _hgrn_chunk(qf_ref[0, lo:lo + c, :], kf_ref[0, lo:lo + c, :], vf_ref[0, lo:lo + c, :],
                              gf_ref[0, lo:lo + c, :], cf, st_f, False, ones_bf)
        of_ref[0, lo:lo + c, :] = o
        lo = rb - (j + 1) * c
        o, st_b = _hgrn_chunk(qb_ref[0, lo:lo + c, :], kb_ref[0, lo:lo + c, :], vb_ref[0, lo:lo + c, :],
                              gb_ref[0, lo:lo + c, :], cb, st_b, True, ones_bf)
        ob_ref[0, lo:lo + c, :] = o
    st_sc[0] = st_f
    st_sc[1] = st_b

    @pl.when(step == pl.num_programs(2) - 1)
    def _():
        sf_ref[0, 0] = st_f
        sb_ref[0, 0] = st_b


def _hgrn_scan(q, v, kf, gf, kb, gb, sf0, sb0):
    b, l, width = q.shape
    heads = width // LANES
    rb = _row_tile(l, 256)
    nc = l // rb
    fwd = pl.BlockSpec((1, rb, LANES), lambda i, h, s: (i, s, h))
    bwd = pl.BlockSpec((1, rb, LANES), lambda i, h, s: (i, nc - 1 - s, h))
    st_spec = pl.BlockSpec((1, 1, LANES, LANES), lambda i, h, s: (i, h, 0, 0))
    n_rows = (len(_HGRN_LEVELS) + 2) * HGRN_CHUNK
    o_shape = jax.ShapeDtypeStruct((b, l, width), F32)
    s_shape = jax.ShapeDtypeStruct((b, heads, LANES, LANES), F32)
    return pl.pallas_call(
        _hgrn_scan_kernel,
        grid=(b, heads, nc),
        in_specs=[fwd, fwd, fwd, fwd, bwd, bwd, bwd, bwd,
                  _resident((n_rows, HGRN_CHUNK)), _resident((n_rows, HGRN_CHUNK)), st_spec, st_spec],
        out_specs=[fwd, bwd, st_spec, st_spec],
        out_shape=[o_shape, o_shape, s_shape, s_shape],
        scratch_shapes=[pltpu.VMEM((2, LANES, LANES), F32)],
        compiler_params=_params("parallel", "parallel", "arbitrary"),
        name="hgrn_scan",
    )(q, v, kf, gf, q, v, kb, gb, _hgrn_constants(False), _hgrn_constants(True), sf0, sb0)


def _hgrn_merge_kernel(x_ref, of_ref, ob_ref, gate_ref, og_ref, w_ref, mod_ref, y_ref):
    width = of_ref.shape[2]
    og = og_ref[...]
    parts = []
    for hd in range(width // LANES):
        sl = slice(hd * LANES, (hd + 1) * LANES)
        o = of_ref[0, :, sl] + ob_ref[0, :, sl]
        parts.append((_rms(o, og) * _silu(gate_ref[0, :, sl])).astype(BF16))
    r = jnp.concatenate(parts, axis=1)
    y = jnp.dot(r, w_ref[...], preferred_element_type=F32)
    y_ref[0] = x_ref[0] + mod_ref[0, 2] * y


def _hgrn_merge(x, o_f, o_b, gate, o_gain, w_out, mod):
    b, l, d = x.shape
    width = o_f.shape[2]
    tm = _row_tile(l, 512)
    wide = pl.BlockSpec((1, tm, width), lambda i, t: (i, t, 0))
    return pl.pallas_call(
        _hgrn_merge_kernel,
        grid=(b, l // tm),
        in_specs=[pl.BlockSpec((1, tm, d), lambda i, t: (i, t, 0)), wide, wide, wide,
                  _resident((1, LANES)),
                  _resident((width, d)),
                  pl.BlockSpec((1, 6, 1, d), lambda i, t: (i % mod.shape[0], 0, 0, 0))],
        out_specs=pl.BlockSpec((1, tm, d), lambda i, t: (i, t, 0)),
        out_shape=jax.ShapeDtypeStruct((b, l, d), F32),
        compiler_params=_params("parallel", "parallel"),
        name="hgrn_merge",
    )(x, o_f, o_b, gate, o_gain, w_out, mod)


def _mla_rope(y, cos, sin_lo, sin_hi):
    half = MLA_ROPE // 2
    return y * cos + pltpu.roll(y, LANES - half, 1) * sin_lo + pltpu.roll(y, half, 1) * sin_hi


def _mla_proj_kernel(x_ref, mod_ref, ng_ref, w_ref, qg_ref, kvg_ref, wq_ref, wk_ref, wv_ref,
                     cos_ref, slo_ref, shi_ref, q_ref, k_ref, v_ref, *, rotate, q_lora, kv_lora, chunk):
    h = _modnorm(x_ref[0], ng_ref[...], mod_ref[0, 0], mod_ref[0, 1]).astype(BF16)
    p = jnp.dot(h, w_ref[...], preferred_element_type=F32)
    cq = _rms(p[:, :q_lora], qg_ref[...]).astype(BF16)
    ckv = _rms(p[:, q_lora:q_lora + kv_lora], kvg_ref[...]).astype(BF16)
    k_rope = p[:, q_lora + kv_lora:]
    scale = (MLA_NOPE + MLA_ROPE) ** -0.5
    if rotate:
        cos, slo, shi = cos_ref[...], slo_ref[...], shi_ref[...]
        k_rope = _mla_rope(k_rope, cos, slo, shi)
    n = wq_ref.shape[1]
    for j in range(n // chunk):
        cs = slice(j * chunk, (j + 1) * chunk)
        pq = jnp.dot(cq, wq_ref[:, cs], preferred_element_type=F32)
        pk = jnp.dot(ckv, wk_ref[:, cs], preferred_element_type=F32)
        v_ref[0, :, cs] = jnp.dot(ckv, wv_ref[:, cs], preferred_element_type=F32).astype(BF16)
        for u in range(chunk // LANES):
            us = slice(u * LANES, (u + 1) * LANES)
            os = slice(j * chunk + u * LANES, j * chunk + (u + 1) * LANES)
            qh = pq[:, us]
            if rotate:
                qh = _mla_rope(qh, cos, slo, shi)
            q_ref[0, :, os] = (qh * scale).astype(BF16)
            k_ref[0, :, os] = (pk[:, us] + k_rope).astype(BF16)


def _mla_proj(x, mod, ng, w_in, qg, kvg, wq, wk, wv, cos, slo, shi, rotate):
    b, l, d = x.shape
    n = wq.shape[1]
    q_lora, kv_lora = wq.shape[0], wk.shape[0]
    tm = _row_tile(l, 512)
    kern = functools.partial(_mla_proj_kernel, rotate=rotate, q_lora=q_lora, kv_lora=kv_lora, chunk=512)
    out = jax.ShapeDtypeStruct((b, l, n), BF16)
    ospec = pl.BlockSpec((1, tm, n), lambda i, t: (i, t, 0))
    tab = pl.BlockSpec((tm, LANES), lambda i, t: (t, 0))
    return pl.pallas_call(
        kern,
        grid=(b, l // tm),
        in_specs=[pl.BlockSpec((1, tm, d), lambda i, t: (i, t, 0)),
                  pl.BlockSpec((1, 6, 1, d), lambda i, t: (i % mod.shape[0], 0, 0, 0)),
                  _resident((1, d)),
                  _resident(w_in.shape),
                  _resident((1, q_lora)),
                  _resident((1, kv_lora)),
                  _resident(wq.shape), _resident(wk.shape), _resident(wv.shape),
                  tab, tab, tab],
        out_specs=[ospec] * 3,
        out_shape=[out] * 3,
        compiler_params=_params("parallel", "parallel"),
        name="mla_proj",
    )(x, mod, ng, w_in, qg, kvg, wq, wk, wv, cos, slo, shi)


def _final_kernel(x_ref, g_ref, y_ref):
    y_ref[0] = _rms(x_ref[0], g_ref[...])


def _final_norm(x, gain):
    b, l, d = x.shape
    tm = _row_tile(l, 1024)
    return pl.pallas_call(
        _final_kernel,
        grid=(b, l // tm),
        in_specs=[pl.BlockSpec((1, tm, d), lambda i, t: (i, t, 0)), _resident((1, d))],
        out_specs=pl.BlockSpec((1, tm, d), lambda i, t: (i, t, 0)),
        out_shape=jax.ShapeDtypeStruct((b, l, d), F32),
        compiler_params=_params("parallel", "parallel"),
        name="final_norm",
    )(x, gain)


def _axial_angles(rows, rot_dim):
    row = jnp.repeat(jnp.arange(rows, dtype=F32), GRID_W)
    col = jnp.tile(jnp.arange(GRID_W, dtype=F32), rows)
    axis_dim = rot_dim // 2
    inv_freq = jnp.power(ROPE_THETA, -jnp.arange(0, axis_dim, 2, dtype=F32) / axis_dim)
    ang = jnp.concatenate([row[:, None] * inv_freq, col[:, None] * inv_freq], axis=-1)
    return jnp.cos(ang), jnp.sin(ang)


def _gqa_tables(rows):
    cos, sin = _axial_angles(rows, GQA_HEAD_DIM)
    return jnp.concatenate([cos, cos], axis=-1), jnp.concatenate([-sin, sin], axis=-1)


def _mla_tables(rows):
    cos, sin = _axial_angles(rows, MLA_ROPE)
    s = cos.shape[0]
    half = MLA_ROPE // 2
    ones = jnp.ones((s, MLA_NOPE), F32)
    zeros = jnp.zeros((s, MLA_NOPE), F32)
    tail1 = jnp.ones((s, LANES - MLA_NOPE - MLA_ROPE), F32)
    tail0 = jnp.zeros((s, LANES - MLA_NOPE - MLA_ROPE), F32)
    zh = jnp.zeros((s, half), F32)
    c = jnp.concatenate([ones, cos, cos, tail1], axis=-1)
    s_lo = jnp.concatenate([zeros, -sin, zh, tail0], axis=-1)
    s_hi = jnp.concatenate([zeros, zh, sin, tail0], axis=-1)
    return c, s_lo, s_hi


def _mla_weights(w_in, w_qb, w_kvb, w_out):
    d = w_in.shape[0]
    q_lora, kv_lora = w_qb.shape[0], w_kvb.shape[0]
    heads = w_qb.shape[1] // (MLA_NOPE + MLA_ROPE)
    kr = jnp.zeros((d, LANES), w_in.dtype).at[:, MLA_NOPE:MLA_NOPE + MLA_ROPE].set(w_in[:, q_lora + kv_lora:])
    w_in_p = jnp.concatenate([w_in[:, :q_lora + kv_lora], kr], axis=1)
    wq = w_qb.reshape(q_lora, heads, MLA_NOPE + MLA_ROPE)
    wq = jnp.pad(wq, ((0, 0), (0, 0), (0, LANES - MLA_NOPE - MLA_ROPE))).reshape(q_lora, heads * LANES)
    wkv = w_kvb.reshape(kv_lora, heads, MLA_NOPE + MLA_V)
    wk = jnp.pad(wkv[:, :, :MLA_NOPE], ((0, 0), (0, 0), (0, LANES - MLA_NOPE))).reshape(kv_lora, heads * LANES)
    wv = jnp.pad(wkv[:, :, MLA_NOPE:], ((0, 0), (0, 0), (0, LANES - MLA_V))).reshape(kv_lora, heads * LANES)
    wo = w_out.reshape(heads, MLA_V, -1)
    wo = jnp.pad(wo, ((0, 0), (0, LANES - MLA_V), (0, 0))).reshape(heads * LANES, -1)
    return w_in_p.astype(BF16), wq.astype(BF16), wk.astype(BF16), wv.astype(BF16), wo.astype(BF16)


def kernel(x, c, ctx, c_ctx, w_ada, b_ada, norm_mix, norm_ffn, ffn_w_in, ffn_conv_w, ffn_conv_b, ffn_w_out,
           gqa_w_in, gqa_q_norm, gqa_k_norm, gqa_w_out, hgrn_w_in, hgrn_out_norm, hgrn_w_out, hgrn_lower_bounds,
           mla_w_in, mla_q_norm, mla_kv_norm, mla_w_qb, mla_w_kvb, mla_w_out, final_norm):
    batch, seq, d = x.shape
    depth = w_ada.shape[0]
    n_mixers = 3
    rows = seq // GRID_W
    assert batch + 1 <= SUBLANES

    cv = jnp.zeros((SUBLANES, d), F32).at[:batch].set(c).at[batch].set(c_ctx)
    mods = _ada_mods(cv, w_ada, b_ada).reshape(depth, SUBLANES, 6, 1, d)
    lb_all = _lower_bounds(hgrn_lower_bounds)

    cos_a, sin_a = _gqa_tables(rows)
    cos_m, slo_m, shi_m = _mla_tables(rows)

    for i in range(depth):
        last = i == depth - 1
        kind = i % n_mixers
        j = i // n_mixers
        mod = mods[i, :batch]
        mod_c = mods[i, batch:batch + 1]
        ng = norm_mix[i][None, :]

        if kind == 0:
            w_in = gqa_w_in[j].astype(BF16)
            w_out = gqa_w_out[j].astype(BF16)
            qg, kg = gqa_q_norm[j][None, :], gqa_k_norm[j][None, :]
            q, k, v = _gqa_proj(x, mod, ng, w_in, qg, kg, cos_a, sin_a, True)
            n_ctx = ctx.shape[1]
            qc, kc, vc = _gqa_proj(ctx, mod_c, ng, w_in, qg, kg, cos_a[:n_ctx], sin_a[:n_ctx], False)
            o = _attention(q, [(k, v), (kc, vc)], GQA_GROUP, 256)
            x = _merge(x, o, w_out, mod)
            if not last:
                oc = _attention(qc, [(kc, vc)], GQA_GROUP, 256)
                ctx = _merge(ctx, oc, w_out, mod_c)
        elif kind == 1:
            w_in = hgrn_w_in[j].astype(BF16)
            w_out = hgrn_w_out[j].astype(BF16)
            lb = lb_all[i][None, :]
            og = hgrn_out_norm[j][None, :]
            heads = w_out.shape[0] // HGRN_DK
            qc, vc, kfc, gfc, kbc, gbc, gatec = _hgrn_proj(ctx, mod_c, ng, w_in, lb)
            q, v, kf, gf, kb, gb, gate = _hgrn_proj(x, mod, ng, w_in, lb)
            s0 = jnp.zeros((batch, heads, HGRN_DK, HGRN_DK), F32)
            oc_f, oc_b, s_f, s_b = _hgrn_scan(qc, vc, kfc, gfc, kbc, gbc, s0, s0)
            o_f, o_b, _, _ = _hgrn_scan(q, v, kf, gf, kb, gb, s_f, s_b)
            x = _hgrn_merge(x, o_f, o_b, gate, og, w_out, mod)
            if not last:
                ctx = _hgrn_merge(ctx, oc_f, oc_b, gatec, og, w_out, mod_c)
        else:
            w_in, wq, wk, wv, w_out = _mla_weights(mla_w_in[j], mla_w_qb[j], mla_w_kvb[j], mla_w_out[j])
            qg, kvg = mla_q_norm[j][None, :], mla_kv_norm[j][None, :]
            n_ctx = ctx.shape[1]
            q, k, v = _mla_proj(x, mod, ng, w_in, qg, kvg, wq, wk, wv, cos_m, slo_m, shi_m, True)
            qc, kc, vc = _mla_proj(ctx, mod_c, ng, w_in, qg, kvg, wq, wk, wv,
                                   cos_m[:n_ctx], slo_m[:n_ctx], shi_m[:n_ctx], False)
            o = _attention(q, [(k, v), (kc, vc)], 1, 512)
            x = _merge(x, o, w_out, mod)
            if not last:
                oc = _attention(qc, [(kc, vc)], 1, 512)
                ctx = _merge(ctx, oc, w_out, mod_c)

        fg = norm_ffn[i][None, :]
        f_in = ffn_w_in[i].astype(BF16)
        f_out = ffn_w_out[i].astype(BF16)
        f_cw = ffn_conv_w[i]
        f_cb = ffn_conv_b[i][None, :]
        x = _conv_ffn(x, mod, fg, f_in, f_cw, f_cb, f_out)
        if not last:
            ctx = _conv_ffn(ctx, mod_c, fg, f_in, f_cw, f_cb, f_out)

    return _final_norm(x, final_norm[None, :])
```

```python
import functools

import numpy as np
import jax
import jax.numpy as jnp
from jax import lax
from jax.experimental import pallas as pl
from jax.experimental.pallas import tpu as pltpu

F32 = jnp.float32
BF16 = jnp.bfloat16
HIGHEST = lax.Precision.HIGHEST

GRID_W = 64
ROPE_THETA = 10000.0
NORM_EPS = 1e-6
CONV_W = 3

LANES = 128
SUBLANES = 8

GQA_HEAD_DIM = 128
GQA_GROUP = 2

HGRN_DK = 128
HGRN_CHUNK = 64
HGRN_PARTS = 5

MLA_NOPE = 64
MLA_ROPE = 32
MLA_V = 64

VMEM_LIMIT = 56 * 1024 * 1024

LOG2_E = 1.4426950408889634
ATTN_UNROLL = 4


def _params(*sem):
    return pltpu.CompilerParams(dimension_semantics=sem, vmem_limit_bytes=VMEM_LIMIT)


def _resident(shape):
    nd = len(shape)
    return pl.BlockSpec(shape, lambda *_: (0,) * nd, pipeline_mode=pl.Buffered(1))


def _silu(x):
    return x / (1.0 + jnp.exp(-x))


def _sigmoid(x):
    return 1.0 / (1.0 + jnp.exp(-x))


def _rms(x, gain):
    return x * lax.rsqrt(jnp.mean(x * x, axis=-1, keepdims=True) + NORM_EPS) * gain


def _modnorm(x, gain, shift, scale):
    return _rms(x, gain) * (1.0 + scale) + shift


def _aligned_ds(start, size):
    if isinstance(start, int):
        return pl.ds(start, size)
    return pl.ds(pl.multiple_of(start, size), size)


def _row_tile(n, want):
    t = min(n, want)
    assert n % t == 0, (n, t)
    return t


def _ada_kernel(cv_ref, w_ref, b_ref, o_ref):
    s = _silu(cv_ref[...])
    o_ref[0] = jnp.dot(s, w_ref[0], precision=HIGHEST, preferred_element_type=F32) + b_ref[0]


def _ada_mods(cv, w_ada, b_ada):
    depth, d, n = w_ada.shape
    tn = 1536
    return pl.pallas_call(
        _ada_kernel,
        grid=(depth, n // tn),
        in_specs=[pl.BlockSpec((SUBLANES, d), lambda i, j: (0, 0)),
                  pl.BlockSpec((1, d, tn), lambda i, j: (i, 0, j)),
                  pl.BlockSpec((1, 1, tn), lambda i, j: (i, 0, j))],
        out_specs=pl.BlockSpec((1, SUBLANES, tn), lambda i, j: (i, 0, j)),
        out_shape=jax.ShapeDtypeStruct((depth, SUBLANES, n), F32),
        compiler_params=_params("arbitrary", "arbitrary"),
        name="ada_mods",
    )(cv, w_ada, b_ada.reshape(depth, 1, n))


def _lb_kernel(x_ref, o_ref):
    depth = x_ref.shape[0]
    rows = [x_ref[i:i + 1, :] for i in range(depth)]
    m = rows[0]
    for r in rows[1:]:
        m = jnp.maximum(m, r)
    e = [jnp.exp(r - m) for r in rows]
    tot = e[0]
    for r in e[1:]:
        tot = tot + r
    p = [r / tot for r in e]
    cum = p[0]
    o_ref[0:1, :] = cum - p[0]
    for i in range(1, depth):
        cum = cum + p[i]
        o_ref[i:i + 1, :] = cum - p[0]


def _lower_bounds(lb_raw):
    return pl.pallas_call(
        _lb_kernel,
        out_shape=jax.ShapeDtypeStruct(lb_raw.shape, F32),
        name="hgrn_lower_bounds",
    )(lb_raw.astype(F32))


def _gqa_proj_kernel(x_ref, mod_ref, ng_ref, w_ref, qg_ref, kg_ref, cos_ref, sin_ref,
                     q_ref, k_ref, v_ref, *, rotate, qd, kd, chunk):
    h = _modnorm(x_ref[0], ng_ref[...], mod_ref[0, 0], mod_ref[0, 1]).astype(BF16)
    qg = qg_ref[...]
    kg = kg_ref[...]
    scale = GQA_HEAD_DIM ** -0.5 * LOG2_E
    if rotate:
        cos = cos_ref[...]
        sin = sin_ref[...]

    def head(p, gain):
        y = _rms(p, gain)
        if rotate:
            y = y * cos + pltpu.roll(y, GQA_HEAD_DIM // 2, 1) * sin
        return y

    n = w_ref.shape[1]
    for j in range(n // chunk):
        p = jnp.dot(h, w_ref[:, j * chunk:(j + 1) * chunk], preferred_element_type=F32)
        for u in range(chunk // LANES):
            col = j * chunk + u * LANES
            ph = p[:, u * LANES:(u + 1) * LANES]
            if col < qd:
                q_ref[0, :, col:col + LANES] = (head(ph, qg) * scale).astype(BF16)
            elif col < qd + kd:
                k_ref[0, :, col - qd:col - qd + LANES] = head(ph, kg).astype(BF16)
            else:
                c0 = col - qd - kd
                v_ref[0, 0, c0:c0 + LANES, :] = ph.T.astype(BF16)


def _gqa_proj(x, mod, ng, w_in, qg, kg, cos, sin, rotate):
    b, l, d = x.shape
    n = w_in.shape[1]
    kd = n // 4
    qd = n - 2 * kd
    tm = _row_tile(l, 512)
    kern = functools.partial(_gqa_proj_kernel, rotate=rotate, qd=qd, kd=kd, chunk=512)
    return pl.pallas_call(
        kern,
        grid=(b, l // tm),
        in_specs=[pl.BlockSpec((1, tm, d), lambda i, t: (i, t, 0)),
                  pl.BlockSpec((1, 6, 1, d), lambda i, t: (i % mod.shape[0], 0, 0, 0)),
                  _resident((1, d)),
                  _resident((d, n)),
                  _resident((1, LANES)),
                  _resident((1, LANES)),
                  pl.BlockSpec((tm, LANES), lambda i, t: (t, 0)),
                  pl.BlockSpec((tm, LANES), lambda i, t: (t, 0))],
        out_specs=[pl.BlockSpec((1, tm, qd), lambda i, t: (i, t, 0)),
                   pl.BlockSpec((1, tm, kd), lambda i, t: (i, t, 0)),
                   pl.BlockSpec((1, 1, kd, tm), lambda i, t: (i, t, 0, 0))],
        out_shape=[jax.ShapeDtypeStruct((b, l, qd), BF16),
                   jax.ShapeDtypeStruct((b, l, kd), BF16),
                   jax.ShapeDtypeStruct((b, l // tm, kd, tm), BF16)],
        compiler_params=_params("parallel", "parallel"),
        name="gqa_proj",
    )(x, mod, ng, w_in, qg, kg, cos, sin)


def _attn_kernel(*refs, group, n_src):
    q_ref = refs[0]
    kv_refs = refs[1:1 + 2 * n_src]
    o_ref = refs[1 + 2 * n_src]
    s_sc = refs[2 + 2 * n_src]
    tq = q_ref.shape[1]
    nq = group * tq

    q = q_ref[0]
    if group > 1:
        q = jnp.concatenate([q[:, g * LANES:(g + 1) * LANES] for g in range(group)], axis=0)

    def fold(a):
        return a.reshape(a.shape[0] // SUBLANES, SUBLANES, nq)

    m8 = jnp.full((SUBLANES, nq), -jnp.inf, F32)
    row0 = 0
    for i in range(n_src):
        k_ref = kv_refs[2 * i]
        n_chunks, _, c = kv_refs[2 * i + 1].shape[1:]

        def scores(j, m8, k_ref=k_ref, c=c, row0=row0):
            s = lax.dot_general(k_ref[0, _aligned_ds(j * c, c), :], q, (((1,), (1,)), ((), ())),
                                preferred_element_type=F32)
            s_sc[_aligned_ds(row0 + j * c, c), :] = s
            return jnp.maximum(m8, jnp.max(fold(s), axis=0))

        m8 = scores(0, m8) if n_chunks == 1 else lax.fori_loop(0, n_chunks, scores, m8, unroll=ATTN_UNROLL)
        row0 += n_chunks * c
    m = jnp.max(m8, axis=0, keepdims=True)

    l8 = jnp.zeros((SUBLANES, nq), F32)
    acc = jnp.zeros((LANES, nq), F32)
    row0 = 0
    for i in range(n_src):
        vt_ref = kv_refs[2 * i + 1]
        n_chunks, _, c = vt_ref.shape[1:]

        def weigh(j, carry, vt_ref=vt_ref, c=c, row0=row0):
            l8, acc = carry
            p = jnp.exp2(s_sc[_aligned_ds(row0 + j * c, c), :] - m)
            l8 = l8 + jnp.sum(fold(p), axis=0)
            acc = acc + jnp.dot(vt_ref[0, j], p.astype(BF16), preferred_element_type=F32)
            return l8, acc

        l8, acc = (weigh(0, (l8, acc)) if n_chunks == 1 else
                   lax.fori_loop(0, n_chunks, weigh, (l8, acc), unroll=ATTN_UNROLL))
        row0 += n_chunks * c

    o = acc / jnp.sum(l8, axis=0, keepdims=True)
    for g in range(group):
        o_ref[0, :, g * LANES:(g + 1) * LANES] = o[:, g * tq:(g + 1) * tq].T.astype(o_ref.dtype)


def _attention(q, kv_sources, group, tq):
    b, nq, hd = q.shape
    hkv = hd // (group * LANES)
    tq = _row_tile(nq, tq)
    n_src = len(kv_sources)
    in_specs = [pl.BlockSpec((1, tq, group * LANES), lambda i, h, t: (i, t, h))]
    args = [q]
    nk_total = 0
    for k, vt in kv_sources:
        nk = k.shape[1]
        n_chunks, _, c = vt.shape[1:]
        assert n_chunks * c == nk
        nk_total += nk
        in_specs.append(pl.BlockSpec((1, nk, LANES), lambda i, h, t: (i, 0, h)))
        in_specs.append(pl.BlockSpec((1, n_chunks, LANES, c), lambda i, h, t: (i, 0, h, 0)))
        args += [k, vt]
    return pl.pallas_call(
        functools.partial(_attn_kernel, group=group, n_src=n_src),
        grid=(b, hkv, nq // tq),
        in_specs=in_specs,
        out_specs=pl.BlockSpec((1, tq, group * LANES), lambda i, h, t: (i, t, h)),
        out_shape=jax.ShapeDtypeStruct((b, nq, hd), BF16),
        scratch_shapes=[pltpu.VMEM((nk_total, group * tq), F32)],
        compiler_params=_params("parallel", "parallel", "arbitrary"),
        name="flash_attention",
    )(*args)


def _merge_kernel(x_ref, o_ref, w_ref, mod_ref, y_ref):
    y = jnp.dot(o_ref[0], w_ref[...], preferred_element_type=F32)
    y_ref[0] = x_ref[0] + mod_ref[0, 2] * y


def _merge(x, o, w_out, mod):
    b, l, d = x.shape
    ko = o.shape[2]
    tm = _row_tile(l, 512)
    return pl.pallas_call(
        _merge_kernel,
        grid=(b, l // tm),
        in_specs=[pl.BlockSpec((1, tm, d), lambda i, t: (i, t, 0)),
                  pl.BlockSpec((1, tm, ko), lambda i, t: (i, t, 0)),
                  _resident((ko, d)),
                  pl.BlockSpec((1, 6, 1, d), lambda i, t: (i % mod.shape[0], 0, 0, 0))],
        out_specs=pl.BlockSpec((1, tm, d), lambda i, t: (i, t, 0)),
        out_shape=jax.ShapeDtypeStruct((b, l, d), F32),
        compiler_params=_params("parallel", "parallel"),
        name="merge_residual",
    )(x, o, w_out, mod)


def _ffn_kernel(xp_ref, x_ref, xn_ref, mod_ref, ng_ref, win_ref, cw_ref, cb_ref, wout_ref, y_ref,
                *, d_ff, chunk):
    t = pl.program_id(1)
    nt = pl.num_programs(1)
    gain = ng_ref[...]
    shift, scale, gate = mod_ref[0, 3], mod_ref[0, 4], mod_ref[0, 5]
    x = x_ref[0]
    tm = x.shape[0]
    halo = xp_ref.shape[1]
    hp = jnp.where(t > 0, _modnorm(xp_ref[0], gain, shift, scale), 0.0)
    hn = jnp.where(t < nt - 1, _modnorm(xn_ref[0], gain, shift, scale), 0.0)
    h = jnp.concatenate([hp, _modnorm(x, gain, shift, scale), hn], axis=0).astype(BF16)
    rows = tm + 2 * halo

    def conv(p, col):
        w = cw_ref[:, col:col + chunk]
        prev = pltpu.roll(p, 1, 0)[halo:halo + tm]
        nxt = pltpu.roll(p, rows - 1, 0)[halo:halo + tm]
        cur = p[halo:halo + tm]
        return ((cb_ref[:, col:col + chunk] + prev * w[0:1]) + cur * w[1:2]) + nxt * w[2:3]

    acc = jnp.zeros((tm, x.shape[1]), F32)
    for c in range(d_ff // chunk):
        pa = jnp.dot(h, win_ref[:, c * chunk:(c + 1) * chunk], preferred_element_type=F32)
        pv = jnp.dot(h, win_ref[:, d_ff + c * chunk:d_ff + (c + 1) * chunk], preferred_element_type=F32)
        g = _silu(conv(pa, c * chunk)) * conv(pv, d_ff + c * chunk)
        acc = acc + jnp.dot(g.astype(BF16), wout_ref[c * chunk:(c + 1) * chunk, :],
                            preferred_element_type=F32)
    y_ref[0] = x + gate * acc


def _conv_ffn(x, mod, ng, w_in, conv_w, conv_b, w_out):
    b, l, d = x.shape
    d_ff = w_out.shape[0]
    tm = _row_tile(l, 512)
    halo = SUBLANES
    per = tm // halo
    last = l // halo - 1
    kern = functools.partial(_ffn_kernel, d_ff=d_ff, chunk=512)
    return pl.pallas_call(
        kern,
        grid=(b, l // tm),
        in_specs=[pl.BlockSpec((1, halo, d), lambda i, t: (i, jnp.maximum(t * per - 1, 0), 0)),
                  pl.BlockSpec((1, tm, d), lambda i, t: (i, t, 0)),
                  pl.BlockSpec((1, halo, d), lambda i, t: (i, jnp.minimum((t + 1) * per, last), 0)),
                  pl.BlockSpec((1, 6, 1, d), lambda i, t: (i % mod.shape[0], 0, 0, 0)),
                  _resident((1, d)),
                  _resident((d, 2 * d_ff)),
                  _resident((CONV_W, 2 * d_ff)),
                  _resident((1, 2 * d_ff)),
                  _resident((d_ff, d))],
        out_specs=pl.BlockSpec((1, tm, d), lambda i, t: (i, t, 0)),
        out_shape=jax.ShapeDtypeStruct((b, l, d), F32),
        compiler_params=_params("parallel", "arbitrary"),
        name="conv_ffn",
    )(x, x, x, mod, ng, w_in, conv_w, conv_b, w_out)


def _hgrn_proj_kernel(x_ref, mod_ref, ng_ref, w_ref, lb_ref, q_ref, v_ref, kf_ref, gf_ref, kb_ref, gb_ref,
                      gate_ref, *, chunk):
    h = _modnorm(x_ref[0], ng_ref[...], mod_ref[0, 0], mod_ref[0, 1]).astype(BF16)
    width = q_ref.shape[2]
    scale = HGRN_DK ** -0.5
    for part in range(HGRN_PARTS):
        for j in range(width // chunk):
            c0 = j * chunk
            p = jnp.dot(h, w_ref[:, part * width + c0:part * width + c0 + chunk], preferred_element_type=F32)
            if part == 0:
                q_ref[0, :, c0:c0 + chunk] = p * scale
            elif part == 1:
                v_ref[0, :, c0:c0 + chunk] = p
            elif part == 4:
                gate_ref[0, :, c0:c0 + chunk] = p
            else:
                lb = lb_ref[:, c0:c0 + chunk]
                f = lb + (1.0 - lb) * _sigmoid(p)
                k_out, g_out = (kf_ref, gf_ref) if part == 2 else (kb_ref, gb_ref)
                k_out[0, :, c0:c0 + chunk] = 1.0 - f
                g_out[0, :, c0:c0 + chunk] = jnp.log(f)


def _hgrn_proj(x, mod, ng, w_in, lb):
    b, l, d = x.shape
    width = w_in.shape[1] // HGRN_PARTS
    tm = _row_tile(l, 512)
    out = jax.ShapeDtypeStruct((b, l, width), F32)
    ospec = pl.BlockSpec((1, tm, width), lambda i, t: (i, t, 0))
    return pl.pallas_call(
        functools.partial(_hgrn_proj_kernel, chunk=512),
        grid=(b, l // tm),
        in_specs=[pl.BlockSpec((1, tm, d), lambda i, t: (i, t, 0)),
                  pl.BlockSpec((1, 6, 1, d), lambda i, t: (i % mod.shape[0], 0, 0, 0)),
                  _resident((1, d)),
                  _resident((d, HGRN_PARTS * width)),
                  _resident((1, width))],
        out_specs=[ospec] * 7,
        out_shape=[out] * 7,
        compiler_params=_params("parallel", "parallel"),
        name="hgrn_proj",
    )(x, mod, ng, w_in, lb)


_HGRN_LEVELS = (32, 16, 8)


def _hgrn_constants(reverse):
    c = HGRN_CHUNK
    idx = np.arange(c)
    if reverse:
        tri = (idx[None, :] >= idx[:, None]).astype(np.float32)
    else:
        tri = (idx[None, :] <= idx[:, None]).astype(np.float32)
    mats = [tri]
    for half in _HGRN_LEVELS:
        blk = idx // (2 * half)
        boundary = blk * 2 * half + (half if reverse else half - 1)
        mats.append(tri[boundary])
    mats.append(np.ones((c, c), np.float32))
    return jnp.asarray(np.concatenate(mats, axis=0))


def _hgrn_chunk(q, k, v, g, cmat, st, reverse, ones_bf):
    c = HGRN_CHUNK
    a = jnp.dot(cmat, g, precision=HIGHEST, preferred_element_type=F32)
    cum = a[0:c]
    tot = a[(len(_HGRN_LEVELS) + 1) * c:(len(_HGRN_LEVELS) + 2) * c]
    row = lax.broadcasted_iota(jnp.int32, (c, c), 0)
    col = lax.broadcasted_iota(jnp.int32, (c, c), 1)
    early, late = (col, row) if not reverse else (row, col)
    scores = jnp.zeros((c, c), F32)
    for li, half in enumerate(_HGRN_LEVELS):
        ref = a[(li + 1) * c:(li + 2) * c]
        qs = q * jnp.exp(jnp.minimum(cum - ref, 0.0))
        ks = k * jnp.exp(jnp.minimum(ref - cum, 0.0))
        s_l = lax.dot_general(qs.astype(BF16), ks.astype(BF16), (((1,), (1,)), ((), ())),
                              preferred_element_type=F32)
        same = (row // (2 * half)) == (col // (2 * half))
        early_half = (early % (2 * half)) < half
        late_half = (late % (2 * half)) >= half
        scores = jnp.where(same & early_half & late_half, s_l, scores)
    sub = _HGRN_LEVELS[-1]
    prods = [q * k]
    for delta in range(1, sub):
        shift = delta if not reverse else c - delta
        k_sh = pltpu.roll(k, shift, 0)
        cum_sh = pltpu.roll(cum, shift, 0)
        prods.append(q * k_sh * jnp.exp(jnp.minimum(cum - cum_sh, 0.0)))
    diag = jnp.dot(jnp.concatenate(prods, axis=0).astype(BF16), ones_bf, preferred_element_type=F32)
    for delta in range(sub):
        d_t = diag[delta * c:(delta + 1) * c, 0:c]
        if not reverse:
            hit = (col == row - delta) & ((row % sub) >= delta)
        else:
            hit = (col == row + delta) & ((row % sub) + delta < sub)
        scores = jnp.where(hit, d_t, scores)
    qe = (q * jnp.exp(cum)).astype(BF16)
    out = jnp.dot(scores.astype(BF16), v.astype(BF16), preferred_element_type=F32)
    out = out + lax.dot_general(qe, st.astype(BF16), (((1,), (1,)), ((), ())), preferred_element_type=F32)
    kd = (k * jnp.exp(tot - cum)).astype(BF16)
    upd = jnp.dot(v.T.astype(BF16), kd, preferred_element_type=F32)
    st_new = st * jnp.exp(tot[0:1]) + upd
    return out, st_new


def _hgrn_scan_kernel(qf_ref, vf_ref, kf_ref, gf_ref, qb_ref, vb_ref, kb_ref, gb_ref, cf_ref, cb_ref,
                      sf0_ref, sb0_ref, of_ref, ob_ref, sf_ref, sb_ref, st_sc):
    step = pl.program_id(2)
    rb = qf_ref.shape[1]
    c = HGRN_CHUNK

    @pl.when(step == 0)
    def _():
        st_sc[0] = sf0_ref[0, 0]
        st_sc[1] = sb0_ref[0, 0]

    ones_bf = jnp.ones((LANES, LANES), BF16)
    cf = cf_ref[...]
    cb = cb_ref[...]
    st_f = st_sc[0]
    st_b = st_sc[1]
    for j in range(rb // c):
        lo = j * c
        o, st_f = _hgrn_chunk(qf_ref[0, lo:lo + c, :], kf_ref[0, lo:lo + c, :], vf_ref[0, lo:lo + c, :],
                              gf_ref[0, lo:lo + c, :], cf, st_f, False, ones_bf)
        of_ref[0, lo:lo + c, :] = o
        lo = rb - (j + 1) * c
        o, st_b = _hgrn_chunk(qb_ref[0, lo:lo + c, :], kb_ref[0, lo:lo + c, :], vb_ref[0, lo:lo + c, :],
                              gb_ref[0, lo:lo + c, :], cb, st_b, True, ones_bf)
        ob_ref[0, lo:lo + c, :] = o
    st_sc[0] = st_f
    st_sc[1] = st_b

    @pl.when(step == pl.num_programs(2) - 1)
    def _():
        sf_ref[0, 0] = st_f
        sb_ref[0, 0] = st_b


def _hgrn_scan(q, v, kf, gf, kb, gb, sf0, sb0):
    b, l, width = q.shape
    heads = width // LANES
    rb = _row_tile(l, 256)
    nc = l // rb
    fwd = pl.BlockSpec((1, rb, LANES), lambda i, h, s: (i, s, h))
    bwd = pl.BlockSpec((1, rb, LANES), lambda i, h, s: (i, nc - 1 - s, h))
    st_spec = pl.BlockSpec((1, 1, LANES, LANES), lambda i, h, s: (i, h, 0, 0))
    n_rows = (len(_HGRN_LEVELS) + 2) * HGRN_CHUNK
    o_shape = jax.ShapeDtypeStruct((b, l, width), F32)
    s_shape = jax.ShapeDtypeStruct((b, heads, LANES, LANES), F32)
    return pl.pallas_call(
        _hgrn_scan_kernel,
        grid=(b, heads, nc),
        in_specs=[fwd, fwd, fwd, fwd, bwd, bwd, bwd, bwd,
                  _resident((n_rows, HGRN_CHUNK)), _resident((n_rows, HGRN_CHUNK)), st_spec, st_spec],
        out_specs=[fwd, bwd, st_spec, st_spec],
        out_shape=[o_shape, o_shape, s_shape, s_shape],
        scratch_shapes=[pltpu.VMEM((2, LANES, LANES), F32)],
        compiler_params=_params("parallel", "parallel", "arbitrary"),
        name="hgrn_scan",
    )(q, v, kf, gf, q, v, kb, gb, _hgrn_constants(False), _hgrn_constants(True), sf0, sb0)


def _hgrn_merge_kernel(x_ref, of_ref, ob_ref, gate_ref, og_ref, w_ref, mod_ref, y_ref):
    width = of_ref.shape[2]
    og = og_ref[...]
    parts = []
    for hd in range(width // LANES):
        sl = slice(hd * LANES, (hd + 1) * LANES)
        o = of_ref[0, :, sl] + ob_ref[0, :, sl]
        parts.append((_rms(o, og) * _silu(gate_ref[0, :, sl])).astype(BF16))
    r = jnp.concatenate(parts, axis=1)
    y = jnp.dot(r, w_ref[...], preferred_element_type=F32)
    y_ref[0] = x_ref[0] + mod_ref[0, 2] * y


def _hgrn_merge(x, o_f, o_b, gate, o_gain, w_out, mod):
    b, l, d = x.shape
    width = o_f.shape[2]
    tm = _row_tile(l, 512)
    wide = pl.BlockSpec((1, tm, width), lambda i, t: (i, t, 0))
    return pl.pallas_call(
        _hgrn_merge_kernel,
        grid=(b, l // tm),
        in_specs=[pl.BlockSpec((1, tm, d), lambda i, t: (i, t, 0)), wide, wide, wide,
                  _resident((1, LANES)),
                  _resident((width, d)),
                  pl.BlockSpec((1, 6, 1, d), lambda i, t: (i % mod.shape[0], 0, 0, 0))],
        out_specs=pl.BlockSpec((1, tm, d), lambda i, t: (i, t, 0)),
        out_shape=jax.ShapeDtypeStruct((b, l, d), F32),
        compiler_params=_params("parallel", "parallel"),
        name="hgrn_merge",
    )(x, o_f, o_b, gate, o_gain, w_out, mod)


def _mla_rope(y, cos, sin_lo, sin_hi):
    half = MLA_ROPE // 2
    return y * cos + pltpu.roll(y, LANES - half, 1) * sin_lo + pltpu.roll(y, half, 1) * sin_hi


def _mla_proj_kernel(x_ref, mod_ref, ng_ref, w_ref, qg_ref, kvg_ref, wq_ref, wk_ref, wv_ref,
                     cos_ref, slo_ref, shi_ref, q_ref, k_ref, v_ref, *, rotate, q_lora, kv_lora, chunk):
    h = _modnorm(x_ref[0], ng_ref[...], mod_ref[0, 0], mod_ref[0, 1]).astype(BF16)
    p = jnp.dot(h, w_ref[...], preferred_element_type=F32)
    cq = _rms(p[:, :q_lora], qg_ref[...]).astype(BF16)
    ckv = _rms(p[:, q_lora:q_lora + kv_lora], kvg_ref[...]).astype(BF16)
    k_rope = p[:, q_lora + kv_lora:]
    scale = (MLA_NOPE + MLA_ROPE) ** -0.5 * LOG2_E
    if rotate:
        cos, slo, shi = cos_ref[...], slo_ref[...], shi_ref[...]
        k_rope = _mla_rope(k_rope, cos, slo, shi)
    n = wq_ref.shape[1]
    for j in range(n // chunk):
        cs = slice(j * chunk, (j + 1) * chunk)
        pq = jnp.dot(cq, wq_ref[:, cs], preferred_element_type=F32)
        pk = jnp.dot(ckv, wk_ref[:, cs], preferred_element_type=F32)
        pv = jnp.dot(ckv, wv_ref[:, cs], preferred_element_type=F32)
        for u in range(chunk // LANES):
            us = slice(u * LANES, (u + 1) * LANES)
            os = slice(j * chunk + u * LANES, j * chunk + (u + 1) * LANES)
            v_ref[0, 0, os, :] = pv[:, us].T.astype(BF16)
            qh = pq[:, us]
            if rotate:
                qh = _mla_rope(qh, cos, slo, shi)
            q_ref[0, :, os] = (qh * scale).astype(BF16)
            k_ref[0, :, os] = (pk[:, us] + k_rope).astype(BF16)


def _mla_proj(x, mod, ng, w_in, qg, kvg, wq, wk, wv, cos, slo, shi, rotate):
    b, l, d = x.shape
    n = wq.shape[1]
    q_lora, kv_lora = wq.shape[0], wk.shape[0]
    tm = _row_tile(l, 512)
    kern = functools.partial(_mla_proj_kernel, rotate=rotate, q_lora=q_lora, kv_lora=kv_lora, chunk=512)
    out = jax.ShapeDtypeStruct((b, l, n), BF16)
    ospec = pl.BlockSpec((1, tm, n), lambda i, t: (i, t, 0))
    tab = pl.BlockSpec((tm, LANES), lambda i, t: (t, 0))
    return pl.pallas_call(
        kern,
        grid=(b, l // tm),
        in_specs=[pl.BlockSpec((1, tm, d), lambda i, t: (i, t, 0)),
                  pl.BlockSpec((1, 6, 1, d), lambda i, t: (i % mod.shape[0], 0, 0, 0)),
                  _resident((1, d)),
                  _resident(w_in.shape),
                  _resident((1, q_lora)),
                  _resident((1, kv_lora)),
                  _resident(wq.shape), _resident(wk.shape), _resident(wv.shape),
                  tab, tab, tab],
        out_specs=[ospec, ospec, pl.BlockSpec((1, 1, n, tm), lambda i, t: (i, t, 0, 0))],
        out_shape=[out, out, jax.ShapeDtypeStruct((b, l // tm, n, tm), BF16)],
        compiler_params=_params("parallel", "parallel"),
        name="mla_proj",
    )(x, mod, ng, w_in, qg, kvg, wq, wk, wv, cos, slo, shi)


def _final_kernel(x_ref, g_ref, y_ref):
    y_ref[0] = _rms(x_ref[0], g_ref[...])


def _final_norm(x, gain):
    b, l, d = x.shape
    tm = _row_tile(l, 1024)
    return pl.pallas_call(
        _final_kernel,
        grid=(b, l // tm),
        in_specs=[pl.BlockSpec((1, tm, d), lambda i, t: (i, t, 0)), _resident((1, d))],
        out_specs=pl.BlockSpec((1, tm, d), lambda i, t: (i, t, 0)),
        out_shape=jax.ShapeDtypeStruct((b, l, d), F32),
        compiler_params=_params("parallel", "parallel"),
        name="final_norm",
    )(x, gain)


def _axial_angles(rows, rot_dim):
    row = jnp.repeat(jnp.arange(rows, dtype=F32), GRID_W)
    col = jnp.tile(jnp.arange(GRID_W, dtype=F32), rows)
    axis_dim = rot_dim // 2
    inv_freq = jnp.power(ROPE_THETA, -jnp.arange(0, axis_dim, 2, dtype=F32) / axis_dim)
    ang = jnp.concatenate([row[:, None] * inv_freq, col[:, None] * inv_freq], axis=-1)
    return jnp.cos(ang), jnp.sin(ang)


def _gqa_tables(rows):
    cos, sin = _axial_angles(rows, GQA_HEAD_DIM)
    return jnp.concatenate([cos, cos], axis=-1), jnp.concatenate([-sin, sin], axis=-1)


def _mla_tables(rows):
    cos, sin = _axial_angles(rows, MLA_ROPE)
    s = cos.shape[0]
    half = MLA_ROPE // 2
    ones = jnp.ones((s, MLA_NOPE), F32)
    zeros = jnp.zeros((s, MLA_NOPE), F32)
    tail1 = jnp.ones((s, LANES - MLA_NOPE - MLA_ROPE), F32)
    tail0 = jnp.zeros((s, LANES - MLA_NOPE - MLA_ROPE), F32)
    zh = jnp.zeros((s, half), F32)
    c = jnp.concatenate([ones, cos, cos, tail1], axis=-1)
    s_lo = jnp.concatenate([zeros, -sin, zh, tail0], axis=-1)
    s_hi = jnp.concatenate([zeros, zh, sin, tail0], axis=-1)
    return c, s_lo, s_hi


def _mla_weights(w_in, w_qb, w_kvb, w_out):
    d = w_in.shape[0]
    q_lora, kv_lora = w_qb.shape[0], w_kvb.shape[0]
    heads = w_qb.shape[1] // (MLA_NOPE + MLA_ROPE)
    kr = jnp.zeros((d, LANES), w_in.dtype).at[:, MLA_NOPE:MLA_NOPE + MLA_ROPE].set(w_in[:, q_lora + kv_lora:])
    w_in_p = jnp.concatenate([w_in[:, :q_lora + kv_lora], kr], axis=1)
    wq = w_qb.reshape(q_lora, heads, MLA_NOPE + MLA_ROPE)
    wq = jnp.pad(wq, ((0, 0), (0, 0), (0, LANES - MLA_NOPE - MLA_ROPE))).reshape(q_lora, heads * LANES)
    wkv = w_kvb.reshape(kv_lora, heads, MLA_NOPE + MLA_V)
    wk = jnp.pad(wkv[:, :, :MLA_NOPE], ((0, 0), (0, 0), (0, LANES - MLA_NOPE))).reshape(kv_lora, heads * LANES)
    wv = jnp.pad(wkv[:, :, MLA_NOPE:], ((0, 0), (0, 0), (0, LANES - MLA_V))).reshape(kv_lora, heads * LANES)
    wo = w_out.reshape(heads, MLA_V, -1)
    wo = jnp.pad(wo, ((0, 0), (0, LANES - MLA_V), (0, 0))).reshape(heads * LANES, -1)
    return w_in_p.astype(BF16), wq.astype(BF16), wk.astype(BF16), wv.astype(BF16), wo.astype(BF16)


def kernel(x, c, ctx, c_ctx, w_ada, b_ada, norm_mix, norm_ffn, ffn_w_in, ffn_conv_w, ffn_conv_b, ffn_w_out,
           gqa_w_in, gqa_q_norm, gqa_k_norm, gqa_w_out, hgrn_w_in, hgrn_out_norm, hgrn_w_out, hgrn_lower_bounds,
           mla_w_in, mla_q_norm, mla_kv_norm, mla_w_qb, mla_w_kvb, mla_w_out, final_norm):
    batch, seq, d = x.shape
    depth = w_ada.shape[0]
    n_mixers = 3
    rows = seq // GRID_W
    assert batch + 1 <= SUBLANES

    cv = jnp.zeros((SUBLANES, d), F32).at[:batch].set(c).at[batch].set(c_ctx)
    mods = _ada_mods(cv, w_ada, b_ada).reshape(depth, SUBLANES, 6, 1, d)
    lb_all = _lower_bounds(hgrn_lower_bounds)

    cos_a, sin_a = _gqa_tables(rows)
    cos_m, slo_m, shi_m = _mla_tables(rows)

    for i in range(depth):
        last = i == depth - 1
        kind = i % n_mixers
        j = i // n_mixers
        mod = mods[i, :batch]
        mod_c = mods[i, batch:batch + 1]
        ng = norm_mix[i][None, :]

        if kind == 0:
            w_in = gqa_w_in[j].astype(BF16)
            w_out = gqa_w_out[j].astype(BF16)
            qg, kg = gqa_q_norm[j][None, :], gqa_k_norm[j][None, :]
            q, k, v = _gqa_proj(x, mod, ng, w_in, qg, kg, cos_a, sin_a, True)
            n_ctx = ctx.shape[1]
            qc, kc, vc = _gqa_proj(ctx, mod_c, ng, w_in, qg, kg, cos_a[:n_ctx], sin_a[:n_ctx], False)
            o = _attention(q, [(k, v), (kc, vc)], GQA_GROUP, 256)
            x = _merge(x, o, w_out, mod)
            if not last:
                oc = _attention(qc, [(kc, vc)], GQA_GROUP, 256)
                ctx = _merge(ctx, oc, w_out, mod_c)
        elif kind == 1:
            w_in = hgrn_w_in[j].astype(BF16)
            w_out = hgrn_w_out[j].astype(BF16)
            lb = lb_all[i][None, :]
            og = hgrn_out_norm[j][None, :]
            heads = w_out.shape[0] // HGRN_DK
            qc, vc, kfc, gfc, kbc, gbc, gatec = _hgrn_proj(ctx, mod_c, ng, w_in, lb)
            q, v, kf, gf, kb, gb, gate = _hgrn_proj(x, mod, ng, w_in, lb)
            s0 = jnp.zeros((batch, heads, HGRN_DK, HGRN_DK), F32)
            oc_f, oc_b, s_f, s_b = _hgrn_scan(qc, vc, kfc, gfc, kbc, gbc, s0, s0)
            o_f, o_b, _, _ = _hgrn_scan(q, v, kf, gf, kb, gb, s_f, s_b)
            x = _hgrn_merge(x, o_f, o_b, gate, og, w_out, mod)
            if not last:
                ctx = _hgrn_merge(ctx, oc_f, oc_b, gatec, og, w_out, mod_c)
        else:
            w_in, wq, wk, wv, w_out = _mla_weights(mla_w_in[j], mla_w_qb[j], mla_w_kvb[j], mla_w_out[j])
            qg, kvg = mla_q_norm[j][None, :], mla_kv_norm[j][None, :]
            n_ctx = ctx.shape[1]
            q, k, v = _mla_proj(x, mod, ng, w_in, qg, kvg, wq, wk, wv, cos_m, slo_m, shi_m, True)
            qc, kc, vc = _mla_proj(ctx, mod_c, ng, w_in, qg, kvg, wq, wk, wv,
                                   cos_m[:n_ctx], slo_m[:n_ctx], shi_m[:n_ctx], False)
            o = _attention(q, [(k, v), (kc, vc)], 1, 512)
            x = _merge(x, o, w_out, mod)
            if not last:
                oc = _attention(qc, [(kc, vc)], 1, 512)
                ctx = _merge(ctx, oc, w_out, mod_c)

        fg = norm_ffn[i][None, :]
        f_in = ffn_w_in[i].astype(BF16)
        f_out = ffn_w_out[i].astype(BF16)
        f_cw = ffn_conv_w[i]
        f_cb = ffn_conv_b[i][None, :]
        x = _conv_ffn(x, mod, fg, f_in, f_cw, f_cb, f_out)
        if not last:
            ctx = _conv_ffn(ctx, mod_c, fg, f_in, f_cw, f_cb, f_out)

    return _final_norm(x, final_norm[None, :])
```

```python
import functools

import numpy as np
import jax
import jax.numpy as jnp
from jax import lax
from jax.experimental import pallas as pl
from jax.experimental.pallas import tpu as pltpu

F32 = jnp.float32
BF16 = jnp.bfloat16
HIGHEST = lax.Precision.HIGHEST

GRID_W = 64
ROPE_THETA = 10000.0
NORM_EPS = 1e-6
CONV_W = 3

LANES = 128
SUBLANES = 8

GQA_HEAD_DIM = 128
GQA_GROUP = 2

HGRN_DK = 128
HGRN_CHUNK = 64
HGRN_PARTS = 5

MLA_NOPE = 64
MLA_ROPE = 32
MLA_V = 64

VMEM_LIMIT = 56 * 1024 * 1024

LOG2_E = 1.4426950408889634
ATTN_UNROLL = 4


def _params(*sem):
    return pltpu.CompilerParams(dimension_semantics=sem, vmem_limit_bytes=VMEM_LIMIT)


def _resident(shape):
    nd = len(shape)
    return pl.BlockSpec(shape, lambda *_: (0,) * nd, pipeline_mode=pl.Buffered(1))


def _silu(x):
    return x / (1.0 + jnp.exp(-x))


def _sigmoid(x):
    return 1.0 / (1.0 + jnp.exp(-x))


def _rms(x, gain):
    return x * lax.rsqrt(jnp.mean(x * x, axis=-1, keepdims=True) + NORM_EPS) * gain


def _modnorm(x, gain, shift, scale):
    return _rms(x, gain) * (1.0 + scale) + shift


def _aligned_ds(start, size):
    if isinstance(start, int):
        return pl.ds(start, size)
    return pl.ds(pl.multiple_of(start, size), size)


def _row_tile(n, want):
    t = min(n, want)
    assert n % t == 0, (n, t)
    return t


def _ada_kernel(cv_ref, w_ref, b_ref, o_ref):
    s = _silu(cv_ref[...])
    o_ref[0] = jnp.dot(s, w_ref[0], precision=HIGHEST, preferred_element_type=F32) + b_ref[0]


def _ada_mods(cv, w_ada, b_ada):
    depth, d, n = w_ada.shape
    tn = 1536
    return pl.pallas_call(
        _ada_kernel,
        grid=(depth, n // tn),
        in_specs=[pl.BlockSpec((SUBLANES, d), lambda i, j: (0, 0)),
                  pl.BlockSpec((1, d, tn), lambda i, j: (i, 0, j)),
                  pl.BlockSpec((1, 1, tn), lambda i, j: (i, 0, j))],
        out_specs=pl.BlockSpec((1, SUBLANES, tn), lambda i, j: (i, 0, j)),
        out_shape=jax.ShapeDtypeStruct((depth, SUBLANES, n), F32),
        compiler_params=_params("arbitrary", "arbitrary"),
        name="ada_mods",
    )(cv, w_ada, b_ada.reshape(depth, 1, n))


def _lb_kernel(x_ref, o_ref):
    depth = x_ref.shape[0]
    rows = [x_ref[i:i + 1, :] for i in range(depth)]
    m = rows[0]
    for r in rows[1:]:
        m = jnp.maximum(m, r)
    e = [jnp.exp(r - m) for r in rows]
    tot = e[0]
    for r in e[1:]:
        tot = tot + r
    p = [r / tot for r in e]
    cum = p[0]
    o_ref[0:1, :] = cum - p[0]
    for i in range(1, depth):
        cum = cum + p[i]
        o_ref[i:i + 1, :] = cum - p[0]


def _lower_bounds(lb_raw):
    return pl.pallas_call(
        _lb_kernel,
        out_shape=jax.ShapeDtypeStruct(lb_raw.shape, F32),
        name="hgrn_lower_bounds",
    )(lb_raw.astype(F32))


def _gqa_proj_kernel(x_ref, mod_ref, ng_ref, w_ref, qg_ref, kg_ref, cos_ref, sin_ref,
                     q_ref, k_ref, v_ref, *, rotate, qd, kd, chunk):
    h = _modnorm(x_ref[0], ng_ref[...], mod_ref[0, 0], mod_ref[0, 1]).astype(BF16)
    qg = qg_ref[...]
    kg = kg_ref[...]
    scale = GQA_HEAD_DIM ** -0.5 * LOG2_E
    if rotate:
        cos = cos_ref[...]
        sin = sin_ref[...]

    def head(p, gain):
        y = _rms(p, gain)
        if rotate:
            y = y * cos + pltpu.roll(y, GQA_HEAD_DIM // 2, 1) * sin
        return y

    n = w_ref.shape[1]
    for j in range(n // chunk):
        p = jnp.dot(h, w_ref[:, j * chunk:(j + 1) * chunk], preferred_element_type=F32)
        for u in range(chunk // LANES):
            col = j * chunk + u * LANES
            ph = p[:, u * LANES:(u + 1) * LANES]
            if col < qd:
                q_ref[0, :, col:col + LANES] = (head(ph, qg) * scale).astype(BF16)
            elif col < qd + kd:
                k_ref[0, :, col - qd:col - qd + LANES] = head(ph, kg).astype(BF16)
            else:
                c0 = col - qd - kd
                v_ref[0, 0, c0:c0 + LANES, :] = ph.T.astype(BF16)


def _gqa_proj(x, mod, ng, w_in, qg, kg, cos, sin, rotate):
    b, l, d = x.shape
    n = w_in.shape[1]
    kd = n // 4
    qd = n - 2 * kd
    tm = _row_tile(l, 512)
    kern = functools.partial(_gqa_proj_kernel, rotate=rotate, qd=qd, kd=kd, chunk=512)
    return pl.pallas_call(
        kern,
        grid=(b, l // tm),
        in_specs=[pl.BlockSpec((1, tm, d), lambda i, t: (i, t, 0)),
                  pl.BlockSpec((1, 6, 1, d), lambda i, t: (i % mod.shape[0], 0, 0, 0)),
                  _resident((1, d)),
                  _resident((d, n)),
                  _resident((1, LANES)),
                  _resident((1, LANES)),
                  pl.BlockSpec((tm, LANES), lambda i, t: (t, 0)),
                  pl.BlockSpec((tm, LANES), lambda i, t: (t, 0))],
        out_specs=[pl.BlockSpec((1, tm, qd), lambda i, t: (i, t, 0)),
                   pl.BlockSpec((1, tm, kd), lambda i, t: (i, t, 0)),
                   pl.BlockSpec((1, 1, kd, tm), lambda i, t: (i, t, 0, 0))],
        out_shape=[jax.ShapeDtypeStruct((b, l, qd), BF16),
                   jax.ShapeDtypeStruct((b, l, kd), BF16),
                   jax.ShapeDtypeStruct((b, l // tm, kd, tm), BF16)],
        compiler_params=_params("parallel", "parallel"),
        name="gqa_proj",
    )(x, mod, ng, w_in, qg, kg, cos, sin)


def _attn_kernel(*refs, group, n_src):
    q_ref = refs[0]
    kv_refs = refs[1:1 + 2 * n_src]
    o_ref = refs[1 + 2 * n_src]
    s_scs = refs[2 + 2 * n_src:4 + 2 * n_src]
    m_sc = refs[4 + 2 * n_src]
    tq = q_ref.shape[1]
    nq = group * tq
    step = pl.program_id(2)

    def fold(a):
        return a.reshape(a.shape[0] // SUBLANES, SUBLANES, nq)

    def run(do_scores, do_weigh, slot):
        prev = 1 - slot
        carry = {}
        if do_scores:
            q = q_ref[0]
            if group > 1:
                q = jnp.concatenate([q[:, g * LANES:(g + 1) * LANES] for g in range(group)], axis=0)
            carry["m8"] = jnp.full((SUBLANES, nq), -jnp.inf, F32)
        if do_weigh:
            m = jnp.max(m_sc[prev], axis=0, keepdims=True)
            carry["l8"] = jnp.zeros((SUBLANES, nq), F32)
            carry["acc"] = jnp.zeros((LANES, nq), F32)
        row0 = 0
        for i in range(n_src):
            k_ref, vt_ref = kv_refs[2 * i], kv_refs[2 * i + 1]
            n_chunks, _, c = vt_ref.shape[1:]

            def body(j, carry, k_ref=k_ref, vt_ref=vt_ref, c=c, row0=row0):
                carry = dict(carry)
                rows = _aligned_ds(row0 + j * c, c)
                if do_scores:
                    s = lax.dot_general(k_ref[0, _aligned_ds(j * c, c), :], q, (((1,), (1,)), ((), ())),
                                        preferred_element_type=F32)
                    s_scs[slot][rows, :] = s
                    carry["m8"] = jnp.maximum(carry["m8"], jnp.max(fold(s), axis=0))
                if do_weigh:
                    p = jnp.exp2(s_scs[prev][rows, :] - m)
                    carry["l8"] = carry["l8"] + jnp.sum(fold(p), axis=0)
                    carry["acc"] = carry["acc"] + jnp.dot(vt_ref[0, j], p.astype(BF16),
                                                          preferred_element_type=F32)
                return carry

            if n_chunks == 1:
                carry = body(0, carry)
            else:
                carry = lax.fori_loop(0, n_chunks, body, carry, unroll=ATTN_UNROLL)
            row0 += n_chunks * c
        if do_scores:
            m_sc[slot] = carry["m8"]
        if do_weigh:
            o = carry["acc"] / jnp.sum(carry["l8"], axis=0, keepdims=True)
            for g in range(group):
                o_ref[0, :, g * LANES:(g + 1) * LANES] = o[:, g * tq:(g + 1) * tq].T.astype(o_ref.dtype)

    n_tiles = pl.num_programs(2) - 1
    middle = (step > 0) & (step < n_tiles)

    @pl.when(step == 0)
    def _():
        run(True, False, 0)

    for parity in range(2):
        @pl.when(middle & (step % 2 == parity))
        def _(parity=parity):
            run(True, True, parity)

        @pl.when((step == n_tiles) & (n_tiles % 2 == parity))
        def _(parity=parity):
            run(False, True, parity)


def _attention(q, kv_sources, group, tq):
    b, nq, hd = q.shape
    hkv = hd // (group * LANES)
    tq = _row_tile(nq, tq)
    nt = nq // tq
    n_src = len(kv_sources)
    in_specs = [pl.BlockSpec((1, tq, group * LANES), lambda i, h, t: (i, jnp.minimum(t, nt - 1), h))]
    args = [q]
    nk_total = 0
    for k, vt in kv_sources:
        nk = k.shape[1]
        n_chunks, _, c = vt.shape[1:]
        assert n_chunks * c == nk and nk_total % c == 0
        nk_total += nk
        in_specs.append(pl.BlockSpec((1, nk, LANES), lambda i, h, t: (i, 0, h)))
        in_specs.append(pl.BlockSpec((1, n_chunks, LANES, c), lambda i, h, t: (i, 0, h, 0)))
        args += [k, vt]
    return pl.pallas_call(
        functools.partial(_attn_kernel, group=group, n_src=n_src),
        grid=(b, hkv, nt + 1),
        in_specs=in_specs,
        out_specs=pl.BlockSpec((1, tq, group * LANES), lambda i, h, t: (i, jnp.maximum(t - 1, 0), h)),
        out_shape=jax.ShapeDtypeStruct((b, nq, hd), BF16),
        scratch_shapes=[pltpu.VMEM((nk_total, group * tq), F32),
                        pltpu.VMEM((nk_total, group * tq), F32),
                        pltpu.VMEM((2, SUBLANES, group * tq), F32)],
        compiler_params=_params("parallel", "parallel", "arbitrary"),
        name="flash_attention",
    )(*args)


def _merge_kernel(x_ref, o_ref, w_ref, mod_ref, y_ref):
    y = jnp.dot(o_ref[0], w_ref[...], preferred_element_type=F32)
    y_ref[0] = x_ref[0] + mod_ref[0, 2] * y


def _merge(x, o, w_out, mod):
    b, l, d = x.shape
    ko = o.shape[2]
    tm = _row_tile(l, 512)
    return pl.pallas_call(
        _merge_kernel,
        grid=(b, l // tm),
        in_specs=[pl.BlockSpec((1, tm, d), lambda i, t: (i, t, 0)),
                  pl.BlockSpec((1, tm, ko), lambda i, t: (i, t, 0)),
                  _resident((ko, d)),
                  pl.BlockSpec((1, 6, 1, d), lambda i, t: (i % mod.shape[0], 0, 0, 0))],
        out_specs=pl.BlockSpec((1, tm, d), lambda i, t: (i, t, 0)),
        out_shape=jax.ShapeDtypeStruct((b, l, d), F32),
        compiler_params=_params("parallel", "parallel"),
        name="merge_residual",
    )(x, o, w_out, mod)


def _ffn_kernel(xp_ref, x_ref, xn_ref, mod_ref, ng_ref, win_ref, cw_ref, cb_ref, wout_ref, y_ref,
                *, d_ff, chunk):
    t = pl.program_id(1)
    nt = pl.num_programs(1)
    gain = ng_ref[...]
    shift, scale, gate = mod_ref[0, 3], mod_ref[0, 4], mod_ref[0, 5]
    x = x_ref[0]
    tm = x.shape[0]
    halo = xp_ref.shape[1]
    hp = jnp.where(t > 0, _modnorm(xp_ref[0], gain, shift, scale), 0.0)
    hn = jnp.where(t < nt - 1, _modnorm(xn_ref[0], gain, shift, scale), 0.0)
    h = jnp.concatenate([hp, _modnorm(x, gain, shift, scale), hn], axis=0).astype(BF16)
    rows = tm + 2 * halo

    def conv(p, col):
        w = cw_ref[:, col:col + chunk]
        prev = pltpu.roll(p, 1, 0)[halo:halo + tm]
        nxt = pltpu.roll(p, rows - 1, 0)[halo:halo + tm]
        cur = p[halo:halo + tm]
        return ((cb_ref[:, col:col + chunk] + prev * w[0:1]) + cur * w[1:2]) + nxt * w[2:3]

    acc = jnp.zeros((tm, x.shape[1]), F32)
    for c in range(d_ff // chunk):
        pa = jnp.dot(h, win_ref[:, c * chunk:(c + 1) * chunk], preferred_element_type=F32)
        pv = jnp.dot(h, win_ref[:, d_ff + c * chunk:d_ff + (c + 1) * chunk], preferred_element_type=F32)
        g = _silu(conv(pa, c * chunk)) * conv(pv, d_ff + c * chunk)
        acc = acc + jnp.dot(g.astype(BF16), wout_ref[c * chunk:(c + 1) * chunk, :],
                            preferred_element_type=F32)
    y_ref[0] = x + gate * acc


def _conv_ffn(x, mod, ng, w_in, conv_w, conv_b, w_out):
    b, l, d = x.shape
    d_ff = w_out.shape[0]
    tm = _row_tile(l, 512)
    halo = SUBLANES
    per = tm // halo
    last = l // halo - 1
    kern = functools.partial(_ffn_kernel, d_ff=d_ff, chunk=512)
    return pl.pallas_call(
        kern,
        grid=(b, l // tm),
        in_specs=[pl.BlockSpec((1, halo, d), lambda i, t: (i, jnp.maximum(t * per - 1, 0), 0)),
                  pl.BlockSpec((1, tm, d), lambda i, t: (i, t, 0)),
                  pl.BlockSpec((1, halo, d), lambda i, t: (i, jnp.minimum((t + 1) * per, last), 0)),
                  pl.BlockSpec((1, 6, 1, d), lambda i, t: (i % mod.shape[0], 0, 0, 0)),
                  _resident((1, d)),
                  _resident((d, 2 * d_ff)),
                  _resident((CONV_W, 2 * d_ff)),
                  _resident((1, 2 * d_ff)),
                  _resident((d_ff, d))],
        out_specs=pl.BlockSpec((1, tm, d), lambda i, t: (i, t, 0)),
        out_shape=jax.ShapeDtypeStruct((b, l, d), F32),
        compiler_params=_params("parallel", "arbitrary"),
        name="conv_ffn",
    )(x, x, x, mod, ng, w_in, conv_w, conv_b, w_out)


def _hgrn_proj_kernel(x_ref, mod_ref, ng_ref, w_ref, lb_ref, q_ref, v_ref, kf_ref, gf_ref, kb_ref, gb_ref,
                      gate_ref, *, chunk):
    h = _modnorm(x_ref[0], ng_ref[...], mod_ref[0, 0], mod_ref[0, 1]).astype(BF16)
    width = q_ref.shape[2]
    scale = HGRN_DK ** -0.5
    for part in range(HGRN_PARTS):
        for j in range(width // chunk):
            c0 = j * chunk
            p = jnp.dot(h, w_ref[:, part * width + c0:part * width + c0 + chunk], preferred_element_type=F32)
            if part == 0:
                q_ref[0, :, c0:c0 + chunk] = p * scale
            elif part == 1:
                v_ref[0, :, c0:c0 + chunk] = p
            elif part == 4:
                gate_ref[0, :, c0:c0 + chunk] = p
            else:
                lb = lb_ref[:, c0:c0 + chunk]
                f = lb + (1.0 - lb) * _sigmoid(p)
                k_out, g_out = (kf_ref, gf_ref) if part == 2 else (kb_ref, gb_ref)
                k_out[0, :, c0:c0 + chunk] = 1.0 - f
                g_out[0, :, c0:c0 + chunk] = jnp.log(f)


def _hgrn_proj(x, mod, ng, w_in, lb):
    b, l, d = x.shape
    width = w_in.shape[1] // HGRN_PARTS
    tm = _row_tile(l, 512)
    out = jax.ShapeDtypeStruct((b, l, width), F32)
    ospec = pl.BlockSpec((1, tm, width), lambda i, t: (i, t, 0))
    return pl.pallas_call(
        functools.partial(_hgrn_proj_kernel, chunk=512),
        grid=(b, l // tm),
        in_specs=[pl.BlockSpec((1, tm, d), lambda i, t: (i, t, 0)),
                  pl.BlockSpec((1, 6, 1, d), lambda i, t: (i % mod.shape[0], 0, 0, 0)),
                  _resident((1, d)),
                  _resident((d, HGRN_PARTS * width)),
                  _resident((1, width))],
        out_specs=[ospec] * 7,
        out_shape=[out] * 7,
        compiler_params=_params("parallel", "parallel"),
        name="hgrn_proj",
    )(x, mod, ng, w_in, lb)


_HGRN_LEVELS = (32, 16, 8)


def _hgrn_constants(reverse):
    c = HGRN_CHUNK
    idx = np.arange(c)
    if reverse:
        tri = (idx[None, :] >= idx[:, None]).astype(np.float32)
    else:
        tri = (idx[None, :] <= idx[:, None]).astype(np.float32)
    mats = [tri]
    for half in _HGRN_LEVELS:
        blk = idx // (2 * half)
        boundary = blk * 2 * half + (half if reverse else half - 1)
        mats.append(tri[boundary])
    mats.append(np.ones((c, c), np.float32))
    return jnp.asarray(np.concatenate(mats, axis=0))


def _hgrn_chunk(q, k, v, g, cmat, st, reverse, ones_bf):
    c = HGRN_CHUNK
    a = jnp.dot(cmat, g, precision=HIGHEST, preferred_element_type=F32)
    cum = a[0:c]
    tot = a[(len(_HGRN_LEVELS) + 1) * c:(len(_HGRN_LEVELS) + 2) * c]
    row = lax.broadcasted_iota(jnp.int32, (c, c), 0)
    col = lax.broadcasted_iota(jnp.int32, (c, c), 1)
    early, late = (col, row) if not reverse else (row, col)
    scores = jnp.zeros((c, c), F32)
    for li, half in enumerate(_HGRN_LEVELS):
        ref = a[(li + 1) * c:(li + 2) * c]
        qs = q * jnp.exp(jnp.minimum(cum - ref, 0.0))
        ks = k * jnp.exp(jnp.minimum(ref - cum, 0.0))
        s_l = lax.dot_general(qs.astype(BF16), ks.astype(BF16), (((1,), (1,)), ((), ())),
                              preferred_element_type=F32)
        same = (row // (2 * half)) == (col // (2 * half))
        early_half = (early % (2 * half)) < half
        late_half = (late % (2 * half)) >= half
        scores = jnp.where(same & early_half & late_half, s_l, scores)
    sub = _HGRN_LEVELS[-1]
    prods = [q * k]
    for delta in range(1, sub):
        shift = delta if not reverse else c - delta
        k_sh = pltpu.roll(k, shift, 0)
        cum_sh = pltpu.roll(cum, shift, 0)
        prods.append(q * k_sh * jnp.exp(jnp.minimum(cum - cum_sh, 0.0)))
    diag = jnp.dot(jnp.concatenate(prods, axis=0).astype(BF16), ones_bf, preferred_element_type=F32)
    for delta in range(sub):
        d_t = diag[delta * c:(delta + 1) * c, 0:c]
        if not reverse:
            hit = (col == row - delta) & ((row % sub) >= delta)
        else:
            hit = (col == row + delta) & ((row % sub) + delta < sub)
        scores = jnp.where(hit, d_t, scores)
    qe = (q * jnp.exp(cum)).astype(BF16)
    out = jnp.dot(scores.astype(BF16), v.astype(BF16), preferred_element_type=F32)
    out = out + lax.dot_general(qe, st.astype(BF16), (((1,), (1,)), ((), ())), preferred_element_type=F32)
    kd = (k * jnp.exp(tot - cum)).astype(BF16)
    upd = jnp.dot(v.T.astype(BF16), kd, preferred_element_type=F32)
    st_new = st * jnp.exp(tot[0:1]) + upd
    return out, st_new


def _hgrn_scan_kernel(qf_ref, vf_ref, kf_ref, gf_ref, qb_ref, vb_ref, kb_ref, gb_ref, cf_ref, cb_ref,
                      sf0_ref, sb0_ref, of_ref, ob_ref, sf_ref, sb_ref, st_sc):
    step = pl.program_id(2)
    rb = qf_ref.shape[1]
    c = HGRN_CHUNK

    @pl.when(step == 0)
    def _():
        st_sc[0] = sf0_ref[0, 0]
        st_sc[1] = sb0_ref[0, 0]

    ones_bf = jnp.ones((LANES, LANES), BF16)
    cf = cf_ref[...]
    cb = cb_ref[...]
    st_f = st_sc[0]
    st_b = st_sc[1]
    for j in range(rb // c):
        lo = j * c
        o, st_f = _hgrn_chunk(qf_ref[0, lo:lo + c, :], kf_ref[0, lo:lo + c, :], vf_ref[0, lo:lo + c, :],
                              gf_ref[0, lo:lo + c, :], cf, st_f, False, ones_bf)
        of_ref[0, lo:lo + c, :] = o
        lo = rb - (j + 1) * c
        o, st_b = _hgrn_chunk(qb_ref[0, lo:lo + c, :], kb_ref[0, lo:lo + c, :], vb_ref[0, lo:lo + c, :],
                              gb_ref[0, lo:lo + c, :], cb, st_b, True, ones_bf)
        ob_ref[0, lo:lo + c, :] = o
    st_sc[0] = st_f
    st_sc[1] = st_b

    @pl.when(step == pl.num_programs(2) - 1)
    def _():
        sf_ref[0, 0] = st_f
        sb_ref[0, 0] = st_b


def _hgrn_scan(q, v, kf, gf, kb, gb, sf0, sb0):
    b, l, width = q.shape
    heads = width // LANES
    rb = _row_tile(l, 256)
    nc = l // rb
    fwd = pl.BlockSpec((1, rb, LANES), lambda i, h, s: (i, s, h))
    bwd = pl.BlockSpec((1, rb, LANES), lambda i, h, s: (i, nc - 1 - s, h))
    st_spec = pl.BlockSpec((1, 1, LANES, LANES), lambda i, h, s: (i, h, 0, 0))
    n_rows = (len(_HGRN_LEVELS) + 2) * HGRN_CHUNK
    o_shape = jax.ShapeDtypeStruct((b, l, width), F32)
    s_shape = jax.ShapeDtypeStruct((b, heads, LANES, LANES), F32)
    return pl.pallas_call(
        _hgrn_scan_kernel,
        grid=(b, heads, nc),
        in_specs=[fwd, fwd, fwd, fwd, bwd, bwd, bwd, bwd,
                  _resident((n_rows, HGRN_CHUNK)), _resident((n_rows, HGRN_CHUNK)), st_spec, st_spec],
        out_specs=[fwd, bwd, st_spec, st_spec],
        out_shape=[o_shape, o_shape, s_shape, s_shape],
        scratch_shapes=[pltpu.VMEM((2, LANES, LANES), F32)],
        compiler_params=_params("parallel", "parallel", "arbitrary"),
        name="hgrn_scan",
    )(q, v, kf, gf, q, v, kb, gb, _hgrn_constants(False), _hgrn_constants(True), sf0, sb0)


def _hgrn_merge_kernel(x_ref, of_ref, ob_ref, gate_ref, og_ref, w_ref, mod_ref, y_ref):
    width = of_ref.shape[2]
    og = og_ref[...]
    parts = []
    for hd in range(width // LANES):
        sl = slice(hd * LANES, (hd + 1) * LANES)
        o = of_ref[0, :, sl] + ob_ref[0, :, sl]
        parts.append((_rms(o, og) * _silu(gate_ref[0, :, sl])).astype(BF16))
    r = jnp.concatenate(parts, axis=1)
    y = jnp.dot(r, w_ref[...], preferred_element_type=F32)
    y_ref[0] = x_ref[0] + mod_ref[0, 2] * y


def _hgrn_merge(x, o_f, o_b, gate, o_gain, w_out, mod):
    b, l, d = x.shape
    width = o_f.shape[2]
    tm = _row_tile(l, 512)
    wide = pl.BlockSpec((1, tm, width), lambda i, t: (i, t, 0))
    return pl.pallas_call(
        _hgrn_merge_kernel,
        grid=(b, l // tm),
        in_specs=[pl.BlockSpec((1, tm, d), lambda i, t: (i, t, 0)), wide, wide, wide,
                  _resident((1, LANES)),
                  _resident((width, d)),
                  pl.BlockSpec((1, 6, 1, d), lambda i, t: (i % mod.shape[0], 0, 0, 0))],
        out_specs=pl.BlockSpec((1, tm, d), lambda i, t: (i, t, 0)),
        out_shape=jax.ShapeDtypeStruct((b, l, d), F32),
        compiler_params=_params("parallel", "parallel"),
        name="hgrn_merge",
    )(x, o_f, o_b, gate, o_gain, w_out, mod)


def _mla_rope(y, cos, sin_lo, sin_hi):
    half = MLA_ROPE // 2
    return y * cos + pltpu.roll(y, LANES - half, 1) * sin_lo + pltpu.roll(y, half, 1) * sin_hi


def _mla_proj_kernel(x_ref, mod_ref, ng_ref, w_ref, qg_ref, kvg_ref, wq_ref, wk_ref, wv_ref,
                     cos_ref, slo_ref, shi_ref, q_ref, k_ref, v_ref, *, rotate, q_lora, kv_lora, chunk):
    h = _modnorm(x_ref[0], ng_ref[...], mod_ref[0, 0], mod_ref[0, 1]).astype(BF16)
    p = jnp.dot(h, w_ref[...], preferred_element_type=F32)
    cq = _rms(p[:, :q_lora], qg_ref[...]).astype(BF16)
    ckv = _rms(p[:, q_lora:q_lora + kv_lora], kvg_ref[...]).astype(BF16)
    k_rope = p[:, q_lora + kv_lora:]
    scale = (MLA_NOPE + MLA_ROPE) ** -0.5 * LOG2_E
    if rotate:
        cos, slo, shi = cos_ref[...], slo_ref[...], shi_ref[...]
        k_rope = _mla_rope(k_rope, cos, slo, shi)
    n = wq_ref.shape[1]
    for j in range(n // chunk):
        cs = slice(j * chunk, (j + 1) * chunk)
        pq = jnp.dot(cq, wq_ref[:, cs], preferred_element_type=F32)
        pk = jnp.dot(ckv, wk_ref[:, cs], preferred_element_type=F32)
        pv = jnp.dot(ckv, wv_ref[:, cs], preferred_element_type=F32)
        for u in range(chunk // LANES):
            us = slice(u * LANES, (u + 1) * LANES)
            os = slice(j * chunk + u * LANES, j * chunk + (u + 1) * LANES)
            v_ref[0, 0, os, :] = pv[:, us].T.astype(BF16)
            qh = pq[:, us]
            if rotate:
                qh = _mla_rope(qh, cos, slo, shi)
            q_ref[0, :, os] = (qh * scale).astype(BF16)
            k_ref[0, :, os] = (pk[:, us] + k_rope).astype(BF16)


def _mla_proj(x, mod, ng, w_in, qg, kvg, wq, wk, wv, cos, slo, shi, rotate):
    b, l, d = x.shape
    n = wq.shape[1]
    q_lora, kv_lora = wq.shape[0], wk.shape[0]
    tm = _row_tile(l, 512)
    kern = functools.partial(_mla_proj_kernel, rotate=rotate, q_lora=q_lora, kv_lora=kv_lora, chunk=512)
    out = jax.ShapeDtypeStruct((b, l, n), BF16)
    ospec = pl.BlockSpec((1, tm, n), lambda i, t: (i, t, 0))
    tab = pl.BlockSpec((tm, LANES), lambda i, t: (t, 0))
    return pl.pallas_call(
        kern,
        grid=(b, l // tm),
        in_specs=[pl.BlockSpec((1, tm, d), lambda i, t: (i, t, 0)),
                  pl.BlockSpec((1, 6, 1, d), lambda i, t: (i % mod.shape[0], 0, 0, 0)),
                  _resident((1, d)),
                  _resident(w_in.shape),
                  _resident((1, q_lora)),
                  _resident((1, kv_lora)),
                  _resident(wq.shape), _resident(wk.shape), _resident(wv.shape),
                  tab, tab, tab],
        out_specs=[ospec, ospec, pl.BlockSpec((1, 1, n, tm), lambda i, t: (i, t, 0, 0))],
        out_shape=[out, out, jax.ShapeDtypeStruct((b, l // tm, n, tm), BF16)],
        compiler_params=_params("parallel", "parallel"),
        name="mla_proj",
    )(x, mod, ng, w_in, qg, kvg, wq, wk, wv, cos, slo, shi)


def _final_kernel(x_ref, g_ref, y_ref):
    y_ref[0] = _rms(x_ref[0], g_ref[...])


def _final_norm(x, gain):
    b, l, d = x.shape
    tm = _row_tile(l, 1024)
    return pl.pallas_call(
        _final_kernel,
        grid=(b, l // tm),
        in_specs=[pl.BlockSpec((1, tm, d), lambda i, t: (i, t, 0)), _resident((1, d))],
        out_specs=pl.BlockSpec((1, tm, d), lambda i, t: (i, t, 0)),
        out_shape=jax.ShapeDtypeStruct((b, l, d), F32),
        compiler_params=_params("parallel", "parallel"),
        name="final_norm",
    )(x, gain)


def _axial_angles(rows, rot_dim):
    row = jnp.repeat(jnp.arange(rows, dtype=F32), GRID_W)
    col = jnp.tile(jnp.arange(GRID_W, dtype=F32), rows)
    axis_dim = rot_dim // 2
    inv_freq = jnp.power(ROPE_THETA, -jnp.arange(0, axis_dim, 2, dtype=F32) / axis_dim)
    ang = jnp.concatenate([row[:, None] * inv_freq, col[:, None] * inv_freq], axis=-1)
    return jnp.cos(ang), jnp.sin(ang)


def _gqa_tables(rows):
    cos, sin = _axial_angles(rows, GQA_HEAD_DIM)
    return jnp.concatenate([cos, cos], axis=-1), jnp.concatenate([-sin, sin], axis=-1)


def _mla_tables(rows):
    cos, sin = _axial_angles(rows, MLA_ROPE)
    s = cos.shape[0]
    half = MLA_ROPE // 2
    ones = jnp.ones((s, MLA_NOPE), F32)
    zeros = jnp.zeros((s, MLA_NOPE), F32)
    tail1 = jnp.ones((s, LANES - MLA_NOPE - MLA_ROPE), F32)
    tail0 = jnp.zeros((s, LANES - MLA_NOPE - MLA_ROPE), F32)
    zh = jnp.zeros((s, half), F32)
    c = jnp.concatenate([ones, cos, cos, tail1], axis=-1)
    s_lo = jnp.concatenate([zeros, -sin, zh, tail0], axis=-1)
    s_hi = jnp.concatenate([zeros, zh, sin, tail0], axis=-1)
    return c, s_lo, s_hi


def _mla_weights(w_in, w_qb, w_kvb, w_out):
    d = w_in.shape[0]
    q_lora, kv_lora = w_qb.shape[0], w_kvb.shape[0]
    heads = w_qb.shape[1] // (MLA_NOPE + MLA_ROPE)
    kr = jnp.zeros((d, LANES), w_in.dtype).at[:, MLA_NOPE:MLA_NOPE + MLA_ROPE].set(w_in[:, q_lora + kv_lora:])
    w_in_p = jnp.concatenate([w_in[:, :q_lora + kv_lora], kr], axis=1)
    wq = w_qb.reshape(q_lora, heads, MLA_NOPE + MLA_ROPE)
    wq = jnp.pad(wq, ((0, 0), (0, 0), (0, LANES - MLA_NOPE - MLA_ROPE))).reshape(q_lora, heads * LANES)
    wkv = w_kvb.reshape(kv_lora, heads, MLA_NOPE + MLA_V)
    wk = jnp.pad(wkv[:, :, :MLA_NOPE], ((0, 0), (0, 0), (0, LANES - MLA_NOPE))).reshape(kv_lora, heads * LANES)
    wv = jnp.pad(wkv[:, :, MLA_NOPE:], ((0, 0), (0, 0), (0, LANES - MLA_V))).reshape(kv_lora, heads * LANES)
    wo = w_out.reshape(heads, MLA_V, -1)
    wo = jnp.pad(wo, ((0, 0), (0, LANES - MLA_V), (0, 0))).reshape(heads * LANES, -1)
    return w_in_p.astype(BF16), wq.astype(BF16), wk.astype(BF16), wv.astype(BF16), wo.astype(BF16)


def kernel(x, c, ctx, c_ctx, w_ada, b_ada, norm_mix, norm_ffn, ffn_w_in, ffn_conv_w, ffn_conv_b, ffn_w_out,
           gqa_w_in, gqa_q_norm, gqa_k_norm, gqa_w_out, hgrn_w_in, hgrn_out_norm, hgrn_w_out, hgrn_lower_bounds,
           mla_w_in, mla_q_norm, mla_kv_norm, mla_w_qb, mla_w_kvb, mla_w_out, final_norm):
    batch, seq, d = x.shape
    depth = w_ada.shape[0]
    n_mixers = 3
    rows = seq // GRID_W
    assert batch + 1 <= SUBLANES

    cv = jnp.zeros((SUBLANES, d), F32).at[:batch].set(c).at[batch].set(c_ctx)
    mods = _ada_mods(cv, w_ada, b_ada).reshape(depth, SUBLANES, 6, 1, d)
    lb_all = _lower_bounds(hgrn_lower_bounds)

    cos_a, sin_a = _gqa_tables(rows)
    cos_m, slo_m, shi_m = _mla_tables(rows)

    for i in range(depth):
        last = i == depth - 1
        kind = i % n_mixers
        j = i // n_mixers
        mod = mods[i, :batch]
        mod_c = mods[i, batch:batch + 1]
        ng = norm_mix[i][None, :]

        if kind == 0:
            w_in = gqa_w_in[j].astype(BF16)
            w_out = gqa_w_out[j].astype(BF16)
            qg, kg = gqa_q_norm[j][None, :], gqa_k_norm[j][None, :]
            q, k, v = _gqa_proj(x, mod, ng, w_in, qg, kg, cos_a, sin_a, True)
            n_ctx = ctx.shape[1]
            qc, kc, vc = _gqa_proj(ctx, mod_c, ng, w_in, qg, kg, cos_a[:n_ctx], sin_a[:n_ctx], False)
            o = _attention(q, [(k, v), (kc, vc)], GQA_GROUP, 256)
            x = _merge(x, o, w_out, mod)
            if not last:
                oc = _attention(qc, [(kc, vc)], GQA_GROUP, 256)
                ctx = _merge(ctx, oc, w_out, mod_c)
        elif kind == 1:
            w_in = hgrn_w_in[j].astype(BF16)
            w_out = hgrn_w_out[j].astype(BF16)
            lb = lb_all[i][None, :]
            og = hgrn_out_norm[j][None, :]
            heads = w_out.shape[0] // HGRN_DK
            qc, vc, kfc, gfc, kbc, gbc, gatec = _hgrn_proj(ctx, mod_c, ng, w_in, lb)
            q, v, kf, gf, kb, gb, gate = _hgrn_proj(x, mod, ng, w_in, lb)
            s0 = jnp.zeros((batch, heads, HGRN_DK, HGRN_DK), F32)
            oc_f, oc_b, s_f, s_b = _hgrn_scan(qc, vc, kfc, gfc, kbc, gbc, s0, s0)
            o_f, o_b, _, _ = _hgrn_scan(q, v, kf, gf, kb, gb, s_f, s_b)
            x = _hgrn_merge(x, o_f, o_b, gate, og, w_out, mod)
            if not last:
                ctx = _hgrn_merge(ctx, oc_f, oc_b, gatec, og, w_out, mod_c)
        else:
            w_in, wq, wk, wv, w_out = _mla_weights(mla_w_in[j], mla_w_qb[j], mla_w_kvb[j], mla_w_out[j])
            qg, kvg = mla_q_norm[j][None, :], mla_kv_norm[j][None, :]
            n_ctx = ctx.shape[1]
            q, k, v = _mla_proj(x, mod, ng, w_in, qg, kvg, wq, wk, wv, cos_m, slo_m, shi_m, True)
            qc, kc, vc = _mla_proj(ctx, mod_c, ng, w_in, qg, kvg, wq, wk, wv,
                                   cos_m[:n_ctx], slo_m[:n_ctx], shi_m[:n_ctx], False)
            o = _attention(q, [(k, v), (kc, vc)], 1, 512)
            x = _merge(x, o, w_out, mod)
            if not last:
                oc = _attention(qc, [(kc, vc)], 1, 512)
                ctx = _merge(ctx, oc, w_out, mod_c)

        fg = norm_ffn[i][None, :]
        f_in = ffn_w_in[i].astype(BF16)
        f_out = ffn_w_out[i].astype(BF16)
        f_cw = ffn_conv_w[i]
        f_cb = ffn_conv_b[i][None, :]
        x = _conv_ffn(x, mod, fg, f_in, f_cw, f_cb, f_out)
        if not last:
            ctx = _conv_ffn(ctx, mod_c, fg, f_in, f_cw, f_cb, f_out)

    return _final_norm(x, final_norm[None, :])
```

```python
import functools

import numpy as np
import jax
import jax.numpy as jnp
from jax import lax
from jax.experimental import pallas as pl
from jax.experimental.pallas import tpu as pltpu

F32 = jnp.float32
BF16 = jnp.bfloat16
HIGHEST = lax.Precision.HIGHEST

GRID_W = 64
ROPE_THETA = 10000.0
NORM_EPS = 1e-6
CONV_W = 3

LANES = 128
SUBLANES = 8

GQA_HEAD_DIM = 128
GQA_GROUP = 2

HGRN_DK = 128
HGRN_CHUNK = 64
HGRN_PARTS = 5

MLA_NOPE = 64
MLA_ROPE = 32
MLA_V = 64

VMEM_LIMIT = 56 * 1024 * 1024

LOG2_E = 1.4426950408889634
ATTN_UNROLL = 4


def _params(*sem):
    return pltpu.CompilerParams(dimension_semantics=sem, vmem_limit_bytes=VMEM_LIMIT)


def _resident(shape):
    nd = len(shape)
    return pl.BlockSpec(shape, lambda *_: (0,) * nd, pipeline_mode=pl.Buffered(1))


def _silu(x):
    return x / (1.0 + jnp.exp(-x))


def _sigmoid(x):
    return 1.0 / (1.0 + jnp.exp(-x))


def _rms(x, gain):
    return x * lax.rsqrt(jnp.mean(x * x, axis=-1, keepdims=True) + NORM_EPS) * gain


def _modnorm(x, gain, shift, scale):
    return _rms(x, gain) * (1.0 + scale) + shift


def _aligned_ds(start, size):
    if isinstance(start, int):
        return pl.ds(start, size)
    return pl.ds(pl.multiple_of(start, size), size)


def _row_tile(n, want):
    t = min(n, want)
    while n % t:
        t -= SUBLANES
    assert t > 0 and t % SUBLANES == 0, (n, want)
    return t


def _ada_kernel(cv_ref, w_ref, b_ref, o_ref):
    s = _silu(cv_ref[...])
    o_ref[0] = jnp.dot(s, w_ref[0], precision=HIGHEST, preferred_element_type=F32) + b_ref[0]


def _ada_mods(cv, w_ada, b_ada):
    depth, d, n = w_ada.shape
    tn = 1536
    return pl.pallas_call(
        _ada_kernel,
        grid=(depth, n // tn),
        in_specs=[pl.BlockSpec((SUBLANES, d), lambda i, j: (0, 0)),
                  pl.BlockSpec((1, d, tn), lambda i, j: (i, 0, j)),
                  pl.BlockSpec((1, 1, tn), lambda i, j: (i, 0, j))],
        out_specs=pl.BlockSpec((1, SUBLANES, tn), lambda i, j: (i, 0, j)),
        out_shape=jax.ShapeDtypeStruct((depth, SUBLANES, n), F32),
        compiler_params=_params("arbitrary", "arbitrary"),
        name="ada_mods",
    )(cv, w_ada, b_ada.reshape(depth, 1, n))


def _lb_kernel(x_ref, o_ref):
    depth = x_ref.shape[0]
    rows = [x_ref[i:i + 1, :] for i in range(depth)]
    m = rows[0]
    for r in rows[1:]:
        m = jnp.maximum(m, r)
    e = [jnp.exp(r - m) for r in rows]
    tot = e[0]
    for r in e[1:]:
        tot = tot + r
    p = [r / tot for r in e]
    cum = p[0]
    o_ref[0:1, :] = cum - p[0]
    for i in range(1, depth):
        cum = cum + p[i]
        o_ref[i:i + 1, :] = cum - p[0]


def _lower_bounds(lb_raw):
    return pl.pallas_call(
        _lb_kernel,
        out_shape=jax.ShapeDtypeStruct(lb_raw.shape, F32),
        name="hgrn_lower_bounds",
    )(lb_raw.astype(F32))


def _gqa_proj_kernel(x_ref, mod_ref, ng_ref, w_ref, qg_ref, kg_ref, cos_ref, sin_ref,
                     q_ref, k_ref, v_ref, *, rotate, qd, kd, chunk):
    h = _modnorm(x_ref[0], ng_ref[...], mod_ref[0, 0], mod_ref[0, 1]).astype(BF16)
    qg = qg_ref[...]
    kg = kg_ref[...]
    scale = GQA_HEAD_DIM ** -0.5 * LOG2_E
    if rotate:
        cos = cos_ref[...]
        sin = sin_ref[...]

    def head(p, gain):
        y = _rms(p, gain)
        if rotate:
            y = y * cos + pltpu.roll(y, GQA_HEAD_DIM // 2, 1) * sin
        return y

    n = w_ref.shape[1]
    for j in range(n // chunk):
        p = jnp.dot(h, w_ref[:, j * chunk:(j + 1) * chunk], preferred_element_type=F32)
        for u in range(chunk // LANES):
            col = j * chunk + u * LANES
            ph = p[:, u * LANES:(u + 1) * LANES]
            if col < qd:
                q_ref[0, :, col:col + LANES] = (head(ph, qg) * scale).astype(BF16)
            elif col < qd + kd:
                k_ref[0, :, col - qd:col - qd + LANES] = head(ph, kg).astype(BF16)
            else:
                c0 = col - qd - kd
                v_ref[0, 0, c0:c0 + LANES, :] = ph.T.astype(BF16)


def _gqa_proj(x, mod, ng, w_in, qg, kg, cos, sin, rotate):
    b, l, d = x.shape
    n = w_in.shape[1]
    kd = n // 4
    qd = n - 2 * kd
    tm = _row_tile(l, 512)
    kern = functools.partial(_gqa_proj_kernel, rotate=rotate, qd=qd, kd=kd, chunk=512)
    return pl.pallas_call(
        kern,
        grid=(b, l // tm),
        in_specs=[pl.BlockSpec((1, tm, d), lambda i, t: (i, t, 0)),
                  pl.BlockSpec((1, 6, 1, d), lambda i, t: (i % mod.shape[0], 0, 0, 0)),
                  _resident((1, d)),
                  _resident((d, n)),
                  _resident((1, LANES)),
                  _resident((1, LANES)),
                  pl.BlockSpec((tm, LANES), lambda i, t: (t, 0)),
                  pl.BlockSpec((tm, LANES), lambda i, t: (t, 0))],
        out_specs=[pl.BlockSpec((1, tm, qd), lambda i, t: (i, t, 0)),
                   pl.BlockSpec((1, tm, kd), lambda i, t: (i, t, 0)),
                   pl.BlockSpec((1, 1, kd, tm), lambda i, t: (i, t, 0, 0))],
        out_shape=[jax.ShapeDtypeStruct((b, l, qd), BF16),
                   jax.ShapeDtypeStruct((b, l, kd), BF16),
                   jax.ShapeDtypeStruct((b, l // tm, kd, tm), BF16)],
        compiler_params=_params("parallel", "parallel"),
        name="gqa_proj",
    )(x, mod, ng, w_in, qg, kg, cos, sin)


def _attn_kernel(*refs, group, n_src):
    q_ref = refs[0]
    kv_refs = refs[1:1 + 2 * n_src]
    o_ref = refs[1 + 2 * n_src]
    s_scs = refs[2 + 2 * n_src:4 + 2 * n_src]
    m_sc = refs[4 + 2 * n_src]
    tq = q_ref.shape[1]
    nq = group * tq
    step = pl.program_id(2)

    def fold(a):
        return a.reshape(a.shape[0] // SUBLANES, SUBLANES, nq)

    def run(do_scores, do_weigh, slot):
        prev = 1 - slot
        carry = {}
        if do_scores:
            q = q_ref[0]
            if group > 1:
                q = jnp.concatenate([q[:, g * LANES:(g + 1) * LANES] for g in range(group)], axis=0)
            carry["m8"] = jnp.full((SUBLANES, nq), -jnp.inf, F32)
        if do_weigh:
            m = jnp.max(m_sc[prev], axis=0, keepdims=True)
            carry["l8"] = jnp.zeros((SUBLANES, nq), F32)
            carry["acc"] = jnp.zeros((LANES, nq), F32)
        row0 = 0
        for i in range(n_src):
            k_ref, vt_ref = kv_refs[2 * i], kv_refs[2 * i + 1]
            n_chunks, _, c = vt_ref.shape[1:]

            def body(j, carry, k_ref=k_ref, vt_ref=vt_ref, c=c, row0=row0):
                carry = dict(carry)
                rows = _aligned_ds(row0 + j * c, c)
                if do_scores:
                    s = lax.dot_general(k_ref[0, _aligned_ds(j * c, c), :], q, (((1,), (1,)), ((), ())),
                                        preferred_element_type=F32)
                    s_scs[slot][rows, :] = s
                    carry["m8"] = jnp.maximum(carry["m8"], jnp.max(fold(s), axis=0))
                if do_weigh:
                    p = jnp.exp2(s_scs[prev][rows, :] - m)
                    carry["l8"] = carry["l8"] + jnp.sum(fold(p), axis=0)
                    carry["acc"] = carry["acc"] + jnp.dot(vt_ref[0, j], p.astype(BF16),
                                                          preferred_element_type=F32)
                return carry

            if n_chunks == 1:
                carry = body(0, carry)
            else:
                carry = lax.fori_loop(0, n_chunks, body, carry, unroll=ATTN_UNROLL)
            row0 += n_chunks * c
        if do_scores:
            m_sc[slot] = carry["m8"]
        if do_weigh:
            o = carry["acc"] / jnp.sum(carry["l8"], axis=0, keepdims=True)
            for g in range(group):
                o_ref[0, :, g * LANES:(g + 1) * LANES] = o[:, g * tq:(g + 1) * tq].T.astype(o_ref.dtype)

    n_tiles = pl.num_programs(2) - 1
    middle = (step > 0) & (step < n_tiles)

    @pl.when(step == 0)
    def _():
        run(True, False, 0)

    for parity in range(2):
        @pl.when(middle & (step % 2 == parity))
        def _(parity=parity):
            run(True, True, parity)

        @pl.when((step == n_tiles) & (n_tiles % 2 == parity))
        def _(parity=parity):
            run(False, True, parity)


def _attention(q, kv_sources, group, tq):
    b, nq, hd = q.shape
    hkv = hd // (group * LANES)
    tq = _row_tile(nq, tq)
    nt = nq // tq
    n_src = len(kv_sources)
    in_specs = [pl.BlockSpec((1, tq, group * LANES), lambda i, h, t: (i, jnp.minimum(t, nt - 1), h))]
    args = [q]
    nk_total = 0
    for k, vt in kv_sources:
        nk = k.shape[1]
        n_chunks, _, c = vt.shape[1:]
        assert n_chunks * c == nk and nk_total % c == 0
        nk_total += nk
        in_specs.append(pl.BlockSpec((1, nk, LANES), lambda i, h, t: (i, 0, h)))
        in_specs.append(pl.BlockSpec((1, n_chunks, LANES, c), lambda i, h, t: (i, 0, h, 0)))
        args += [k, vt]
    return pl.pallas_call(
        functools.partial(_attn_kernel, group=group, n_src=n_src),
        grid=(b, hkv, nt + 1),
        in_specs=in_specs,
        out_specs=pl.BlockSpec((1, tq, group * LANES), lambda i, h, t: (i, jnp.maximum(t - 1, 0), h)),
        out_shape=jax.ShapeDtypeStruct((b, nq, hd), BF16),
        scratch_shapes=[pltpu.VMEM((nk_total, group * tq), F32),
                        pltpu.VMEM((nk_total, group * tq), F32),
                        pltpu.VMEM((2, SUBLANES, group * tq), F32)],
        compiler_params=_params("parallel", "parallel", "arbitrary"),
        name="flash_attention",
    )(*args)


def _merge_kernel(x_ref, o_ref, w_ref, mod_ref, y_ref):
    y = jnp.dot(o_ref[0], w_ref[...], preferred_element_type=F32)
    y_ref[0] = x_ref[0] + mod_ref[0, 2] * y


def _merge(x, o, w_out, mod):
    b, l, d = x.shape
    ko = o.shape[2]
    tm = _row_tile(l, 512)
    return pl.pallas_call(
        _merge_kernel,
        grid=(b, l // tm),
        in_specs=[pl.BlockSpec((1, tm, d), lambda i, t: (i, t, 0)),
                  pl.BlockSpec((1, tm, ko), lambda i, t: (i, t, 0)),
                  _resident((ko, d)),
                  pl.BlockSpec((1, 6, 1, d), lambda i, t: (i % mod.shape[0], 0, 0, 0))],
        out_specs=pl.BlockSpec((1, tm, d), lambda i, t: (i, t, 0)),
        out_shape=jax.ShapeDtypeStruct((b, l, d), F32),
        compiler_params=_params("parallel", "parallel"),
        name="merge_residual",
    )(x, o, w_out, mod)


def _ffn_kernel(xp_ref, x_ref, xn_ref, mod_ref, ng_ref, win_ref, cw_ref, cb_ref, wout_ref, y_ref,
                *, d_ff, chunk):
    t = pl.program_id(1)
    nt = pl.num_programs(1)
    gain = ng_ref[...]
    shift, scale, gate = mod_ref[0, 3], mod_ref[0, 4], mod_ref[0, 5]
    x = x_ref[0]
    tm = x.shape[0]
    halo = xp_ref.shape[1]
    hp = jnp.where(t > 0, _modnorm(xp_ref[0], gain, shift, scale), 0.0)
    hn = jnp.where(t < nt - 1, _modnorm(xn_ref[0], gain, shift, scale), 0.0)
    h = jnp.concatenate([hp, _modnorm(x, gain, shift, scale), hn], axis=0).astype(BF16)
    rows = tm + 2 * halo

    def conv(p, col):
        w = cw_ref[:, col:col + chunk]
        prev = pltpu.roll(p, 1, 0)[halo:halo + tm]
        nxt = pltpu.roll(p, rows - 1, 0)[halo:halo + tm]
        cur = p[halo:halo + tm]
        return ((cb_ref[:, col:col + chunk] + prev * w[0:1]) + cur * w[1:2]) + nxt * w[2:3]

    acc = jnp.zeros((tm, x.shape[1]), F32)
    for c in range(d_ff // chunk):
        pa = jnp.dot(h, win_ref[:, c * chunk:(c + 1) * chunk], preferred_element_type=F32)
        pv = jnp.dot(h, win_ref[:, d_ff + c * chunk:d_ff + (c + 1) * chunk], preferred_element_type=F32)
        g = _silu(conv(pa, c * chunk)) * conv(pv, d_ff + c * chunk)
        acc = acc + jnp.dot(g.astype(BF16), wout_ref[c * chunk:(c + 1) * chunk, :],
                            preferred_element_type=F32)
    y_ref[0] = x + gate * acc


def _conv_ffn(x, mod, ng, w_in, conv_w, conv_b, w_out):
    b, l, d = x.shape
    d_ff = w_out.shape[0]
    tm = _row_tile(l, 512)
    halo = SUBLANES
    per = tm // halo
    last = l // halo - 1
    kern = functools.partial(_ffn_kernel, d_ff=d_ff, chunk=512)
    return pl.pallas_call(
        kern,
        grid=(b, l // tm),
        in_specs=[pl.BlockSpec((1, halo, d), lambda i, t: (i, jnp.maximum(t * per - 1, 0), 0)),
                  pl.BlockSpec((1, tm, d), lambda i, t: (i, t, 0)),
                  pl.BlockSpec((1, halo, d), lambda i, t: (i, jnp.minimum((t + 1) * per, last), 0)),
                  pl.BlockSpec((1, 6, 1, d), lambda i, t: (i % mod.shape[0], 0, 0, 0)),
                  _resident((1, d)),
                  _resident((d, 2 * d_ff)),
                  _resident((CONV_W, 2 * d_ff)),
                  _resident((1, 2 * d_ff)),
                  _resident((d_ff, d))],
        out_specs=pl.BlockSpec((1, tm, d), lambda i, t: (i, t, 0)),
        out_shape=jax.ShapeDtypeStruct((b, l, d), F32),
        compiler_params=_params("parallel", "arbitrary"),
        name="conv_ffn",
    )(x, x, x, mod, ng, w_in, conv_w, conv_b, w_out)


def _hgrn_proj_kernel(x_ref, mod_ref, ng_ref, w_ref, lb_ref, q_ref, v_ref, kf_ref, gf_ref, kb_ref, gb_ref,
                      gate_ref, *, chunk):
    h = _modnorm(x_ref[0], ng_ref[...], mod_ref[0, 0], mod_ref[0, 1]).astype(BF16)
    width = q_ref.shape[2]
    scale = HGRN_DK ** -0.5
    for part in range(HGRN_PARTS):
        for j in range(width // chunk):
            c0 = j * chunk
            p = jnp.dot(h, w_ref[:, part * width + c0:part * width + c0 + chunk], preferred_element_type=F32)
            if part == 0:
                q_ref[0, :, c0:c0 + chunk] = p * scale
            elif part == 1:
                v_ref[0, :, c0:c0 + chunk] = p
            elif part == 4:
                gate_ref[0, :, c0:c0 + chunk] = p
            else:
                lb = lb_ref[:, c0:c0 + chunk]
                f = lb + (1.0 - lb) * _sigmoid(p)
                k_out, g_out = (kf_ref, gf_ref) if part == 2 else (kb_ref, gb_ref)
                k_out[0, :, c0:c0 + chunk] = 1.0 - f
                g_out[0, :, c0:c0 + chunk] = jnp.log(f)


def _hgrn_proj(x, mod, ng, w_in, lb):
    b, l, d = x.shape
    width = w_in.shape[1] // HGRN_PARTS
    tm = _row_tile(l, 512)
    out = jax.ShapeDtypeStruct((b, l, width), F32)
    ospec = pl.BlockSpec((1, tm, width), lambda i, t: (i, t, 0))
    return pl.pallas_call(
        functools.partial(_hgrn_proj_kernel, chunk=512),
        grid=(b, l // tm),
        in_specs=[pl.BlockSpec((1, tm, d), lambda i, t: (i, t, 0)),
                  pl.BlockSpec((1, 6, 1, d), lambda i, t: (i % mod.shape[0], 0, 0, 0)),
                  _resident((1, d)),
                  _resident((d, HGRN_PARTS * width)),
                  _resident((1, width))],
        out_specs=[ospec] * 7,
        out_shape=[out] * 7,
        compiler_params=_params("parallel", "parallel"),
        name="hgrn_proj",
    )(x, mod, ng, w_in, lb)


_HGRN_LEVELS = (32, 16, 8, 4)


def _hgrn_tri(reverse):
    idx = np.arange(HGRN_CHUNK)
    tri = idx[None, :] >= idx[:, None] if reverse else idx[None, :] <= idx[:, None]
    return jnp.asarray(tri.astype(np.float32), dtype=BF16)


def _hgrn_block(q, k, v, g, tri, st, reverse):
    c = HGRN_CHUNK
    n = q.shape[0] // c

    g = g * LOG2_E
    g_hi = g.astype(BF16)
    rest = g - g_hi.astype(F32)
    g_mid = rest.astype(BF16)
    g_lo = (rest - g_mid.astype(F32)).astype(BF16)
    pieces = [piece[i * c:(i + 1) * c] for i in range(n) for piece in (g_hi, g_mid, g_lo)]
    sums = jnp.dot(tri, jnp.concatenate(pieces, axis=1), preferred_element_type=F32)
    cum = jnp.stack([(sums[:, (3 * i) * LANES:(3 * i + 1) * LANES]
                      + sums[:, (3 * i + 1) * LANES:(3 * i + 2) * LANES])
                     + sums[:, (3 * i + 2) * LANES:(3 * i + 3) * LANES] for i in range(n)], axis=0)

    q3, k3, v3 = (a.reshape(n, c, LANES) for a in (q, k, v))
    tot = cum[:, 0:1, :] if reverse else cum[:, c - 1:c, :]
    row = lax.broadcasted_iota(jnp.int32, (c, c), 0)
    col = lax.broadcasted_iota(jnp.int32, (c, c), 1)
    early, late = (row, col) if reverse else (col, row)

    scores = jnp.zeros((n, c, c), F32)
    for half in _HGRN_LEVELS:
        blk = 2 * half
        cb = cum.reshape(n * (c // blk), blk, LANES)
        b_row = half if reverse else half - 1
        d = cb - cb[:, b_row:b_row + 1, :]
        e = jnp.exp2(jnp.minimum(d, -d)).reshape(n, c, LANES)
        s_l = lax.dot_general((q3 * e).astype(BF16), (k3 * e).astype(BF16), (((2,), (2,)), ((0,), (0,))),
                              preferred_element_type=F32)
        pick = ((row // blk) == (col // blk)) & ((early % blk) < half) & ((late % blk) >= half)
        scores = jnp.where(pick[None], s_l, scores)

    sub = _HGRN_LEVELS[-1]
    cum2 = cum.reshape(n * c, LANES)
    prods = [q * k]
    for delta in range(1, sub):
        shift = n * c - delta if reverse else delta
        decay = jnp.exp2(jnp.minimum(cum2 - pltpu.roll(cum2, shift, 0), 0.0))
        prods.append(q * pltpu.roll(k, shift, 0) * decay)
    ones_bf = jnp.ones((LANES, LANES), BF16)
    diag = jnp.dot(jnp.concatenate(prods, axis=0).astype(BF16), ones_bf, preferred_element_type=F32)
    for delta in range(sub):
        d_t = diag[delta * n * c:(delta + 1) * n * c, 0:c].reshape(n, c, c)
        if reverse:
            hit = (col == row + delta) & ((row % sub) + delta < sub)
        else:
            hit = (col == row - delta) & ((row % sub) >= delta)
        scores = jnp.where(hit[None], d_t, scores)

    out = lax.dot_general(scores.astype(BF16), v3.astype(BF16), (((2,), (1,)), ((0,), (0,))),
                          preferred_element_type=F32)
    qe = (q3 * jnp.exp2(cum)).astype(BF16)
    kd = (k3 * jnp.exp2(tot - cum)).astype(BF16)
    e_tot = jnp.exp2(tot)
    v_t = jnp.stack([v3[i].T for i in range(n)], axis=0).astype(BF16)
    upd = lax.dot_general(v_t, kd, (((2,), (1,)), ((0,), (0,))), preferred_element_type=F32)
    states = [None] * n
    for i in (range(n - 1, -1, -1) if reverse else range(n)):
        states[i] = st
        st = st * e_tot[i] + upd[i]
    out = out + lax.dot_general(qe, jnp.stack(states, axis=0).astype(BF16), (((2,), (2,)), ((0,), (0,))),
                                preferred_element_type=F32)
    return out.reshape(n * c, LANES), st


def _hgrn_scan_kernel(qf_ref, vf_ref, kf_ref, gf_ref, qb_ref, vb_ref, kb_ref, gb_ref, tf_ref, tb_ref,
                      sf0_ref, sb0_ref, of_ref, ob_ref, sf_ref, sb_ref, st_sc):
    step = pl.program_id(2)

    @pl.when(step == 0)
    def _():
        st_sc[0] = sf0_ref[0, 0]
        st_sc[1] = sb0_ref[0, 0]

    o, st_f = _hgrn_block(qf_ref[0], kf_ref[0], vf_ref[0], gf_ref[0], tf_ref[...], st_sc[0], False)
    of_ref[0] = o
    o, st_b = _hgrn_block(qb_ref[0], kb_ref[0], vb_ref[0], gb_ref[0], tb_ref[...], st_sc[1], True)
    ob_ref[0] = o
    st_sc[0] = st_f
    st_sc[1] = st_b

    @pl.when(step == pl.num_programs(2) - 1)
    def _():
        sf_ref[0, 0] = st_f
        sb_ref[0, 0] = st_b


def _hgrn_scan(q, v, kf, gf, kb, gb, sf0, sb0):
    b, l, width = q.shape
    heads = width // LANES
    rb = _row_tile(l, 256)
    nc = l // rb
    fwd = pl.BlockSpec((1, rb, LANES), lambda i, h, s: (i, s, h))
    bwd = pl.BlockSpec((1, rb, LANES), lambda i, h, s: (i, nc - 1 - s, h))
    st_spec = pl.BlockSpec((1, 1, LANES, LANES), lambda i, h, s: (i, h, 0, 0))
    o_shape = jax.ShapeDtypeStruct((b, l, width), F32)
    s_shape = jax.ShapeDtypeStruct((b, heads, LANES, LANES), F32)
    return pl.pallas_call(
        _hgrn_scan_kernel,
        grid=(b, heads, nc),
        in_specs=[fwd, fwd, fwd, fwd, bwd, bwd, bwd, bwd,
                  _resident((HGRN_CHUNK, HGRN_CHUNK)), _resident((HGRN_CHUNK, HGRN_CHUNK)), st_spec, st_spec],
        out_specs=[fwd, bwd, st_spec, st_spec],
        out_shape=[o_shape, o_shape, s_shape, s_shape],
        scratch_shapes=[pltpu.VMEM((2, LANES, LANES), F32)],
        compiler_params=_params("parallel", "parallel", "arbitrary"),
        name="hgrn_scan",
    )(q, v, kf, gf, q, v, kb, gb, _hgrn_tri(False), _hgrn_tri(True), sf0, sb0)


def _hgrn_merge_kernel(x_ref, of_ref, ob_ref, gate_ref, og_ref, w_ref, mod_ref, y_ref):
    width = of_ref.shape[2]
    og = og_ref[...]
    parts = []
    for hd in range(width // LANES):
        sl = slice(hd * LANES, (hd + 1) * LANES)
        o = of_ref[0, :, sl] + ob_ref[0, :, sl]
        parts.append((_rms(o, og) * _silu(gate_ref[0, :, sl])).astype(BF16))
    r = jnp.concatenate(parts, axis=1)
    y = jnp.dot(r, w_ref[...], preferred_element_type=F32)
    y_ref[0] = x_ref[0] + mod_ref[0, 2] * y


def _hgrn_merge(x, o_f, o_b, gate, o_gain, w_out, mod):
    b, l, d = x.shape
    width = o_f.shape[2]
    tm = _row_tile(l, 512)
    wide = pl.BlockSpec((1, tm, width), lambda i, t: (i, t, 0))
    return pl.pallas_call(
        _hgrn_merge_kernel,
        grid=(b, l // tm),
        in_specs=[pl.BlockSpec((1, tm, d), lambda i, t: (i, t, 0)), wide, wide, wide,
                  _resident((1, LANES)),
                  _resident((width, d)),
                  pl.BlockSpec((1, 6, 1, d), lambda i, t: (i % mod.shape[0], 0, 0, 0))],
        out_specs=pl.BlockSpec((1, tm, d), lambda i, t: (i, t, 0)),
        out_shape=jax.ShapeDtypeStruct((b, l, d), F32),
        compiler_params=_params("parallel", "parallel"),
        name="hgrn_merge",
    )(x, o_f, o_b, gate, o_gain, w_out, mod)


def _mla_rope(y, cos, sin_lo, sin_hi):
    half = MLA_ROPE // 2
    return y * cos + pltpu.roll(y, LANES - half, 1) * sin_lo + pltpu.roll(y, half, 1) * sin_hi


def _mla_proj_kernel(x_ref, mod_ref, ng_ref, w_ref, qg_ref, kvg_ref, wq_ref, wk_ref, wv_ref,
                     cos_ref, slo_ref, shi_ref, q_ref, k_ref, v_ref, *, rotate, q_lora, kv_lora, chunk):
    h = _modnorm(x_ref[0], ng_ref[...], mod_ref[0, 0], mod_ref[0, 1]).astype(BF16)
    p = jnp.dot(h, w_ref[...], preferred_element_type=F32)
    cq = _rms(p[:, :q_lora], qg_ref[...]).astype(BF16)
    ckv = _rms(p[:, q_lora:q_lora + kv_lora], kvg_ref[...]).astype(BF16)
    k_rope = p[:, q_lora + kv_lora:]
    scale = (MLA_NOPE + MLA_ROPE) ** -0.5 * LOG2_E
    if rotate:
        cos, slo, shi = cos_ref[...], slo_ref[...], shi_ref[...]
        k_rope = _mla_rope(k_rope, cos, slo, shi)
    n = wq_ref.shape[1]
    for j in range(n // chunk):
        cs = slice(j * chunk, (j + 1) * chunk)
        pq = jnp.dot(cq, wq_ref[:, cs], preferred_element_type=F32)
        pk = jnp.dot(ckv, wk_ref[:, cs], preferred_element_type=F32)
        pv = jnp.dot(ckv, wv_ref[:, cs], preferred_element_type=F32)
        for u in range(chunk // LANES):
            us = slice(u * LANES, (u + 1) * LANES)
            os = slice(j * chunk + u * LANES, j * chunk + (u + 1) * LANES)
            v_ref[0, 0, os, :] = pv[:, us].T.astype(BF16)
            qh = pq[:, us]
            if rotate:
                qh = _mla_rope(qh, cos, slo, shi)
            q_ref[0, :, os] = (qh * scale).astype(BF16)
            k_ref[0, :, os] = (pk[:, us] + k_rope).astype(BF16)


def _mla_proj(x, mod, ng, w_in, qg, kvg, wq, wk, wv, cos, slo, shi, rotate):
    b, l, d = x.shape
    n = wq.shape[1]
    q_lora, kv_lora = wq.shape[0], wk.shape[0]
    tm = _row_tile(l, 512)
    kern = functools.partial(_mla_proj_kernel, rotate=rotate, q_lora=q_lora, kv_lora=kv_lora, chunk=512)
    out = jax.ShapeDtypeStruct((b, l, n), BF16)
    ospec = pl.BlockSpec((1, tm, n), lambda i, t: (i, t, 0))
    tab = pl.BlockSpec((tm, LANES), lambda i, t: (t, 0))
    return pl.pallas_call(
        kern,
        grid=(b, l // tm),
        in_specs=[pl.BlockSpec((1, tm, d), lambda i, t: (i, t, 0)),
                  pl.BlockSpec((1, 6, 1, d), lambda i, t: (i % mod.shape[0], 0, 0, 0)),
                  _resident((1, d)),
                  _resident(w_in.shape),
                  _resident((1, q_lora)),
                  _resident((1, kv_lora)),
                  _resident(wq.shape), _resident(wk.shape), _resident(wv.shape),
                  tab, tab, tab],
        out_specs=[ospec, ospec, pl.BlockSpec((1, 1, n, tm), lambda i, t: (i, t, 0, 0))],
        out_shape=[out, out, jax.ShapeDtypeStruct((b, l // tm, n, tm), BF16)],
        compiler_params=_params("parallel", "parallel"),
        name="mla_proj",
    )(x, mod, ng, w_in, qg, kvg, wq, wk, wv, cos, slo, shi)


def _final_kernel(x_ref, g_ref, y_ref):
    y_ref[0] = _rms(x_ref[0], g_ref[...])


def _final_norm(x, gain):
    b, l, d = x.shape
    tm = _row_tile(l, 1024)
    return pl.pallas_call(
        _final_kernel,
        grid=(b, l // tm),
        in_specs=[pl.BlockSpec((1, tm, d), lambda i, t: (i, t, 0)), _resident((1, d))],
        out_specs=pl.BlockSpec((1, tm, d), lambda i, t: (i, t, 0)),
        out_shape=jax.ShapeDtypeStruct((b, l, d), F32),
        compiler_params=_params("parallel", "parallel"),
        name="final_norm",
    )(x, gain)


def _axial_angles(rows, rot_dim):
    row = jnp.repeat(jnp.arange(rows, dtype=F32), GRID_W)
    col = jnp.tile(jnp.arange(GRID_W, dtype=F32), rows)
    axis_dim = rot_dim // 2
    inv_freq = jnp.power(ROPE_THETA, -jnp.arange(0, axis_dim, 2, dtype=F32) / axis_dim)
    ang = jnp.concatenate([row[:, None] * inv_freq, col[:, None] * inv_freq], axis=-1)
    return jnp.cos(ang), jnp.sin(ang)


def _gqa_tables(rows):
    cos, sin = _axial_angles(rows, GQA_HEAD_DIM)
    return jnp.concatenate([cos, cos], axis=-1), jnp.concatenate([-sin, sin], axis=-1)


def _mla_tables(rows):
    cos, sin = _axial_angles(rows, MLA_ROPE)
    s = cos.shape[0]
    half = MLA_ROPE // 2
    ones = jnp.ones((s, MLA_NOPE), F32)
    zeros = jnp.zeros((s, MLA_NOPE), F32)
    tail1 = jnp.ones((s, LANES - MLA_NOPE - MLA_ROPE), F32)
    tail0 = jnp.zeros((s, LANES - MLA_NOPE - MLA_ROPE), F32)
    zh = jnp.zeros((s, half), F32)
    c = jnp.concatenate([ones, cos, cos, tail1], axis=-1)
    s_lo = jnp.concatenate([zeros, -sin, zh, tail0], axis=-1)
    s_hi = jnp.concatenate([zeros, zh, sin, tail0], axis=-1)
    return c, s_lo, s_hi


def _mla_weights(w_in, w_qb, w_kvb, w_out):
    d = w_in.shape[0]
    q_lora, kv_lora = w_qb.shape[0], w_kvb.shape[0]
    heads = w_qb.shape[1] // (MLA_NOPE + MLA_ROPE)
    kr = jnp.zeros((d, LANES), w_in.dtype).at[:, MLA_NOPE:MLA_NOPE + MLA_ROPE].set(w_in[:, q_lora + kv_lora:])
    w_in_p = jnp.concatenate([w_in[:, :q_lora + kv_lora], kr], axis=1)
    wq = w_qb.reshape(q_lora, heads, MLA_NOPE + MLA_ROPE)
    wq = jnp.pad(wq, ((0, 0), (0, 0), (0, LANES - MLA_NOPE - MLA_ROPE))).reshape(q_lora, heads * LANES)
    wkv = w_kvb.reshape(kv_lora, heads, MLA_NOPE + MLA_V)
    wk = jnp.pad(wkv[:, :, :MLA_NOPE], ((0, 0), (0, 0), (0, LANES - MLA_NOPE))).reshape(kv_lora, heads * LANES)
    wv = jnp.pad(wkv[:, :, MLA_NOPE:], ((0, 0), (0, 0), (0, LANES - MLA_V))).reshape(kv_lora, heads * LANES)
    wo = w_out.reshape(heads, MLA_V, -1)
    wo = jnp.pad(wo, ((0, 0), (0, LANES - MLA_V), (0, 0))).reshape(heads * LANES, -1)
    return w_in_p.astype(BF16), wq.astype(BF16), wk.astype(BF16), wv.astype(BF16), wo.astype(BF16)


def kernel(x, c, ctx, c_ctx, w_ada, b_ada, norm_mix, norm_ffn, ffn_w_in, ffn_conv_w, ffn_conv_b, ffn_w_out,
           gqa_w_in, gqa_q_norm, gqa_k_norm, gqa_w_out, hgrn_w_in, hgrn_out_norm, hgrn_w_out, hgrn_lower_bounds,
           mla_w_in, mla_q_norm, mla_kv_norm, mla_w_qb, mla_w_kvb, mla_w_out, final_norm):
    batch, seq, d = x.shape
    depth = w_ada.shape[0]
    n_mixers = 3
    rows = seq // GRID_W
    assert batch + 1 <= SUBLANES

    cv = jnp.zeros((SUBLANES, d), F32).at[:batch].set(c).at[batch].set(c_ctx)
    mods = _ada_mods(cv, w_ada, b_ada).reshape(depth, SUBLANES, 6, 1, d)
    lb_all = _lower_bounds(hgrn_lower_bounds)

    cos_a, sin_a = _gqa_tables(rows)
    cos_m, slo_m, shi_m = _mla_tables(rows)

    for i in range(depth):
        last = i == depth - 1
        kind = i % n_mixers
        j = i // n_mixers
        mod = mods[i, :batch]
        mod_c = mods[i, batch:batch + 1]
        ng = norm_mix[i][None, :]

        if kind == 0:
            w_in = gqa_w_in[j].astype(BF16)
            w_out = gqa_w_out[j].astype(BF16)
            qg, kg = gqa_q_norm[j][None, :], gqa_k_norm[j][None, :]
            q, k, v = _gqa_proj(x, mod, ng, w_in, qg, kg, cos_a, sin_a, True)
            n_ctx = ctx.shape[1]
            qc, kc, vc = _gqa_proj(ctx, mod_c, ng, w_in, qg, kg, cos_a[:n_ctx], sin_a[:n_ctx], False)
            o = _attention(q, [(k, v), (kc, vc)], GQA_GROUP, 256)
            x = _merge(x, o, w_out, mod)
            if not last:
                oc = _attention(qc, [(kc, vc)], GQA_GROUP, 256)
                ctx = _merge(ctx, oc, w_out, mod_c)
        elif kind == 1:
            w_in = hgrn_w_in[j].astype(BF16)
            w_out = hgrn_w_out[j].astype(BF16)
            lb = lb_all[i][None, :]
            og = hgrn_out_norm[j][None, :]
            heads = w_out.shape[0] // HGRN_DK
            qc, vc, kfc, gfc, kbc, gbc, gatec = _hgrn_proj(ctx, mod_c, ng, w_in, lb)
            q, v, kf, gf, kb, gb, gate = _hgrn_proj(x, mod, ng, w_in, lb)
            s0 = jnp.zeros((batch, heads, HGRN_DK, HGRN_DK), F32)
            oc_f, oc_b, s_f, s_b = _hgrn_scan(qc, vc, kfc, gfc, kbc, gbc, s0, s0)
            o_f, o_b, _, _ = _hgrn_scan(q, v, kf, gf, kb, gb, s_f, s_b)
            x = _hgrn_merge(x, o_f, o_b, gate, og, w_out, mod)
            if not last:
                ctx = _hgrn_merge(ctx, oc_f, oc_b, gatec, og, w_out, mod_c)
        else:
            w_in, wq, wk, wv, w_out = _mla_weights(mla_w_in[j], mla_w_qb[j], mla_w_kvb[j], mla_w_out[j])
            qg, kvg = mla_q_norm[j][None, :], mla_kv_norm[j][None, :]
            n_ctx = ctx.shape[1]
            q, k, v = _mla_proj(x, mod, ng, w_in, qg, kvg, wq, wk, wv, cos_m, slo_m, shi_m, True)
            qc, kc, vc = _mla_proj(ctx, mod_c, ng, w_in, qg, kvg, wq, wk, wv,
                                   cos_m[:n_ctx], slo_m[:n_ctx], shi_m[:n_ctx], False)
            o = _attention(q, [(k, v), (kc, vc)], 1, 512)
            x = _merge(x, o, w_out, mod)
            if not last:
                oc = _attention(qc, [(kc, vc)], 1, 512)
                ctx = _merge(ctx, oc, w_out, mod_c)

        fg = norm_ffn[i][None, :]
        f_in = ffn_w_in[i].astype(BF16)
        f_out = ffn_w_out[i].astype(BF16)
        f_cw = ffn_conv_w[i]
        f_cb = ffn_conv_b[i][None, :]
        x = _conv_ffn(x, mod, fg, f_in, f_cw, f_cb, f_out)
        if not last:
            ctx = _conv_ffn(ctx, mod_c, fg, f_in, f_cw, f_cb, f_out)

    return _final_norm(x, final_norm[None, :])
```

```python
import functools

import numpy as np
import jax
import jax.numpy as jnp
from jax import lax
from jax.experimental import pallas as pl
from jax.experimental.pallas import tpu as pltpu

F32 = jnp.float32
BF16 = jnp.bfloat16
HIGHEST = lax.Precision.HIGHEST

GRID_W = 64
ROPE_THETA = 10000.0
NORM_EPS = 1e-6
CONV_W = 3

LANES = 128
SUBLANES = 8

GQA_HEAD_DIM = 128
GQA_GROUP = 2

HGRN_DK = 128
HGRN_CHUNK = 64
HGRN_PARTS = 5

MLA_NOPE = 64
MLA_ROPE = 32
MLA_V = 64

VMEM_LIMIT = 56 * 1024 * 1024

LOG2_E = 1.4426950408889634
ATTN_UNROLL = 8


def _params(*sem):
    return pltpu.CompilerParams(dimension_semantics=sem, vmem_limit_bytes=VMEM_LIMIT)


def _resident(shape):
    nd = len(shape)
    return pl.BlockSpec(shape, lambda *_: (0,) * nd, pipeline_mode=pl.Buffered(1))


def _silu(x):
    return x / (1.0 + jnp.exp(-x))


def _sigmoid(x):
    return 1.0 / (1.0 + jnp.exp(-x))


def _neg_abs(x):
    bits = lax.bitcast_convert_type(x, jnp.uint32) | jnp.uint32(0x80000000)
    return lax.bitcast_convert_type(bits, F32)


def _rms(x, gain):
    return x * lax.rsqrt(jnp.mean(x * x, axis=-1, keepdims=True) + NORM_EPS) * gain


def _modnorm(x, gain, shift, scale):
    return _rms(x, gain) * (1.0 + scale) + shift


def _aligned_ds(start, size):
    if isinstance(start, int):
        return pl.ds(start, size)
    return pl.ds(pl.multiple_of(start, size), size)


def _row_tile(n, want):
    t = min(n, want)
    while n % t:
        t -= SUBLANES
    assert t > 0 and t % SUBLANES == 0, (n, want)
    return t


def _ada_kernel(cv_ref, w_ref, b_ref, o_ref):
    s = _silu(cv_ref[...])
    o_ref[0] = jnp.dot(s, w_ref[0], precision=HIGHEST, preferred_element_type=F32) + b_ref[0]


def _ada_mods(cv, w_ada, b_ada):
    depth, d, n = w_ada.shape
    tn = 1536
    return pl.pallas_call(
        _ada_kernel,
        grid=(depth, n // tn),
        in_specs=[pl.BlockSpec((SUBLANES, d), lambda i, j: (0, 0)),
                  pl.BlockSpec((1, d, tn), lambda i, j: (i, 0, j)),
                  pl.BlockSpec((1, 1, tn), lambda i, j: (i, 0, j))],
        out_specs=pl.BlockSpec((1, SUBLANES, tn), lambda i, j: (i, 0, j)),
        out_shape=jax.ShapeDtypeStruct((depth, SUBLANES, n), F32),
        compiler_params=_params("arbitrary", "arbitrary"),
        name="ada_mods",
    )(cv, w_ada, b_ada.reshape(depth, 1, n))


def _lb_kernel(x_ref, o_ref):
    depth = x_ref.shape[0]
    rows = [x_ref[i:i + 1, :] for i in range(depth)]
    m = rows[0]
    for r in rows[1:]:
        m = jnp.maximum(m, r)
    e = [jnp.exp(r - m) for r in rows]
    tot = e[0]
    for r in e[1:]:
        tot = tot + r
    p = [r / tot for r in e]
    cum = p[0]
    o_ref[0:1, :] = cum - p[0]
    for i in range(1, depth):
        cum = cum + p[i]
        o_ref[i:i + 1, :] = cum - p[0]


def _lower_bounds(lb_raw):
    return pl.pallas_call(
        _lb_kernel,
        out_shape=jax.ShapeDtypeStruct(lb_raw.shape, F32),
        name="hgrn_lower_bounds",
    )(lb_raw.astype(F32))


def _gqa_proj_kernel(x_ref, mod_ref, ng_ref, w_ref, qg_ref, kg_ref, cos_ref, sin_ref,
                     q_ref, k_ref, v_ref, *, rotate, qd, kd, chunk):
    h = _modnorm(x_ref[0], ng_ref[...], mod_ref[0, 0], mod_ref[0, 1]).astype(BF16)
    qg = qg_ref[...]
    kg = kg_ref[...]
    scale = GQA_HEAD_DIM ** -0.5 * LOG2_E
    if rotate:
        cos = cos_ref[...]
        sin = sin_ref[...]

    def head(p, gain):
        y = _rms(p, gain)
        if rotate:
            y = y * cos + pltpu.roll(y, GQA_HEAD_DIM // 2, 1) * sin
        return y

    n = w_ref.shape[1]
    for j in range(n // chunk):
        p = jnp.dot(h, w_ref[:, j * chunk:(j + 1) * chunk], preferred_element_type=F32)
        for u in range(chunk // LANES):
            col = j * chunk + u * LANES
            ph = p[:, u * LANES:(u + 1) * LANES]
            if col < qd:
                q_ref[0, :, col:col + LANES] = (head(ph, qg) * scale).astype(BF16)
            elif col < qd + kd:
                k_ref[0, :, col - qd:col - qd + LANES] = head(ph, kg).astype(BF16)
            else:
                c0 = col - qd - kd
                v_ref[0, 0, c0:c0 + LANES, :] = ph.T.astype(BF16)


def _gqa_proj(x, mod, ng, w_in, qg, kg, cos, sin, rotate):
    b, l, d = x.shape
    n = w_in.shape[1]
    kd = n // 4
    qd = n - 2 * kd
    tm = _row_tile(l, 512)
    kern = functools.partial(_gqa_proj_kernel, rotate=rotate, qd=qd, kd=kd, chunk=512)
    return pl.pallas_call(
        kern,
        grid=(b, l // tm),
        in_specs=[pl.BlockSpec((1, tm, d), lambda i, t: (i, t, 0)),
                  pl.BlockSpec((1, 6, 1, d), lambda i, t: (i % mod.shape[0], 0, 0, 0)),
                  _resident((1, d)),
                  _resident((d, n)),
                  _resident((1, LANES)),
                  _resident((1, LANES)),
                  pl.BlockSpec((tm, LANES), lambda i, t: (t, 0)),
                  pl.BlockSpec((tm, LANES), lambda i, t: (t, 0))],
        out_specs=[pl.BlockSpec((1, tm, qd), lambda i, t: (i, t, 0)),
                   pl.BlockSpec((1, tm, kd), lambda i, t: (i, t, 0)),
                   pl.BlockSpec((1, 1, kd, tm), lambda i, t: (i, t, 0, 0))],
        out_shape=[jax.ShapeDtypeStruct((b, l, qd), BF16),
                   jax.ShapeDtypeStruct((b, l, kd), BF16),
                   jax.ShapeDtypeStruct((b, l // tm, kd, tm), BF16)],
        compiler_params=_params("parallel", "parallel"),
        name="gqa_proj",
    )(x, mod, ng, w_in, qg, kg, cos, sin)


def _attn_kernel(*refs, group, n_src):
    q_ref = refs[0]
    kv_refs = refs[1:1 + 2 * n_src]
    o_ref = refs[1 + 2 * n_src]
    s_scs = refs[2 + 2 * n_src:4 + 2 * n_src]
    m_sc = refs[4 + 2 * n_src]
    tq = q_ref.shape[1]
    nq = group * tq
    step = pl.program_id(2)

    def fold(a):
        return a.reshape(a.shape[0] // SUBLANES, SUBLANES, nq)

    def run(do_scores, do_weigh, slot):
        prev = 1 - slot
        carry = {}
        if do_scores:
            q = q_ref[0]
            if group > 1:
                q = jnp.concatenate([q[:, g * LANES:(g + 1) * LANES] for g in range(group)], axis=0)
            carry["m8"] = jnp.full((SUBLANES, nq), -jnp.inf, F32)
        if do_weigh:
            m = jnp.max(m_sc[prev], axis=0, keepdims=True)
            carry["l8"] = jnp.zeros((SUBLANES, nq), F32)
            carry["acc"] = jnp.zeros((LANES, nq), F32)
        row0 = 0
        for i in range(n_src):
            k_ref, vt_ref = kv_refs[2 * i], kv_refs[2 * i + 1]
            n_chunks, _, c = vt_ref.shape[1:]

            def body(j, carry, k_ref=k_ref, vt_ref=vt_ref, c=c, row0=row0):
                carry = dict(carry)
                rows = _aligned_ds(row0 + j * c, c)
                if do_scores:
                    s = lax.dot_general(k_ref[0, _aligned_ds(j * c, c), :], q, (((1,), (1,)), ((), ())),
                                        preferred_element_type=F32)
                    s_scs[slot][rows, :] = s
                    carry["m8"] = jnp.maximum(carry["m8"], jnp.max(fold(s), axis=0))
                if do_weigh:
                    p = jnp.exp2(s_scs[prev][rows, :] - m)
                    carry["l8"] = carry["l8"] + jnp.sum(fold(p), axis=0)
                    carry["acc"] = carry["acc"] + jnp.dot(vt_ref[0, j], p.astype(BF16),
                                                          preferred_element_type=F32)
                return carry

            if n_chunks == 1:
                carry = body(0, carry)
            else:
                carry = lax.fori_loop(0, n_chunks, body, carry, unroll=ATTN_UNROLL)
            row0 += n_chunks * c
        if do_scores:
            m_sc[slot] = carry["m8"]
        if do_weigh:
            o = carry["acc"] / jnp.sum(carry["l8"], axis=0, keepdims=True)
            for g in range(group):
                o_ref[0, :, g * LANES:(g + 1) * LANES] = o[:, g * tq:(g + 1) * tq].T.astype(o_ref.dtype)

    n_tiles = pl.num_programs(2) - 1
    middle = (step > 0) & (step < n_tiles)

    @pl.when(step == 0)
    def _():
        run(True, False, 0)

    for parity in range(2):
        @pl.when(middle & (step % 2 == parity))
        def _(parity=parity):
            run(True, True, parity)

        @pl.when((step == n_tiles) & (n_tiles % 2 == parity))
        def _(parity=parity):
            run(False, True, parity)


def _attention(q, kv_sources, group, tq):
    b, nq, hd = q.shape
    hkv = hd // (group * LANES)
    tq = _row_tile(nq, tq)
    nt = nq // tq
    n_src = len(kv_sources)
    in_specs = [pl.BlockSpec((1, tq, group * LANES), lambda i, h, t: (i, jnp.minimum(t, nt - 1), h))]
    args = [q]
    nk_total = 0
    for k, vt in kv_sources:
        nk = k.shape[1]
        n_chunks, _, c = vt.shape[1:]
        assert n_chunks * c == nk and nk_total % c == 0
        nk_total += nk
        in_specs.append(pl.BlockSpec((1, nk, LANES), lambda i, h, t: (i, 0, h)))
        in_specs.append(pl.BlockSpec((1, n_chunks, LANES, c), lambda i, h, t: (i, 0, h, 0)))
        args += [k, vt]
    return pl.pallas_call(
        functools.partial(_attn_kernel, group=group, n_src=n_src),
        grid=(b, hkv, nt + 1),
        in_specs=in_specs,
        out_specs=pl.BlockSpec((1, tq, group * LANES), lambda i, h, t: (i, jnp.maximum(t - 1, 0), h)),
        out_shape=jax.ShapeDtypeStruct((b, nq, hd), BF16),
        scratch_shapes=[pltpu.VMEM((nk_total, group * tq), F32),
                        pltpu.VMEM((nk_total, group * tq), F32),
                        pltpu.VMEM((2, SUBLANES, group * tq), F32)],
        compiler_params=_params("parallel", "parallel", "arbitrary"),
        name="flash_attention",
    )(*args)


def _merge_kernel(x_ref, o_ref, w_ref, mod_ref, y_ref):
    y = jnp.dot(o_ref[0], w_ref[...], preferred_element_type=F32)
    y_ref[0] = x_ref[0] + mod_ref[0, 2] * y


def _merge(x, o, w_out, mod):
    b, l, d = x.shape
    ko = o.shape[2]
    tm = _row_tile(l, 512)
    return pl.pallas_call(
        _merge_kernel,
        grid=(b, l // tm),
        in_specs=[pl.BlockSpec((1, tm, d), lambda i, t: (i, t, 0)),
                  pl.BlockSpec((1, tm, ko), lambda i, t: (i, t, 0)),
                  _resident((ko, d)),
                  pl.BlockSpec((1, 6, 1, d), lambda i, t: (i % mod.shape[0], 0, 0, 0))],
        out_specs=pl.BlockSpec((1, tm, d), lambda i, t: (i, t, 0)),
        out_shape=jax.ShapeDtypeStruct((b, l, d), F32),
        compiler_params=_params("parallel", "parallel"),
        name="merge_residual",
    )(x, o, w_out, mod)


def _ffn_kernel(xp_ref, x_ref, xn_ref, mod_ref, ng_ref, win_ref, cw_ref, cb_ref, wout_ref, y_ref,
                *, d_ff, chunk):
    t = pl.program_id(1)
    nt = pl.num_programs(1)
    gain = ng_ref[...]
    shift, scale, gate = mod_ref[0, 3], mod_ref[0, 4], mod_ref[0, 5]
    x = x_ref[0]
    tm = x.shape[0]
    halo = xp_ref.shape[1]
    hp = jnp.where(t > 0, _modnorm(xp_ref[0], gain, shift, scale), 0.0)
    hn = jnp.where(t < nt - 1, _modnorm(xn_ref[0], gain, shift, scale), 0.0)
    h = jnp.concatenate([hp, _modnorm(x, gain, shift, scale), hn], axis=0).astype(BF16)
    rows = tm + 2 * halo

    def conv(p, col):
        w = cw_ref[:, col:col + chunk]
        prev = pltpu.roll(p, 1, 0)[halo:halo + tm]
        nxt = pltpu.roll(p, rows - 1, 0)[halo:halo + tm]
        cur = p[halo:halo + tm]
        return ((cb_ref[:, col:col + chunk] + prev * w[0:1]) + cur * w[1:2]) + nxt * w[2:3]

    acc = jnp.zeros((tm, x.shape[1]), F32)
    for c in range(d_ff // chunk):
        pa = jnp.dot(h, win_ref[:, c * chunk:(c + 1) * chunk], preferred_element_type=F32)
        pv = jnp.dot(h, win_ref[:, d_ff + c * chunk:d_ff + (c + 1) * chunk], preferred_element_type=F32)
        g = _silu(conv(pa, c * chunk)) * conv(pv, d_ff + c * chunk)
        acc = acc + jnp.dot(g.astype(BF16), wout_ref[c * chunk:(c + 1) * chunk, :],
                            preferred_element_type=F32)
    y_ref[0] = x + gate * acc


def _conv_ffn(x, mod, ng, w_in, conv_w, conv_b, w_out):
    b, l, d = x.shape
    d_ff = w_out.shape[0]
    tm = _row_tile(l, 512)
    halo = SUBLANES
    per = tm // halo
    last = l // halo - 1
    kern = functools.partial(_ffn_kernel, d_ff=d_ff, chunk=512)
    return pl.pallas_call(
        kern,
        grid=(b, l // tm),
        in_specs=[pl.BlockSpec((1, halo, d), lambda i, t: (i, jnp.maximum(t * per - 1, 0), 0)),
                  pl.BlockSpec((1, tm, d), lambda i, t: (i, t, 0)),
                  pl.BlockSpec((1, halo, d), lambda i, t: (i, jnp.minimum((t + 1) * per, last), 0)),
                  pl.BlockSpec((1, 6, 1, d), lambda i, t: (i % mod.shape[0], 0, 0, 0)),
                  _resident((1, d)),
                  _resident((d, 2 * d_ff)),
                  _resident((CONV_W, 2 * d_ff)),
                  _resident((1, 2 * d_ff)),
                  _resident((d_ff, d))],
        out_specs=pl.BlockSpec((1, tm, d), lambda i, t: (i, t, 0)),
        out_shape=jax.ShapeDtypeStruct((b, l, d), F32),
        compiler_params=_params("parallel", "arbitrary"),
        name="conv_ffn",
    )(x, x, x, mod, ng, w_in, conv_w, conv_b, w_out)


def _hgrn_proj_kernel(x_ref, mod_ref, ng_ref, w_ref, lb_ref, q_ref, v_ref, kf_ref, gf_ref, kb_ref, gb_ref,
                      gate_ref, *, chunk):
    h = _modnorm(x_ref[0], ng_ref[...], mod_ref[0, 0], mod_ref[0, 1]).astype(BF16)
    width = q_ref.shape[2]
    scale = HGRN_DK ** -0.5
    for part in range(HGRN_PARTS):
        for j in range(width // chunk):
            c0 = j * chunk
            p = jnp.dot(h, w_ref[:, part * width + c0:part * width + c0 + chunk], preferred_element_type=F32)
            if part == 0:
                q_ref[0, :, c0:c0 + chunk] = p * scale
            elif part == 1:
                v_ref[0, :, c0:c0 + chunk] = p
            elif part == 4:
                gate_ref[0, :, c0:c0 + chunk] = p
            else:
                lb = lb_ref[:, c0:c0 + chunk]
                f = lb + (1.0 - lb) * _sigmoid(p)
                k_out, g_out = (kf_ref, gf_ref) if part == 2 else (kb_ref, gb_ref)
                k_out[0, :, c0:c0 + chunk] = 1.0 - f
                g_out[0, :, c0:c0 + chunk] = jnp.log(f)


def _hgrn_proj(x, mod, ng, w_in, lb):
    b, l, d = x.shape
    width = w_in.shape[1] // HGRN_PARTS
    tm = _row_tile(l, 512)
    out = jax.ShapeDtypeStruct((b, l, width), F32)
    ospec = pl.BlockSpec((1, tm, width), lambda i, t: (i, t, 0))
    return pl.pallas_call(
        functools.partial(_hgrn_proj_kernel, chunk=512),
        grid=(b, l // tm),
        in_specs=[pl.BlockSpec((1, tm, d), lambda i, t: (i, t, 0)),
                  pl.BlockSpec((1, 6, 1, d), lambda i, t: (i % mod.shape[0], 0, 0, 0)),
                  _resident((1, d)),
                  _resident((d, HGRN_PARTS * width)),
                  _resident((1, width))],
        out_specs=[ospec] * 7,
        out_shape=[out] * 7,
        compiler_params=_params("parallel", "parallel"),
        name="hgrn_proj",
    )(x, mod, ng, w_in, lb)


_HGRN_LEVELS = (32, 16, 8, 4)


def _hgrn_tri(reverse):
    idx = np.arange(HGRN_CHUNK)
    tri = idx[None, :] >= idx[:, None] if reverse else idx[None, :] <= idx[:, None]
    return jnp.asarray(tri.astype(np.float32), dtype=BF16)


def _hgrn_block(q, k, v, g, tri, st, reverse):
    c = HGRN_CHUNK
    n = q.shape[0] // c

    g = g * LOG2_E
    g_hi = g.astype(BF16)
    rest = g - g_hi.astype(F32)
    g_mid = rest.astype(BF16)
    g_lo = (rest - g_mid.astype(F32)).astype(BF16)
    pieces = [piece[i * c:(i + 1) * c] for i in range(n) for piece in (g_hi, g_mid, g_lo)]
    sums = jnp.dot(tri, jnp.concatenate(pieces, axis=1), preferred_element_type=F32)
    cum = jnp.stack([(sums[:, (3 * i) * LANES:(3 * i + 1) * LANES]
                      + sums[:, (3 * i + 1) * LANES:(3 * i + 2) * LANES])
                     + sums[:, (3 * i + 2) * LANES:(3 * i + 3) * LANES] for i in range(n)], axis=0)

    q3, k3, v3 = (a.reshape(n, c, LANES) for a in (q, k, v))
    tot = cum[:, 0:1, :] if reverse else cum[:, c - 1:c, :]
    row = lax.broadcasted_iota(jnp.int32, (c, c), 0)
    col = lax.broadcasted_iota(jnp.int32, (c, c), 1)
    early, late = (row, col) if reverse else (col, row)

    sub_row = lax.broadcasted_iota(jnp.int32, (c, LANES), 0)
    scores = jnp.zeros((n, c, c), F32)
    for half in _HGRN_LEVELS:
        blk = 2 * half
        cb = cum.reshape(n * (c // blk), blk, LANES)
        b_row = half if reverse else half - 1
        d = cb - cb[:, b_row:b_row + 1, :]
        e = jnp.exp2(_neg_abs(d)).reshape(n, c, LANES)
        query_row = ((sub_row % blk) < half) if reverse else ((sub_row % blk) >= half)
        z = (jnp.where(query_row[None], q3, k3) * e).astype(BF16)
        s_l = lax.dot_general(z, z, (((2,), (2,)), ((0,), (0,))), preferred_element_type=F32)
        pick = ((row // blk) == (col // blk)) & ((early % blk) < half) & ((late % blk) >= half)
        scores = jnp.where(pick[None], s_l, scores)

    sub = _HGRN_LEVELS[-1]
    cum2 = cum.reshape(n * c, LANES)
    prods = [q * k]
    for delta in range(1, sub):
        shift = n * c - delta if reverse else delta
        decay = jnp.exp2(jnp.minimum(cum2 - pltpu.roll(cum2, shift, 0), 0.0))
        prods.append(q * pltpu.roll(k, shift, 0) * decay)
    ones_bf = jnp.ones((LANES, LANES), BF16)
    diag = jnp.dot(jnp.concatenate(prods, axis=0).astype(BF16), ones_bf, preferred_element_type=F32)
    for delta in range(sub):
        d_t = diag[delta * n * c:(delta + 1) * n * c, 0:c].reshape(n, c, c)
        if reverse:
            hit = (col == row + delta) & ((row % sub) + delta < sub)
        else:
            hit = (col == row - delta) & ((row % sub) >= delta)
        scores = jnp.where(hit[None], d_t, scores)

    out = lax.dot_general(scores.astype(BF16), v3.astype(BF16), (((2,), (1,)), ((0,), (0,))),
                          preferred_element_type=F32)
    qe = (q3 * jnp.exp2(cum)).astype(BF16)
    kd = (k3 * jnp.exp2(tot - cum)).astype(BF16)
    e_tot = jnp.exp2(tot)
    v_t = jnp.stack([v3[i].T for i in range(n)], axis=0).astype(BF16)
    upd = lax.dot_general(v_t, kd, (((2,), (1,)), ((0,), (0,))), preferred_element_type=F32)
    states = [None] * n
    for i in (range(n - 1, -1, -1) if reverse else range(n)):
        states[i] = st
        st = st * e_tot[i] + upd[i]
    out = out + lax.dot_general(qe, jnp.stack(states, axis=0).astype(BF16), (((2,), (2,)), ((0,), (0,))),
                                preferred_element_type=F32)
    return out.reshape(n * c, LANES), st


def _hgrn_scan_kernel(qf_ref, vf_ref, kf_ref, gf_ref, qb_ref, vb_ref, kb_ref, gb_ref, tf_ref, tb_ref,
                      sf0_ref, sb0_ref, of_ref, ob_ref, sf_ref, sb_ref, st_sc):
    step = pl.program_id(2)

    @pl.when(step == 0)
    def _():
        st_sc[0] = sf0_ref[0, 0]
        st_sc[1] = sb0_ref[0, 0]

    o, st_f = _hgrn_block(qf_ref[0], kf_ref[0], vf_ref[0], gf_ref[0], tf_ref[...], st_sc[0], False)
    of_ref[0] = o
    o, st_b = _hgrn_block(qb_ref[0], kb_ref[0], vb_ref[0], gb_ref[0], tb_ref[...], st_sc[1], True)
    ob_ref[0] = o
    st_sc[0] = st_f
    st_sc[1] = st_b

    @pl.when(step == pl.num_programs(2) - 1)
    def _():
        sf_ref[0, 0] = st_f
        sb_ref[0, 0] = st_b


def _hgrn_scan(q, v, kf, gf, kb, gb, sf0, sb0):
    b, l, width = q.shape
    heads = width // LANES
    rb = _row_tile(l, 1024)
    nc = l // rb
    fwd = pl.BlockSpec((1, rb, LANES), lambda i, h, s: (i, s, h))
    bwd = pl.BlockSpec((1, rb, LANES), lambda i, h, s: (i, nc - 1 - s, h))
    st_spec = pl.BlockSpec((1, 1, LANES, LANES), lambda i, h, s: (i, h, 0, 0))
    o_shape = jax.ShapeDtypeStruct((b, l, width), F32)
    s_shape = jax.ShapeDtypeStruct((b, heads, LANES, LANES), F32)
    return pl.pallas_call(
        _hgrn_scan_kernel,
        grid=(b, heads, nc),
        in_specs=[fwd, fwd, fwd, fwd, bwd, bwd, bwd, bwd,
                  _resident((HGRN_CHUNK, HGRN_CHUNK)), _resident((HGRN_CHUNK, HGRN_CHUNK)), st_spec, st_spec],
        out_specs=[fwd, bwd, st_spec, st_spec],
        out_shape=[o_shape, o_shape, s_shape, s_shape],
        scratch_shapes=[pltpu.VMEM((2, LANES, LANES), F32)],
        compiler_params=_params("parallel", "parallel", "arbitrary"),
        name="hgrn_scan",
    )(q, v, kf, gf, q, v, kb, gb, _hgrn_tri(False), _hgrn_tri(True), sf0, sb0)


def _hgrn_merge_kernel(x_ref, of_ref, ob_ref, gate_ref, og_ref, w_ref, mod_ref, y_ref):
    width = of_ref.shape[2]
    og = og_ref[...]
    parts = []
    for hd in range(width // LANES):
        sl = slice(hd * LANES, (hd + 1) * LANES)
        o = of_ref[0, :, sl] + ob_ref[0, :, sl]
        parts.append((_rms(o, og) * _silu(gate_ref[0, :, sl])).astype(BF16))
    r = jnp.concatenate(parts, axis=1)
    y = jnp.dot(r, w_ref[...], preferred_element_type=F32)
    y_ref[0] = x_ref[0] + mod_ref[0, 2] * y


def _hgrn_merge(x, o_f, o_b, gate, o_gain, w_out, mod):
    b, l, d = x.shape
    width = o_f.shape[2]
    tm = _row_tile(l, 512)
    wide = pl.BlockSpec((1, tm, width), lambda i, t: (i, t, 0))
    return pl.pallas_call(
        _hgrn_merge_kernel,
        grid=(b, l // tm),
        in_specs=[pl.BlockSpec((1, tm, d), lambda i, t: (i, t, 0)), wide, wide, wide,
                  _resident((1, LANES)),
                  _resident((width, d)),
                  pl.BlockSpec((1, 6, 1, d), lambda i, t: (i % mod.shape[0], 0, 0, 0))],
        out_specs=pl.BlockSpec((1, tm, d), lambda i, t: (i, t, 0)),
        out_shape=jax.ShapeDtypeStruct((b, l, d), F32),
        compiler_params=_params("parallel", "parallel"),
        name="hgrn_merge",
    )(x, o_f, o_b, gate, o_gain, w_out, mod)


def _mla_rope(y, cos, sin_lo, sin_hi):
    half = MLA_ROPE // 2
    return y * cos + pltpu.roll(y, LANES - half, 1) * sin_lo + pltpu.roll(y, half, 1) * sin_hi


def _mla_proj_kernel(x_ref, mod_ref, ng_ref, w_ref, qg_ref, kvg_ref, wq_ref, wk_ref, wv_ref,
                     cos_ref, slo_ref, shi_ref, q_ref, k_ref, v_ref, *, rotate, q_lora, kv_lora, chunk):
    h = _modnorm(x_ref[0], ng_ref[...], mod_ref[0, 0], mod_ref[0, 1]).astype(BF16)
    p = jnp.dot(h, w_ref[...], preferred_element_type=F32)
    cq = _rms(p[:, :q_lora], qg_ref[...]).astype(BF16)
    ckv = _rms(p[:, q_lora:q_lora + kv_lora], kvg_ref[...]).astype(BF16)
    k_rope = p[:, q_lora + kv_lora:]
    scale = (MLA_NOPE + MLA_ROPE) ** -0.5 * LOG2_E
    if rotate:
        cos, slo, shi = cos_ref[...], slo_ref[...], shi_ref[...]
        k_rope = _mla_rope(k_rope, cos, slo, shi)
    n = wq_ref.shape[1]
    for j in range(n // chunk):
        cs = slice(j * chunk, (j + 1) * chunk)
        pq = jnp.dot(cq, wq_ref[:, cs], preferred_element_type=F32)
        pk = jnp.dot(ckv, wk_ref[:, cs], preferred_element_type=F32)
        pv = jnp.dot(ckv, wv_ref[:, cs], preferred_element_type=F32)
        for u in range(chunk // LANES):
            us = slice(u * LANES, (u + 1) * LANES)
            os = slice(j * chunk + u * LANES, j * chunk + (u + 1) * LANES)
            v_ref[0, 0, os, :] = pv[:, us].T.astype(BF16)
            qh = pq[:, us]
            if rotate:
                qh = _mla_rope(qh, cos, slo, shi)
            q_ref[0, :, os] = (qh * scale).astype(BF16)
            k_ref[0, :, os] = (pk[:, us] + k_rope).astype(BF16)


def _mla_proj(x, mod, ng, w_in, qg, kvg, wq, wk, wv, cos, slo, shi, rotate):
    b, l, d = x.shape
    n = wq.shape[1]
    q_lora, kv_lora = wq.shape[0], wk.shape[0]
    tm = _row_tile(l, 512)
    kern = functools.partial(_mla_proj_kernel, rotate=rotate, q_lora=q_lora, kv_lora=kv_lora, chunk=512)
    out = jax.ShapeDtypeStruct((b, l, n), BF16)
    ospec = pl.BlockSpec((1, tm, n), lambda i, t: (i, t, 0))
    tab = pl.BlockSpec((tm, LANES), lambda i, t: (t, 0))
    return pl.pallas_call(
        kern,
        grid=(b, l // tm),
        in_specs=[pl.BlockSpec((1, tm, d), lambda i, t: (i, t, 0)),
                  pl.BlockSpec((1, 6, 1, d), lambda i, t: (i % mod.shape[0], 0, 0, 0)),
                  _resident((1, d)),
                  _resident(w_in.shape),
                  _resident((1, q_lora)),
                  _resident((1, kv_lora)),
                  _resident(wq.shape), _resident(wk.shape), _resident(wv.shape),
                  tab, tab, tab],
        out_specs=[ospec, ospec, pl.BlockSpec((1, 1, n, tm), lambda i, t: (i, t, 0, 0))],
        out_shape=[out, out, jax.ShapeDtypeStruct((b, l // tm, n, tm), BF16)],
        compiler_params=_params("parallel", "parallel"),
        name="mla_proj",
    )(x, mod, ng, w_in, qg, kvg, wq, wk, wv, cos, slo, shi)


def _final_kernel(x_ref, g_ref, y_ref):
    y_ref[0] = _rms(x_ref[0], g_ref[...])


def _final_norm(x, gain):
    b, l, d = x.shape
    tm = _row_tile(l, 1024)
    return pl.pallas_call(
        _final_kernel,
        grid=(b, l // tm),
        in_specs=[pl.BlockSpec((1, tm, d), lambda i, t: (i, t, 0)), _resident((1, d))],
        out_specs=pl.BlockSpec((1, tm, d), lambda i, t: (i, t, 0)),
        out_shape=jax.ShapeDtypeStruct((b, l, d), F32),
        compiler_params=_params("parallel", "parallel"),
        name="final_norm",
    )(x, gain)


def _axial_angles(rows, rot_dim):
    row = jnp.repeat(jnp.arange(rows, dtype=F32), GRID_W)
    col = jnp.tile(jnp.arange(GRID_W, dtype=F32), rows)
    axis_dim = rot_dim // 2
    inv_freq = jnp.power(ROPE_THETA, -jnp.arange(0, axis_dim, 2, dtype=F32) / axis_dim)
    ang = jnp.concatenate([row[:, None] * inv_freq, col[:, None] * inv_freq], axis=-1)
    return jnp.cos(ang), jnp.sin(ang)


def _gqa_tables(rows):
    cos, sin = _axial_angles(rows, GQA_HEAD_DIM)
    return jnp.concatenate([cos, cos], axis=-1), jnp.concatenate([-sin, sin], axis=-1)


def _mla_tables(rows):
    cos, sin = _axial_angles(rows, MLA_ROPE)
    s = cos.shape[0]
    half = MLA_ROPE // 2
    ones = jnp.ones((s, MLA_NOPE), F32)
    zeros = jnp.zeros((s, MLA_NOPE), F32)
    tail1 = jnp.ones((s, LANES - MLA_NOPE - MLA_ROPE), F32)
    tail0 = jnp.zeros((s, LANES - MLA_NOPE - MLA_ROPE), F32)
    zh = jnp.zeros((s, half), F32)
    c = jnp.concatenate([ones, cos, cos, tail1], axis=-1)
    s_lo = jnp.concatenate([zeros, -sin, zh, tail0], axis=-1)
    s_hi = jnp.concatenate([zeros, zh, sin, tail0], axis=-1)
    return c, s_lo, s_hi


def _mla_weights(w_in, w_qb, w_kvb, w_out):
    d = w_in.shape[0]
    q_lora, kv_lora = w_qb.shape[0], w_kvb.shape[0]
    heads = w_qb.shape[1] // (MLA_NOPE + MLA_ROPE)
    kr = jnp.zeros((d, LANES), w_in.dtype).at[:, MLA_NOPE:MLA_NOPE + MLA_ROPE].set(w_in[:, q_lora + kv_lora:])
    w_in_p = jnp.concatenate([w_in[:, :q_lora + kv_lora], kr], axis=1)
    wq = w_qb.reshape(q_lora, heads, MLA_NOPE + MLA_ROPE)
    wq = jnp.pad(wq, ((0, 0), (0, 0), (0, LANES - MLA_NOPE - MLA_ROPE))).reshape(q_lora, heads * LANES)
    wkv = w_kvb.reshape(kv_lora, heads, MLA_NOPE + MLA_V)
    wk = jnp.pad(wkv[:, :, :MLA_NOPE], ((0, 0), (0, 0), (0, LANES - MLA_NOPE))).reshape(kv_lora, heads * LANES)
    wv = jnp.pad(wkv[:, :, MLA_NOPE:], ((0, 0), (0, 0), (0, LANES - MLA_V))).reshape(kv_lora, heads * LANES)
    wo = w_out.reshape(heads, MLA_V, -1)
    wo = jnp.pad(wo, ((0, 0), (0, LANES - MLA_V), (0, 0))).reshape(heads * LANES, -1)
    return w_in_p.astype(BF16), wq.astype(BF16), wk.astype(BF16), wv.astype(BF16), wo.astype(BF16)


def kernel(x, c, ctx, c_ctx, w_ada, b_ada, norm_mix, norm_ffn, ffn_w_in, ffn_conv_w, ffn_conv_b, ffn_w_out,
           gqa_w_in, gqa_q_norm, gqa_k_norm, gqa_w_out, hgrn_w_in, hgrn_out_norm, hgrn_w_out, hgrn_lower_bounds,
           mla_w_in, mla_q_norm, mla_kv_norm, mla_w_qb, mla_w_kvb, mla_w_out, final_norm):
    batch, seq, d = x.shape
    depth = w_ada.shape[0]
    n_mixers = 3
    rows = seq // GRID_W
    assert batch + 1 <= SUBLANES

    cv = jnp.zeros((SUBLANES, d), F32).at[:batch].set(c).at[batch].set(c_ctx)
    mods = _ada_mods(cv, w_ada, b_ada).reshape(depth, SUBLANES, 6, 1, d)
    lb_all = _lower_bounds(hgrn_lower_bounds)

    cos_a, sin_a = _gqa_tables(rows)
    cos_m, slo_m, shi_m = _mla_tables(rows)

    for i in range(depth):
        last = i == depth - 1
        kind = i % n_mixers
        j = i // n_mixers
        mod = mods[i, :batch]
        mod_c = mods[i, batch:batch + 1]
        ng = norm_mix[i][None, :]

        if kind == 0:
            w_in = gqa_w_in[j].astype(BF16)
            w_out = gqa_w_out[j].astype(BF16)
            qg, kg = gqa_q_norm[j][None, :], gqa_k_norm[j][None, :]
            q, k, v = _gqa_proj(x, mod, ng, w_in, qg, kg, cos_a, sin_a, True)
            n_ctx = ctx.shape[1]
            qc, kc, vc = _gqa_proj(ctx, mod_c, ng, w_in, qg, kg, cos_a[:n_ctx], sin_a[:n_ctx], False)
            o = _attention(q, [(k, v), (kc, vc)], GQA_GROUP, 256)
            x = _merge(x, o, w_out, mod)
            if not last:
                oc = _attention(qc, [(kc, vc)], GQA_GROUP, 256)
                ctx = _merge(ctx, oc, w_out, mod_c)
        elif kind == 1:
            w_in = hgrn_w_in[j].astype(BF16)
            w_out = hgrn_w_out[j].astype(BF16)
            lb = lb_all[i][None, :]
            og = hgrn_out_norm[j][None, :]
            heads = w_out.shape[0] // HGRN_DK
            qc, vc, kfc, gfc, kbc, gbc, gatec = _hgrn_proj(ctx, mod_c, ng, w_in, lb)
            q, v, kf, gf, kb, gb, gate = _hgrn_proj(x, mod, ng, w_in, lb)
            s0 = jnp.zeros((batch, heads, HGRN_DK, HGRN_DK), F32)
            oc_f, oc_b, s_f, s_b = _hgrn_scan(qc, vc, kfc, gfc, kbc, gbc, s0, s0)
            o_f, o_b, _, _ = _hgrn_scan(q, v, kf, gf, kb, gb, s_f, s_b)
            x = _hgrn_merge(x, o_f, o_b, gate, og, w_out, mod)
            if not last:
                ctx = _hgrn_merge(ctx, oc_f, oc_b, gatec, og, w_out, mod_c)
        else:
            w_in, wq, wk, wv, w_out = _mla_weights(mla_w_in[j], mla_w_qb[j], mla_w_kvb[j], mla_w_out[j])
            qg, kvg = mla_q_norm[j][None, :], mla_kv_norm[j][None, :]
            n_ctx = ctx.shape[1]
            q, k, v = _mla_proj(x, mod, ng, w_in, qg, kvg, wq, wk, wv, cos_m, slo_m, shi_m, True)
            qc, kc, vc = _mla_proj(ctx, mod_c, ng, w_in, qg, kvg, wq, wk, wv,
                                   cos_m[:n_ctx], slo_m[:n_ctx], shi_m[:n_ctx], False)
            o = _attention(q, [(k, v), (kc, vc)], 1, 512)
            x = _merge(x, o, w_out, mod)
            if not last:
                oc = _attention(qc, [(kc, vc)], 1, 512)
                ctx = _merge(ctx, oc, w_out, mod_c)

        fg = norm_ffn[i][None, :]
        f_in = ffn_w_in[i].astype(BF16)
        f_out = ffn_w_out[i].astype(BF16)
        f_cw = ffn_conv_w[i]
        f_cb = ffn_conv_b[i][None, :]
        x = _conv_ffn(x, mod, fg, f_in, f_cw, f_cb, f_out)
        if not last:
            ctx = _conv_ffn(ctx, mod_c, fg, f_in, f_cw, f_cb, f_out)

    return _final_norm(x, final_norm[None, :])
```

```python
import functools

import numpy as np
import jax
import jax.numpy as jnp
from jax import lax
from jax.experimental import pallas as pl
from jax.experimental.pallas import tpu as pltpu

F32 = jnp.float32
BF16 = jnp.bfloat16
HIGHEST = lax.Precision.HIGHEST

GRID_W = 64
ROPE_THETA = 10000.0
NORM_EPS = 1e-6
CONV_W = 3

LANES = 128
SUBLANES = 8

GQA_HEAD_DIM = 128
GQA_GROUP = 2

HGRN_DK = 128
HGRN_CHUNK = 64
HGRN_PARTS = 5

MLA_NOPE = 64
MLA_ROPE = 32
MLA_V = 64

VMEM_LIMIT = 56 * 1024 * 1024

LOG2_E = 1.4426950408889634
ATTN_UNROLL = 8


def _params(*sem):
    return pltpu.CompilerParams(dimension_semantics=sem, vmem_limit_bytes=VMEM_LIMIT)


def _resident(shape):
    nd = len(shape)
    return pl.BlockSpec(shape, lambda *_: (0,) * nd, pipeline_mode=pl.Buffered(1))


def _silu(x):
    return x / (1.0 + jnp.exp(-x))


def _sigmoid(x):
    return 1.0 / (1.0 + jnp.exp(-x))


def _neg_abs(x):
    bits = lax.bitcast_convert_type(x, jnp.uint32) | jnp.uint32(0x80000000)
    return lax.bitcast_convert_type(bits, F32)


def _rms(x, gain):
    return x * lax.rsqrt(jnp.mean(x * x, axis=-1, keepdims=True) + NORM_EPS) * gain


def _modnorm(x, gain, shift, scale):
    return _rms(x, gain) * (1.0 + scale) + shift


def _aligned_ds(start, size):
    if isinstance(start, int):
        return pl.ds(start, size)
    return pl.ds(pl.multiple_of(start, size), size)


def _row_tile(n, want):
    t = min(n, want)
    while n % t:
        t -= SUBLANES
    assert t > 0 and t % SUBLANES == 0, (n, want)
    return t


def _ada_kernel(cv_ref, w_ref, b_ref, o_ref):
    s = _silu(cv_ref[...])
    o_ref[0] = jnp.dot(s, w_ref[0], precision=HIGHEST, preferred_element_type=F32) + b_ref[0]


def _ada_mods(cv, w_ada, b_ada):
    depth, d, n = w_ada.shape
    tn = 1536
    return pl.pallas_call(
        _ada_kernel,
        grid=(depth, n // tn),
        in_specs=[pl.BlockSpec((SUBLANES, d), lambda i, j: (0, 0)),
                  pl.BlockSpec((1, d, tn), lambda i, j: (i, 0, j)),
                  pl.BlockSpec((1, 1, tn), lambda i, j: (i, 0, j))],
        out_specs=pl.BlockSpec((1, SUBLANES, tn), lambda i, j: (i, 0, j)),
        out_shape=jax.ShapeDtypeStruct((depth, SUBLANES, n), F32),
        compiler_params=_params("arbitrary", "arbitrary"),
        name="ada_mods",
    )(cv, w_ada, b_ada.reshape(depth, 1, n))


def _lb_kernel(x_ref, o_ref):
    depth = x_ref.shape[0]
    rows = [x_ref[i:i + 1, :] for i in range(depth)]
    m = rows[0]
    for r in rows[1:]:
        m = jnp.maximum(m, r)
    e = [jnp.exp(r - m) for r in rows]
    tot = e[0]
    for r in e[1:]:
        tot = tot + r
    p = [r / tot for r in e]
    cum = p[0]
    o_ref[0:1, :] = cum - p[0]
    for i in range(1, depth):
        cum = cum + p[i]
        o_ref[i:i + 1, :] = cum - p[0]


def _lower_bounds(lb_raw):
    return pl.pallas_call(
        _lb_kernel,
        out_shape=jax.ShapeDtypeStruct(lb_raw.shape, F32),
        name="hgrn_lower_bounds",
    )(lb_raw.astype(F32))


def _gqa_proj_kernel(x_ref, mod_ref, ng_ref, w_ref, qg_ref, kg_ref, cos_ref, sin_ref,
                     q_ref, k_ref, v_ref, *, rotate, qd, kd, chunk):
    h = _modnorm(x_ref[0], ng_ref[...], mod_ref[0, 0], mod_ref[0, 1]).astype(BF16)
    qg = qg_ref[...]
    kg = kg_ref[...]
    scale = GQA_HEAD_DIM ** -0.5 * LOG2_E
    if rotate:
        cos = cos_ref[...]
        sin = sin_ref[...]

    def head(p, gain):
        y = _rms(p, gain)
        if rotate:
            y = y * cos + pltpu.roll(y, GQA_HEAD_DIM // 2, 1) * sin
        return y

    n = w_ref.shape[1]
    for j in range(n // chunk):
        p = jnp.dot(h, w_ref[:, j * chunk:(j + 1) * chunk], preferred_element_type=F32)
        for u in range(chunk // LANES):
            col = j * chunk + u * LANES
            ph = p[:, u * LANES:(u + 1) * LANES]
            if col < qd:
                q_ref[0, :, col:col + LANES] = (head(ph, qg) * scale).astype(BF16)
            elif col < qd + kd:
                k_ref[0, :, col - qd:col - qd + LANES] = head(ph, kg).astype(BF16)
            else:
                c0 = col - qd - kd
                v_ref[0, 0, c0:c0 + LANES, :] = ph.T.astype(BF16)


def _gqa_proj(x, mod, ng, w_in, qg, kg, cos, sin, rotate):
    b, l, d = x.shape
    n = w_in.shape[1]
    kd = n // 4
    qd = n - 2 * kd
    tm = _row_tile(l, 512)
    kern = functools.partial(_gqa_proj_kernel, rotate=rotate, qd=qd, kd=kd, chunk=512)
    return pl.pallas_call(
        kern,
        grid=(b, l // tm),
        in_specs=[pl.BlockSpec((1, tm, d), lambda i, t: (i, t, 0)),
                  pl.BlockSpec((1, 6, 1, d), lambda i, t: (i % mod.shape[0], 0, 0, 0)),
                  _resident((1, d)),
                  _resident((d, n)),
                  _resident((1, LANES)),
                  _resident((1, LANES)),
                  pl.BlockSpec((tm, LANES), lambda i, t: (t, 0)),
                  pl.BlockSpec((tm, LANES), lambda i, t: (t, 0))],
        out_specs=[pl.BlockSpec((1, tm, qd), lambda i, t: (i, t, 0)),
                   pl.BlockSpec((1, tm, kd), lambda i, t: (i, t, 0)),
                   pl.BlockSpec((1, 1, kd, tm), lambda i, t: (i, t, 0, 0))],
        out_shape=[jax.ShapeDtypeStruct((b, l, qd), BF16),
                   jax.ShapeDtypeStruct((b, l, kd), BF16),
                   jax.ShapeDtypeStruct((b, l // tm, kd, tm), BF16)],
        compiler_params=_params("parallel", "parallel"),
        name="gqa_proj",
    )(x, mod, ng, w_in, qg, kg, cos, sin)


def _attn_kernel(*refs, group, shared, n_src):
    q_ref = refs[0]
    kv_refs = refs[1:1 + 2 * n_src]
    o_ref = refs[1 + 2 * n_src]
    s_scs = refs[2 + 2 * n_src:4 + 2 * n_src]
    m_sc = refs[4 + 2 * n_src]
    tq = q_ref.shape[1]
    nq = group * tq
    dv = LANES if shared else LANES // group
    step = pl.program_id(2)

    def fold(a):
        return a.reshape(a.shape[0] // SUBLANES, SUBLANES, nq)

    def run(do_scores, do_weigh, slot):
        prev = 1 - slot
        carry = {}
        if do_scores:
            q = q_ref[0]
            q_heads = [q[:, g * LANES:(g + 1) * LANES] for g in range(group)]
            q_all = jnp.concatenate(q_heads, axis=0)
            carry["m8"] = jnp.full((SUBLANES, nq), -jnp.inf, F32)
        if do_weigh:
            m = jnp.max(m_sc[prev], axis=0, keepdims=True)
            carry["l8"] = jnp.zeros((SUBLANES, nq), F32)
            carry["acc"] = jnp.zeros((dv, nq), F32)
        row0 = 0
        for i in range(n_src):
            k_ref, vt_ref = kv_refs[2 * i], kv_refs[2 * i + 1]
            n_chunks, _, c = vt_ref.shape[1:]

            def body(j, carry, k_ref=k_ref, vt_ref=vt_ref, c=c, row0=row0):
                carry = dict(carry)
                rows = _aligned_ds(row0 + j * c, c)
                if do_scores:
                    kc = k_ref[0, _aligned_ds(j * c, c), :]
                    nt_dims = (((1,), (1,)), ((), ()))
                    if shared:
                        s = lax.dot_general(kc, q_all, nt_dims, preferred_element_type=F32)
                    else:
                        s = jnp.concatenate(
                            [lax.dot_general(kc[:, g * LANES:(g + 1) * LANES], q_heads[g], nt_dims,
                                             preferred_element_type=F32) for g in range(group)], axis=1)
                    s_scs[slot][rows, :] = s
                    carry["m8"] = jnp.maximum(carry["m8"], jnp.max(fold(s), axis=0))
                if do_weigh:
                    p = jnp.exp2(s_scs[prev][rows, :] - m)
                    carry["l8"] = carry["l8"] + jnp.sum(fold(p), axis=0)
                    pb = p.astype(BF16)
                    vt = vt_ref[0, j]
                    if shared:
                        upd = jnp.dot(vt, pb, preferred_element_type=F32)
                    else:
                        upd = jnp.concatenate(
                            [jnp.dot(vt[g * dv:(g + 1) * dv], pb[:, g * tq:(g + 1) * tq],
                                     preferred_element_type=F32) for g in range(group)], axis=1)
                    carry["acc"] = carry["acc"] + upd
                return carry

            if n_chunks == 1:
                carry = body(0, carry)
            else:
                carry = lax.fori_loop(0, n_chunks, body, carry, unroll=ATTN_UNROLL)
            row0 += n_chunks * c
        if do_scores:
            m_sc[slot] = carry["m8"]
        if do_weigh:
            o = carry["acc"] / jnp.sum(carry["l8"], axis=0, keepdims=True)
            o_ref[0] = jnp.concatenate([o[:, g * tq:(g + 1) * tq].T for g in range(group)],
                                       axis=1).astype(o_ref.dtype)

    n_tiles = pl.num_programs(2) - 1
    middle = (step > 0) & (step < n_tiles)

    @pl.when(step == 0)
    def _():
        run(True, False, 0)

    for parity in range(2):
        @pl.when(middle & (step % 2 == parity))
        def _(parity=parity):
            run(True, True, parity)

        @pl.when((step == n_tiles) & (n_tiles % 2 == parity))
        def _(parity=parity):
            run(False, True, parity)


def _attention(q, kv_sources, group, shared, tq):
    b, nq, hd = q.shape
    hkv = hd // (group * LANES)
    kw = LANES if shared else group * LANES
    ow = group * LANES if shared else LANES
    tq = _row_tile(nq, tq)
    nt = nq // tq
    n_src = len(kv_sources)
    in_specs = [pl.BlockSpec((1, tq, group * LANES), lambda i, h, t: (i, jnp.minimum(t, nt - 1), h))]
    args = [q]
    nk_total = 0
    for k, vt in kv_sources:
        nk = k.shape[1]
        n_chunks, _, c = vt.shape[1:]
        assert n_chunks * c == nk and nk_total % c == 0
        nk_total += nk
        in_specs.append(pl.BlockSpec((1, nk, kw), lambda i, h, t: (i, 0, h)))
        in_specs.append(pl.BlockSpec((1, n_chunks, LANES, c), lambda i, h, t: (i, 0, h, 0)))
        args += [k, vt]
    return pl.pallas_call(
        functools.partial(_attn_kernel, group=group, shared=shared, n_src=n_src),
        grid=(b, hkv, nt + 1),
        in_specs=in_specs,
        out_specs=pl.BlockSpec((1, tq, ow), lambda i, h, t: (i, jnp.maximum(t - 1, 0), h)),
        out_shape=jax.ShapeDtypeStruct((b, nq, hkv * ow), BF16),
        scratch_shapes=[pltpu.VMEM((nk_total, group * tq), F32),
                        pltpu.VMEM((nk_total, group * tq), F32),
                        pltpu.VMEM((2, SUBLANES, group * tq), F32)],
        compiler_params=_params("parallel", "parallel", "arbitrary"),
        name="flash_attention",
    )(*args)


def _merge_kernel(x_ref, o_ref, w_ref, mod_ref, y_ref):
    y = jnp.dot(o_ref[0], w_ref[...], preferred_element_type=F32)
    y_ref[0] = x_ref[0] + mod_ref[0, 2] * y


def _merge(x, o, w_out, mod):
    b, l, d = x.shape
    ko = o.shape[2]
    tm = _row_tile(l, 512)
    return pl.pallas_call(
        _merge_kernel,
        grid=(b, l // tm),
        in_specs=[pl.BlockSpec((1, tm, d), lambda i, t: (i, t, 0)),
                  pl.BlockSpec((1, tm, ko), lambda i, t: (i, t, 0)),
                  _resident((ko, d)),
                  pl.BlockSpec((1, 6, 1, d), lambda i, t: (i % mod.shape[0], 0, 0, 0))],
        out_specs=pl.BlockSpec((1, tm, d), lambda i, t: (i, t, 0)),
        out_shape=jax.ShapeDtypeStruct((b, l, d), F32),
        compiler_params=_params("parallel", "parallel"),
        name="merge_residual",
    )(x, o, w_out, mod)


def _ffn_kernel(xp_ref, x_ref, xn_ref, mod_ref, ng_ref, win_ref, cw_ref, cb_ref, wout_ref, y_ref,
                *, d_ff, chunk):
    t = pl.program_id(1)
    nt = pl.num_programs(1)
    gain = ng_ref[...]
    shift, scale, gate = mod_ref[0, 3], mod_ref[0, 4], mod_ref[0, 5]
    x = x_ref[0]
    tm = x.shape[0]
    halo = xp_ref.shape[1]
    hp = jnp.where(t > 0, _modnorm(xp_ref[0], gain, shift, scale), 0.0)
    hn = jnp.where(t < nt - 1, _modnorm(xn_ref[0], gain, shift, scale), 0.0)
    h = jnp.concatenate([hp, _modnorm(x, gain, shift, scale), hn], axis=0).astype(BF16)
    rows = tm + 2 * halo

    def conv(p, col):
        w = cw_ref[:, col:col + chunk]
        prev = pltpu.roll(p, 1, 0)[halo:halo + tm]
        nxt = pltpu.roll(p, rows - 1, 0)[halo:halo + tm]
        cur = p[halo:halo + tm]
        return ((cb_ref[:, col:col + chunk] + prev * w[0:1]) + cur * w[1:2]) + nxt * w[2:3]

    acc = jnp.zeros((tm, x.shape[1]), F32)
    for c in range(d_ff // chunk):
        pa = jnp.dot(h, win_ref[:, c * chunk:(c + 1) * chunk], preferred_element_type=F32)
        pv = jnp.dot(h, win_ref[:, d_ff + c * chunk:d_ff + (c + 1) * chunk], preferred_element_type=F32)
        g = _silu(conv(pa, c * chunk)) * conv(pv, d_ff + c * chunk)
        acc = acc + jnp.dot(g.astype(BF16), wout_ref[c * chunk:(c + 1) * chunk, :],
                            preferred_element_type=F32)
    y_ref[0] = x + gate * acc


def _conv_ffn(x, mod, ng, w_in, conv_w, conv_b, w_out):
    b, l, d = x.shape
    d_ff = w_out.shape[0]
    tm = _row_tile(l, 512)
    halo = SUBLANES
    per = tm // halo
    last = l // halo - 1
    kern = functools.partial(_ffn_kernel, d_ff=d_ff, chunk=d_ff)
    return pl.pallas_call(
        kern,
        grid=(b, l // tm),
        in_specs=[pl.BlockSpec((1, halo, d), lambda i, t: (i, jnp.maximum(t * per - 1, 0), 0)),
                  pl.BlockSpec((1, tm, d), lambda i, t: (i, t, 0)),
                  pl.BlockSpec((1, halo, d), lambda i, t: (i, jnp.minimum((t + 1) * per, last), 0)),
                  pl.BlockSpec((1, 6, 1, d), lambda i, t: (i % mod.shape[0], 0, 0, 0)),
                  _resident((1, d)),
                  _resident((d, 2 * d_ff)),
                  _resident((CONV_W, 2 * d_ff)),
                  _resident((1, 2 * d_ff)),
                  _resident((d_ff, d))],
        out_specs=pl.BlockSpec((1, tm, d), lambda i, t: (i, t, 0)),
        out_shape=jax.ShapeDtypeStruct((b, l, d), F32),
        compiler_params=_params("parallel", "arbitrary"),
        name="conv_ffn",
    )(x, x, x, mod, ng, w_in, conv_w, conv_b, w_out)


def _hgrn_proj_kernel(x_ref, mod_ref, ng_ref, w_ref, lb_ref, q_ref, v_ref, kf_ref, gf_ref, kb_ref, gb_ref,
                      gate_ref, *, chunk):
    h = _modnorm(x_ref[0], ng_ref[...], mod_ref[0, 0], mod_ref[0, 1]).astype(BF16)
    width = q_ref.shape[2]
    scale = HGRN_DK ** -0.5
    for part in range(HGRN_PARTS):
        for j in range(width // chunk):
            c0 = j * chunk
            p = jnp.dot(h, w_ref[:, part * width + c0:part * width + c0 + chunk], preferred_element_type=F32)
            if part == 0:
                q_ref[0, :, c0:c0 + chunk] = p * scale
            elif part == 1:
                v_ref[0, :, c0:c0 + chunk] = p
            elif part == 4:
                gate_ref[0, :, c0:c0 + chunk] = p
            else:
                lb = lb_ref[:, c0:c0 + chunk]
                f = lb + (1.0 - lb) * _sigmoid(p)
                k_out, g_out = (kf_ref, gf_ref) if part == 2 else (kb_ref, gb_ref)
                k_out[0, :, c0:c0 + chunk] = 1.0 - f
                g_out[0, :, c0:c0 + chunk] = jnp.log(f)


def _hgrn_proj(x, mod, ng, w_in, lb):
    b, l, d = x.shape
    width = w_in.shape[1] // HGRN_PARTS
    tm = _row_tile(l, 512)
    out = jax.ShapeDtypeStruct((b, l, width), F32)
    ospec = pl.BlockSpec((1, tm, width), lambda i, t: (i, t, 0))
    return pl.pallas_call(
        functools.partial(_hgrn_proj_kernel, chunk=512),
        grid=(b, l // tm),
        in_specs=[pl.BlockSpec((1, tm, d), lambda i, t: (i, t, 0)),
                  pl.BlockSpec((1, 6, 1, d), lambda i, t: (i % mod.shape[0], 0, 0, 0)),
                  _resident((1, d)),
                  _resident((d, HGRN_PARTS * width)),
                  _resident((1, width))],
        out_specs=[ospec] * 7,
        out_shape=[out] * 7,
        compiler_params=_params("parallel", "parallel"),
        name="hgrn_proj",
    )(x, mod, ng, w_in, lb)


_HGRN_LEVELS = (32, 16, 8, 4)


def _hgrn_tri(reverse):
    idx = np.arange(HGRN_CHUNK)
    tri = idx[None, :] >= idx[:, None] if reverse else idx[None, :] <= idx[:, None]
    return jnp.asarray(tri.astype(np.float32), dtype=BF16)


def _hgrn_block(q, k, v, g, tri, st, reverse):
    c = HGRN_CHUNK
    n = q.shape[0] // c

    g = g * LOG2_E
    g_hi = g.astype(BF16)
    rest = g - g_hi.astype(F32)
    g_mid = rest.astype(BF16)
    g_lo = (rest - g_mid.astype(F32)).astype(BF16)
    pieces = [piece[i * c:(i + 1) * c] for i in range(n) for piece in (g_hi, g_mid, g_lo)]
    sums = jnp.dot(tri, jnp.concatenate(pieces, axis=1), preferred_element_type=F32)
    cum = jnp.stack([(sums[:, (3 * i) * LANES:(3 * i + 1) * LANES]
                      + sums[:, (3 * i + 1) * LANES:(3 * i + 2) * LANES])
                     + sums[:, (3 * i + 2) * LANES:(3 * i + 3) * LANES] for i in range(n)], axis=0)

    q3, k3, v3 = (a.reshape(n, c, LANES) for a in (q, k, v))
    tot = cum[:, 0:1, :] if reverse else cum[:, c - 1:c, :]
    row = lax.broadcasted_iota(jnp.int32, (c, c), 0)
    col = lax.broadcasted_iota(jnp.int32, (c, c), 1)
    early, late = (row, col) if reverse else (col, row)

    sub_row = lax.broadcasted_iota(jnp.int32, (c, LANES), 0)
    scores = jnp.zeros((n, c, c), F32)
    for half in _HGRN_LEVELS:
        blk = 2 * half
        cb = cum.reshape(n * (c // blk), blk, LANES)
        b_row = half if reverse else half - 1
        d = cb - cb[:, b_row:b_row + 1, :]
        e = jnp.exp2(_neg_abs(d)).reshape(n, c, LANES)
        query_row = ((sub_row % blk) < half) if reverse else ((sub_row % blk) >= half)
        z = (jnp.where(query_row[None], q3, k3) * e).astype(BF16)
        s_l = lax.dot_general(z, z, (((2,), (2,)), ((0,), (0,))), preferred_element_type=F32)
        pick = ((row // blk) == (col // blk)) & ((early % blk) < half) & ((late % blk) >= half)
        scores = jnp.where(pick[None], s_l, scores)

    sub = _HGRN_LEVELS[-1]
    cum2 = cum.reshape(n * c, LANES)
    prods = [q * k]
    for delta in range(1, sub):
        shift = n * c - delta if reverse else delta
        decay = jnp.exp2(jnp.minimum(cum2 - pltpu.roll(cum2, shift, 0), 0.0))
        prods.append(q * pltpu.roll(k, shift, 0) * decay)
    ones_bf = jnp.ones((LANES, LANES), BF16)
    diag = jnp.dot(jnp.concatenate(prods, axis=0).astype(BF16), ones_bf, preferred_element_type=F32)
    for delta in range(sub):
        d_t = diag[delta * n * c:(delta + 1) * n * c, 0:c].reshape(n, c, c)
        if reverse:
            hit = (col == row + delta) & ((row % sub) + delta < sub)
        else:
            hit = (col == row - delta) & ((row % sub) >= delta)
        scores = jnp.where(hit[None], d_t, scores)

    out = lax.dot_general(scores.astype(BF16), v3.astype(BF16), (((2,), (1,)), ((0,), (0,))),
                          preferred_element_type=F32)
    qe = (q3 * jnp.exp2(cum)).astype(BF16)
    kd = (k3 * jnp.exp2(tot - cum)).astype(BF16)
    e_tot = jnp.exp2(tot)
    v_t = jnp.stack([v3[i].T for i in range(n)], axis=0).astype(BF16)
    upd = lax.dot_general(v_t, kd, (((2,), (1,)), ((0,), (0,))), preferred_element_type=F32)
    states = [None] * n
    for i in (range(n - 1, -1, -1) if reverse else range(n)):
        states[i] = st
        st = st * e_tot[i] + upd[i]
    out = out + lax.dot_general(qe, jnp.stack(states, axis=0).astype(BF16), (((2,), (2,)), ((0,), (0,))),
                                preferred_element_type=F32)
    return out.reshape(n * c, LANES), st


def _hgrn_scan_kernel(qf_ref, vf_ref, kf_ref, gf_ref, qb_ref, vb_ref, kb_ref, gb_ref, tf_ref, tb_ref,
                      sf0_ref, sb0_ref, of_ref, ob_ref, sf_ref, sb_ref, st_sc):
    step = pl.program_id(2)

    @pl.when(step == 0)
    def _():
        st_sc[0] = sf0_ref[0, 0]
        st_sc[1] = sb0_ref[0, 0]

    o, st_f = _hgrn_block(qf_ref[0], kf_ref[0], vf_ref[0], gf_ref[0], tf_ref[...], st_sc[0], False)
    of_ref[0] = o
    o, st_b = _hgrn_block(qb_ref[0], kb_ref[0], vb_ref[0], gb_ref[0], tb_ref[...], st_sc[1], True)
    ob_ref[0] = o
    st_sc[0] = st_f
    st_sc[1] = st_b

    @pl.when(step == pl.num_programs(2) - 1)
    def _():
        sf_ref[0, 0] = st_f
        sb_ref[0, 0] = st_b


def _hgrn_scan(q, v, kf, gf, kb, gb, sf0, sb0):
    b, l, width = q.shape
    heads = width // LANES
    rb = _row_tile(l, 1024)
    nc = l // rb
    fwd = pl.BlockSpec((1, rb, LANES), lambda i, h, s: (i, s, h))
    bwd = pl.BlockSpec((1, rb, LANES), lambda i, h, s: (i, nc - 1 - s, h))
    st_spec = pl.BlockSpec((1, 1, LANES, LANES), lambda i, h, s: (i, h, 0, 0))
    o_shape = jax.ShapeDtypeStruct((b, l, width), F32)
    s_shape = jax.ShapeDtypeStruct((b, heads, LANES, LANES), F32)
    return pl.pallas_call(
        _hgrn_scan_kernel,
        grid=(b, heads, nc),
        in_specs=[fwd, fwd, fwd, fwd, bwd, bwd, bwd, bwd,
                  _resident((HGRN_CHUNK, HGRN_CHUNK)), _resident((HGRN_CHUNK, HGRN_CHUNK)), st_spec, st_spec],
        out_specs=[fwd, bwd, st_spec, st_spec],
        out_shape=[o_shape, o_shape, s_shape, s_shape],
        scratch_shapes=[pltpu.VMEM((2, LANES, LANES), F32)],
        compiler_params=_params("parallel", "parallel", "arbitrary"),
        name="hgrn_scan",
    )(q, v, kf, gf, q, v, kb, gb, _hgrn_tri(False), _hgrn_tri(True), sf0, sb0)


def _hgrn_merge_kernel(x_ref, of_ref, ob_ref, gate_ref, og_ref, w_ref, mod_ref, y_ref):
    width = of_ref.shape[2]
    og = og_ref[...]
    parts = []
    for hd in range(width // LANES):
        sl = slice(hd * LANES, (hd + 1) * LANES)
        o = of_ref[0, :, sl] + ob_ref[0, :, sl]
        parts.append((_rms(o, og) * _silu(gate_ref[0, :, sl])).astype(BF16))
    r = jnp.concatenate(parts, axis=1)
    y = jnp.dot(r, w_ref[...], preferred_element_type=F32)
    y_ref[0] = x_ref[0] + mod_ref[0, 2] * y


def _hgrn_merge(x, o_f, o_b, gate, o_gain, w_out, mod):
    b, l, d = x.shape
    width = o_f.shape[2]
    tm = _row_tile(l, 512)
    wide = pl.BlockSpec((1, tm, width), lambda i, t: (i, t, 0))
    return pl.pallas_call(
        _hgrn_merge_kernel,
        grid=(b, l // tm),
        in_specs=[pl.BlockSpec((1, tm, d), lambda i, t: (i, t, 0)), wide, wide, wide,
                  _resident((1, LANES)),
                  _resident((width, d)),
                  pl.BlockSpec((1, 6, 1, d), lambda i, t: (i % mod.shape[0], 0, 0, 0))],
        out_specs=pl.BlockSpec((1, tm, d), lambda i, t: (i, t, 0)),
        out_shape=jax.ShapeDtypeStruct((b, l, d), F32),
        compiler_params=_params("parallel", "parallel"),
        name="hgrn_merge",
    )(x, o_f, o_b, gate, o_gain, w_out, mod)


def _mla_rope(y, cos, sin_lo, sin_hi):
    half = MLA_ROPE // 2
    return y * cos + pltpu.roll(y, LANES - half, 1) * sin_lo + pltpu.roll(y, half, 1) * sin_hi


def _mla_proj_kernel(x_ref, mod_ref, ng_ref, w_ref, qg_ref, kvg_ref, wq_ref, wk_ref, wv_ref,
                     cos_ref, slo_ref, shi_ref, q_ref, k_ref, v_ref, *, rotate, q_lora, kv_lora, chunk):
    h = _modnorm(x_ref[0], ng_ref[...], mod_ref[0, 0], mod_ref[0, 1]).astype(BF16)
    p = jnp.dot(h, w_ref[...], preferred_element_type=F32)
    cq = _rms(p[:, :q_lora], qg_ref[...]).astype(BF16)
    ckv = _rms(p[:, q_lora:q_lora + kv_lora], kvg_ref[...]).astype(BF16)
    k_rope = p[:, q_lora + kv_lora:]
    scale = (MLA_NOPE + MLA_ROPE) ** -0.5 * LOG2_E
    if rotate:
        cos, slo, shi = cos_ref[...], slo_ref[...], shi_ref[...]
        k_rope = _mla_rope(k_rope, cos, slo, shi)
    n = wq_ref.shape[1]
    for j in range(n // chunk):
        cs = slice(j * chunk, (j + 1) * chunk)
        pq = jnp.dot(cq, wq_ref[:, cs], preferred_element_type=F32)
        pk = jnp.dot(ckv, wk_ref[:, cs], preferred_element_type=F32)
        for u in range(chunk // LANES):
            us = slice(u * LANES, (u + 1) * LANES)
            os = slice(j * chunk + u * LANES, j * chunk + (u + 1) * LANES)
            qh = pq[:, us]
            if rotate:
                qh = _mla_rope(qh, cos, slo, shi)
            q_ref[0, :, os] = (qh * scale).astype(BF16)
            k_ref[0, :, os] = (pk[:, us] + k_rope).astype(BF16)
    for j in range(wv_ref.shape[1] // chunk):
        pv = jnp.dot(ckv, wv_ref[:, j * chunk:(j + 1) * chunk], preferred_element_type=F32)
        for u in range(chunk // LANES):
            os = slice(j * chunk + u * LANES, j * chunk + (u + 1) * LANES)
            v_ref[0, 0, os, :] = pv[:, u * LANES:(u + 1) * LANES].T.astype(BF16)


def _mla_proj(x, mod, ng, w_in, qg, kvg, wq, wk, wv, cos, slo, shi, rotate):
    b, l, d = x.shape
    n = wq.shape[1]
    q_lora, kv_lora = wq.shape[0], wk.shape[0]
    tm = _row_tile(l, 512)
    kern = functools.partial(_mla_proj_kernel, rotate=rotate, q_lora=q_lora, kv_lora=kv_lora, chunk=512)
    out = jax.ShapeDtypeStruct((b, l, n), BF16)
    ospec = pl.BlockSpec((1, tm, n), lambda i, t: (i, t, 0))
    tab = pl.BlockSpec((tm, LANES), lambda i, t: (t, 0))
    return pl.pallas_call(
        kern,
        grid=(b, l // tm),
        in_specs=[pl.BlockSpec((1, tm, d), lambda i, t: (i, t, 0)),
                  pl.BlockSpec((1, 6, 1, d), lambda i, t: (i % mod.shape[0], 0, 0, 0)),
                  _resident((1, d)),
                  _resident(w_in.shape),
                  _resident((1, q_lora)),
                  _resident((1, kv_lora)),
                  _resident(wq.shape), _resident(wk.shape), _resident(wv.shape),
                  tab, tab, tab],
        out_specs=[ospec, ospec, pl.BlockSpec((1, 1, wv.shape[1], tm), lambda i, t: (i, t, 0, 0))],
        out_shape=[out, out, jax.ShapeDtypeStruct((b, l // tm, wv.shape[1], tm), BF16)],
        compiler_params=_params("parallel", "parallel"),
        name="mla_proj",
    )(x, mod, ng, w_in, qg, kvg, wq, wk, wv, cos, slo, shi)


def _final_kernel(x_ref, g_ref, y_ref):
    y_ref[0] = _rms(x_ref[0], g_ref[...])


def _final_norm(x, gain):
    b, l, d = x.shape
    tm = _row_tile(l, 1024)
    return pl.pallas_call(
        _final_kernel,
        grid=(b, l // tm),
        in_specs=[pl.BlockSpec((1, tm, d), lambda i, t: (i, t, 0)), _resident((1, d))],
        out_specs=pl.BlockSpec((1, tm, d), lambda i, t: (i, t, 0)),
        out_shape=jax.ShapeDtypeStruct((b, l, d), F32),
        compiler_params=_params("parallel", "parallel"),
        name="final_norm",
    )(x, gain)


def _axial_angles(rows, rot_dim):
    row = jnp.repeat(jnp.arange(rows, dtype=F32), GRID_W)
    col = jnp.tile(jnp.arange(GRID_W, dtype=F32), rows)
    axis_dim = rot_dim // 2
    inv_freq = jnp.power(ROPE_THETA, -jnp.arange(0, axis_dim, 2, dtype=F32) / axis_dim)
    ang = jnp.concatenate([row[:, None] * inv_freq, col[:, None] * inv_freq], axis=-1)
    return jnp.cos(ang), jnp.sin(ang)


def _gqa_tables(rows):
    cos, sin = _axial_angles(rows, GQA_HEAD_DIM)
    return jnp.concatenate([cos, cos], axis=-1), jnp.concatenate([-sin, sin], axis=-1)


def _mla_tables(rows):
    cos, sin = _axial_angles(rows, MLA_ROPE)
    s = cos.shape[0]
    half = MLA_ROPE // 2
    ones = jnp.ones((s, MLA_NOPE), F32)
    zeros = jnp.zeros((s, MLA_NOPE), F32)
    tail1 = jnp.ones((s, LANES - MLA_NOPE - MLA_ROPE), F32)
    tail0 = jnp.zeros((s, LANES - MLA_NOPE - MLA_ROPE), F32)
    zh = jnp.zeros((s, half), F32)
    c = jnp.concatenate([ones, cos, cos, tail1], axis=-1)
    s_lo = jnp.concatenate([zeros, -sin, zh, tail0], axis=-1)
    s_hi = jnp.concatenate([zeros, zh, sin, tail0], axis=-1)
    return c, s_lo, s_hi


def _mla_weights(w_in, w_qb, w_kvb, w_out):
    d = w_in.shape[0]
    q_lora, kv_lora = w_qb.shape[0], w_kvb.shape[0]
    heads = w_qb.shape[1] // (MLA_NOPE + MLA_ROPE)
    kr = jnp.zeros((d, LANES), w_in.dtype).at[:, MLA_NOPE:MLA_NOPE + MLA_ROPE].set(w_in[:, q_lora + kv_lora:])
    w_in_p = jnp.concatenate([w_in[:, :q_lora + kv_lora], kr], axis=1)
    wq = w_qb.reshape(q_lora, heads, MLA_NOPE + MLA_ROPE)
    wq = jnp.pad(wq, ((0, 0), (0, 0), (0, LANES - MLA_NOPE - MLA_ROPE))).reshape(q_lora, heads * LANES)
    wkv = w_kvb.reshape(kv_lora, heads, MLA_NOPE + MLA_V)
    wk = jnp.pad(wkv[:, :, :MLA_NOPE], ((0, 0), (0, 0), (0, LANES - MLA_NOPE))).reshape(kv_lora, heads * LANES)
    wv = wkv[:, :, MLA_NOPE:].reshape(kv_lora, heads * MLA_V)
    return w_in_p.astype(BF16), wq.astype(BF16), wk.astype(BF16), wv.astype(BF16), w_out.astype(BF16)


def kernel(x, c, ctx, c_ctx, w_ada, b_ada, norm_mix, norm_ffn, ffn_w_in, ffn_conv_w, ffn_conv_b, ffn_w_out,
           gqa_w_in, gqa_q_norm, gqa_k_norm, gqa_w_out, hgrn_w_in, hgrn_out_norm, hgrn_w_out, hgrn_lower_bounds,
           mla_w_in, mla_q_norm, mla_kv_norm, mla_w_qb, mla_w_kvb, mla_w_out, final_norm):
    batch, seq, d = x.shape
    depth = w_ada.shape[0]
    n_mixers = 3
    rows = seq // GRID_W
    assert batch + 1 <= SUBLANES

    cv = jnp.zeros((SUBLANES, d), F32).at[:batch].set(c).at[batch].set(c_ctx)
    mods = _ada_mods(cv, w_ada, b_ada).reshape(depth, SUBLANES, 6, 1, d)
    lb_all = _lower_bounds(hgrn_lower_bounds)

    cos_a, sin_a = _gqa_tables(rows)
    cos_m, slo_m, shi_m = _mla_tables(rows)

    for i in range(depth):
        last = i == depth - 1
        kind = i % n_mixers
        j = i // n_mixers
        mod = mods[i, :batch]
        mod_c = mods[i, batch:batch + 1]
        ng = norm_mix[i][None, :]

        if kind == 0:
            w_in = gqa_w_in[j].astype(BF16)
            w_out = gqa_w_out[j].astype(BF16)
            qg, kg = gqa_q_norm[j][None, :], gqa_k_norm[j][None, :]
            q, k, v = _gqa_proj(x, mod, ng, w_in, qg, kg, cos_a, sin_a, True)
            n_ctx = ctx.shape[1]
            qc, kc, vc = _gqa_proj(ctx, mod_c, ng, w_in, qg, kg, cos_a[:n_ctx], sin_a[:n_ctx], False)
            o = _attention(q, [(k, v), (kc, vc)], GQA_GROUP, True, 256)
            x = _merge(x, o, w_out, mod)
            if not last:
                oc = _attention(qc, [(kc, vc)], GQA_GROUP, True, 256)
                ctx = _merge(ctx, oc, w_out, mod_c)
        elif kind == 1:
            w_in = hgrn_w_in[j].astype(BF16)
            w_out = hgrn_w_out[j].astype(BF16)
            lb = lb_all[i][None, :]
            og = hgrn_out_norm[j][None, :]
            heads = w_out.shape[0] // HGRN_DK
            qc, vc, kfc, gfc, kbc, gbc, gatec = _hgrn_proj(ctx, mod_c, ng, w_in, lb)
            q, v, kf, gf, kb, gb, gate = _hgrn_proj(x, mod, ng, w_in, lb)
            s0 = jnp.zeros((batch, heads, HGRN_DK, HGRN_DK), F32)
            oc_f, oc_b, s_f, s_b = _hgrn_scan(qc, vc, kfc, gfc, kbc, gbc, s0, s0)
            o_f, o_b, _, _ = _hgrn_scan(q, v, kf, gf, kb, gb, s_f, s_b)
            x = _hgrn_merge(x, o_f, o_b, gate, og, w_out, mod)
            if not last:
                ctx = _hgrn_merge(ctx, oc_f, oc_b, gatec, og, w_out, mod_c)
        else:
            w_in, wq, wk, wv, w_out = _mla_weights(mla_w_in[j], mla_w_qb[j], mla_w_kvb[j], mla_w_out[j])
            qg, kvg = mla_q_norm[j][None, :], mla_kv_norm[j][None, :]
            n_ctx = ctx.shape[1]
            q, k, v = _mla_proj(x, mod, ng, w_in, qg, kvg, wq, wk, wv, cos_m, slo_m, shi_m, True)
            qc, kc, vc = _mla_proj(ctx, mod_c, ng, w_in, qg, kvg, wq, wk, wv,
                                   cos_m[:n_ctx], slo_m[:n_ctx], shi_m[:n_ctx], False)
            o = _attention(q, [(k, v), (kc, vc)], 2, False, 256)
            x = _merge(x, o, w_out, mod)
            if not last:
                oc = _attention(qc, [(kc, vc)], 2, False, 256)
                ctx = _merge(ctx, oc, w_out, mod_c)

        fg = norm_ffn[i][None, :]
        f_in = ffn_w_in[i].astype(BF16)
        f_out = ffn_w_out[i].astype(BF16)
        f_cw = ffn_conv_w[i]
        f_cb = ffn_conv_b[i][None, :]
        x = _conv_ffn(x, mod, fg, f_in, f_cw, f_cb, f_out)
        if not last:
            ctx = _conv_ffn(ctx, mod_c, fg, f_in, f_cw, f_cb, f_out)

    return _final_norm(x, final_norm[None, :])
```

```python
import functools

import numpy as np
import jax
import jax.numpy as jnp
from jax import lax
from jax.experimental import pallas as pl
from jax.experimental.pallas import tpu as pltpu

F32 = jnp.float32
BF16 = jnp.bfloat16
HIGHEST = lax.Precision.HIGHEST

GRID_W = 64
ROPE_THETA = 10000.0
NORM_EPS = 1e-6
CONV_W = 3

LANES = 128
SUBLANES = 8

GQA_HEAD_DIM = 128
GQA_GROUP = 2

HGRN_DK = 128
HGRN_CHUNK = 64
HGRN_PARTS = 5

MLA_NOPE = 64
MLA_ROPE = 32
MLA_V = 64

VMEM_LIMIT = 56 * 1024 * 1024

LOG2_E = 1.4426950408889634
ATTN_UNROLL = 8


def _params(*sem):
    return pltpu.CompilerParams(dimension_semantics=sem, vmem_limit_bytes=VMEM_LIMIT)


def _resident(shape):
    nd = len(shape)
    return pl.BlockSpec(shape, lambda *_: (0,) * nd, pipeline_mode=pl.Buffered(1))


def _silu(x):
    return x / (1.0 + jnp.exp(-x))


def _sigmoid(x):
    return 1.0 / (1.0 + jnp.exp(-x))


def _neg_abs(x):
    bits = lax.bitcast_convert_type(x, jnp.uint32) | jnp.uint32(0x80000000)
    return lax.bitcast_convert_type(bits, F32)


def _rms(x, gain):
    return x * lax.rsqrt(jnp.mean(x * x, axis=-1, keepdims=True) + NORM_EPS) * gain


def _modnorm(x, gain, shift, scale):
    return _rms(x, gain) * (1.0 + scale) + shift


def _aligned_ds(start, size):
    if isinstance(start, int):
        return pl.ds(start, size)
    return pl.ds(pl.multiple_of(start, size), size)


def _row_tile(n, want):
    t = min(n, want)
    while n % t:
        t -= SUBLANES
    assert t > 0 and t % SUBLANES == 0, (n, want)
    return t


def _ada_kernel(cv_ref, w_ref, b_ref, o_ref):
    s = _silu(cv_ref[...])
    o_ref[0] = jnp.dot(s, w_ref[0], precision=HIGHEST, preferred_element_type=F32) + b_ref[0]


def _ada_mods(cv, w_ada, b_ada):
    depth, d, n = w_ada.shape
    tn = 1536
    return pl.pallas_call(
        _ada_kernel,
        grid=(depth, n // tn),
        in_specs=[pl.BlockSpec((SUBLANES, d), lambda i, j: (0, 0)),
                  pl.BlockSpec((1, d, tn), lambda i, j: (i, 0, j)),
                  pl.BlockSpec((1, 1, tn), lambda i, j: (i, 0, j))],
        out_specs=pl.BlockSpec((1, SUBLANES, tn), lambda i, j: (i, 0, j)),
        out_shape=jax.ShapeDtypeStruct((depth, SUBLANES, n), F32),
        compiler_params=_params("arbitrary", "arbitrary"),
        name="ada_mods",
    )(cv, w_ada, b_ada.reshape(depth, 1, n))


def _lb_kernel(x_ref, o_ref):
    depth = x_ref.shape[0]
    rows = [x_ref[i:i + 1, :] for i in range(depth)]
    m = rows[0]
    for r in rows[1:]:
        m = jnp.maximum(m, r)
    e = [jnp.exp(r - m) for r in rows]
    tot = e[0]
    for r in e[1:]:
        tot = tot + r
    p = [r / tot for r in e]
    cum = p[0]
    o_ref[0:1, :] = cum - p[0]
    for i in range(1, depth):
        cum = cum + p[i]
        o_ref[i:i + 1, :] = cum - p[0]


def _lower_bounds(lb_raw):
    return pl.pallas_call(
        _lb_kernel,
        out_shape=jax.ShapeDtypeStruct(lb_raw.shape, F32),
        name="hgrn_lower_bounds",
    )(lb_raw.astype(F32))


def _gqa_proj_kernel(x_ref, mod_ref, ng_ref, w_ref, qg_ref, kg_ref, cos_ref, sin_ref,
                     q_ref, k_ref, v_ref, *, rotate, qd, kd, chunk):
    h = _modnorm(x_ref[0], ng_ref[...], mod_ref[0, 0], mod_ref[0, 1]).astype(BF16)
    qg = qg_ref[...]
    kg = kg_ref[...]
    scale = GQA_HEAD_DIM ** -0.5 * LOG2_E
    if rotate:
        cos = cos_ref[...]
        sin = sin_ref[...]

    def head(p, gain):
        y = _rms(p, gain)
        if rotate:
            y = y * cos + pltpu.roll(y, GQA_HEAD_DIM // 2, 1) * sin
        return y

    n = w_ref.shape[1]
    for j in range(n // chunk):
        p = jnp.dot(h, w_ref[:, j * chunk:(j + 1) * chunk], preferred_element_type=F32)
        for u in range(chunk // LANES):
            col = j * chunk + u * LANES
            ph = p[:, u * LANES:(u + 1) * LANES]
            if col < qd:
                q_ref[0, :, col:col + LANES] = (head(ph, qg) * scale).astype(BF16)
            elif col < qd + kd:
                k_ref[0, :, col - qd:col - qd + LANES] = head(ph, kg).astype(BF16)
            else:
                c0 = col - qd - kd
                v_ref[0, 0, c0:c0 + LANES, :] = ph.T.astype(BF16)


def _gqa_proj(x, mod, ng, w_in, qg, kg, cos, sin, rotate):
    b, l, d = x.shape
    n = w_in.shape[1]
    kd = n // 4
    qd = n - 2 * kd
    tm = _row_tile(l, 512)
    kern = functools.partial(_gqa_proj_kernel, rotate=rotate, qd=qd, kd=kd, chunk=512)
    return pl.pallas_call(
        kern,
        grid=(b, l // tm),
        in_specs=[pl.BlockSpec((1, tm, d), lambda i, t: (i, t, 0)),
                  pl.BlockSpec((1, 6, 1, d), lambda i, t: (i % mod.shape[0], 0, 0, 0)),
                  _resident((1, d)),
                  _resident((d, n)),
                  _resident((1, LANES)),
                  _resident((1, LANES)),
                  pl.BlockSpec((tm, LANES), lambda i, t: (t, 0)),
                  pl.BlockSpec((tm, LANES), lambda i, t: (t, 0))],
        out_specs=[pl.BlockSpec((1, tm, qd), lambda i, t: (i, t, 0)),
                   pl.BlockSpec((1, tm, kd), lambda i, t: (i, t, 0)),
                   pl.BlockSpec((1, 1, kd, tm), lambda i, t: (i, t, 0, 0))],
        out_shape=[jax.ShapeDtypeStruct((b, l, qd), BF16),
                   jax.ShapeDtypeStruct((b, l, kd), BF16),
                   jax.ShapeDtypeStruct((b, l // tm, kd, tm), BF16)],
        compiler_params=_params("parallel", "parallel"),
        name="gqa_proj",
    )(x, mod, ng, w_in, qg, kg, cos, sin)


def _attn_kernel(*refs, group, shared, n_src):
    q_ref = refs[0]
    kv_refs = refs[1:1 + 2 * n_src]
    o_ref = refs[1 + 2 * n_src]
    s_scs = refs[2 + 2 * n_src:4 + 2 * n_src]
    m_sc = refs[4 + 2 * n_src]
    tq = q_ref.shape[1]
    nq = group * tq
    dv = LANES if shared else LANES // group
    step = pl.program_id(2)

    def fold(a):
        return a.reshape(a.shape[0] // SUBLANES, SUBLANES, nq)

    def run(do_scores, do_weigh, slot):
        prev = 1 - slot
        carry = {}
        if do_scores:
            q = q_ref[0]
            q_heads = [q[:, g * LANES:(g + 1) * LANES] for g in range(group)]
            q_all = jnp.concatenate(q_heads, axis=0)
            carry["m8"] = jnp.full((SUBLANES, nq), -jnp.inf, F32)
        if do_weigh:
            m = jnp.max(m_sc[prev], axis=0, keepdims=True)
            carry["l8"] = jnp.zeros((SUBLANES, nq), F32)
            carry["acc"] = jnp.zeros((dv, nq), F32)
        row0 = 0
        for i in range(n_src):
            k_ref, vt_ref = kv_refs[2 * i], kv_refs[2 * i + 1]
            n_chunks, _, c = vt_ref.shape[1:]

            def body(j, carry, k_ref=k_ref, vt_ref=vt_ref, c=c, row0=row0):
                carry = dict(carry)
                rows = _aligned_ds(row0 + j * c, c)
                if do_scores:
                    kc = k_ref[0, _aligned_ds(j * c, c), :]
                    nt_dims = (((1,), (1,)), ((), ()))
                    if shared:
                        s = lax.dot_general(kc, q_all, nt_dims, preferred_element_type=F32)
                    else:
                        s = jnp.concatenate(
                            [lax.dot_general(kc[:, g * LANES:(g + 1) * LANES], q_heads[g], nt_dims,
                                             preferred_element_type=F32) for g in range(group)], axis=1)
                    s_scs[slot][rows, :] = s
                    carry["m8"] = jnp.maximum(carry["m8"], jnp.max(fold(s), axis=0))
                if do_weigh:
                    p = jnp.exp2(s_scs[prev][rows, :] - m)
                    carry["l8"] = carry["l8"] + jnp.sum(fold(p), axis=0)
                    pb = p.astype(BF16)
                    vt = vt_ref[0, j]
                    if shared:
                        upd = jnp.dot(vt, pb, preferred_element_type=F32)
                    else:
                        upd = jnp.concatenate(
                            [jnp.dot(vt[g * dv:(g + 1) * dv], pb[:, g * tq:(g + 1) * tq],
                                     preferred_element_type=F32) for g in range(group)], axis=1)
                    carry["acc"] = carry["acc"] + upd
                return carry

            if n_chunks == 1:
                carry = body(0, carry)
            else:
                carry = lax.fori_loop(0, n_chunks, body, carry, unroll=ATTN_UNROLL)
            row0 += n_chunks * c
        if do_scores:
            m_sc[slot] = carry["m8"]
        if do_weigh:
            o = carry["acc"] / jnp.sum(carry["l8"], axis=0, keepdims=True)
            o_ref[0] = jnp.concatenate([o[:, g * tq:(g + 1) * tq].T for g in range(group)],
                                       axis=1).astype(o_ref.dtype)

    n_tiles = pl.num_programs(2) - 1
    middle = (step > 0) & (step < n_tiles)

    @pl.when(step == 0)
    def _():
        run(True, False, 0)

    for parity in range(2):
        @pl.when(middle & (step % 2 == parity))
        def _(parity=parity):
            run(True, True, parity)

        @pl.when((step == n_tiles) & (n_tiles % 2 == parity))
        def _(parity=parity):
            run(False, True, parity)


def _attention(q, kv_sources, group, shared, tq):
    b, nq, hd = q.shape
    hkv = hd // (group * LANES)
    kw = LANES if shared else group * LANES
    ow = group * LANES if shared else LANES
    tq = _row_tile(nq, tq)
    nt = nq // tq
    n_src = len(kv_sources)
    in_specs = [pl.BlockSpec((1, tq, group * LANES), lambda i, h, t: (i, jnp.minimum(t, nt - 1), h))]
    args = [q]
    nk_total = 0
    for k, vt in kv_sources:
        nk = k.shape[1]
        n_chunks, _, c = vt.shape[1:]
        assert n_chunks * c == nk and nk_total % c == 0
        nk_total += nk
        in_specs.append(pl.BlockSpec((1, nk, kw), lambda i, h, t: (i, 0, h)))
        in_specs.append(pl.BlockSpec((1, n_chunks, LANES, c), lambda i, h, t: (i, 0, h, 0)))
        args += [k, vt]
    return pl.pallas_call(
        functools.partial(_attn_kernel, group=group, shared=shared, n_src=n_src),
        grid=(b, hkv, nt + 1),
        in_specs=in_specs,
        out_specs=pl.BlockSpec((1, tq, ow), lambda i, h, t: (i, jnp.maximum(t - 1, 0), h)),
        out_shape=jax.ShapeDtypeStruct((b, nq, hkv * ow), BF16),
        scratch_shapes=[pltpu.VMEM((nk_total, group * tq), F32),
                        pltpu.VMEM((nk_total, group * tq), F32),
                        pltpu.VMEM((2, SUBLANES, group * tq), F32)],
        compiler_params=_params("parallel", "parallel", "arbitrary"),
        name="flash_attention",
    )(*args)


def _merge_kernel(x_ref, o_ref, w_ref, mod_ref, y_ref):
    y = jnp.dot(o_ref[0], w_ref[...], preferred_element_type=F32)
    y_ref[0] = x_ref[0] + mod_ref[0, 2] * y


def _merge(x, o, w_out, mod):
    b, l, d = x.shape
    ko = o.shape[2]
    tm = _row_tile(l, 512)
    return pl.pallas_call(
        _merge_kernel,
        grid=(b, l // tm),
        in_specs=[pl.BlockSpec((1, tm, d), lambda i, t: (i, t, 0)),
                  pl.BlockSpec((1, tm, ko), lambda i, t: (i, t, 0)),
                  _resident((ko, d)),
                  pl.BlockSpec((1, 6, 1, d), lambda i, t: (i % mod.shape[0], 0, 0, 0))],
        out_specs=pl.BlockSpec((1, tm, d), lambda i, t: (i, t, 0)),
        out_shape=jax.ShapeDtypeStruct((b, l, d), F32),
        compiler_params=_params("parallel", "parallel"),
        name="merge_residual",
    )(x, o, w_out, mod)


def _ffn_kernel(xp_ref, x_ref, xn_ref, mod_ref, ng_ref, win_ref, cw_ref, cb_ref, wout_ref, y_ref,
                *, d_ff, chunk):
    t = pl.program_id(1)
    nt = pl.num_programs(1)
    gain = ng_ref[...]
    shift, scale, gate = mod_ref[0, 3], mod_ref[0, 4], mod_ref[0, 5]
    x = x_ref[0]
    tm = x.shape[0]
    halo = xp_ref.shape[1]
    hp = jnp.where(t > 0, _modnorm(xp_ref[0], gain, shift, scale), 0.0)
    hn = jnp.where(t < nt - 1, _modnorm(xn_ref[0], gain, shift, scale), 0.0)
    h = jnp.concatenate([hp, _modnorm(x, gain, shift, scale), hn], axis=0).astype(BF16)
    rows = tm + 2 * halo

    def conv(p, col):
        w = cw_ref[:, col:col + chunk]
        prev = pltpu.roll(p, 1, 0)[halo:halo + tm]
        nxt = pltpu.roll(p, rows - 1, 0)[halo:halo + tm]
        cur = p[halo:halo + tm]
        return ((cb_ref[:, col:col + chunk] + prev * w[0:1]) + cur * w[1:2]) + nxt * w[2:3]

    acc = jnp.zeros((tm, x.shape[1]), F32)
    for c in range(d_ff // chunk):
        pa = jnp.dot(h, win_ref[:, c * chunk:(c + 1) * chunk], preferred_element_type=F32)
        pv = jnp.dot(h, win_ref[:, d_ff + c * chunk:d_ff + (c + 1) * chunk], preferred_element_type=F32)
        g = _silu(conv(pa, c * chunk)) * conv(pv, d_ff + c * chunk)
        acc = acc + jnp.dot(g.astype(BF16), wout_ref[c * chunk:(c + 1) * chunk, :],
                            preferred_element_type=F32)
    y_ref[0] = x + gate * acc


def _conv_ffn(x, mod, ng, w_in, conv_w, conv_b, w_out):
    b, l, d = x.shape
    d_ff = w_out.shape[0]
    tm = _row_tile(l, 512)
    halo = SUBLANES
    per = tm // halo
    last = l // halo - 1
    kern = functools.partial(_ffn_kernel, d_ff=d_ff, chunk=d_ff)
    return pl.pallas_call(
        kern,
        grid=(b, l // tm),
        in_specs=[pl.BlockSpec((1, halo, d), lambda i, t: (i, jnp.maximum(t * per - 1, 0), 0)),
                  pl.BlockSpec((1, tm, d), lambda i, t: (i, t, 0)),
                  pl.BlockSpec((1, halo, d), lambda i, t: (i, jnp.minimum((t + 1) * per, last), 0)),
                  pl.BlockSpec((1, 6, 1, d), lambda i, t: (i % mod.shape[0], 0, 0, 0)),
                  _resident((1, d)),
                  _resident((d, 2 * d_ff)),
                  _resident((CONV_W, 2 * d_ff)),
                  _resident((1, 2 * d_ff)),
                  _resident((d_ff, d))],
        out_specs=pl.BlockSpec((1, tm, d), lambda i, t: (i, t, 0)),
        out_shape=jax.ShapeDtypeStruct((b, l, d), F32),
        compiler_params=_params("parallel", "arbitrary"),
        name="conv_ffn",
    )(x, x, x, mod, ng, w_in, conv_w, conv_b, w_out)


def _hgrn_proj_kernel(x_ref, mod_ref, ng_ref, w_ref, lb_ref, q_ref, v_ref, kf_ref, gf_ref, kb_ref, gb_ref,
                      gate_ref, *, chunk):
    h = _modnorm(x_ref[0], ng_ref[...], mod_ref[0, 0], mod_ref[0, 1]).astype(BF16)
    width = q_ref.shape[2]
    scale = HGRN_DK ** -0.5
    for part in range(HGRN_PARTS):
        for j in range(width // chunk):
            c0 = j * chunk
            p = jnp.dot(h, w_ref[:, part * width + c0:part * width + c0 + chunk], preferred_element_type=F32)
            if part == 0:
                q_ref[0, :, c0:c0 + chunk] = p * scale
            elif part == 1:
                v_ref[0, :, c0:c0 + chunk] = p
            elif part == 4:
                gate_ref[0, :, c0:c0 + chunk] = p
            else:
                lb = lb_ref[:, c0:c0 + chunk]
                f = lb + (1.0 - lb) * _sigmoid(p)
                k_out, g_out = (kf_ref, gf_ref) if part == 2 else (kb_ref, gb_ref)
                k_out[0, :, c0:c0 + chunk] = 1.0 - f
                g_out[0, :, c0:c0 + chunk] = jnp.log(f)


def _hgrn_proj(x, mod, ng, w_in, lb):
    b, l, d = x.shape
    width = w_in.shape[1] // HGRN_PARTS
    tm = _row_tile(l, 512)
    out = jax.ShapeDtypeStruct((b, l, width), F32)
    ospec = pl.BlockSpec((1, tm, width), lambda i, t: (i, t, 0))
    return pl.pallas_call(
        functools.partial(_hgrn_proj_kernel, chunk=512),
        grid=(b, l // tm),
        in_specs=[pl.BlockSpec((1, tm, d), lambda i, t: (i, t, 0)),
                  pl.BlockSpec((1, 6, 1, d), lambda i, t: (i % mod.shape[0], 0, 0, 0)),
                  _resident((1, d)),
                  _resident((d, HGRN_PARTS * width)),
                  _resident((1, width))],
        out_specs=[ospec] * 7,
        out_shape=[out] * 7,
        compiler_params=_params("parallel", "parallel"),
        name="hgrn_proj",
    )(x, mod, ng, w_in, lb)


_HGRN_LEVELS = (32, 16, 8, 4)


def _hgrn_tri(reverse):
    idx = np.arange(HGRN_CHUNK)
    tri = idx[None, :] >= idx[:, None] if reverse else idx[None, :] <= idx[:, None]
    return jnp.asarray(tri.astype(np.float32), dtype=BF16)


def _hgrn_block(q, k, v, g, tri, st, reverse):
    c = HGRN_CHUNK
    n = q.shape[0] // c

    g = g * LOG2_E
    g_hi = g.astype(BF16)
    rest = g - g_hi.astype(F32)
    g_mid = rest.astype(BF16)
    g_lo = (rest - g_mid.astype(F32)).astype(BF16)
    pieces = [piece[i * c:(i + 1) * c] for i in range(n) for piece in (g_hi, g_mid, g_lo)]
    sums = jnp.dot(tri, jnp.concatenate(pieces, axis=1), preferred_element_type=F32)
    cum = jnp.stack([(sums[:, (3 * i) * LANES:(3 * i + 1) * LANES]
                      + sums[:, (3 * i + 1) * LANES:(3 * i + 2) * LANES])
                     + sums[:, (3 * i + 2) * LANES:(3 * i + 3) * LANES] for i in range(n)], axis=0)

    q3, k3, v3 = (a.reshape(n, c, LANES) for a in (q, k, v))
    tot = cum[:, 0:1, :] if reverse else cum[:, c - 1:c, :]
    row = lax.broadcasted_iota(jnp.int32, (c, c), 0)
    col = lax.broadcasted_iota(jnp.int32, (c, c), 1)
    early, late = (row, col) if reverse else (col, row)

    sub_row = lax.broadcasted_iota(jnp.int32, (c, LANES), 0)
    scores = jnp.zeros((n, c, c), F32)
    for half in _HGRN_LEVELS:
        blk = 2 * half
        cb = cum.reshape(n * (c // blk), blk, LANES)
        b_row = half if reverse else half - 1
        d = cb - cb[:, b_row:b_row + 1, :]
        e = jnp.exp2(_neg_abs(d)).reshape(n, c, LANES)
        query_row = ((sub_row % blk) < half) if reverse else ((sub_row % blk) >= half)
        z = (jnp.where(query_row[None], q3, k3) * e).astype(BF16)
        s_l = lax.dot_general(z, z, (((2,), (2,)), ((0,), (0,))), preferred_element_type=F32)
        pick = ((row // blk) == (col // blk)) & ((early % blk) < half) & ((late % blk) >= half)
        scores = jnp.where(pick[None], s_l, scores)

    sub = _HGRN_LEVELS[-1]
    cum2 = cum.reshape(n * c, LANES)
    prods = [q * k]
    for delta in range(1, sub):
        shift = n * c - delta if reverse else delta
        decay = jnp.exp2(jnp.minimum(cum2 - pltpu.roll(cum2, shift, 0), 0.0))
        prods.append(q * pltpu.roll(k, shift, 0) * decay)
    ones_bf = jnp.ones((LANES, LANES), BF16)
    diag = jnp.dot(jnp.concatenate(prods, axis=0).astype(BF16), ones_bf, preferred_element_type=F32)
    for delta in range(sub):
        d_t = diag[delta * n * c:(delta + 1) * n * c, 0:c].reshape(n, c, c)
        if reverse:
            hit = (col == row + delta) & ((row % sub) + delta < sub)
        else:
            hit = (col == row - delta) & ((row % sub) >= delta)
        scores = jnp.where(hit[None], d_t, scores)

    out = lax.dot_general(scores.astype(BF16), v3.astype(BF16), (((2,), (1,)), ((0,), (0,))),
                          preferred_element_type=F32)
    qe = (q3 * jnp.exp2(cum)).astype(BF16)
    kd = (k3 * jnp.exp2(tot - cum)).astype(BF16)
    e_tot = jnp.exp2(tot)
    v_t = jnp.stack([v3[i].T for i in range(n)], axis=0).astype(BF16)
    upd = lax.dot_general(v_t, kd, (((2,), (1,)), ((0,), (0,))), preferred_element_type=F32)
    states = [None] * n
    for i in (range(n - 1, -1, -1) if reverse else range(n)):
        states[i] = st
        st = st * e_tot[i] + upd[i]
    out = out + lax.dot_general(qe, jnp.stack(states, axis=0).astype(BF16), (((2,), (2,)), ((0,), (0,))),
                                preferred_element_type=F32)
    return out.reshape(n * c, LANES), st


def _hgrn_scan_kernel(qf_ref, vf_ref, kf_ref, gf_ref, qb_ref, vb_ref, kb_ref, gb_ref, tf_ref, tb_ref,
                      sf0_ref, sb0_ref, of_ref, ob_ref, sf_ref, sb_ref, st_sc):
    step = pl.program_id(2)

    @pl.when(step == 0)
    def _():
        st_sc[0] = sf0_ref[0, 0]
        st_sc[1] = sb0_ref[0, 0]

    o, st_f = _hgrn_block(qf_ref[0], kf_ref[0], vf_ref[0], gf_ref[0], tf_ref[...], st_sc[0], False)
    of_ref[0] = o
    o, st_b = _hgrn_block(qb_ref[0], kb_ref[0], vb_ref[0], gb_ref[0], tb_ref[...], st_sc[1], True)
    ob_ref[0] = o
    st_sc[0] = st_f
    st_sc[1] = st_b

    @pl.when(step == pl.num_programs(2) - 1)
    def _():
        sf_ref[0, 0] = st_f
        sb_ref[0, 0] = st_b


def _hgrn_scan(q, v, kf, gf, kb, gb, sf0, sb0):
    b, l, width = q.shape
    heads = width // LANES
    rb = _row_tile(l, 1024)
    nc = l // rb
    fwd = pl.BlockSpec((1, rb, LANES), lambda i, h, s: (i, s, h))
    bwd = pl.BlockSpec((1, rb, LANES), lambda i, h, s: (i, nc - 1 - s, h))
    st_spec = pl.BlockSpec((1, 1, LANES, LANES), lambda i, h, s: (i, h, 0, 0))
    o_shape = jax.ShapeDtypeStruct((b, l, width), F32)
    s_shape = jax.ShapeDtypeStruct((b, heads, LANES, LANES), F32)
    return pl.pallas_call(
        _hgrn_scan_kernel,
        grid=(b, heads, nc),
        in_specs=[fwd, fwd, fwd, fwd, bwd, bwd, bwd, bwd,
                  _resident((HGRN_CHUNK, HGRN_CHUNK)), _resident((HGRN_CHUNK, HGRN_CHUNK)), st_spec, st_spec],
        out_specs=[fwd, bwd, st_spec, st_spec],
        out_shape=[o_shape, o_shape, s_shape, s_shape],
        scratch_shapes=[pltpu.VMEM((2, LANES, LANES), F32)],
        compiler_params=_params("parallel", "parallel", "arbitrary"),
        name="hgrn_scan",
    )(q, v, kf, gf, q, v, kb, gb, _hgrn_tri(False), _hgrn_tri(True), sf0, sb0)


def _hgrn_merge_kernel(x_ref, of_ref, ob_ref, gate_ref, og_ref, w_ref, mod_ref, y_ref):
    width = of_ref.shape[2]
    og = og_ref[...]
    parts = []
    for hd in range(width // LANES):
        sl = slice(hd * LANES, (hd + 1) * LANES)
        o = of_ref[0, :, sl] + ob_ref[0, :, sl]
        parts.append((_rms(o, og) * _silu(gate_ref[0, :, sl])).astype(BF16))
    r = jnp.concatenate(parts, axis=1)
    y = jnp.dot(r, w_ref[...], preferred_element_type=F32)
    y_ref[0] = x_ref[0] + mod_ref[0, 2] * y


def _hgrn_merge(x, o_f, o_b, gate, o_gain, w_out, mod):
    b, l, d = x.shape
    width = o_f.shape[2]
    tm = _row_tile(l, 512)
    wide = pl.BlockSpec((1, tm, width), lambda i, t: (i, t, 0))
    return pl.pallas_call(
        _hgrn_merge_kernel,
        grid=(b, l // tm),
        in_specs=[pl.BlockSpec((1, tm, d), lambda i, t: (i, t, 0)), wide, wide, wide,
                  _resident((1, LANES)),
                  _resident((width, d)),
                  pl.BlockSpec((1, 6, 1, d), lambda i, t: (i % mod.shape[0], 0, 0, 0))],
        out_specs=pl.BlockSpec((1, tm, d), lambda i, t: (i, t, 0)),
        out_shape=jax.ShapeDtypeStruct((b, l, d), F32),
        compiler_params=_params("parallel", "parallel"),
        name="hgrn_merge",
    )(x, o_f, o_b, gate, o_gain, w_out, mod)


def _mla_rope(y, cos, sin_lo, sin_hi):
    half = MLA_ROPE // 2
    return y * cos + pltpu.roll(y, LANES - half, 1) * sin_lo + pltpu.roll(y, half, 1) * sin_hi


def _mla_proj_kernel(x_ref, mod_ref, ng_ref, w_ref, qg_ref, kvg_ref, wq_ref, wk_ref, wv_ref,
                     cos_ref, slo_ref, shi_ref, q_ref, k_ref, v_ref, *, rotate, q_lora, kv_lora, chunk):
    h = _modnorm(x_ref[0], ng_ref[...], mod_ref[0, 0], mod_ref[0, 1]).astype(BF16)
    p = jnp.dot(h, w_ref[...], preferred_element_type=F32)
    cq = _rms(p[:, :q_lora], qg_ref[...]).astype(BF16)
    ckv = _rms(p[:, q_lora:q_lora + kv_lora], kvg_ref[...]).astype(BF16)
    k_rope = p[:, q_lora + kv_lora:]
    scale = (MLA_NOPE + MLA_ROPE) ** -0.5 * LOG2_E
    if rotate:
        cos, slo, shi = cos_ref[...], slo_ref[...], shi_ref[...]
        k_rope = _mla_rope(k_rope, cos, slo, shi)
    n = wq_ref.shape[1]
    for j in range(n // chunk):
        cs = slice(j * chunk, (j + 1) * chunk)
        pq = jnp.dot(cq, wq_ref[:, cs], preferred_element_type=F32)
        pk = jnp.dot(ckv, wk_ref[:, cs], preferred_element_type=F32)
        for u in range(chunk // LANES):
            us = slice(u * LANES, (u + 1) * LANES)
            os = slice(j * chunk + u * LANES, j * chunk + (u + 1) * LANES)
            qh = pq[:, us]
            if rotate:
                qh = _mla_rope(qh, cos, slo, shi)
            q_ref[0, :, os] = (qh * scale).astype(BF16)
            k_ref[0, :, os] = (pk[:, us] + k_rope).astype(BF16)
    for j in range(wv_ref.shape[1] // chunk):
        pv = jnp.dot(ckv, wv_ref[:, j * chunk:(j + 1) * chunk], preferred_element_type=F32)
        for u in range(chunk // LANES):
            os = slice(j * chunk + u * LANES, j * chunk + (u + 1) * LANES)
            v_ref[0, 0, os, :] = pv[:, u * LANES:(u + 1) * LANES].T.astype(BF16)


def _mla_proj(x, mod, ng, w_in, qg, kvg, wq, wk, wv, cos, slo, shi, rotate):
    b, l, d = x.shape
    n = wq.shape[1]
    q_lora, kv_lora = wq.shape[0], wk.shape[0]
    tm = _row_tile(l, 512)
    kern = functools.partial(_mla_proj_kernel, rotate=rotate, q_lora=q_lora, kv_lora=kv_lora, chunk=512)
    out = jax.ShapeDtypeStruct((b, l, n), BF16)
    ospec = pl.BlockSpec((1, tm, n), lambda i, t: (i, t, 0))
    tab = pl.BlockSpec((tm, LANES), lambda i, t: (t, 0))
    return pl.pallas_call(
        kern,
        grid=(b, l // tm),
        in_specs=[pl.BlockSpec((1, tm, d), lambda i, t: (i, t, 0)),
                  pl.BlockSpec((1, 6, 1, d), lambda i, t: (i % mod.shape[0], 0, 0, 0)),
                  _resident((1, d)),
                  _resident(w_in.shape),
                  _resident((1, q_lora)),
                  _resident((1, kv_lora)),
                  _resident(wq.shape), _resident(wk.shape), _resident(wv.shape),
                  tab, tab, tab],
        out_specs=[ospec, ospec, pl.BlockSpec((1, 1, wv.shape[1], tm), lambda i, t: (i, t, 0, 0))],
        out_shape=[out, out, jax.ShapeDtypeStruct((b, l // tm, wv.shape[1], tm), BF16)],
        compiler_params=_params("parallel", "parallel"),
        name="mla_proj",
    )(x, mod, ng, w_in, qg, kvg, wq, wk, wv, cos, slo, shi)


def _final_kernel(x_ref, g_ref, y_ref):
    y_ref[0] = _rms(x_ref[0], g_ref[...])


def _final_norm(x, gain):
    b, l, d = x.shape
    tm = _row_tile(l, 1024)
    return pl.pallas_call(
        _final_kernel,
        grid=(b, l // tm),
        in_specs=[pl.BlockSpec((1, tm, d), lambda i, t: (i, t, 0)), _resident((1, d))],
        out_specs=pl.BlockSpec((1, tm, d), lambda i, t: (i, t, 0)),
        out_shape=jax.ShapeDtypeStruct((b, l, d), F32),
        compiler_params=_params("parallel", "parallel"),
        name="final_norm",
    )(x, gain)


def _axial_angles(rows, rot_dim):
    row = jnp.repeat(jnp.arange(rows, dtype=F32), GRID_W)
    col = jnp.tile(jnp.arange(GRID_W, dtype=F32), rows)
    axis_dim = rot_dim // 2
    inv_freq = jnp.power(ROPE_THETA, -jnp.arange(0, axis_dim, 2, dtype=F32) / axis_dim)
    ang = jnp.concatenate([row[:, None] * inv_freq, col[:, None] * inv_freq], axis=-1)
    return jnp.cos(ang), jnp.sin(ang)


def _gqa_tables(rows):
    cos, sin = _axial_angles(rows, GQA_HEAD_DIM)
    return jnp.concatenate([cos, cos], axis=-1), jnp.concatenate([-sin, sin], axis=-1)


def _mla_tables(rows):
    cos, sin = _axial_angles(rows, MLA_ROPE)
    s = cos.shape[0]
    half = MLA_ROPE // 2
    ones = jnp.ones((s, MLA_NOPE), F32)
    zeros = jnp.zeros((s, MLA_NOPE), F32)
    tail1 = jnp.ones((s, LANES - MLA_NOPE - MLA_ROPE), F32)
    tail0 = jnp.zeros((s, LANES - MLA_NOPE - MLA_ROPE), F32)
    zh = jnp.zeros((s, half), F32)
    c = jnp.concatenate([ones, cos, cos, tail1], axis=-1)
    s_lo = jnp.concatenate([zeros, -sin, zh, tail0], axis=-1)
    s_hi = jnp.concatenate([zeros, zh, sin, tail0], axis=-1)
    return c, s_lo, s_hi


def _mla_weights(w_in, w_qb, w_kvb, w_out):
    d = w_in.shape[0]
    q_lora, kv_lora = w_qb.shape[0], w_kvb.shape[0]
    heads = w_qb.shape[1] // (MLA_NOPE + MLA_ROPE)
    kr = jnp.zeros((d, LANES), w_in.dtype).at[:, MLA_NOPE:MLA_NOPE + MLA_ROPE].set(w_in[:, q_lora + kv_lora:])
    w_in_p = jnp.concatenate([w_in[:, :q_lora + kv_lora], kr], axis=1)
    wq = w_qb.reshape(q_lora, heads, MLA_NOPE + MLA_ROPE)
    wq = jnp.pad(wq, ((0, 0), (0, 0), (0, LANES - MLA_NOPE - MLA_ROPE))).reshape(q_lora, heads * LANES)
    wkv = w_kvb.reshape(kv_lora, heads, MLA_NOPE + MLA_V)
    wk = jnp.pad(wkv[:, :, :MLA_NOPE], ((0, 0), (0, 0), (0, LANES - MLA_NOPE))).reshape(kv_lora, heads * LANES)
    wv = jnp.pad(wkv[:, :, MLA_NOPE:], ((0, 0), (0, 0), (0, LANES - MLA_V))).reshape(kv_lora, heads * LANES)
    wo = w_out.reshape(heads, MLA_V, -1)
    wo = jnp.pad(wo, ((0, 0), (0, LANES - MLA_V), (0, 0))).reshape(heads * LANES, -1)
    return w_in_p.astype(BF16), wq.astype(BF16), wk.astype(BF16), wv.astype(BF16), wo.astype(BF16)


def kernel(x, c, ctx, c_ctx, w_ada, b_ada, norm_mix, norm_ffn, ffn_w_in, ffn_conv_w, ffn_conv_b, ffn_w_out,
           gqa_w_in, gqa_q_norm, gqa_k_norm, gqa_w_out, hgrn_w_in, hgrn_out_norm, hgrn_w_out, hgrn_lower_bounds,
           mla_w_in, mla_q_norm, mla_kv_norm, mla_w_qb, mla_w_kvb, mla_w_out, final_norm):
    batch, seq, d = x.shape
    depth = w_ada.shape[0]
    n_mixers = 3
    rows = seq // GRID_W
    assert batch + 1 <= SUBLANES

    cv = jnp.zeros((SUBLANES, d), F32).at[:batch].set(c).at[batch].set(c_ctx)
    mods = _ada_mods(cv, w_ada, b_ada).reshape(depth, SUBLANES, 6, 1, d)
    lb_all = _lower_bounds(hgrn_lower_bounds)

    cos_a, sin_a = _gqa_tables(rows)
    cos_m, slo_m, shi_m = _mla_tables(rows)

    for i in range(depth):
        last = i == depth - 1
        kind = i % n_mixers
        j = i // n_mixers
        mod = mods[i, :batch]
        mod_c = mods[i, batch:batch + 1]
        ng = norm_mix[i][None, :]

        if kind == 0:
            w_in = gqa_w_in[j].astype(BF16)
            w_out = gqa_w_out[j].astype(BF16)
            qg, kg = gqa_q_norm[j][None, :], gqa_k_norm[j][None, :]
            q, k, v = _gqa_proj(x, mod, ng, w_in, qg, kg, cos_a, sin_a, True)
            n_ctx = ctx.shape[1]
            qc, kc, vc = _gqa_proj(ctx, mod_c, ng, w_in, qg, kg, cos_a[:n_ctx], sin_a[:n_ctx], False)
            o = _attention(q, [(k, v), (kc, vc)], GQA_GROUP, True, 256)
            x = _merge(x, o, w_out, mod)
            if not last:
                oc = _attention(qc, [(kc, vc)], GQA_GROUP, True, 256)
                ctx = _merge(ctx, oc, w_out, mod_c)
        elif kind == 1:
            w_in = hgrn_w_in[j].astype(BF16)
            w_out = hgrn_w_out[j].astype(BF16)
            lb = lb_all[i][None, :]
            og = hgrn_out_norm[j][None, :]
            heads = w_out.shape[0] // HGRN_DK
            qc, vc, kfc, gfc, kbc, gbc, gatec = _hgrn_proj(ctx, mod_c, ng, w_in, lb)
            q, v, kf, gf, kb, gb, gate = _hgrn_proj(x, mod, ng, w_in, lb)
            s0 = jnp.zeros((batch, heads, HGRN_DK, HGRN_DK), F32)
            oc_f, oc_b, s_f, s_b = _hgrn_scan(qc, vc, kfc, gfc, kbc, gbc, s0, s0)
            o_f, o_b, _, _ = _hgrn_scan(q, v, kf, gf, kb, gb, s_f, s_b)
            x = _hgrn_merge(x, o_f, o_b, gate, og, w_out, mod)
            if not last:
                ctx = _hgrn_merge(ctx, oc_f, oc_b, gatec, og, w_out, mod_c)
        else:
            w_in, wq, wk, wv, w_out = _mla_weights(mla_w_in[j], mla_w_qb[j], mla_w_kvb[j], mla_w_out[j])
            qg, kvg = mla_q_norm[j][None, :], mla_kv_norm[j][None, :]
            n_ctx = ctx.shape[1]
            q, k, v = _mla_proj(x, mod, ng, w_in, qg, kvg, wq, wk, wv, cos_m, slo_m, shi_m, True)
            qc, kc, vc = _mla_proj(ctx, mod_c, ng, w_in, qg, kvg, wq, wk, wv,
                                   cos_m[:n_ctx], slo_m[:n_ctx], shi_m[:n_ctx], False)
            o = _attention(q, [(k, v), (kc, vc)], 1, True, 512)
            x = _merge(x, o, w_out, mod)
            if not last:
                oc = _attention(qc, [(kc, vc)], 1, True, 512)
                ctx = _merge(ctx, oc, w_out, mod_c)

        fg = norm_ffn[i][None, :]
        f_in = ffn_w_in[i].astype(BF16)
        f_out = ffn_w_out[i].astype(BF16)
        f_cw = ffn_conv_w[i]
        f_cb = ffn_conv_b[i][None, :]
        x = _conv_ffn(x, mod, fg, f_in, f_cw, f_cb, f_out)
        if not last:
            ctx = _conv_ffn(ctx, mod_c, fg, f_in, f_cw, f_cb, f_out)

    return _final_norm(x, final_norm[None, :])
```

```python
import functools

import numpy as np
import jax
import jax.numpy as jnp
from jax import lax
from jax.experimental import pallas as pl
from jax.experimental.pallas import tpu as pltpu

F32 = jnp.float32
BF16 = jnp.bfloat16
HIGHEST = lax.Precision.HIGHEST

GRID_W = 64
ROPE_THETA = 10000.0
NORM_EPS = 1e-6
CONV_W = 3

LANES = 128
SUBLANES = 8

GQA_HEAD_DIM = 128
GQA_GROUP = 2

HGRN_DK = 128
HGRN_CHUNK = 64
HGRN_PARTS = 5

MLA_NOPE = 64
MLA_ROPE = 32
MLA_V = 64

VMEM_LIMIT = 56 * 1024 * 1024

LOG2_E = 1.4426950408889634
ATTN_UNROLL = 16


def _params(*sem):
    return pltpu.CompilerParams(dimension_semantics=sem, vmem_limit_bytes=VMEM_LIMIT)


def _resident(shape):
    nd = len(shape)
    return pl.BlockSpec(shape, lambda *_: (0,) * nd, pipeline_mode=pl.Buffered(1))


def _silu(x):
    return x / (1.0 + jnp.exp(-x))


def _sigmoid(x):
    return 1.0 / (1.0 + jnp.exp(-x))


def _neg_abs(x):
    bits = lax.bitcast_convert_type(x, jnp.uint32) | jnp.uint32(0x80000000)
    return lax.bitcast_convert_type(bits, F32)


def _rms(x, gain):
    return x * lax.rsqrt(jnp.mean(x * x, axis=-1, keepdims=True) + NORM_EPS) * gain


def _modnorm(x, gain, shift, scale):
    return _rms(x, gain) * (1.0 + scale) + shift


def _aligned_ds(start, size):
    if isinstance(start, int):
        return pl.ds(start, size)
    return pl.ds(pl.multiple_of(start, size), size)


def _row_tile(n, want):
    t = min(n, want)
    while n % t:
        t -= SUBLANES
    assert t > 0 and t % SUBLANES == 0, (n, want)
    return t


def _ada_kernel(cv_ref, w_ref, b_ref, o_ref):
    s = _silu(cv_ref[...])
    o_ref[0] = jnp.dot(s, w_ref[0], precision=HIGHEST, preferred_element_type=F32) + b_ref[0]


def _ada_mods(cv, w_ada, b_ada):
    depth, d, n = w_ada.shape
    tn = 1536
    return pl.pallas_call(
        _ada_kernel,
        grid=(depth, n // tn),
        in_specs=[pl.BlockSpec((SUBLANES, d), lambda i, j: (0, 0)),
                  pl.BlockSpec((1, d, tn), lambda i, j: (i, 0, j)),
                  pl.BlockSpec((1, 1, tn), lambda i, j: (i, 0, j))],
        out_specs=pl.BlockSpec((1, SUBLANES, tn), lambda i, j: (i, 0, j)),
        out_shape=jax.ShapeDtypeStruct((depth, SUBLANES, n), F32),
        compiler_params=_params("arbitrary", "arbitrary"),
        name="ada_mods",
    )(cv, w_ada, b_ada.reshape(depth, 1, n))


def _lb_kernel(x_ref, o_ref):
    depth = x_ref.shape[0]
    rows = [x_ref[i:i + 1, :] for i in range(depth)]
    m = rows[0]
    for r in rows[1:]:
        m = jnp.maximum(m, r)
    e = [jnp.exp(r - m) for r in rows]
    tot = e[0]
    for r in e[1:]:
        tot = tot + r
    p = [r / tot for r in e]
    cum = p[0]
    o_ref[0:1, :] = cum - p[0]
    for i in range(1, depth):
        cum = cum + p[i]
        o_ref[i:i + 1, :] = cum - p[0]


def _lower_bounds(lb_raw):
    return pl.pallas_call(
        _lb_kernel,
        out_shape=jax.ShapeDtypeStruct(lb_raw.shape, F32),
        name="hgrn_lower_bounds",
    )(lb_raw.astype(F32))


def _gqa_proj_kernel(x_ref, mod_ref, ng_ref, w_ref, qg_ref, kg_ref, cos_ref, sin_ref,
                     q_ref, k_ref, v_ref, *, rotate, qd, kd, chunk):
    h = _modnorm(x_ref[0], ng_ref[...], mod_ref[0, 0], mod_ref[0, 1]).astype(BF16)
    scale = GQA_HEAD_DIM ** -0.5 * LOG2_E
    qg = qg_ref[...] * scale
    kg = kg_ref[...]
    if rotate:
        cos = cos_ref[...]
        sin = sin_ref[...]

    def head(p, gain):
        y = _rms(p, gain)
        if rotate:
            y = y * cos + pltpu.roll(y, GQA_HEAD_DIM // 2, 1) * sin
        return y

    n = w_ref.shape[1]
    for j in range(n // chunk):
        p = jnp.dot(h, w_ref[:, j * chunk:(j + 1) * chunk], preferred_element_type=F32)
        for u in range(chunk // LANES):
            col = j * chunk + u * LANES
            ph = p[:, u * LANES:(u + 1) * LANES]
            if col < qd:
                q_ref[0, :, col:col + LANES] = head(ph, qg).astype(BF16)
            elif col < qd + kd:
                k_ref[0, :, col - qd:col - qd + LANES] = head(ph, kg).astype(BF16)
            else:
                c0 = col - qd - kd
                v_ref[0, 0, c0:c0 + LANES, :] = ph.T.astype(BF16)


def _gqa_proj(x, mod, ng, w_in, qg, kg, cos, sin, rotate):
    b, l, d = x.shape
    n = w_in.shape[1]
    kd = n // 4
    qd = n - 2 * kd
    tm = _row_tile(l, 512)
    kern = functools.partial(_gqa_proj_kernel, rotate=rotate, qd=qd, kd=kd, chunk=512)
    return pl.pallas_call(
        kern,
        grid=(b, l // tm),
        in_specs=[pl.BlockSpec((1, tm, d), lambda i, t: (i, t, 0)),
                  pl.BlockSpec((1, 6, 1, d), lambda i, t: (i % mod.shape[0], 0, 0, 0)),
                  _resident((1, d)),
                  _resident((d, n)),
                  _resident((1, LANES)),
                  _resident((1, LANES)),
                  pl.BlockSpec((tm, LANES), lambda i, t: (t, 0)),
                  pl.BlockSpec((tm, LANES), lambda i, t: (t, 0))],
        out_specs=[pl.BlockSpec((1, tm, qd), lambda i, t: (i, t, 0)),
                   pl.BlockSpec((1, tm, kd), lambda i, t: (i, t, 0)),
                   pl.BlockSpec((1, 1, kd, tm), lambda i, t: (i, t, 0, 0))],
        out_shape=[jax.ShapeDtypeStruct((b, l, qd), BF16),
                   jax.ShapeDtypeStruct((b, l, kd), BF16),
                   jax.ShapeDtypeStruct((b, l // tm, kd, tm), BF16)],
        compiler_params=_params("parallel", "parallel"),
        name="gqa_proj",
    )(x, mod, ng, w_in, qg, kg, cos, sin)


def _attn_kernel(*refs, group, shared, n_src):
    q_ref = refs[0]
    kv_refs = refs[1:1 + 2 * n_src]
    o_ref = refs[1 + 2 * n_src]
    s_scs = refs[2 + 2 * n_src:4 + 2 * n_src]
    m_sc = refs[4 + 2 * n_src]
    tq = q_ref.shape[1]
    nq = group * tq
    dv = LANES if shared else LANES // group
    step = pl.program_id(2)

    def fold(a):
        return a.reshape(a.shape[0] // SUBLANES, SUBLANES, nq)

    def run(do_scores, do_weigh, slot):
        prev = 1 - slot
        carry = {}
        if do_scores:
            q = q_ref[0]
            q_heads = [q[:, g * LANES:(g + 1) * LANES] for g in range(group)]
            q_all = jnp.concatenate(q_heads, axis=0)
            carry["m8"] = jnp.full((SUBLANES, nq), -jnp.inf, F32)
        if do_weigh:
            m = jnp.max(m_sc[prev], axis=0, keepdims=True)
            carry["l8"] = jnp.zeros((SUBLANES, nq), F32)
            carry["acc"] = jnp.zeros((dv, nq), F32)
        row0 = 0
        for i in range(n_src):
            k_ref, vt_ref = kv_refs[2 * i], kv_refs[2 * i + 1]
            n_chunks, _, c = vt_ref.shape[1:]

            def body(j, carry, k_ref=k_ref, vt_ref=vt_ref, c=c, row0=row0):
                carry = dict(carry)
                rows = _aligned_ds(row0 + j * c, c)
                if do_scores:
                    kc = k_ref[0, _aligned_ds(j * c, c), :]
                    nt_dims = (((1,), (1,)), ((), ()))
                    if shared:
                        s = lax.dot_general(kc, q_all, nt_dims, preferred_element_type=F32)
                    else:
                        s = jnp.concatenate(
                            [lax.dot_general(kc[:, g * LANES:(g + 1) * LANES], q_heads[g], nt_dims,
                                             preferred_element_type=F32) for g in range(group)], axis=1)
                    s_scs[slot][rows, :] = s
                    carry["m8"] = jnp.maximum(carry["m8"], jnp.max(fold(s), axis=0))
                if do_weigh:
                    p = jnp.exp2(s_scs[prev][rows, :] - m)
                    carry["l8"] = carry["l8"] + jnp.sum(fold(p), axis=0)
                    pb = p.astype(BF16)
                    vt = vt_ref[0, j]
                    if shared:
                        upd = jnp.dot(vt, pb, preferred_element_type=F32)
                    else:
                        upd = jnp.concatenate(
                            [jnp.dot(vt[g * dv:(g + 1) * dv], pb[:, g * tq:(g + 1) * tq],
                                     preferred_element_type=F32) for g in range(group)], axis=1)
                    carry["acc"] = carry["acc"] + upd
                return carry

            if n_chunks == 1:
                carry = body(0, carry)
            else:
                carry = lax.fori_loop(0, n_chunks, body, carry, unroll=ATTN_UNROLL)
            row0 += n_chunks * c
        if do_scores:
            m_sc[slot] = carry["m8"]
        if do_weigh:
            o = carry["acc"] / jnp.sum(carry["l8"], axis=0, keepdims=True)
            o_ref[0] = jnp.concatenate([o[:, g * tq:(g + 1) * tq].T for g in range(group)],
                                       axis=1).astype(o_ref.dtype)

    n_tiles = pl.num_programs(2) - 1
    middle = (step > 0) & (step < n_tiles)

    @pl.when(step == 0)
    def _():
        run(True, False, 0)

    for parity in range(2):
        @pl.when(middle & (step % 2 == parity))
        def _(parity=parity):
            run(True, True, parity)

        @pl.when((step == n_tiles) & (n_tiles % 2 == parity))
        def _(parity=parity):
            run(False, True, parity)


def _attention(q, kv_sources, group, shared, tq):
    b, nq, hd = q.shape
    hkv = hd // (group * LANES)
    kw = LANES if shared else group * LANES
    ow = group * LANES if shared else LANES
    tq = _row_tile(nq, tq)
    nt = nq // tq
    n_src = len(kv_sources)
    in_specs = [pl.BlockSpec((1, tq, group * LANES), lambda i, h, t: (i, jnp.minimum(t, nt - 1), h))]
    args = [q]
    nk_total = 0
    for k, vt in kv_sources:
        nk = k.shape[1]
        n_chunks, _, c = vt.shape[1:]
        assert n_chunks * c == nk and nk_total % c == 0
        nk_total += nk
        in_specs.append(pl.BlockSpec((1, nk, kw), lambda i, h, t: (i, 0, h)))
        in_specs.append(pl.BlockSpec((1, n_chunks, LANES, c), lambda i, h, t: (i, 0, h, 0)))
        args += [k, vt]
    return pl.pallas_call(
        functools.partial(_attn_kernel, group=group, shared=shared, n_src=n_src),
        grid=(b, hkv, nt + 1),
        in_specs=in_specs,
        out_specs=pl.BlockSpec((1, tq, ow), lambda i, h, t: (i, jnp.maximum(t - 1, 0), h)),
        out_shape=jax.ShapeDtypeStruct((b, nq, hkv * ow), BF16),
        scratch_shapes=[pltpu.VMEM((nk_total, group * tq), F32),
                        pltpu.VMEM((nk_total, group * tq), F32),
                        pltpu.VMEM((2, SUBLANES, group * tq), F32)],
        compiler_params=_params("parallel", "parallel", "arbitrary"),
        name="flash_attention",
    )(*args)


def _merge_kernel(x_ref, o_ref, w_ref, mod_ref, y_ref):
    y = jnp.dot(o_ref[0], w_ref[...], preferred_element_type=F32)
    y_ref[0] = x_ref[0] + mod_ref[0, 2] * y


def _merge(x, o, w_out, mod):
    b, l, d = x.shape
    ko = o.shape[2]
    tm = _row_tile(l, 512)
    return pl.pallas_call(
        _merge_kernel,
        grid=(b, l // tm),
        in_specs=[pl.BlockSpec((1, tm, d), lambda i, t: (i, t, 0)),
                  pl.BlockSpec((1, tm, ko), lambda i, t: (i, t, 0)),
                  _resident((ko, d)),
                  pl.BlockSpec((1, 6, 1, d), lambda i, t: (i % mod.shape[0], 0, 0, 0))],
        out_specs=pl.BlockSpec((1, tm, d), lambda i, t: (i, t, 0)),
        out_shape=jax.ShapeDtypeStruct((b, l, d), F32),
        compiler_params=_params("parallel", "parallel"),
        name="merge_residual",
    )(x, o, w_out, mod)


def _ffn_kernel(xp_ref, x_ref, xn_ref, mod_ref, ng_ref, win_ref, cw_ref, cb_ref, wout_ref, og_ref, y_ref,
                *, d_ff, chunk, out_norm):
    t = pl.program_id(1)
    nt = pl.num_programs(1)
    gain = ng_ref[...]
    shift, scale, gate = mod_ref[0, 3], mod_ref[0, 4], mod_ref[0, 5]
    x = x_ref[0]
    tm = x.shape[0]
    halo = xp_ref.shape[1]
    hp = jnp.where(t > 0, _modnorm(xp_ref[0], gain, shift, scale), 0.0)
    hn = jnp.where(t < nt - 1, _modnorm(xn_ref[0], gain, shift, scale), 0.0)
    h = jnp.concatenate([hp, _modnorm(x, gain, shift, scale), hn], axis=0).astype(BF16)
    rows = tm + 2 * halo

    def conv(p, col):
        w = cw_ref[:, col:col + chunk]
        prev = pltpu.roll(p, 1, 0)[halo:halo + tm]
        nxt = pltpu.roll(p, rows - 1, 0)[halo:halo + tm]
        cur = p[halo:halo + tm]
        return ((cb_ref[:, col:col + chunk] + prev * w[0:1]) + cur * w[1:2]) + nxt * w[2:3]

    acc = jnp.zeros((tm, x.shape[1]), F32)
    for c in range(d_ff // chunk):
        pa = jnp.dot(h, win_ref[:, c * chunk:(c + 1) * chunk], preferred_element_type=F32)
        pv = jnp.dot(h, win_ref[:, d_ff + c * chunk:d_ff + (c + 1) * chunk], preferred_element_type=F32)
        g = _silu(conv(pa, c * chunk)) * conv(pv, d_ff + c * chunk)
        acc = acc + jnp.dot(g.astype(BF16), wout_ref[c * chunk:(c + 1) * chunk, :],
                            preferred_element_type=F32)
    y = x + gate * acc
    y_ref[0] = _rms(y, og_ref[...]) if out_norm else y


def _conv_ffn(x, mod, ng, w_in, conv_w, conv_b, w_out, out_gain, out_norm):
    b, l, d = x.shape
    d_ff = w_out.shape[0]
    tm = _row_tile(l, 512)
    halo = SUBLANES
    per = tm // halo
    last = l // halo - 1
    kern = functools.partial(_ffn_kernel, d_ff=d_ff, chunk=d_ff, out_norm=out_norm)
    return pl.pallas_call(
        kern,
        grid=(b, l // tm),
        in_specs=[pl.BlockSpec((1, halo, d), lambda i, t: (i, jnp.maximum(t * per - 1, 0), 0)),
                  pl.BlockSpec((1, tm, d), lambda i, t: (i, t, 0)),
                  pl.BlockSpec((1, halo, d), lambda i, t: (i, jnp.minimum((t + 1) * per, last), 0)),
                  pl.BlockSpec((1, 6, 1, d), lambda i, t: (i % mod.shape[0], 0, 0, 0)),
                  _resident((1, d)),
                  _resident((d, 2 * d_ff)),
                  _resident((CONV_W, 2 * d_ff)),
                  _resident((1, 2 * d_ff)),
                  _resident((d_ff, d)),
                  _resident((1, d))],
        out_specs=pl.BlockSpec((1, tm, d), lambda i, t: (i, t, 0)),
        out_shape=jax.ShapeDtypeStruct((b, l, d), F32),
        compiler_params=_params("parallel", "arbitrary"),
        name="conv_ffn",
    )(x, x, x, mod, ng, w_in, conv_w, conv_b, w_out, out_gain)


def _hgrn_proj_kernel(x_ref, mod_ref, ng_ref, w_ref, lb_ref, q_ref, v_ref, kf_ref, gf_ref, kb_ref, gb_ref,
                      gate_ref, *, chunk):
    h = _modnorm(x_ref[0], ng_ref[...], mod_ref[0, 0], mod_ref[0, 1]).astype(BF16)
    width = q_ref.shape[2]
    scale = HGRN_DK ** -0.5
    for part in range(HGRN_PARTS):
        for j in range(width // chunk):
            c0 = j * chunk
            p = jnp.dot(h, w_ref[:, part * width + c0:part * width + c0 + chunk], preferred_element_type=F32)
            if part == 0:
                q_ref[0, :, c0:c0 + chunk] = p * scale
            elif part == 1:
                v_ref[0, :, c0:c0 + chunk] = p
            elif part == 4:
                gate_ref[0, :, c0:c0 + chunk] = p
            else:
                lb = lb_ref[:, c0:c0 + chunk]
                f = lb + (1.0 - lb) * _sigmoid(p)
                k_out, g_out = (kf_ref, gf_ref) if part == 2 else (kb_ref, gb_ref)
                k_out[0, :, c0:c0 + chunk] = 1.0 - f
                g_out[0, :, c0:c0 + chunk] = jnp.log(f)


def _hgrn_proj(x, mod, ng, w_in, lb):
    b, l, d = x.shape
    width = w_in.shape[1] // HGRN_PARTS
    tm = _row_tile(l, 512)
    out = jax.ShapeDtypeStruct((b, l, width), F32)
    ospec = pl.BlockSpec((1, tm, width), lambda i, t: (i, t, 0))
    return pl.pallas_call(
        functools.partial(_hgrn_proj_kernel, chunk=512),
        grid=(b, l // tm),
        in_specs=[pl.BlockSpec((1, tm, d), lambda i, t: (i, t, 0)),
                  pl.BlockSpec((1, 6, 1, d), lambda i, t: (i % mod.shape[0], 0, 0, 0)),
                  _resident((1, d)),
                  _resident((d, HGRN_PARTS * width)),
                  _resident((1, width))],
        out_specs=[ospec] * 7,
        out_shape=[out] * 7,
        compiler_params=_params("parallel", "parallel"),
        name="hgrn_proj",
    )(x, mod, ng, w_in, lb)


_HGRN_LEVELS = (32, 16, 8, 4)


def _hgrn_tri(reverse):
    idx = np.arange(HGRN_CHUNK)
    tri = idx[None, :] >= idx[:, None] if reverse else idx[None, :] <= idx[:, None]
    return jnp.asarray(tri.astype(np.float32), dtype=BF16)


def _hgrn_block(q, k, v, g, tri, st, reverse):
    c = HGRN_CHUNK
    n = q.shape[0] // c

    g = g * LOG2_E
    g_hi = g.astype(BF16)
    rest = g - g_hi.astype(F32)
    g_mid = rest.astype(BF16)
    g_lo = (rest - g_mid.astype(F32)).astype(BF16)
    pieces = [piece[i * c:(i + 1) * c] for i in range(n) for piece in (g_hi, g_mid, g_lo)]
    sums = jnp.dot(tri, jnp.concatenate(pieces, axis=1), preferred_element_type=F32)
    cum = jnp.stack([(sums[:, (3 * i) * LANES:(3 * i + 1) * LANES]
                      + sums[:, (3 * i + 1) * LANES:(3 * i + 2) * LANES])
                     + sums[:, (3 * i + 2) * LANES:(3 * i + 3) * LANES] for i in range(n)], axis=0)

    q3, k3, v3 = (a.reshape(n, c, LANES) for a in (q, k, v))
    tot = cum[:, 0:1, :] if reverse else cum[:, c - 1:c, :]
    row = lax.broadcasted_iota(jnp.int32, (c, c), 0)
    col = lax.broadcasted_iota(jnp.int32, (c, c), 1)
    early, late = (row, col) if reverse else (col, row)

    sub_row = lax.broadcasted_iota(jnp.int32, (c, LANES), 0)
    scores = jnp.zeros((n, c, c), F32)
    for half in _HGRN_LEVELS:
        blk = 2 * half
        cb = cum.reshape(n * (c // blk), blk, LANES)
        b_row = half if reverse else half - 1
        d = cb - cb[:, b_row:b_row + 1, :]
        e = jnp.exp2(_neg_abs(d)).reshape(n, c, LANES)
        query_row = ((sub_row % blk) < half) if reverse else ((sub_row % blk) >= half)
        z = (jnp.where(query_row[None], q3, k3) * e).astype(BF16)
        s_l = lax.dot_general(z, z, (((2,), (2,)), ((0,), (0,))), preferred_element_type=F32)
        pick = ((row // blk) == (col // blk)) & ((early % blk) < half) & ((late % blk) >= half)
        scores = jnp.where(pick[None], s_l, scores)

    sub = _HGRN_LEVELS[-1]
    cum2 = cum.reshape(n * c, LANES)
    prods = [q * k]
    for delta in range(1, sub):
        shift = n * c - delta if reverse else delta
        decay = jnp.exp2(jnp.minimum(cum2 - pltpu.roll(cum2, shift, 0), 0.0))
        prods.append(q * pltpu.roll(k, shift, 0) * decay)
    ones_bf = jnp.ones((LANES, LANES), BF16)
    diag = jnp.dot(jnp.concatenate(prods, axis=0).astype(BF16), ones_bf, preferred_element_type=F32)
    for delta in range(sub):
        d_t = diag[delta * n * c:(delta + 1) * n * c, 0:c].reshape(n, c, c)
        if reverse:
            hit = (col == row + delta) & ((row % sub) + delta < sub)
        else:
            hit = (col == row - delta) & ((row % sub) >= delta)
        scores = jnp.where(hit[None], d_t, scores)

    out = lax.dot_general(scores.astype(BF16), v3.astype(BF16), (((2,), (1,)), ((0,), (0,))),
                          preferred_element_type=F32)
    qe = (q3 * jnp.exp2(cum)).astype(BF16)
    kd = (k3 * jnp.exp2(tot - cum)).astype(BF16)
    e_tot = jnp.exp2(tot)
    v_t = jnp.stack([v3[i].T for i in range(n)], axis=0).astype(BF16)
    upd = lax.dot_general(v_t, kd, (((2,), (1,)), ((0,), (0,))), preferred_element_type=F32)
    states = [None] * n
    for i in (range(n - 1, -1, -1) if reverse else range(n)):
        states[i] = st
        st = st * e_tot[i] + upd[i]
    out = out + lax.dot_general(qe, jnp.stack(states, axis=0).astype(BF16), (((2,), (2,)), ((0,), (0,))),
                                preferred_element_type=F32)
    return out.reshape(n * c, LANES), st


def _hgrn_scan_kernel(qf_ref, vf_ref, kf_ref, gf_ref, qb_ref, vb_ref, kb_ref, gb_ref, tf_ref, tb_ref,
                      sf0_ref, sb0_ref, of_ref, ob_ref, sf_ref, sb_ref, st_sc):
    step = pl.program_id(2)

    @pl.when(step == 0)
    def _():
        st_sc[0] = sf0_ref[0, 0]
        st_sc[1] = sb0_ref[0, 0]

    o, st_f = _hgrn_block(qf_ref[0], kf_ref[0], vf_ref[0], gf_ref[0], tf_ref[...], st_sc[0], False)
    of_ref[0] = o
    o, st_b = _hgrn_block(qb_ref[0], kb_ref[0], vb_ref[0], gb_ref[0], tb_ref[...], st_sc[1], True)
    ob_ref[0] = o
    st_sc[0] = st_f
    st_sc[1] = st_b

    @pl.when(step == pl.num_programs(2) - 1)
    def _():
        sf_ref[0, 0] = st_f
        sb_ref[0, 0] = st_b


def _hgrn_scan(q, v, kf, gf, kb, gb, sf0, sb0):
    b, l, width = q.shape
    heads = width // LANES
    rb = _row_tile(l, 1024)
    nc = l // rb
    fwd = pl.BlockSpec((1, rb, LANES), lambda i, h, s: (i, s, h))
    bwd = pl.BlockSpec((1, rb, LANES), lambda i, h, s: (i, nc - 1 - s, h))
    st_spec = pl.BlockSpec((1, 1, LANES, LANES), lambda i, h, s: (i, h, 0, 0))
    o_shape = jax.ShapeDtypeStruct((b, l, width), F32)
    s_shape = jax.ShapeDtypeStruct((b, heads, LANES, LANES), F32)
    return pl.pallas_call(
        _hgrn_scan_kernel,
        grid=(b, heads, nc),
        in_specs=[fwd, fwd, fwd, fwd, bwd, bwd, bwd, bwd,
                  _resident((HGRN_CHUNK, HGRN_CHUNK)), _resident((HGRN_CHUNK, HGRN_CHUNK)), st_spec, st_spec],
        out_specs=[fwd, bwd, st_spec, st_spec],
        out_shape=[o_shape, o_shape, s_shape, s_shape],
        scratch_shapes=[pltpu.VMEM((2, LANES, LANES), F32)],
        compiler_params=_params("parallel", "parallel", "arbitrary"),
        name="hgrn_scan",
    )(q, v, kf, gf, q, v, kb, gb, _hgrn_tri(False), _hgrn_tri(True), sf0, sb0)


def _hgrn_merge_kernel(x_ref, of_ref, ob_ref, gate_ref, og_ref, w_ref, mod_ref, y_ref):
    width = of_ref.shape[2]
    og = og_ref[...]
    parts = []
    for hd in range(width // LANES):
        sl = slice(hd * LANES, (hd + 1) * LANES)
        o = of_ref[0, :, sl] + ob_ref[0, :, sl]
        parts.append((_rms(o, og) * _silu(gate_ref[0, :, sl])).astype(BF16))
    r = jnp.concatenate(parts, axis=1)
    y = jnp.dot(r, w_ref[...], preferred_element_type=F32)
    y_ref[0] = x_ref[0] + mod_ref[0, 2] * y


def _hgrn_merge(x, o_f, o_b, gate, o_gain, w_out, mod):
    b, l, d = x.shape
    width = o_f.shape[2]
    tm = _row_tile(l, 512)
    wide = pl.BlockSpec((1, tm, width), lambda i, t: (i, t, 0))
    return pl.pallas_call(
        _hgrn_merge_kernel,
        grid=(b, l // tm),
        in_specs=[pl.BlockSpec((1, tm, d), lambda i, t: (i, t, 0)), wide, wide, wide,
                  _resident((1, LANES)),
                  _resident((width, d)),
                  pl.BlockSpec((1, 6, 1, d), lambda i, t: (i % mod.shape[0], 0, 0, 0))],
        out_specs=pl.BlockSpec((1, tm, d), lambda i, t: (i, t, 0)),
        out_shape=jax.ShapeDtypeStruct((b, l, d), F32),
        compiler_params=_params("parallel", "parallel"),
        name="hgrn_merge",
    )(x, o_f, o_b, gate, o_gain, w_out, mod)


def _mla_rope(y, cos, sin_lo, sin_hi):
    half = MLA_ROPE // 2
    return y * cos + pltpu.roll(y, LANES - half, 1) * sin_lo + pltpu.roll(y, half, 1) * sin_hi


def _mla_proj_kernel(x_ref, mod_ref, ng_ref, w_ref, qg_ref, kvg_ref, wq_ref, wk_ref, wv_ref,
                     cos_ref, slo_ref, shi_ref, q_ref, k_ref, v_ref, *, rotate, q_lora, kv_lora, chunk):
    h = _modnorm(x_ref[0], ng_ref[...], mod_ref[0, 0], mod_ref[0, 1]).astype(BF16)
    p = jnp.dot(h, w_ref[...], preferred_element_type=F32)
    cq = _rms(p[:, :q_lora], qg_ref[...]).astype(BF16)
    ckv = _rms(p[:, q_lora:q_lora + kv_lora], kvg_ref[...]).astype(BF16)
    k_rope = p[:, q_lora + kv_lora:]
    scale = (MLA_NOPE + MLA_ROPE) ** -0.5 * LOG2_E
    if rotate:
        cos, slo, shi = cos_ref[...], slo_ref[...], shi_ref[...]
        k_rope = _mla_rope(k_rope, cos, slo, shi)
    n = wq_ref.shape[1]
    for j in range(n // chunk):
        cs = slice(j * chunk, (j + 1) * chunk)
        pq = jnp.dot(cq, wq_ref[:, cs], preferred_element_type=F32)
        pk = jnp.dot(ckv, wk_ref[:, cs], preferred_element_type=F32)
        for u in range(chunk // LANES):
            us = slice(u * LANES, (u + 1) * LANES)
            os = slice(j * chunk + u * LANES, j * chunk + (u + 1) * LANES)
            qh = pq[:, us]
            if rotate:
                qh = _mla_rope(qh, cos, slo, shi)
            q_ref[0, :, os] = (qh * scale).astype(BF16)
            k_ref[0, :, os] = (pk[:, us] + k_rope).astype(BF16)
    for j in range(wv_ref.shape[1] // chunk):
        pv = jnp.dot(ckv, wv_ref[:, j * chunk:(j + 1) * chunk], preferred_element_type=F32)
        for u in range(chunk // LANES):
            os = slice(j * chunk + u * LANES, j * chunk + (u + 1) * LANES)
            v_ref[0, 0, os, :] = pv[:, u * LANES:(u + 1) * LANES].T.astype(BF16)


def _mla_proj(x, mod, ng, w_in, qg, kvg, wq, wk, wv, cos, slo, shi, rotate):
    b, l, d = x.shape
    n = wq.shape[1]
    q_lora, kv_lora = wq.shape[0], wk.shape[0]
    tm = _row_tile(l, 512)
    kern = functools.partial(_mla_proj_kernel, rotate=rotate, q_lora=q_lora, kv_lora=kv_lora, chunk=512)
    out = jax.ShapeDtypeStruct((b, l, n), BF16)
    ospec = pl.BlockSpec((1, tm, n), lambda i, t: (i, t, 0))
    tab = pl.BlockSpec((tm, LANES), lambda i, t: (t, 0))
    return pl.pallas_call(
        kern,
        grid=(b, l // tm),
        in_specs=[pl.BlockSpec((1, tm, d), lambda i, t: (i, t, 0)),
                  pl.BlockSpec((1, 6, 1, d), lambda i, t: (i % mod.shape[0], 0, 0, 0)),
                  _resident((1, d)),
                  _resident(w_in.shape),
                  _resident((1, q_lora)),
                  _resident((1, kv_lora)),
                  _resident(wq.shape), _resident(wk.shape), _resident(wv.shape),
                  tab, tab, tab],
        out_specs=[ospec, ospec, pl.BlockSpec((1, 1, wv.shape[1], tm), lambda i, t: (i, t, 0, 0))],
        out_shape=[out, out, jax.ShapeDtypeStruct((b, l // tm, wv.shape[1], tm), BF16)],
        compiler_params=_params("parallel", "parallel"),
        name="mla_proj",
    )(x, mod, ng, w_in, qg, kvg, wq, wk, wv, cos, slo, shi)


def _axial_angles(rows, rot_dim):
    row = jnp.repeat(jnp.arange(rows, dtype=F32), GRID_W)
    col = jnp.tile(jnp.arange(GRID_W, dtype=F32), rows)
    axis_dim = rot_dim // 2
    inv_freq = jnp.power(ROPE_THETA, -jnp.arange(0, axis_dim, 2, dtype=F32) / axis_dim)
    ang = jnp.concatenate([row[:, None] * inv_freq, col[:, None] * inv_freq], axis=-1)
    return jnp.cos(ang), jnp.sin(ang)


def _gqa_tables(rows):
    cos, sin = _axial_angles(rows, GQA_HEAD_DIM)
    return jnp.concatenate([cos, cos], axis=-1), jnp.concatenate([-sin, sin], axis=-1)


def _mla_tables(rows):
    cos, sin = _axial_angles(rows, MLA_ROPE)
    s = cos.shape[0]
    half = MLA_ROPE // 2
    ones = jnp.ones((s, MLA_NOPE), F32)
    zeros = jnp.zeros((s, MLA_NOPE), F32)
    tail1 = jnp.ones((s, LANES - MLA_NOPE - MLA_ROPE), F32)
    tail0 = jnp.zeros((s, LANES - MLA_NOPE - MLA_ROPE), F32)
    zh = jnp.zeros((s, half), F32)
    c = jnp.concatenate([ones, cos, cos, tail1], axis=-1)
    s_lo = jnp.concatenate([zeros, -sin, zh, tail0], axis=-1)
    s_hi = jnp.concatenate([zeros, zh, sin, tail0], axis=-1)
    return c, s_lo, s_hi


def _mla_weights(w_in, w_qb, w_kvb, w_out):
    d = w_in.shape[0]
    q_lora, kv_lora = w_qb.shape[0], w_kvb.shape[0]
    heads = w_qb.shape[1] // (MLA_NOPE + MLA_ROPE)
    kr = jnp.zeros((d, LANES), w_in.dtype).at[:, MLA_NOPE:MLA_NOPE + MLA_ROPE].set(w_in[:, q_lora + kv_lora:])
    w_in_p = jnp.concatenate([w_in[:, :q_lora + kv_lora], kr], axis=1)
    wq = w_qb.reshape(q_lora, heads, MLA_NOPE + MLA_ROPE)
    wq = jnp.pad(wq, ((0, 0), (0, 0), (0, LANES - MLA_NOPE - MLA_ROPE))).reshape(q_lora, heads * LANES)
    wkv = w_kvb.reshape(kv_lora, heads, MLA_NOPE + MLA_V)
    wk = jnp.pad(wkv[:, :, :MLA_NOPE], ((0, 0), (0, 0), (0, LANES - MLA_NOPE))).reshape(kv_lora, heads * LANES)
    wv = jnp.pad(wkv[:, :, MLA_NOPE:], ((0, 0), (0, 0), (0, LANES - MLA_V))).reshape(kv_lora, heads * LANES)
    wo = w_out.reshape(heads, MLA_V, -1)
    wo = jnp.pad(wo, ((0, 0), (0, LANES - MLA_V), (0, 0))).reshape(heads * LANES, -1)
    return w_in_p.astype(BF16), wq.astype(BF16), wk.astype(BF16), wv.astype(BF16), wo.astype(BF16)


def kernel(x, c, ctx, c_ctx, w_ada, b_ada, norm_mix, norm_ffn, ffn_w_in, ffn_conv_w, ffn_conv_b, ffn_w_out,
           gqa_w_in, gqa_q_norm, gqa_k_norm, gqa_w_out, hgrn_w_in, hgrn_out_norm, hgrn_w_out, hgrn_lower_bounds,
           mla_w_in, mla_q_norm, mla_kv_norm, mla_w_qb, mla_w_kvb, mla_w_out, final_norm):
    batch, seq, d = x.shape
    depth = w_ada.shape[0]
    n_mixers = 3
    rows = seq // GRID_W
    assert batch + 1 <= SUBLANES

    cv = jnp.zeros((SUBLANES, d), F32).at[:batch].set(c).at[batch].set(c_ctx)
    mods = _ada_mods(cv, w_ada, b_ada).reshape(depth, SUBLANES, 6, 1, d)
    lb_all = _lower_bounds(hgrn_lower_bounds)

    cos_a, sin_a = _gqa_tables(rows)
    cos_m, slo_m, shi_m = _mla_tables(rows)

    for i in range(depth):
        last = i == depth - 1
        kind = i % n_mixers
        j = i // n_mixers
        mod = mods[i, :batch]
        mod_c = mods[i, batch:batch + 1]
        ng = norm_mix[i][None, :]

        if kind == 0:
            w_in = gqa_w_in[j].astype(BF16)
            w_out = gqa_w_out[j].astype(BF16)
            qg, kg = gqa_q_norm[j][None, :], gqa_k_norm[j][None, :]
            q, k, v = _gqa_proj(x, mod, ng, w_in, qg, kg, cos_a, sin_a, True)
            n_ctx = ctx.shape[1]
            qc, kc, vc = _gqa_proj(ctx, mod_c, ng, w_in, qg, kg, cos_a[:n_ctx], sin_a[:n_ctx], False)
            o = _attention(q, [(k, v), (kc, vc)], GQA_GROUP, True, 256)
            x = _merge(x, o, w_out, mod)
            if not last:
                oc = _attention(qc, [(kc, vc)], GQA_GROUP, True, 256)
                ctx = _merge(ctx, oc, w_out, mod_c)
        elif kind == 1:
            w_in = hgrn_w_in[j].astype(BF16)
            w_out = hgrn_w_out[j].astype(BF16)
            lb = lb_all[i][None, :]
            og = hgrn_out_norm[j][None, :]
            heads = w_out.shape[0] // HGRN_DK
            qc, vc, kfc, gfc, kbc, gbc, gatec = _hgrn_proj(ctx, mod_c, ng, w_in, lb)
            q, v, kf, gf, kb, gb, gate = _hgrn_proj(x, mod, ng, w_in, lb)
            s0 = jnp.zeros((batch, heads, HGRN_DK, HGRN_DK), F32)
            oc_f, oc_b, s_f, s_b = _hgrn_scan(qc, vc, kfc, gfc, kbc, gbc, s0, s0)
            o_f, o_b, _, _ = _hgrn_scan(q, v, kf, gf, kb, gb, s_f, s_b)
            x = _hgrn_merge(x, o_f, o_b, gate, og, w_out, mod)
            if not last:
                ctx = _hgrn_merge(ctx, oc_f, oc_b, gatec, og, w_out, mod_c)
        else:
            w_in, wq, wk, wv, w_out = _mla_weights(mla_w_in[j], mla_w_qb[j], mla_w_kvb[j], mla_w_out[j])
            qg, kvg = mla_q_norm[j][None, :], mla_kv_norm[j][None, :]
            n_ctx = ctx.shape[1]
            q, k, v = _mla_proj(x, mod, ng, w_in, qg, kvg, wq, wk, wv, cos_m, slo_m, shi_m, True)
            qc, kc, vc = _mla_proj(ctx, mod_c, ng, w_in, qg, kvg, wq, wk, wv,
                                   cos_m[:n_ctx], slo_m[:n_ctx], shi_m[:n_ctx], False)
            o = _attention(q, [(k, v), (kc, vc)], 1, True, 512)
            x = _merge(x, o, w_out, mod)
            if not last:
                oc = _attention(qc, [(kc, vc)], 1, True, 512)
                ctx = _merge(ctx, oc, w_out, mod_c)

        fg = norm_ffn[i][None, :]
        f_in = ffn_w_in[i].astype(BF16)
        f_out = ffn_w_out[i].astype(BF16)
        f_cw = ffn_conv_w[i]
        f_cb = ffn_conv_b[i][None, :]
        x = _conv_ffn(x, mod, fg, f_in, f_cw, f_cb, f_out, final_norm[None, :], last)
        if not last:
            ctx = _conv_ffn(ctx, mod_c, fg, f_in, f_cw, f_cb, f_out, final_norm[None, :], False)

    return x
```

```python
import functools

import numpy as np
import jax
import jax.numpy as jnp
from jax import lax
from jax.experimental import pallas as pl
from jax.experimental.pallas import tpu as pltpu

F32 = jnp.float32
BF16 = jnp.bfloat16
HIGHEST = lax.Precision.HIGHEST

GRID_W = 64
ROPE_THETA = 10000.0
NORM_EPS = 1e-6
CONV_W = 3

LANES = 128
SUBLANES = 8

GQA_HEAD_DIM = 128
GQA_GROUP = 2

HGRN_DK = 128
HGRN_CHUNK = 64
HGRN_PARTS = 5

MLA_NOPE = 64
MLA_ROPE = 32
MLA_V = 64

VMEM_LIMIT = 56 * 1024 * 1024

LOG2_E = 1.4426950408889634
ATTN_UNROLL = 16


def _params(*sem):
    return pltpu.CompilerParams(dimension_semantics=sem, vmem_limit_bytes=VMEM_LIMIT)


def _resident(shape):
    nd = len(shape)
    return pl.BlockSpec(shape, lambda *_: (0,) * nd, pipeline_mode=pl.Buffered(1))


def _silu(x):
    return x / (1.0 + jnp.exp(-x))


def _sigmoid(x):
    return 1.0 / (1.0 + jnp.exp(-x))


def _neg_abs(x):
    bits = lax.bitcast_convert_type(x, jnp.uint32) | jnp.uint32(0x80000000)
    return lax.bitcast_convert_type(bits, F32)


def _rms(x, gain):
    return x * lax.rsqrt(jnp.mean(x * x, axis=-1, keepdims=True) + NORM_EPS) * gain


def _modnorm(x, gain, shift, scale):
    return _rms(x, gain) * (1.0 + scale) + shift


def _aligned_ds(start, size):
    if isinstance(start, int):
        return pl.ds(start, size)
    return pl.ds(pl.multiple_of(start, size), size)


def _row_tile(n, want):
    t = min(n, want)
    while n % t:
        t -= SUBLANES
    assert t > 0 and t % SUBLANES == 0, (n, want)
    return t


def _ada_kernel(cv_ref, w_ref, b_ref, o_ref):
    s = _silu(cv_ref[...])
    o_ref[0] = jnp.dot(s, w_ref[0], precision=HIGHEST, preferred_element_type=F32) + b_ref[0]


def _ada_mods(cv, w_ada, b_ada):
    depth, d, n = w_ada.shape
    tn = 1536
    return pl.pallas_call(
        _ada_kernel,
        grid=(depth, n // tn),
        in_specs=[pl.BlockSpec((SUBLANES, d), lambda i, j: (0, 0)),
                  pl.BlockSpec((1, d, tn), lambda i, j: (i, 0, j)),
                  pl.BlockSpec((1, 1, tn), lambda i, j: (i, 0, j))],
        out_specs=pl.BlockSpec((1, SUBLANES, tn), lambda i, j: (i, 0, j)),
        out_shape=jax.ShapeDtypeStruct((depth, SUBLANES, n), F32),
        compiler_params=_params("arbitrary", "arbitrary"),
        name="ada_mods",
    )(cv, w_ada, b_ada.reshape(depth, 1, n))


def _lb_kernel(x_ref, o_ref):
    depth = x_ref.shape[0]
    rows = [x_ref[i:i + 1, :] for i in range(depth)]
    m = rows[0]
    for r in rows[1:]:
        m = jnp.maximum(m, r)
    e = [jnp.exp(r - m) for r in rows]
    tot = e[0]
    for r in e[1:]:
        tot = tot + r
    p = [r / tot for r in e]
    cum = p[0]
    o_ref[0:1, :] = cum - p[0]
    for i in range(1, depth):
        cum = cum + p[i]
        o_ref[i:i + 1, :] = cum - p[0]


def _lower_bounds(lb_raw):
    return pl.pallas_call(
        _lb_kernel,
        out_shape=jax.ShapeDtypeStruct(lb_raw.shape, F32),
        name="hgrn_lower_bounds",
    )(lb_raw.astype(F32))


def _gqa_proj_kernel(x_ref, mod_ref, ng_ref, w_ref, qg_ref, kg_ref, cos_ref, sin_ref,
                     q_ref, k_ref, v_ref, *, rotate, qd, kd, chunk):
    h = _modnorm(x_ref[0], ng_ref[...], mod_ref[0, 0], mod_ref[0, 1]).astype(BF16)
    scale = GQA_HEAD_DIM ** -0.5 * LOG2_E
    qg = qg_ref[...] * scale
    kg = kg_ref[...]
    if rotate:
        cos = cos_ref[...]
        sin = sin_ref[...]

    ones_bf = jnp.ones((LANES, LANES), BF16)

    def head(p, gain):
        ss = jnp.dot((p * p).astype(BF16), ones_bf, preferred_element_type=F32)
        y = p * lax.rsqrt(ss * (1.0 / GQA_HEAD_DIM) + NORM_EPS) * gain
        if rotate:
            y = y * cos + pltpu.roll(y, GQA_HEAD_DIM // 2, 1) * sin
        return y

    n = w_ref.shape[1]
    for j in range(n // chunk):
        p = jnp.dot(h, w_ref[:, j * chunk:(j + 1) * chunk], preferred_element_type=F32)
        for u in range(chunk // LANES):
            col = j * chunk + u * LANES
            ph = p[:, u * LANES:(u + 1) * LANES]
            if col < qd:
                q_ref[0, :, col:col + LANES] = head(ph, qg).astype(BF16)
            elif col < qd + kd:
                k_ref[0, :, col - qd:col - qd + LANES] = head(ph, kg).astype(BF16)
            else:
                c0 = col - qd - kd
                v_ref[0, 0, c0:c0 + LANES, :] = ph.T.astype(BF16)


def _gqa_proj(x, mod, ng, w_in, qg, kg, cos, sin, rotate):
    b, l, d = x.shape
    n = w_in.shape[1]
    kd = n // 4
    qd = n - 2 * kd
    tm = _row_tile(l, 512)
    kern = functools.partial(_gqa_proj_kernel, rotate=rotate, qd=qd, kd=kd, chunk=512)
    return pl.pallas_call(
        kern,
        grid=(b, l // tm),
        in_specs=[pl.BlockSpec((1, tm, d), lambda i, t: (i, t, 0)),
                  pl.BlockSpec((1, 6, 1, d), lambda i, t: (i % mod.shape[0], 0, 0, 0)),
                  _resident((1, d)),
                  _resident((d, n)),
                  _resident((1, LANES)),
                  _resident((1, LANES)),
                  pl.BlockSpec((tm, LANES), lambda i, t: (t, 0)),
                  pl.BlockSpec((tm, LANES), lambda i, t: (t, 0))],
        out_specs=[pl.BlockSpec((1, tm, qd), lambda i, t: (i, t, 0)),
                   pl.BlockSpec((1, tm, kd), lambda i, t: (i, t, 0)),
                   pl.BlockSpec((1, 1, kd, tm), lambda i, t: (i, t, 0, 0))],
        out_shape=[jax.ShapeDtypeStruct((b, l, qd), BF16),
                   jax.ShapeDtypeStruct((b, l, kd), BF16),
                   jax.ShapeDtypeStruct((b, l // tm, kd, tm), BF16)],
        compiler_params=_params("parallel", "parallel"),
        name="gqa_proj",
    )(x, mod, ng, w_in, qg, kg, cos, sin)


def _attn_kernel(*refs, group, shared, n_src):
    q_ref = refs[0]
    kv_refs = refs[1:1 + 2 * n_src]
    o_ref = refs[1 + 2 * n_src]
    s_scs = refs[2 + 2 * n_src:4 + 2 * n_src]
    m_sc = refs[4 + 2 * n_src]
    tq = q_ref.shape[1]
    nq = group * tq
    dv = LANES if shared else LANES // group
    step = pl.program_id(2)

    def fold(a):
        return a.reshape(a.shape[0] // SUBLANES, SUBLANES, nq)

    def run(do_scores, do_weigh, slot):
        prev = 1 - slot
        carry = {}
        if do_scores:
            q = q_ref[0]
            q_heads = [q[:, g * LANES:(g + 1) * LANES] for g in range(group)]
            q_all = jnp.concatenate(q_heads, axis=0)
            carry["m8"] = jnp.full((SUBLANES, nq), -jnp.inf, F32)
        if do_weigh:
            m = jnp.max(m_sc[prev], axis=0, keepdims=True)
            carry["l8"] = jnp.zeros((SUBLANES, nq), F32)
            carry["acc"] = jnp.zeros((dv, nq), F32)
        row0 = 0
        for i in range(n_src):
            k_ref, vt_ref = kv_refs[2 * i], kv_refs[2 * i + 1]
            n_chunks, _, c = vt_ref.shape[1:]

            def body(j, carry, k_ref=k_ref, vt_ref=vt_ref, c=c, row0=row0):
                carry = dict(carry)
                rows = _aligned_ds(row0 + j * c, c)
                if do_scores:
                    kc = k_ref[0, _aligned_ds(j * c, c), :]
                    nt_dims = (((1,), (1,)), ((), ()))
                    if shared:
                        s = lax.dot_general(kc, q_all, nt_dims, preferred_element_type=F32)
                    else:
                        s = jnp.concatenate(
                            [lax.dot_general(kc[:, g * LANES:(g + 1) * LANES], q_heads[g], nt_dims,
                                             preferred_element_type=F32) for g in range(group)], axis=1)
                    s_scs[slot][rows, :] = s
                    carry["m8"] = jnp.maximum(carry["m8"], jnp.max(fold(s), axis=0))
                if do_weigh:
                    p = jnp.exp2(s_scs[prev][rows, :] - m)
                    carry["l8"] = carry["l8"] + jnp.sum(fold(p), axis=0)
                    pb = p.astype(BF16)
                    vt = vt_ref[0, j]
                    if shared:
                        upd = jnp.dot(vt, pb, preferred_element_type=F32)
                    else:
                        upd = jnp.concatenate(
                            [jnp.dot(vt[g * dv:(g + 1) * dv], pb[:, g * tq:(g + 1) * tq],
                                     preferred_element_type=F32) for g in range(group)], axis=1)
                    carry["acc"] = carry["acc"] + upd
                return carry

            if n_chunks == 1:
                carry = body(0, carry)
            else:
                carry = lax.fori_loop(0, n_chunks, body, carry, unroll=ATTN_UNROLL)
            row0 += n_chunks * c
        if do_scores:
            m_sc[slot] = carry["m8"]
        if do_weigh:
            o = carry["acc"] / jnp.sum(carry["l8"], axis=0, keepdims=True)
            o_ref[0] = jnp.concatenate([o[:, g * tq:(g + 1) * tq].T for g in range(group)],
                                       axis=1).astype(o_ref.dtype)

    n_tiles = pl.num_programs(2) - 1
    middle = (step > 0) & (step < n_tiles)

    @pl.when(step == 0)
    def _():
        run(True, False, 0)

    for parity in range(2):
        @pl.when(middle & (step % 2 == parity))
        def _(parity=parity):
            run(True, True, parity)

        @pl.when((step == n_tiles) & (n_tiles % 2 == parity))
        def _(parity=parity):
            run(False, True, parity)


def _attention(q, kv_sources, group, shared, tq):
    b, nq, hd = q.shape
    hkv = hd // (group * LANES)
    kw = LANES if shared else group * LANES
    ow = group * LANES if shared else LANES
    tq = _row_tile(nq, tq)
    nt = nq // tq
    n_src = len(kv_sources)
    in_specs = [pl.BlockSpec((1, tq, group * LANES), lambda i, h, t: (i, jnp.minimum(t, nt - 1), h))]
    args = [q]
    nk_total = 0
    for k, vt in kv_sources:
        nk = k.shape[1]
        n_chunks, _, c = vt.shape[1:]
        assert n_chunks * c == nk and nk_total % c == 0
        nk_total += nk
        in_specs.append(pl.BlockSpec((1, nk, kw), lambda i, h, t: (i, 0, h)))
        in_specs.append(pl.BlockSpec((1, n_chunks, LANES, c), lambda i, h, t: (i, 0, h, 0)))
        args += [k, vt]
    return pl.pallas_call(
        functools.partial(_attn_kernel, group=group, shared=shared, n_src=n_src),
        grid=(b, hkv, nt + 1),
        in_specs=in_specs,
        out_specs=pl.BlockSpec((1, tq, ow), lambda i, h, t: (i, jnp.maximum(t - 1, 0), h)),
        out_shape=jax.ShapeDtypeStruct((b, nq, hkv * ow), BF16),
        scratch_shapes=[pltpu.VMEM((nk_total, group * tq), F32),
                        pltpu.VMEM((nk_total, group * tq), F32),
                        pltpu.VMEM((2, SUBLANES, group * tq), F32)],
        compiler_params=_params("parallel", "parallel", "arbitrary"),
        name="flash_attention",
    )(*args)


def _merge_kernel(x_ref, o_ref, w_ref, mod_ref, y_ref):
    y = jnp.dot(o_ref[0], w_ref[...], preferred_element_type=F32)
    y_ref[0] = x_ref[0] + mod_ref[0, 2] * y


def _merge(x, o, w_out, mod):
    b, l, d = x.shape
    ko = o.shape[2]
    tm = _row_tile(l, 512)
    return pl.pallas_call(
        _merge_kernel,
        grid=(b, l // tm),
        in_specs=[pl.BlockSpec((1, tm, d), lambda i, t: (i, t, 0)),
                  pl.BlockSpec((1, tm, ko), lambda i, t: (i, t, 0)),
                  _resident((ko, d)),
                  pl.BlockSpec((1, 6, 1, d), lambda i, t: (i % mod.shape[0], 0, 0, 0))],
        out_specs=pl.BlockSpec((1, tm, d), lambda i, t: (i, t, 0)),
        out_shape=jax.ShapeDtypeStruct((b, l, d), F32),
        compiler_params=_params("parallel", "parallel"),
        name="merge_residual",
    )(x, o, w_out, mod)


def _ffn_kernel(xp_ref, x_ref, xn_ref, mod_ref, ng_ref, win_ref, cw_ref, cb_ref, wout_ref, og_ref, y_ref,
                *, d_ff, out_norm):
    t = pl.program_id(1)
    nt = pl.num_programs(1)
    gain = ng_ref[...]
    shift, scale, gate = mod_ref[0, 3], mod_ref[0, 4], mod_ref[0, 5]
    x, xp, xn = x_ref[0], xp_ref[0], xn_ref[0]
    tm = x.shape[0]
    halo = xp.shape[0]
    hp = jnp.where(t > 0, _modnorm(xp, gain, shift, scale), 0.0)
    hn = jnp.where(t < nt - 1, _modnorm(xn, gain, shift, scale), 0.0)
    h = jnp.concatenate([hp, _modnorm(x, gain, shift, scale), hn], axis=0).astype(BF16)
    rows = tm + 2 * halo

    def conv(p, col):
        w = cw_ref[:, col:col + d_ff]
        prev = pltpu.roll(p, 1, 0)[halo:halo + tm]
        nxt = pltpu.roll(p, rows - 1, 0)[halo:halo + tm]
        cur = p[halo:halo + tm]
        return ((cb_ref[:, col:col + d_ff] + prev * w[0:1]) + cur * w[1:2]) + nxt * w[2:3]

    pa = jnp.dot(h, win_ref[:, :d_ff], preferred_element_type=F32)
    pv = jnp.dot(h, win_ref[:, d_ff:], preferred_element_type=F32)
    g = _silu(conv(pa, 0)) * conv(pv, d_ff)
    y = x + gate * jnp.dot(g.astype(BF16), wout_ref[...], preferred_element_type=F32)
    y_ref[0] = _rms(y, og_ref[...]) if out_norm else y


def _conv_ffn(x, mod, ng, w_in, conv_w, conv_b, w_out, out_gain, out_norm):
    b, l, d = x.shape
    d_ff = w_out.shape[0]
    tm = _row_tile(l, 512)
    halo = SUBLANES
    per = tm // halo
    last = l // halo - 1
    return pl.pallas_call(
        functools.partial(_ffn_kernel, d_ff=d_ff, out_norm=out_norm),
        grid=(b, l // tm),
        in_specs=[pl.BlockSpec((1, halo, d), lambda i, t: (i, jnp.maximum(t * per - 1, 0), 0)),
                  pl.BlockSpec((1, tm, d), lambda i, t: (i, t, 0)),
                  pl.BlockSpec((1, halo, d), lambda i, t: (i, jnp.minimum((t + 1) * per, last), 0)),
                  pl.BlockSpec((1, 6, 1, d), lambda i, t: (i % mod.shape[0], 0, 0, 0)),
                  _resident((1, d)),
                  _resident((d, 2 * d_ff)),
                  _resident((CONV_W, 2 * d_ff)),
                  _resident((1, 2 * d_ff)),
                  _resident((d_ff, d)),
                  _resident((1, d))],
        out_specs=pl.BlockSpec((1, tm, d), lambda i, t: (i, t, 0)),
        out_shape=jax.ShapeDtypeStruct((b, l, d), F32),
        compiler_params=_params("parallel", "arbitrary"),
        name="conv_ffn",
    )(x, x, x, mod, ng, w_in, conv_w, conv_b, w_out, out_gain)


def _hgrn_proj_kernel(x_ref, mod_ref, ng_ref, w_ref, lb_ref, q_ref, v_ref, kf_ref, gf_ref, kb_ref, gb_ref,
                      gate_ref, *, chunk):
    h = _modnorm(x_ref[0], ng_ref[...], mod_ref[0, 0], mod_ref[0, 1]).astype(BF16)
    width = q_ref.shape[2]
    scale = HGRN_DK ** -0.5
    for part in range(HGRN_PARTS):
        for j in range(width // chunk):
            c0 = j * chunk
            p = jnp.dot(h, w_ref[:, part * width + c0:part * width + c0 + chunk], preferred_element_type=F32)
            if part == 0:
                q_ref[0, :, c0:c0 + chunk] = p * scale
            elif part == 1:
                v_ref[0, :, c0:c0 + chunk] = p
            elif part == 4:
                gate_ref[0, :, c0:c0 + chunk] = p
            else:
                lb = lb_ref[:, c0:c0 + chunk]
                f = lb + (1.0 - lb) * _sigmoid(p)
                k_out, g_out = (kf_ref, gf_ref) if part == 2 else (kb_ref, gb_ref)
                k_out[0, :, c0:c0 + chunk] = 1.0 - f
                g_out[0, :, c0:c0 + chunk] = jnp.log(f)


def _hgrn_proj(x, mod, ng, w_in, lb):
    b, l, d = x.shape
    width = w_in.shape[1] // HGRN_PARTS
    tm = _row_tile(l, 512)
    out = jax.ShapeDtypeStruct((b, l, width), F32)
    ospec = pl.BlockSpec((1, tm, width), lambda i, t: (i, t, 0))
    return pl.pallas_call(
        functools.partial(_hgrn_proj_kernel, chunk=512),
        grid=(b, l // tm),
        in_specs=[pl.BlockSpec((1, tm, d), lambda i, t: (i, t, 0)),
                  pl.BlockSpec((1, 6, 1, d), lambda i, t: (i % mod.shape[0], 0, 0, 0)),
                  _resident((1, d)),
                  _resident((d, HGRN_PARTS * width)),
                  _resident((1, width))],
        out_specs=[ospec] * 7,
        out_shape=[out] * 7,
        compiler_params=_params("parallel", "parallel"),
        name="hgrn_proj",
    )(x, mod, ng, w_in, lb)


_HGRN_LEVELS = (32, 16, 8, 4)


def _hgrn_tri(reverse):
    idx = np.arange(HGRN_CHUNK)
    tri = idx[None, :] >= idx[:, None] if reverse else idx[None, :] <= idx[:, None]
    return jnp.asarray(tri.astype(np.float32), dtype=BF16)


def _hgrn_block(q, k, v, g, tri, st, reverse):
    c = HGRN_CHUNK
    n = q.shape[0] // c

    g = g * LOG2_E
    g_hi = g.astype(BF16)
    rest = g - g_hi.astype(F32)
    g_mid = rest.astype(BF16)
    g_lo = (rest - g_mid.astype(F32)).astype(BF16)
    pieces = [piece[i * c:(i + 1) * c] for i in range(n) for piece in (g_hi, g_mid, g_lo)]
    sums = jnp.dot(tri, jnp.concatenate(pieces, axis=1), preferred_element_type=F32)
    cum = jnp.stack([(sums[:, (3 * i) * LANES:(3 * i + 1) * LANES]
                      + sums[:, (3 * i + 1) * LANES:(3 * i + 2) * LANES])
                     + sums[:, (3 * i + 2) * LANES:(3 * i + 3) * LANES] for i in range(n)], axis=0)

    q3, k3, v3 = (a.reshape(n, c, LANES) for a in (q, k, v))
    tot = cum[:, 0:1, :] if reverse else cum[:, c - 1:c, :]
    row = lax.broadcasted_iota(jnp.int32, (c, c), 0)
    col = lax.broadcasted_iota(jnp.int32, (c, c), 1)
    early, late = (row, col) if reverse else (col, row)

    sub_row = lax.broadcasted_iota(jnp.int32, (c, LANES), 0)
    scores = jnp.zeros((n, c, c), F32)
    for half in _HGRN_LEVELS:
        blk = 2 * half
        cb = cum.reshape(n * (c // blk), blk, LANES)
        b_row = half if reverse else half - 1
        d = cb - cb[:, b_row:b_row + 1, :]
        e = jnp.exp2(_neg_abs(d)).reshape(n, c, LANES)
        query_row = ((sub_row % blk) < half) if reverse else ((sub_row % blk) >= half)
        z = (jnp.where(query_row[None], q3, k3) * e).astype(BF16)
        s_l = lax.dot_general(z, z, (((2,), (2,)), ((0,), (0,))), preferred_element_type=F32)
        pick = ((row // blk) == (col // blk)) & ((early % blk) < half) & ((late % blk) >= half)
        scores = jnp.where(pick[None], s_l, scores)

    sub = _HGRN_LEVELS[-1]
    cum2 = cum.reshape(n * c, LANES)
    prods = [q * k]
    for delta in range(1, sub):
        shift = n * c - delta if reverse else delta
        decay = jnp.exp2(jnp.minimum(cum2 - pltpu.roll(cum2, shift, 0), 0.0))
        prods.append(q * pltpu.roll(k, shift, 0) * decay)
    ones_bf = jnp.ones((LANES, LANES), BF16)
    diag = jnp.dot(jnp.concatenate(prods, axis=0).astype(BF16), ones_bf, preferred_element_type=F32)
    for delta in range(sub):
        d_t = diag[delta * n * c:(delta + 1) * n * c, 0:c].reshape(n, c, c)
        if reverse:
            hit = (col == row + delta) & ((row % sub) + delta < sub)
        else:
            hit = (col == row - delta) & ((row % sub) >= delta)
        scores = jnp.where(hit[None], d_t, scores)

    out = lax.dot_general(scores.astype(BF16), v3.astype(BF16), (((2,), (1,)), ((0,), (0,))),
                          preferred_element_type=F32)
    qe = (q3 * jnp.exp2(cum)).astype(BF16)
    kd = (k3 * jnp.exp2(tot - cum)).astype(BF16)
    e_tot = jnp.exp2(tot)
    v_t = jnp.stack([v3[i].T for i in range(n)], axis=0).astype(BF16)
    upd = lax.dot_general(v_t, kd, (((2,), (1,)), ((0,), (0,))), preferred_element_type=F32)
    states = [None] * n
    for i in (range(n - 1, -1, -1) if reverse else range(n)):
        states[i] = st
        st = st * e_tot[i] + upd[i]
    out = out + lax.dot_general(qe, jnp.stack(states, axis=0).astype(BF16), (((2,), (2,)), ((0,), (0,))),
                                preferred_element_type=F32)
    return out.reshape(n * c, LANES), st


def _hgrn_scan_kernel(qf_ref, vf_ref, kf_ref, gf_ref, qb_ref, vb_ref, kb_ref, gb_ref, tf_ref, tb_ref,
                      sf0_ref, sb0_ref, of_ref, ob_ref, sf_ref, sb_ref, st_sc):
    step = pl.program_id(2)

    @pl.when(step == 0)
    def _():
        st_sc[0] = sf0_ref[0, 0]
        st_sc[1] = sb0_ref[0, 0]

    o, st_f = _hgrn_block(qf_ref[0], kf_ref[0], vf_ref[0], gf_ref[0], tf_ref[...], st_sc[0], False)
    of_ref[0] = o
    o, st_b = _hgrn_block(qb_ref[0], kb_ref[0], vb_ref[0], gb_ref[0], tb_ref[...], st_sc[1], True)
    ob_ref[0] = o
    st_sc[0] = st_f
    st_sc[1] = st_b

    @pl.when(step == pl.num_programs(2) - 1)
    def _():
        sf_ref[0, 0] = st_f
        sb_ref[0, 0] = st_b


def _hgrn_scan(q, v, kf, gf, kb, gb, sf0, sb0):
    b, l, width = q.shape
    heads = width // LANES
    rb = _row_tile(l, 1024)
    nc = l // rb
    fwd = pl.BlockSpec((1, rb, LANES), lambda i, h, s: (i, s, h))
    bwd = pl.BlockSpec((1, rb, LANES), lambda i, h, s: (i, nc - 1 - s, h))
    st_spec = pl.BlockSpec((1, 1, LANES, LANES), lambda i, h, s: (i, h, 0, 0))
    o_shape = jax.ShapeDtypeStruct((b, l, width), F32)
    s_shape = jax.ShapeDtypeStruct((b, heads, LANES, LANES), F32)
    return pl.pallas_call(
        _hgrn_scan_kernel,
        grid=(b, heads, nc),
        in_specs=[fwd, fwd, fwd, fwd, bwd, bwd, bwd, bwd,
                  _resident((HGRN_CHUNK, HGRN_CHUNK)), _resident((HGRN_CHUNK, HGRN_CHUNK)), st_spec, st_spec],
        out_specs=[fwd, bwd, st_spec, st_spec],
        out_shape=[o_shape, o_shape, s_shape, s_shape],
        scratch_shapes=[pltpu.VMEM((2, LANES, LANES), F32)],
        compiler_params=_params("parallel", "parallel", "arbitrary"),
        name="hgrn_scan",
    )(q, v, kf, gf, q, v, kb, gb, _hgrn_tri(False), _hgrn_tri(True), sf0, sb0)


def _hgrn_merge_kernel(x_ref, of_ref, ob_ref, gate_ref, og_ref, w_ref, mod_ref, y_ref):
    width = of_ref.shape[2]
    og = og_ref[...]
    parts = []
    for hd in range(width // LANES):
        sl = slice(hd * LANES, (hd + 1) * LANES)
        o = of_ref[0, :, sl] + ob_ref[0, :, sl]
        parts.append((_rms(o, og) * _silu(gate_ref[0, :, sl])).astype(BF16))
    r = jnp.concatenate(parts, axis=1)
    y = jnp.dot(r, w_ref[...], preferred_element_type=F32)
    y_ref[0] = x_ref[0] + mod_ref[0, 2] * y


def _hgrn_merge(x, o_f, o_b, gate, o_gain, w_out, mod):
    b, l, d = x.shape
    width = o_f.shape[2]
    tm = _row_tile(l, 512)
    wide = pl.BlockSpec((1, tm, width), lambda i, t: (i, t, 0))
    return pl.pallas_call(
        _hgrn_merge_kernel,
        grid=(b, l // tm),
        in_specs=[pl.BlockSpec((1, tm, d), lambda i, t: (i, t, 0)), wide, wide, wide,
                  _resident((1, LANES)),
                  _resident((width, d)),
                  pl.BlockSpec((1, 6, 1, d), lambda i, t: (i % mod.shape[0], 0, 0, 0))],
        out_specs=pl.BlockSpec((1, tm, d), lambda i, t: (i, t, 0)),
        out_shape=jax.ShapeDtypeStruct((b, l, d), F32),
        compiler_params=_params("parallel", "parallel"),
        name="hgrn_merge",
    )(x, o_f, o_b, gate, o_gain, w_out, mod)


def _mla_rope(y, cos, sin_lo, sin_hi):
    half = MLA_ROPE // 2
    return y * cos + pltpu.roll(y, LANES - half, 1) * sin_lo + pltpu.roll(y, half, 1) * sin_hi


def _mla_proj_kernel(x_ref, mod_ref, ng_ref, w_ref, qg_ref, kvg_ref, wq_ref, wk_ref, wv_ref,
                     cos_ref, slo_ref, shi_ref, q_ref, k_ref, v_ref, *, rotate, q_lora, kv_lora, chunk):
    h = _modnorm(x_ref[0], ng_ref[...], mod_ref[0, 0], mod_ref[0, 1]).astype(BF16)
    p = jnp.dot(h, w_ref[...], preferred_element_type=F32)
    cq = _rms(p[:, :q_lora], qg_ref[...]).astype(BF16)
    ckv = _rms(p[:, q_lora:q_lora + kv_lora], kvg_ref[...]).astype(BF16)
    k_rope = p[:, q_lora + kv_lora:]
    scale = (MLA_NOPE + MLA_ROPE) ** -0.5 * LOG2_E
    if rotate:
        cos, slo, shi = cos_ref[...], slo_ref[...], shi_ref[...]
        k_rope = _mla_rope(k_rope, cos, slo, shi)
    n = wq_ref.shape[1]
    for j in range(n // chunk):
        cs = slice(j * chunk, (j + 1) * chunk)
        pq = jnp.dot(cq, wq_ref[:, cs], preferred_element_type=F32)
        pk = jnp.dot(ckv, wk_ref[:, cs], preferred_element_type=F32)
        for u in range(chunk // LANES):
            us = slice(u * LANES, (u + 1) * LANES)
            os = slice(j * chunk + u * LANES, j * chunk + (u + 1) * LANES)
            qh = pq[:, us]
            if rotate:
                qh = _mla_rope(qh, cos, slo, shi)
            q_ref[0, :, os] = (qh * scale).astype(BF16)
            k_ref[0, :, os] = (pk[:, us] + k_rope).astype(BF16)
    for j in range(wv_ref.shape[1] // chunk):
        pv = jnp.dot(ckv, wv_ref[:, j * chunk:(j + 1) * chunk], preferred_element_type=F32)
        for u in range(chunk // LANES):
            os = slice(j * chunk + u * LANES, j * chunk + (u + 1) * LANES)
            v_ref[0, 0, os, :] = pv[:, u * LANES:(u + 1) * LANES].T.astype(BF16)


def _mla_proj(x, mod, ng, w_in, qg, kvg, wq, wk, wv, cos, slo, shi, rotate):
    b, l, d = x.shape
    n = wq.shape[1]
    q_lora, kv_lora = wq.shape[0], wk.shape[0]
    tm = _row_tile(l, 512)
    kern = functools.partial(_mla_proj_kernel, rotate=rotate, q_lora=q_lora, kv_lora=kv_lora, chunk=512)
    out = jax.ShapeDtypeStruct((b, l, n), BF16)
    ospec = pl.BlockSpec((1, tm, n), lambda i, t: (i, t, 0))
    tab = pl.BlockSpec((tm, LANES), lambda i, t: (t, 0))
    return pl.pallas_call(
        kern,
        grid=(b, l // tm),
        in_specs=[pl.BlockSpec((1, tm, d), lambda i, t: (i, t, 0)),
                  pl.BlockSpec((1, 6, 1, d), lambda i, t: (i % mod.shape[0], 0, 0, 0)),
                  _resident((1, d)),
                  _resident(w_in.shape),
                  _resident((1, q_lora)),
                  _resident((1, kv_lora)),
                  _resident(wq.shape), _resident(wk.shape), _resident(wv.shape),
                  tab, tab, tab],
        out_specs=[ospec, ospec, pl.BlockSpec((1, 1, wv.shape[1], tm), lambda i, t: (i, t, 0, 0))],
        out_shape=[out, out, jax.ShapeDtypeStruct((b, l // tm, wv.shape[1], tm), BF16)],
        compiler_params=_params("parallel", "parallel"),
        name="mla_proj",
    )(x, mod, ng, w_in, qg, kvg, wq, wk, wv, cos, slo, shi)


def _axial_angles(rows, rot_dim):
    row = jnp.repeat(jnp.arange(rows, dtype=F32), GRID_W)
    col = jnp.tile(jnp.arange(GRID_W, dtype=F32), rows)
    axis_dim = rot_dim // 2
    inv_freq = jnp.power(ROPE_THETA, -jnp.arange(0, axis_dim, 2, dtype=F32) / axis_dim)
    ang = jnp.concatenate([row[:, None] * inv_freq, col[:, None] * inv_freq], axis=-1)
    return jnp.cos(ang), jnp.sin(ang)


def _gqa_tables(rows):
    cos, sin = _axial_angles(rows, GQA_HEAD_DIM)
    return jnp.concatenate([cos, cos], axis=-1), jnp.concatenate([-sin, sin], axis=-1)


def _mla_tables(rows):
    cos, sin = _axial_angles(rows, MLA_ROPE)
    s = cos.shape[0]
    half = MLA_ROPE // 2
    ones = jnp.ones((s, MLA_NOPE), F32)
    zeros = jnp.zeros((s, MLA_NOPE), F32)
    tail1 = jnp.ones((s, LANES - MLA_NOPE - MLA_ROPE), F32)
    tail0 = jnp.zeros((s, LANES - MLA_NOPE - MLA_ROPE), F32)
    zh = jnp.zeros((s, half), F32)
    c = jnp.concatenate([ones, cos, cos, tail1], axis=-1)
    s_lo = jnp.concatenate([zeros, -sin, zh, tail0], axis=-1)
    s_hi = jnp.concatenate([zeros, zh, sin, tail0], axis=-1)
    return c, s_lo, s_hi


def _mla_weights(w_in, w_qb, w_kvb, w_out):
    d = w_in.shape[0]
    q_lora, kv_lora = w_qb.shape[0], w_kvb.shape[0]
    heads = w_qb.shape[1] // (MLA_NOPE + MLA_ROPE)
    kr = jnp.zeros((d, LANES), w_in.dtype).at[:, MLA_NOPE:MLA_NOPE + MLA_ROPE].set(w_in[:, q_lora + kv_lora:])
    w_in_p = jnp.concatenate([w_in[:, :q_lora + kv_lora], kr], axis=1)
    wq = w_qb.reshape(q_lora, heads, MLA_NOPE + MLA_ROPE)
    wq = jnp.pad(wq, ((0, 0), (0, 0), (0, LANES - MLA_NOPE - MLA_ROPE))).reshape(q_lora, heads * LANES)
    wkv = w_kvb.reshape(kv_lora, heads, MLA_NOPE + MLA_V)
    wk = jnp.pad(wkv[:, :, :MLA_NOPE], ((0, 0), (0, 0), (0, LANES - MLA_NOPE))).reshape(kv_lora, heads * LANES)
    wv = jnp.pad(wkv[:, :, MLA_NOPE:], ((0, 0), (0, 0), (0, LANES - MLA_V))).reshape(kv_lora, heads * LANES)
    wo = w_out.reshape(heads, MLA_V, -1)
    wo = jnp.pad(wo, ((0, 0), (0, LANES - MLA_V), (0, 0))).reshape(heads * LANES, -1)
    return w_in_p.astype(BF16), wq.astype(BF16), wk.astype(BF16), wv.astype(BF16), wo.astype(BF16)


def kernel(x, c, ctx, c_ctx, w_ada, b_ada, norm_mix, norm_ffn, ffn_w_in, ffn_conv_w, ffn_conv_b, ffn_w_out,
           gqa_w_in, gqa_q_norm, gqa_k_norm, gqa_w_out, hgrn_w_in, hgrn_out_norm, hgrn_w_out, hgrn_lower_bounds,
           mla_w_in, mla_q_norm, mla_kv_norm, mla_w_qb, mla_w_kvb, mla_w_out, final_norm):
    batch, seq, d = x.shape
    depth = w_ada.shape[0]
    n_mixers = 3
    rows = seq // GRID_W
    assert batch + 1 <= SUBLANES

    cv = jnp.zeros((SUBLANES, d), F32).at[:batch].set(c).at[batch].set(c_ctx)
    mods = _ada_mods(cv, w_ada, b_ada).reshape(depth, SUBLANES, 6, 1, d)
    lb_all = _lower_bounds(hgrn_lower_bounds)

    cos_a, sin_a = _gqa_tables(rows)
    cos_m, slo_m, shi_m = _mla_tables(rows)

    for i in range(depth):
        last = i == depth - 1
        kind = i % n_mixers
        j = i // n_mixers
        mod = mods[i, :batch]
        mod_c = mods[i, batch:batch + 1]
        ng = norm_mix[i][None, :]

        if kind == 0:
            w_in = gqa_w_in[j].astype(BF16)
            w_out = gqa_w_out[j].astype(BF16)
            qg, kg = gqa_q_norm[j][None, :], gqa_k_norm[j][None, :]
            q, k, v = _gqa_proj(x, mod, ng, w_in, qg, kg, cos_a, sin_a, True)
            n_ctx = ctx.shape[1]
            qc, kc, vc = _gqa_proj(ctx, mod_c, ng, w_in, qg, kg, cos_a[:n_ctx], sin_a[:n_ctx], False)
            o = _attention(q, [(k, v), (kc, vc)], GQA_GROUP, True, 256)
            x = _merge(x, o, w_out, mod)
            if not last:
                oc = _attention(qc, [(kc, vc)], GQA_GROUP, True, 256)
                ctx = _merge(ctx, oc, w_out, mod_c)
        elif kind == 1:
            w_in = hgrn_w_in[j].astype(BF16)
            w_out = hgrn_w_out[j].astype(BF16)
            lb = lb_all[i][None, :]
            og = hgrn_out_norm[j][None, :]
            heads = w_out.shape[0] // HGRN_DK
            qc, vc, kfc, gfc, kbc, gbc, gatec = _hgrn_proj(ctx, mod_c, ng, w_in, lb)
            q, v, kf, gf, kb, gb, gate = _hgrn_proj(x, mod, ng, w_in, lb)
            s0 = jnp.zeros((batch, heads, HGRN_DK, HGRN_DK), F32)
            oc_f, oc_b, s_f, s_b = _hgrn_scan(qc, vc, kfc, gfc, kbc, gbc, s0, s0)
            o_f, o_b, _, _ = _hgrn_scan(q, v, kf, gf, kb, gb, s_f, s_b)
            x = _hgrn_merge(x, o_f, o_b, gate, og, w_out, mod)
            if not last:
                ctx = _hgrn_merge(ctx, oc_f, oc_b, gatec, og, w_out, mod_c)
        else:
            w_in, wq, wk, wv, w_out = _mla_weights(mla_w_in[j], mla_w_qb[j], mla_w_kvb[j], mla_w_out[j])
            qg, kvg = mla_q_norm[j][None, :], mla_kv_norm[j][None, :]
            n_ctx = ctx.shape[1]
            q, k, v = _mla_proj(x, mod, ng, w_in, qg, kvg, wq, wk, wv, cos_m, slo_m, shi_m, True)
            qc, kc, vc = _mla_proj(ctx, mod_c, ng, w_in, qg, kvg, wq, wk, wv,
                                   cos_m[:n_ctx], slo_m[:n_ctx], shi_m[:n_ctx], False)
            o = _attention(q, [(k, v), (kc, vc)], 1, True, 512)
            x = _merge(x, o, w_out, mod)
            if not last:
                oc = _attention(qc, [(kc, vc)], 1, True, 512)
                ctx = _merge(ctx, oc, w_out, mod_c)

        fg = norm_ffn[i][None, :]
        f_in = ffn_w_in[i].astype(BF16)
        f_out = ffn_w_out[i].astype(BF16)
        f_cw = ffn_conv_w[i]
        f_cb = ffn_conv_b[i][None, :]
        x = _conv_ffn(x, mod, fg, f_in, f_cw, f_cb, f_out, final_norm[None, :], last)
        if not last:
            ctx = _conv_ffn(ctx, mod_c, fg, f_in, f_cw, f_cb, f_out, final_norm[None, :], False)

    return x
```

```python
import functools

import numpy as np
import jax
import jax.numpy as jnp
from jax import lax
from jax.experimental import pallas as pl
from jax.experimental.pallas import tpu as pltpu

F32 = jnp.float32
BF16 = jnp.bfloat16
HIGHEST = lax.Precision.HIGHEST

GRID_W = 64
ROPE_THETA = 10000.0
NORM_EPS = 1e-6
CONV_W = 3

LANES = 128
SUBLANES = 8

GQA_HEAD_DIM = 128
GQA_GROUP = 2

HGRN_DK = 128
HGRN_CHUNK = 64
HGRN_PARTS = 5

MLA_NOPE = 64
MLA_ROPE = 32
MLA_V = 64

VMEM_LIMIT = 56 * 1024 * 1024

LOG2_E = 1.4426950408889634
ATTN_UNROLL = 16


def _params(*sem):
    return pltpu.CompilerParams(dimension_semantics=sem, vmem_limit_bytes=VMEM_LIMIT)


def _resident(shape):
    nd = len(shape)
    return pl.BlockSpec(shape, lambda *_: (0,) * nd, pipeline_mode=pl.Buffered(1))


def _silu(x):
    return x / (1.0 + jnp.exp(-x))


def _sigmoid(x):
    return 1.0 / (1.0 + jnp.exp(-x))


def _neg_abs(x):
    bits = lax.bitcast_convert_type(x, jnp.uint32) | jnp.uint32(0x80000000)
    return lax.bitcast_convert_type(bits, F32)


def _rms(x, gain):
    return x * lax.rsqrt(jnp.mean(x * x, axis=-1, keepdims=True) + NORM_EPS) * gain


def _modnorm(x, gain, shift, scale):
    return _rms(x, gain) * (1.0 + scale) + shift


def _aligned_ds(start, size):
    if isinstance(start, int):
        return pl.ds(start, size)
    return pl.ds(pl.multiple_of(start, size), size)


def _row_tile(n, want):
    t = min(n, want)
    while n % t:
        t -= SUBLANES
    assert t > 0 and t % SUBLANES == 0, (n, want)
    return t


def _ada_kernel(cv_ref, w_ref, b_ref, o_ref):
    s = _silu(cv_ref[...])
    o_ref[0] = jnp.dot(s, w_ref[0], precision=HIGHEST, preferred_element_type=F32) + b_ref[0]


def _ada_mods(cv, w_ada, b_ada):
    depth, d, n = w_ada.shape
    tn = 1536
    return pl.pallas_call(
        _ada_kernel,
        grid=(depth, n // tn),
        in_specs=[pl.BlockSpec((SUBLANES, d), lambda i, j: (0, 0)),
                  pl.BlockSpec((1, d, tn), lambda i, j: (i, 0, j)),
                  pl.BlockSpec((1, 1, tn), lambda i, j: (i, 0, j))],
        out_specs=pl.BlockSpec((1, SUBLANES, tn), lambda i, j: (i, 0, j)),
        out_shape=jax.ShapeDtypeStruct((depth, SUBLANES, n), F32),
        compiler_params=_params("arbitrary", "arbitrary"),
        name="ada_mods",
    )(cv, w_ada, b_ada.reshape(depth, 1, n))


def _lb_kernel(x_ref, o_ref):
    depth = x_ref.shape[0]
    rows = [x_ref[i:i + 1, :] for i in range(depth)]
    m = rows[0]
    for r in rows[1:]:
        m = jnp.maximum(m, r)
    e = [jnp.exp(r - m) for r in rows]
    tot = e[0]
    for r in e[1:]:
        tot = tot + r
    p = [r / tot for r in e]
    cum = p[0]
    o_ref[0:1, :] = cum - p[0]
    for i in range(1, depth):
        cum = cum + p[i]
        o_ref[i:i + 1, :] = cum - p[0]


def _lower_bounds(lb_raw):
    return pl.pallas_call(
        _lb_kernel,
        out_shape=jax.ShapeDtypeStruct(lb_raw.shape, F32),
        name="hgrn_lower_bounds",
    )(lb_raw.astype(F32))


def _gqa_proj_kernel(x_ref, mod_ref, ng_ref, w_ref, qg_ref, kg_ref, cos_ref, sin_ref,
                     q_ref, k_ref, v_ref, *, rotate, qd, kd, chunk):
    h = _modnorm(x_ref[0], ng_ref[...], mod_ref[0, 0], mod_ref[0, 1]).astype(BF16)
    scale = GQA_HEAD_DIM ** -0.5 * LOG2_E
    qg = qg_ref[...] * scale
    kg = kg_ref[...]
    if rotate:
        cos = cos_ref[...]
        sin = sin_ref[...]

    ones_bf = jnp.ones((LANES, LANES), BF16)

    def head(p, gain):
        ss = jnp.dot((p * p).astype(BF16), ones_bf, preferred_element_type=F32)
        y = p * lax.rsqrt(ss * (1.0 / GQA_HEAD_DIM) + NORM_EPS) * gain
        if rotate:
            y = y * cos + pltpu.roll(y, GQA_HEAD_DIM // 2, 1) * sin
        return y

    n = w_ref.shape[1]
    for j in range(n // chunk):
        p = jnp.dot(h, w_ref[:, j * chunk:(j + 1) * chunk], preferred_element_type=F32)
        for u in range(chunk // LANES):
            col = j * chunk + u * LANES
            ph = p[:, u * LANES:(u + 1) * LANES]
            if col < qd:
                q_ref[0, :, col:col + LANES] = head(ph, qg).astype(BF16)
            elif col < qd + kd:
                k_ref[0, :, col - qd:col - qd + LANES] = head(ph, kg).astype(BF16)
            else:
                c0 = col - qd - kd
                v_ref[0, 0, c0:c0 + LANES, :] = ph.T.astype(BF16)


def _gqa_proj(x, mod, ng, w_in, qg, kg, cos, sin, rotate):
    b, l, d = x.shape
    n = w_in.shape[1]
    kd = n // 4
    qd = n - 2 * kd
    tm = _row_tile(l, 512)
    kern = functools.partial(_gqa_proj_kernel, rotate=rotate, qd=qd, kd=kd, chunk=512)
    return pl.pallas_call(
        kern,
        grid=(b, l // tm),
        in_specs=[pl.BlockSpec((1, tm, d), lambda i, t: (i, t, 0)),
                  pl.BlockSpec((1, 6, 1, d), lambda i, t: (i % mod.shape[0], 0, 0, 0)),
                  _resident((1, d)),
                  _resident((d, n)),
                  _resident((1, LANES)),
                  _resident((1, LANES)),
                  pl.BlockSpec((tm, LANES), lambda i, t: (t, 0)),
                  pl.BlockSpec((tm, LANES), lambda i, t: (t, 0))],
        out_specs=[pl.BlockSpec((1, tm, qd), lambda i, t: (i, t, 0)),
                   pl.BlockSpec((1, tm, kd), lambda i, t: (i, t, 0)),
                   pl.BlockSpec((1, 1, kd, tm), lambda i, t: (i, t, 0, 0))],
        out_shape=[jax.ShapeDtypeStruct((b, l, qd), BF16),
                   jax.ShapeDtypeStruct((b, l, kd), BF16),
                   jax.ShapeDtypeStruct((b, l // tm, kd, tm), BF16)],
        compiler_params=_params("parallel", "parallel"),
        name="gqa_proj",
    )(x, mod, ng, w_in, qg, kg, cos, sin)


def _attn_kernel(*refs, group, shared, n_src):
    q_ref = refs[0]
    kv_refs = refs[1:1 + 2 * n_src]
    o_ref = refs[1 + 2 * n_src]
    s_sc, m_sc = refs[2 + 2 * n_src:]
    tq = q_ref.shape[1]
    nq = group * tq
    dv = LANES if shared else LANES // group
    step = pl.program_id(2)

    def fold(a):
        return a.reshape(a.shape[0] // SUBLANES, SUBLANES, nq)

    def run(do_scores, do_weigh):
        carry = {}
        if do_scores:
            q = q_ref[0]
            q_heads = [q[:, g * LANES:(g + 1) * LANES] for g in range(group)]
            q_all = jnp.concatenate(q_heads, axis=0)
            carry["m8"] = jnp.full((SUBLANES, nq), -jnp.inf, F32)
        if do_weigh:
            m = jnp.max(m_sc[...], axis=0, keepdims=True)
            carry["l8"] = jnp.zeros((SUBLANES, nq), F32)
            carry["acc"] = jnp.zeros((dv, nq), F32)
        row0 = 0
        for i in range(n_src):
            k_ref, vt_ref = kv_refs[2 * i], kv_refs[2 * i + 1]
            n_chunks, _, c = vt_ref.shape[1:]

            def body(j, carry, k_ref=k_ref, vt_ref=vt_ref, c=c, row0=row0):
                carry = dict(carry)
                rows = _aligned_ds(row0 + j * c, c)
                if do_weigh:
                    s_old = s_sc[rows, :]
                if do_scores:
                    kc = k_ref[0, _aligned_ds(j * c, c), :]
                    nt_dims = (((1,), (1,)), ((), ()))
                    if shared:
                        s = lax.dot_general(kc, q_all, nt_dims, preferred_element_type=F32)
                    else:
                        s = jnp.concatenate(
                            [lax.dot_general(kc[:, g * LANES:(g + 1) * LANES], q_heads[g], nt_dims,
                                             preferred_element_type=F32) for g in range(group)], axis=1)
                    s_sc[rows, :] = s
                    carry["m8"] = jnp.maximum(carry["m8"], jnp.max(fold(s), axis=0))
                if do_weigh:
                    p = jnp.exp2(s_old - m)
                    carry["l8"] = carry["l8"] + jnp.sum(fold(p), axis=0)
                    pb = p.astype(BF16)
                    vt = vt_ref[0, j]
                    if shared:
                        upd = jnp.dot(vt, pb, preferred_element_type=F32)
                    else:
                        upd = jnp.concatenate(
                            [jnp.dot(vt[g * dv:(g + 1) * dv], pb[:, g * tq:(g + 1) * tq],
                                     preferred_element_type=F32) for g in range(group)], axis=1)
                    carry["acc"] = carry["acc"] + upd
                return carry

            if n_chunks == 1:
                carry = body(0, carry)
            else:
                carry = lax.fori_loop(0, n_chunks, body, carry, unroll=ATTN_UNROLL)
            row0 += n_chunks * c
        if do_scores:
            m_sc[...] = carry["m8"]
        if do_weigh:
            o = carry["acc"] / jnp.sum(carry["l8"], axis=0, keepdims=True)
            o_ref[0] = jnp.concatenate([o[:, g * tq:(g + 1) * tq].T for g in range(group)],
                                       axis=1).astype(o_ref.dtype)

    n_tiles = pl.num_programs(2) - 1

    @pl.when(step == 0)
    def _():
        run(True, False)

    @pl.when((step > 0) & (step < n_tiles))
    def _():
        run(True, True)

    @pl.when(step == n_tiles)
    def _():
        run(False, True)


def _attention(q, kv_sources, group, shared, tq):
    b, nq, hd = q.shape
    hkv = hd // (group * LANES)
    kw = LANES if shared else group * LANES
    ow = group * LANES if shared else LANES
    tq = _row_tile(nq, tq)
    nt = nq // tq
    n_src = len(kv_sources)
    in_specs = [pl.BlockSpec((1, tq, group * LANES), lambda i, h, t: (i, jnp.minimum(t, nt - 1), h))]
    args = [q]
    nk_total = 0
    for k, vt in kv_sources:
        nk = k.shape[1]
        n_chunks, _, c = vt.shape[1:]
        assert n_chunks * c == nk and nk_total % c == 0
        nk_total += nk
        in_specs.append(pl.BlockSpec((1, nk, kw), lambda i, h, t: (i, 0, h)))
        in_specs.append(pl.BlockSpec((1, n_chunks, LANES, c), lambda i, h, t: (i, 0, h, 0)))
        args += [k, vt]
    return pl.pallas_call(
        functools.partial(_attn_kernel, group=group, shared=shared, n_src=n_src),
        grid=(b, hkv, nt + 1),
        in_specs=in_specs,
        out_specs=pl.BlockSpec((1, tq, ow), lambda i, h, t: (i, jnp.maximum(t - 1, 0), h)),
        out_shape=jax.ShapeDtypeStruct((b, nq, hkv * ow), BF16),
        scratch_shapes=[pltpu.VMEM((nk_total, group * tq), F32),
                        pltpu.VMEM((SUBLANES, group * tq), F32)],
        compiler_params=_params("parallel", "parallel", "arbitrary"),
        name="flash_attention",
    )(*args)


def _merge_kernel(x_ref, o_ref, w_ref, mod_ref, y_ref):
    y = jnp.dot(o_ref[0], w_ref[...], preferred_element_type=F32)
    y_ref[0] = x_ref[0] + mod_ref[0, 2] * y


def _merge(x, o, w_out, mod):
    b, l, d = x.shape
    ko = o.shape[2]
    tm = _row_tile(l, 512)
    return pl.pallas_call(
        _merge_kernel,
        grid=(b, l // tm),
        in_specs=[pl.BlockSpec((1, tm, d), lambda i, t: (i, t, 0)),
                  pl.BlockSpec((1, tm, ko), lambda i, t: (i, t, 0)),
                  _resident((ko, d)),
                  pl.BlockSpec((1, 6, 1, d), lambda i, t: (i % mod.shape[0], 0, 0, 0))],
        out_specs=pl.BlockSpec((1, tm, d), lambda i, t: (i, t, 0)),
        out_shape=jax.ShapeDtypeStruct((b, l, d), F32),
        compiler_params=_params("parallel", "parallel"),
        name="merge_residual",
    )(x, o, w_out, mod)


def _ffn_kernel(xp_ref, x_ref, xn_ref, mod_ref, ng_ref, win_ref, cw_ref, cb_ref, wout_ref, og_ref, y_ref,
                *, d_ff, out_norm):
    t = pl.program_id(1)
    nt = pl.num_programs(1)
    gain = ng_ref[...]
    shift, scale, gate = mod_ref[0, 3], mod_ref[0, 4], mod_ref[0, 5]
    x, xp, xn = x_ref[0], xp_ref[0], xn_ref[0]
    tm = x.shape[0]
    halo = xp.shape[0]
    hp = jnp.where(t > 0, _modnorm(xp, gain, shift, scale), 0.0)
    hn = jnp.where(t < nt - 1, _modnorm(xn, gain, shift, scale), 0.0)
    h = jnp.concatenate([hp, _modnorm(x, gain, shift, scale), hn], axis=0).astype(BF16)
    rows = tm + 2 * halo

    def conv(p, col):
        w = cw_ref[:, col:col + d_ff]
        prev = pltpu.roll(p, 1, 0)[halo:halo + tm]
        nxt = pltpu.roll(p, rows - 1, 0)[halo:halo + tm]
        cur = p[halo:halo + tm]
        return ((cb_ref[:, col:col + d_ff] + prev * w[0:1]) + cur * w[1:2]) + nxt * w[2:3]

    pa = jnp.dot(h, win_ref[:, :d_ff], preferred_element_type=F32)
    pv = jnp.dot(h, win_ref[:, d_ff:], preferred_element_type=F32)
    g = _silu(conv(pa, 0)) * conv(pv, d_ff)
    y = x + gate * jnp.dot(g.astype(BF16), wout_ref[...], preferred_element_type=F32)
    y_ref[0] = _rms(y, og_ref[...]) if out_norm else y


def _conv_ffn(x, mod, ng, w_in, conv_w, conv_b, w_out, out_gain, out_norm):
    b, l, d = x.shape
    d_ff = w_out.shape[0]
    tm = _row_tile(l, 512)
    halo = SUBLANES
    per = tm // halo
    last = l // halo - 1
    return pl.pallas_call(
        functools.partial(_ffn_kernel, d_ff=d_ff, out_norm=out_norm),
        grid=(b, l // tm),
        in_specs=[pl.BlockSpec((1, halo, d), lambda i, t: (i, jnp.maximum(t * per - 1, 0), 0)),
                  pl.BlockSpec((1, tm, d), lambda i, t: (i, t, 0)),
                  pl.BlockSpec((1, halo, d), lambda i, t: (i, jnp.minimum((t + 1) * per, last), 0)),
                  pl.BlockSpec((1, 6, 1, d), lambda i, t: (i % mod.shape[0], 0, 0, 0)),
                  _resident((1, d)),
                  _resident((d, 2 * d_ff)),
                  _resident((CONV_W, 2 * d_ff)),
                  _resident((1, 2 * d_ff)),
                  _resident((d_ff, d)),
                  _resident((1, d))],
        out_specs=pl.BlockSpec((1, tm, d), lambda i, t: (i, t, 0)),
        out_shape=jax.ShapeDtypeStruct((b, l, d), F32),
        compiler_params=_params("parallel", "arbitrary"),
        name="conv_ffn",
    )(x, x, x, mod, ng, w_in, conv_w, conv_b, w_out, out_gain)


def _hgrn_proj_kernel(x_ref, mod_ref, ng_ref, w_ref, lb_ref, q_ref, v_ref, kf_ref, gf_ref, kb_ref, gb_ref,
                      gate_ref, *, chunk):
    h = _modnorm(x_ref[0], ng_ref[...], mod_ref[0, 0], mod_ref[0, 1]).astype(BF16)
    width = q_ref.shape[2]
    scale = HGRN_DK ** -0.5
    for part in range(HGRN_PARTS):
        for j in range(width // chunk):
            c0 = j * chunk
            p = jnp.dot(h, w_ref[:, part * width + c0:part * width + c0 + chunk], preferred_element_type=F32)
            if part == 0:
                q_ref[0, :, c0:c0 + chunk] = p * scale
            elif part == 1:
                v_ref[0, :, c0:c0 + chunk] = p
            elif part == 4:
                gate_ref[0, :, c0:c0 + chunk] = p
            else:
                lb = lb_ref[:, c0:c0 + chunk]
                f = lb + (1.0 - lb) * _sigmoid(p)
                k_out, g_out = (kf_ref, gf_ref) if part == 2 else (kb_ref, gb_ref)
                k_out[0, :, c0:c0 + chunk] = 1.0 - f
                g_out[0, :, c0:c0 + chunk] = jnp.log(f)


def _hgrn_proj(x, mod, ng, w_in, lb):
    b, l, d = x.shape
    width = w_in.shape[1] // HGRN_PARTS
    tm = _row_tile(l, 512)
    out = jax.ShapeDtypeStruct((b, l, width), F32)
    ospec = pl.BlockSpec((1, tm, width), lambda i, t: (i, t, 0))
    return pl.pallas_call(
        functools.partial(_hgrn_proj_kernel, chunk=512),
        grid=(b, l // tm),
        in_specs=[pl.BlockSpec((1, tm, d), lambda i, t: (i, t, 0)),
                  pl.BlockSpec((1, 6, 1, d), lambda i, t: (i % mod.shape[0], 0, 0, 0)),
                  _resident((1, d)),
                  _resident((d, HGRN_PARTS * width)),
                  _resident((1, width))],
        out_specs=[ospec] * 7,
        out_shape=[out] * 7,
        compiler_params=_params("parallel", "parallel"),
        name="hgrn_proj",
    )(x, mod, ng, w_in, lb)


_HGRN_LEVELS = (32, 16, 8, 4)


def _hgrn_tri(reverse):
    idx = np.arange(HGRN_CHUNK)
    tri = idx[None, :] >= idx[:, None] if reverse else idx[None, :] <= idx[:, None]
    return jnp.asarray(tri.astype(np.float32), dtype=BF16)


def _hgrn_block(q, k, v, g, tri, st, reverse):
    c = HGRN_CHUNK
    n = q.shape[0] // c

    g = g * LOG2_E
    g_hi = g.astype(BF16)
    rest = g - g_hi.astype(F32)
    g_mid = rest.astype(BF16)
    g_lo = (rest - g_mid.astype(F32)).astype(BF16)
    pieces = [piece[i * c:(i + 1) * c] for i in range(n) for piece in (g_hi, g_mid, g_lo)]
    sums = jnp.dot(tri, jnp.concatenate(pieces, axis=1), preferred_element_type=F32)
    cum = jnp.stack([(sums[:, (3 * i) * LANES:(3 * i + 1) * LANES]
                      + sums[:, (3 * i + 1) * LANES:(3 * i + 2) * LANES])
                     + sums[:, (3 * i + 2) * LANES:(3 * i + 3) * LANES] for i in range(n)], axis=0)

    q3, k3, v3 = (a.reshape(n, c, LANES) for a in (q, k, v))
    tot = cum[:, 0:1, :] if reverse else cum[:, c - 1:c, :]
    row = lax.broadcasted_iota(jnp.int32, (c, c), 0)
    col = lax.broadcasted_iota(jnp.int32, (c, c), 1)
    early, late = (row, col) if reverse else (col, row)

    sub_row = lax.broadcasted_iota(jnp.int32, (c, LANES), 0)
    scores = jnp.zeros((n, c, c), F32)
    for half in _HGRN_LEVELS:
        blk = 2 * half
        cb = cum.reshape(n * (c // blk), blk, LANES)
        b_row = half if reverse else half - 1
        d = cb - cb[:, b_row:b_row + 1, :]
        e = jnp.exp2(_neg_abs(d)).reshape(n, c, LANES)
        query_row = ((sub_row % blk) < half) if reverse else ((sub_row % blk) >= half)
        z = (jnp.where(query_row[None], q3, k3) * e).astype(BF16)
        s_l = lax.dot_general(z, z, (((2,), (2,)), ((0,), (0,))), preferred_element_type=F32)
        pick = ((row // blk) == (col // blk)) & ((early % blk) < half) & ((late % blk) >= half)
        scores = jnp.where(pick[None], s_l, scores)

    sub = _HGRN_LEVELS[-1]
    cum2 = cum.reshape(n * c, LANES)
    prods = [q * k]
    for delta in range(1, sub):
        shift = n * c - delta if reverse else delta
        decay = jnp.exp2(jnp.minimum(cum2 - pltpu.roll(cum2, shift, 0), 0.0))
        prods.append(q * pltpu.roll(k, shift, 0) * decay)
    ones_bf = jnp.ones((LANES, LANES), BF16)
    diag = jnp.dot(jnp.concatenate(prods, axis=0).astype(BF16), ones_bf, preferred_element_type=F32)
    for delta in range(sub):
        d_t = diag[delta * n * c:(delta + 1) * n * c, 0:c].reshape(n, c, c)
        if reverse:
            hit = (col == row + delta) & ((row % sub) + delta < sub)
        else:
            hit = (col == row - delta) & ((row % sub) >= delta)
        scores = jnp.where(hit[None], d_t, scores)

    out = lax.dot_general(scores.astype(BF16), v3.astype(BF16), (((2,), (1,)), ((0,), (0,))),
                          preferred_element_type=F32)
    qe = (q3 * jnp.exp2(cum)).astype(BF16)
    kd = (k3 * jnp.exp2(tot - cum)).astype(BF16)
    e_tot = jnp.exp2(tot)
    v_t = jnp.stack([v3[i].T for i in range(n)], axis=0).astype(BF16)
    upd = lax.dot_general(v_t, kd, (((2,), (1,)), ((0,), (0,))), preferred_element_type=F32)
    states = [None] * n
    for i in (range(n - 1, -1, -1) if reverse else range(n)):
        states[i] = st
        st = st * e_tot[i] + upd[i]
    out = out + lax.dot_general(qe, jnp.stack(states, axis=0).astype(BF16), (((2,), (2,)), ((0,), (0,))),
                                preferred_element_type=F32)
    return out.reshape(n * c, LANES), st


def _hgrn_scan_kernel(qf_ref, vf_ref, kf_ref, gf_ref, qb_ref, vb_ref, kb_ref, gb_ref, tf_ref, tb_ref,
                      sf0_ref, sb0_ref, of_ref, ob_ref, sf_ref, sb_ref, st_sc):
    step = pl.program_id(2)

    @pl.when(step == 0)
    def _():
        st_sc[0] = sf0_ref[0, 0]
        st_sc[1] = sb0_ref[0, 0]

    o, st_f = _hgrn_block(qf_ref[0], kf_ref[0], vf_ref[0], gf_ref[0], tf_ref[...], st_sc[0], False)
    of_ref[0] = o
    o, st_b = _hgrn_block(qb_ref[0], kb_ref[0], vb_ref[0], gb_ref[0], tb_ref[...], st_sc[1], True)
    ob_ref[0] = o
    st_sc[0] = st_f
    st_sc[1] = st_b

    @pl.when(step == pl.num_programs(2) - 1)
    def _():
        sf_ref[0, 0] = st_f
        sb_ref[0, 0] = st_b


def _hgrn_scan(q, v, kf, gf, kb, gb, sf0, sb0):
    b, l, width = q.shape
    heads = width // LANES
    rb = _row_tile(l, 1024)
    nc = l // rb
    fwd = pl.BlockSpec((1, rb, LANES), lambda i, h, s: (i, s, h))
    bwd = pl.BlockSpec((1, rb, LANES), lambda i, h, s: (i, nc - 1 - s, h))
    st_spec = pl.BlockSpec((1, 1, LANES, LANES), lambda i, h, s: (i, h, 0, 0))
    o_shape = jax.ShapeDtypeStruct((b, l, width), F32)
    s_shape = jax.ShapeDtypeStruct((b, heads, LANES, LANES), F32)
    return pl.pallas_call(
        _hgrn_scan_kernel,
        grid=(b, heads, nc),
        in_specs=[fwd, fwd, fwd, fwd, bwd, bwd, bwd, bwd,
                  _resident((HGRN_CHUNK, HGRN_CHUNK)), _resident((HGRN_CHUNK, HGRN_CHUNK)), st_spec, st_spec],
        out_specs=[fwd, bwd, st_spec, st_spec],
        out_shape=[o_shape, o_shape, s_shape, s_shape],
        scratch_shapes=[pltpu.VMEM((2, LANES, LANES), F32)],
        compiler_params=_params("parallel", "parallel", "arbitrary"),
        name="hgrn_scan",
    )(q, v, kf, gf, q, v, kb, gb, _hgrn_tri(False), _hgrn_tri(True), sf0, sb0)


def _hgrn_merge_kernel(x_ref, of_ref, ob_ref, gate_ref, og_ref, w_ref, mod_ref, y_ref):
    width = of_ref.shape[2]
    og = og_ref[...]
    parts = []
    for hd in range(width // LANES):
        sl = slice(hd * LANES, (hd + 1) * LANES)
        o = of_ref[0, :, sl] + ob_ref[0, :, sl]
        parts.append((_rms(o, og) * _silu(gate_ref[0, :, sl])).astype(BF16))
    r = jnp.concatenate(parts, axis=1)
    y = jnp.dot(r, w_ref[...], preferred_element_type=F32)
    y_ref[0] = x_ref[0] + mod_ref[0, 2] * y


def _hgrn_merge(x, o_f, o_b, gate, o_gain, w_out, mod):
    b, l, d = x.shape
    width = o_f.shape[2]
    tm = _row_tile(l, 512)
    wide = pl.BlockSpec((1, tm, width), lambda i, t: (i, t, 0))
    return pl.pallas_call(
        _hgrn_merge_kernel,
        grid=(b, l // tm),
        in_specs=[pl.BlockSpec((1, tm, d), lambda i, t: (i, t, 0)), wide, wide, wide,
                  _resident((1, LANES)),
                  _resident((width, d)),
                  pl.BlockSpec((1, 6, 1, d), lambda i, t: (i % mod.shape[0], 0, 0, 0))],
        out_specs=pl.BlockSpec((1, tm, d), lambda i, t: (i, t, 0)),
        out_shape=jax.ShapeDtypeStruct((b, l, d), F32),
        compiler_params=_params("parallel", "parallel"),
        name="hgrn_merge",
    )(x, o_f, o_b, gate, o_gain, w_out, mod)


def _mla_rope(y, cos, sin_lo, sin_hi):
    half = MLA_ROPE // 2
    return y * cos + pltpu.roll(y, LANES - half, 1) * sin_lo + pltpu.roll(y, half, 1) * sin_hi


def _mla_proj_kernel(x_ref, mod_ref, ng_ref, w_ref, qg_ref, kvg_ref, wq_ref, wk_ref, wv_ref,
                     cos_ref, slo_ref, shi_ref, q_ref, k_ref, v_ref, *, rotate, q_lora, kv_lora, chunk):
    h = _modnorm(x_ref[0], ng_ref[...], mod_ref[0, 0], mod_ref[0, 1]).astype(BF16)
    p = jnp.dot(h, w_ref[...], preferred_element_type=F32)
    cq = _rms(p[:, :q_lora], qg_ref[...]).astype(BF16)
    ckv = _rms(p[:, q_lora:q_lora + kv_lora], kvg_ref[...]).astype(BF16)
    k_rope = p[:, q_lora + kv_lora:]
    scale = (MLA_NOPE + MLA_ROPE) ** -0.5 * LOG2_E
    if rotate:
        cos, slo, shi = cos_ref[...], slo_ref[...], shi_ref[...]
        k_rope = _mla_rope(k_rope, cos, slo, shi)
    n = wq_ref.shape[1]
    for j in range(n // chunk):
        cs = slice(j * chunk, (j + 1) * chunk)
        pq = jnp.dot(cq, wq_ref[:, cs], preferred_element_type=F32)
        pk = jnp.dot(ckv, wk_ref[:, cs], preferred_element_type=F32)
        for u in range(chunk // LANES):
            us = slice(u * LANES, (u + 1) * LANES)
            os = slice(j * chunk + u * LANES, j * chunk + (u + 1) * LANES)
            qh = pq[:, us]
            if rotate:
                qh = _mla_rope(qh, cos, slo, shi)
            q_ref[0, :, os] = (qh * scale).astype(BF16)
            k_ref[0, :, os] = (pk[:, us] + k_rope).astype(BF16)
    for j in range(wv_ref.shape[1] // chunk):
        pv = jnp.dot(ckv, wv_ref[:, j * chunk:(j + 1) * chunk], preferred_element_type=F32)
        for u in range(chunk // LANES):
            os = slice(j * chunk + u * LANES, j * chunk + (u + 1) * LANES)
            v_ref[0, 0, os, :] = pv[:, u * LANES:(u + 1) * LANES].T.astype(BF16)


def _mla_proj(x, mod, ng, w_in, qg, kvg, wq, wk, wv, cos, slo, shi, rotate):
    b, l, d = x.shape
    n = wq.shape[1]
    q_lora, kv_lora = wq.shape[0], wk.shape[0]
    tm = _row_tile(l, 512)
    kern = functools.partial(_mla_proj_kernel, rotate=rotate, q_lora=q_lora, kv_lora=kv_lora, chunk=512)
    out = jax.ShapeDtypeStruct((b, l, n), BF16)
    ospec = pl.BlockSpec((1, tm, n), lambda i, t: (i, t, 0))
    tab = pl.BlockSpec((tm, LANES), lambda i, t: (t, 0))
    return pl.pallas_call(
        kern,
        grid=(b, l // tm),
        in_specs=[pl.BlockSpec((1, tm, d), lambda i, t: (i, t, 0)),
                  pl.BlockSpec((1, 6, 1, d), lambda i, t: (i % mod.shape[0], 0, 0, 0)),
                  _resident((1, d)),
                  _resident(w_in.shape),
                  _resident((1, q_lora)),
                  _resident((1, kv_lora)),
                  _resident(wq.shape), _resident(wk.shape), _resident(wv.shape),
                  tab, tab, tab],
        out_specs=[ospec, ospec, pl.BlockSpec((1, 1, wv.shape[1], tm), lambda i, t: (i, t, 0, 0))],
        out_shape=[out, out, jax.ShapeDtypeStruct((b, l // tm, wv.shape[1], tm), BF16)],
        compiler_params=_params("parallel", "parallel"),
        name="mla_proj",
    )(x, mod, ng, w_in, qg, kvg, wq, wk, wv, cos, slo, shi)


def _axial_angles(rows, rot_dim):
    row = jnp.repeat(jnp.arange(rows, dtype=F32), GRID_W)
    col = jnp.tile(jnp.arange(GRID_W, dtype=F32), rows)
    axis_dim = rot_dim // 2
    inv_freq = jnp.power(ROPE_THETA, -jnp.arange(0, axis_dim, 2, dtype=F32) / axis_dim)
    ang = jnp.concatenate([row[:, None] * inv_freq, col[:, None] * inv_freq], axis=-1)
    return jnp.cos(ang), jnp.sin(ang)


def _gqa_tables(rows):
    cos, sin = _axial_angles(rows, GQA_HEAD_DIM)
    return jnp.concatenate([cos, cos], axis=-1), jnp.concatenate([-sin, sin], axis=-1)


def _mla_tables(rows):
    cos, sin = _axial_angles(rows, MLA_ROPE)
    s = cos.shape[0]
    half = MLA_ROPE // 2
    ones = jnp.ones((s, MLA_NOPE), F32)
    zeros = jnp.zeros((s, MLA_NOPE), F32)
    tail1 = jnp.ones((s, LANES - MLA_NOPE - MLA_ROPE), F32)
    tail0 = jnp.zeros((s, LANES - MLA_NOPE - MLA_ROPE), F32)
    zh = jnp.zeros((s, half), F32)
    c = jnp.concatenate([ones, cos, cos, tail1], axis=-1)
    s_lo = jnp.concatenate([zeros, -sin, zh, tail0], axis=-1)
    s_hi = jnp.concatenate([zeros, zh, sin, tail0], axis=-1)
    return c, s_lo, s_hi


def _mla_weights(w_in, w_qb, w_kvb, w_out):
    d = w_in.shape[0]
    q_lora, kv_lora = w_qb.shape[0], w_kvb.shape[0]
    heads = w_qb.shape[1] // (MLA_NOPE + MLA_ROPE)
    kr = jnp.zeros((d, LANES), w_in.dtype).at[:, MLA_NOPE:MLA_NOPE + MLA_ROPE].set(w_in[:, q_lora + kv_lora:])
    w_in_p = jnp.concatenate([w_in[:, :q_lora + kv_lora], kr], axis=1)
    wq = w_qb.reshape(q_lora, heads, MLA_NOPE + MLA_ROPE)
    wq = jnp.pad(wq, ((0, 0), (0, 0), (0, LANES - MLA_NOPE - MLA_ROPE))).reshape(q_lora, heads * LANES)
    wkv = w_kvb.reshape(kv_lora, heads, MLA_NOPE + MLA_V)
    wk = jnp.pad(wkv[:, :, :MLA_NOPE], ((0, 0), (0, 0), (0, LANES - MLA_NOPE))).reshape(kv_lora, heads * LANES)
    wv = jnp.pad(wkv[:, :, MLA_NOPE:], ((0, 0), (0, 0), (0, LANES - MLA_V))).reshape(kv_lora, heads * LANES)
    wo = w_out.reshape(heads, MLA_V, -1)
    wo = jnp.pad(wo, ((0, 0), (0, LANES - MLA_V), (0, 0))).reshape(heads * LANES, -1)
    return w_in_p.astype(BF16), wq.astype(BF16), wk.astype(BF16), wv.astype(BF16), wo.astype(BF16)


def kernel(x, c, ctx, c_ctx, w_ada, b_ada, norm_mix, norm_ffn, ffn_w_in, ffn_conv_w, ffn_conv_b, ffn_w_out,
           gqa_w_in, gqa_q_norm, gqa_k_norm, gqa_w_out, hgrn_w_in, hgrn_out_norm, hgrn_w_out, hgrn_lower_bounds,
           mla_w_in, mla_q_norm, mla_kv_norm, mla_w_qb, mla_w_kvb, mla_w_out, final_norm):
    batch, seq, d = x.shape
    depth = w_ada.shape[0]
    n_mixers = 3
    rows = seq // GRID_W
    assert batch + 1 <= SUBLANES

    cv = jnp.zeros((SUBLANES, d), F32).at[:batch].set(c).at[batch].set(c_ctx)
    mods = _ada_mods(cv, w_ada, b_ada).reshape(depth, SUBLANES, 6, 1, d)
    lb_all = _lower_bounds(hgrn_lower_bounds)

    cos_a, sin_a = _gqa_tables(rows)
    cos_m, slo_m, shi_m = _mla_tables(rows)

    for i in range(depth):
        last = i == depth - 1
        kind = i % n_mixers
        j = i // n_mixers
        mod = mods[i, :batch]
        mod_c = mods[i, batch:batch + 1]
        ng = norm_mix[i][None, :]

        if kind == 0:
            w_in = gqa_w_in[j].astype(BF16)
            w_out = gqa_w_out[j].astype(BF16)
            qg, kg = gqa_q_norm[j][None, :], gqa_k_norm[j][None, :]
            q, k, v = _gqa_proj(x, mod, ng, w_in, qg, kg, cos_a, sin_a, True)
            n_ctx = ctx.shape[1]
            qc, kc, vc = _gqa_proj(ctx, mod_c, ng, w_in, qg, kg, cos_a[:n_ctx], sin_a[:n_ctx], False)
            o = _attention(q, [(k, v), (kc, vc)], GQA_GROUP, True, 256)
            x = _merge(x, o, w_out, mod)
            if not last:
                oc = _attention(qc, [(kc, vc)], GQA_GROUP, True, 256)
                ctx = _merge(ctx, oc, w_out, mod_c)
        elif kind == 1:
            w_in = hgrn_w_in[j].astype(BF16)
            w_out = hgrn_w_out[j].astype(BF16)
            lb = lb_all[i][None, :]
            og = hgrn_out_norm[j][None, :]
            heads = w_out.shape[0] // HGRN_DK
            qc, vc, kfc, gfc, kbc, gbc, gatec = _hgrn_proj(ctx, mod_c, ng, w_in, lb)
            q, v, kf, gf, kb, gb, gate = _hgrn_proj(x, mod, ng, w_in, lb)
            s0 = jnp.zeros((batch, heads, HGRN_DK, HGRN_DK), F32)
            oc_f, oc_b, s_f, s_b = _hgrn_scan(qc, vc, kfc, gfc, kbc, gbc, s0, s0)
            o_f, o_b, _, _ = _hgrn_scan(q, v, kf, gf, kb, gb, s_f, s_b)
            x = _hgrn_merge(x, o_f, o_b, gate, og, w_out, mod)
            if not last:
                ctx = _hgrn_merge(ctx, oc_f, oc_b, gatec, og, w_out, mod_c)
        else:
            w_in, wq, wk, wv, w_out = _mla_weights(mla_w_in[j], mla_w_qb[j], mla_w_kvb[j], mla_w_out[j])
            qg, kvg = mla_q_norm[j][None, :], mla_kv_norm[j][None, :]
            n_ctx = ctx.shape[1]
            q, k, v = _mla_proj(x, mod, ng, w_in, qg, kvg, wq, wk, wv, cos_m, slo_m, shi_m, True)
            qc, kc, vc = _mla_proj(ctx, mod_c, ng, w_in, qg, kvg, wq, wk, wv,
                                   cos_m[:n_ctx], slo_m[:n_ctx], shi_m[:n_ctx], False)
            o = _attention(q, [(k, v), (kc, vc)], 1, True, 512)
            x = _merge(x, o, w_out, mod)
            if not last:
                oc = _attention(qc, [(kc, vc)], 1, True, 512)
                ctx = _merge(ctx, oc, w_out, mod_c)

        fg = norm_ffn[i][None, :]
        f_in = ffn_w_in[i].astype(BF16)
        f_out = ffn_w_out[i].astype(BF16)
        f_cw = ffn_conv_w[i]
        f_cb = ffn_conv_b[i][None, :]
        x = _conv_ffn(x, mod, fg, f_in, f_cw, f_cb, f_out, final_norm[None, :], last)
        if not last:
            ctx = _conv_ffn(ctx, mod_c, fg, f_in, f_cw, f_cb, f_out, final_norm[None, :], False)

    return x
```

```python
import functools

import numpy as np
import jax
import jax.numpy as jnp
from jax import lax
from jax.experimental import pallas as pl
from jax.experimental.pallas import tpu as pltpu

F32 = jnp.float32
BF16 = jnp.bfloat16
HIGHEST = lax.Precision.HIGHEST

GRID_W = 64
ROPE_THETA = 10000.0
NORM_EPS = 1e-6
CONV_W = 3

LANES = 128
SUBLANES = 8

GQA_HEAD_DIM = 128
GQA_GROUP = 2

HGRN_DK = 128
HGRN_CHUNK = 64
HGRN_PARTS = 5

MLA_NOPE = 64
MLA_ROPE = 32
MLA_V = 64

VMEM_LIMIT = 56 * 1024 * 1024

LOG2_E = 1.4426950408889634
ATTN_UNROLL = 16


def _params(*sem):
    return pltpu.CompilerParams(dimension_semantics=sem, vmem_limit_bytes=VMEM_LIMIT)


def _resident(shape):
    nd = len(shape)
    return pl.BlockSpec(shape, lambda *_: (0,) * nd, pipeline_mode=pl.Buffered(1))


def _silu(x):
    return x / (1.0 + jnp.exp(-x))


def _sigmoid(x):
    return 1.0 / (1.0 + jnp.exp(-x))


def _neg_abs(x):
    bits = lax.bitcast_convert_type(x, jnp.uint32) | jnp.uint32(0x80000000)
    return lax.bitcast_convert_type(bits, F32)


def _rms(x, gain):
    return x * lax.rsqrt(jnp.mean(x * x, axis=-1, keepdims=True) + NORM_EPS) * gain


def _head_rms(p, gain):
    ones_bf = jnp.ones((LANES, LANES), BF16)
    ss = jnp.dot((p * p).astype(BF16), ones_bf, preferred_element_type=F32)
    return p * lax.rsqrt(ss * (1.0 / LANES) + NORM_EPS) * gain


def _modnorm(x, gain, shift, scale):
    return _rms(x, gain) * (1.0 + scale) + shift


def _aligned_ds(start, size):
    if isinstance(start, int):
        return pl.ds(start, size)
    return pl.ds(pl.multiple_of(start, size), size)


def _row_tile(n, want):
    t = min(n, want)
    while n % t:
        t -= SUBLANES
    assert t > 0 and t % SUBLANES == 0, (n, want)
    return t


def _ada_kernel(cv_ref, w_ref, b_ref, o_ref):
    s = _silu(cv_ref[...])
    o_ref[0] = jnp.dot(s, w_ref[0], precision=HIGHEST, preferred_element_type=F32) + b_ref[0]


def _ada_mods(cv, w_ada, b_ada):
    depth, d, n = w_ada.shape
    tn = 1536
    return pl.pallas_call(
        _ada_kernel,
        grid=(depth, n // tn),
        in_specs=[pl.BlockSpec((SUBLANES, d), lambda i, j: (0, 0)),
                  pl.BlockSpec((1, d, tn), lambda i, j: (i, 0, j)),
                  pl.BlockSpec((1, 1, tn), lambda i, j: (i, 0, j))],
        out_specs=pl.BlockSpec((1, SUBLANES, tn), lambda i, j: (i, 0, j)),
        out_shape=jax.ShapeDtypeStruct((depth, SUBLANES, n), F32),
        compiler_params=_params("arbitrary", "arbitrary"),
        name="ada_mods",
    )(cv, w_ada, b_ada.reshape(depth, 1, n))


def _lb_kernel(x_ref, o_ref):
    depth = x_ref.shape[0]
    rows = [x_ref[i:i + 1, :] for i in range(depth)]
    m = rows[0]
    for r in rows[1:]:
        m = jnp.maximum(m, r)
    e = [jnp.exp(r - m) for r in rows]
    tot = e[0]
    for r in e[1:]:
        tot = tot + r
    p = [r / tot for r in e]
    cum = p[0]
    o_ref[0:1, :] = cum - p[0]
    for i in range(1, depth):
        cum = cum + p[i]
        o_ref[i:i + 1, :] = cum - p[0]


def _lower_bounds(lb_raw):
    return pl.pallas_call(
        _lb_kernel,
        out_shape=jax.ShapeDtypeStruct(lb_raw.shape, F32),
        name="hgrn_lower_bounds",
    )(lb_raw.astype(F32))


def _gqa_proj_kernel(x_ref, mod_ref, ng_ref, w_ref, qg_ref, kg_ref, cos_ref, sin_ref,
                     q_ref, k_ref, v_ref, *, rotate, qd, kd, chunk):
    h = _modnorm(x_ref[0], ng_ref[...], mod_ref[0, 0], mod_ref[0, 1]).astype(BF16)
    scale = GQA_HEAD_DIM ** -0.5 * LOG2_E
    qg = qg_ref[...] * scale
    kg = kg_ref[...]
    if rotate:
        cos = cos_ref[...]
        sin = sin_ref[...]

    def head(p, gain):
        y = _head_rms(p, gain)
        if rotate:
            y = y * cos + pltpu.roll(y, GQA_HEAD_DIM // 2, 1) * sin
        return y

    n = w_ref.shape[1]
    for j in range(n // chunk):
        p = jnp.dot(h, w_ref[:, j * chunk:(j + 1) * chunk], preferred_element_type=F32)
        for u in range(chunk // LANES):
            col = j * chunk + u * LANES
            ph = p[:, u * LANES:(u + 1) * LANES]
            if col < qd:
                q_ref[0, :, col:col + LANES] = head(ph, qg).astype(BF16)
            elif col < qd + kd:
                k_ref[0, :, col - qd:col - qd + LANES] = head(ph, kg).astype(BF16)
            else:
                c0 = col - qd - kd
                v_ref[0, 0, c0:c0 + LANES, :] = ph.T.astype(BF16)


def _gqa_proj(x, mod, ng, w_in, qg, kg, cos, sin, rotate):
    b, l, d = x.shape
    n = w_in.shape[1]
    kd = n // 4
    qd = n - 2 * kd
    tm = _row_tile(l, 512)
    kern = functools.partial(_gqa_proj_kernel, rotate=rotate, qd=qd, kd=kd, chunk=512)
    return pl.pallas_call(
        kern,
        grid=(b, l // tm),
        in_specs=[pl.BlockSpec((1, tm, d), lambda i, t: (i, t, 0)),
                  pl.BlockSpec((1, 6, 1, d), lambda i, t: (i % mod.shape[0], 0, 0, 0)),
                  _resident((1, d)),
                  _resident((d, n)),
                  _resident((1, LANES)),
                  _resident((1, LANES)),
                  pl.BlockSpec((tm, LANES), lambda i, t: (t, 0)),
                  pl.BlockSpec((tm, LANES), lambda i, t: (t, 0))],
        out_specs=[pl.BlockSpec((1, tm, qd), lambda i, t: (i, t, 0)),
                   pl.BlockSpec((1, tm, kd), lambda i, t: (i, t, 0)),
                   pl.BlockSpec((1, 1, kd, tm), lambda i, t: (i, t, 0, 0))],
        out_shape=[jax.ShapeDtypeStruct((b, l, qd), BF16),
                   jax.ShapeDtypeStruct((b, l, kd), BF16),
                   jax.ShapeDtypeStruct((b, l // tm, kd, tm), BF16)],
        compiler_params=_params("parallel", "parallel"),
        name="gqa_proj",
    )(x, mod, ng, w_in, qg, kg, cos, sin)


def _attn_kernel(*refs, group, shared, n_src):
    q_ref = refs[0]
    kv_refs = refs[1:1 + 2 * n_src]
    o_ref = refs[1 + 2 * n_src]
    s_sc, m_sc = refs[2 + 2 * n_src:]
    tq = q_ref.shape[1]
    nq = group * tq
    dv = LANES if shared else LANES // group
    step = pl.program_id(2)

    def fold(a):
        return a.reshape(a.shape[0] // SUBLANES, SUBLANES, nq)

    def run(do_scores, do_weigh):
        carry = {}
        if do_scores:
            q = q_ref[0]
            q_heads = [q[:, g * LANES:(g + 1) * LANES] for g in range(group)]
            q_all = jnp.concatenate(q_heads, axis=0)
            carry["m8"] = jnp.full((SUBLANES, nq), -jnp.inf, F32)
        if do_weigh:
            m = jnp.max(m_sc[...], axis=0, keepdims=True)
            carry["l8"] = jnp.zeros((SUBLANES, nq), F32)
            carry["acc"] = jnp.zeros((dv, nq), F32)
        row0 = 0
        for i in range(n_src):
            k_ref, vt_ref = kv_refs[2 * i], kv_refs[2 * i + 1]
            n_chunks, _, c = vt_ref.shape[1:]

            def body(j, carry, k_ref=k_ref, vt_ref=vt_ref, c=c, row0=row0):
                carry = dict(carry)
                rows = _aligned_ds(row0 + j * c, c)
                if do_weigh:
                    s_old = s_sc[rows, :]
                if do_scores:
                    kc = k_ref[0, _aligned_ds(j * c, c), :]
                    nt_dims = (((1,), (1,)), ((), ()))
                    if shared:
                        s = lax.dot_general(kc, q_all, nt_dims, preferred_element_type=F32)
                    else:
                        s = jnp.concatenate(
                            [lax.dot_general(kc[:, g * LANES:(g + 1) * LANES], q_heads[g], nt_dims,
                                             preferred_element_type=F32) for g in range(group)], axis=1)
                    s_sc[rows, :] = s
                    carry["m8"] = jnp.maximum(carry["m8"], jnp.max(fold(s), axis=0))
                if do_weigh:
                    p = jnp.exp2(s_old - m)
                    carry["l8"] = carry["l8"] + jnp.sum(fold(p), axis=0)
                    pb = p.astype(BF16)
                    vt = vt_ref[0, j]
                    if shared:
                        upd = jnp.dot(vt, pb, preferred_element_type=F32)
                    else:
                        upd = jnp.concatenate(
                            [jnp.dot(vt[g * dv:(g + 1) * dv], pb[:, g * tq:(g + 1) * tq],
                                     preferred_element_type=F32) for g in range(group)], axis=1)
                    carry["acc"] = carry["acc"] + upd
                return carry

            if n_chunks == 1:
                carry = body(0, carry)
            else:
                carry = lax.fori_loop(0, n_chunks, body, carry, unroll=ATTN_UNROLL)
            row0 += n_chunks * c
        if do_scores:
            m_sc[...] = carry["m8"]
        if do_weigh:
            o = carry["acc"] / jnp.sum(carry["l8"], axis=0, keepdims=True)
            o_ref[0] = jnp.concatenate([o[:, g * tq:(g + 1) * tq].T for g in range(group)],
                                       axis=1).astype(o_ref.dtype)

    n_tiles = pl.num_programs(2) - 1

    @pl.when(step == 0)
    def _():
        run(True, False)

    @pl.when((step > 0) & (step < n_tiles))
    def _():
        run(True, True)

    @pl.when(step == n_tiles)
    def _():
        run(False, True)


def _attention(q, kv_sources, group, shared, tq):
    b, nq, hd = q.shape
    hkv = hd // (group * LANES)
    kw = LANES if shared else group * LANES
    ow = group * LANES if shared else LANES
    tq = _row_tile(nq, tq)
    nt = nq // tq
    n_src = len(kv_sources)
    in_specs = [pl.BlockSpec((1, tq, group * LANES), lambda i, h, t: (i, jnp.minimum(t, nt - 1), h))]
    args = [q]
    nk_total = 0
    for k, vt in kv_sources:
        nk = k.shape[1]
        n_chunks, _, c = vt.shape[1:]
        assert n_chunks * c == nk and nk_total % c == 0
        nk_total += nk
        in_specs.append(pl.BlockSpec((1, nk, kw), lambda i, h, t: (i, 0, h)))
        in_specs.append(pl.BlockSpec((1, n_chunks, LANES, c), lambda i, h, t: (i, 0, h, 0)))
        args += [k, vt]
    return pl.pallas_call(
        functools.partial(_attn_kernel, group=group, shared=shared, n_src=n_src),
        grid=(b, hkv, nt + 1),
        in_specs=in_specs,
        out_specs=pl.BlockSpec((1, tq, ow), lambda i, h, t: (i, jnp.maximum(t - 1, 0), h)),
        out_shape=jax.ShapeDtypeStruct((b, nq, hkv * ow), BF16),
        scratch_shapes=[pltpu.VMEM((nk_total, group * tq), F32),
                        pltpu.VMEM((SUBLANES, group * tq), F32)],
        compiler_params=_params("parallel", "parallel", "arbitrary"),
        name="flash_attention",
    )(*args)


def _merge_kernel(x_ref, o_ref, w_ref, mod_ref, y_ref):
    y = jnp.dot(o_ref[0], w_ref[...], preferred_element_type=F32)
    y_ref[0] = x_ref[0] + mod_ref[0, 2] * y


def _merge(x, o, w_out, mod):
    b, l, d = x.shape
    ko = o.shape[2]
    tm = _row_tile(l, 512)
    return pl.pallas_call(
        _merge_kernel,
        grid=(b, l // tm),
        in_specs=[pl.BlockSpec((1, tm, d), lambda i, t: (i, t, 0)),
                  pl.BlockSpec((1, tm, ko), lambda i, t: (i, t, 0)),
                  _resident((ko, d)),
                  pl.BlockSpec((1, 6, 1, d), lambda i, t: (i % mod.shape[0], 0, 0, 0))],
        out_specs=pl.BlockSpec((1, tm, d), lambda i, t: (i, t, 0)),
        out_shape=jax.ShapeDtypeStruct((b, l, d), F32),
        compiler_params=_params("parallel", "parallel"),
        name="merge_residual",
    )(x, o, w_out, mod)


def _ffn_kernel(xp_ref, x_ref, xn_ref, mod_ref, ng_ref, win_ref, cw_ref, cb_ref, wout_ref, og_ref, y_ref,
                *, d_ff, out_norm):
    t = pl.program_id(1)
    nt = pl.num_programs(1)
    gain = ng_ref[...]
    shift, scale, gate = mod_ref[0, 3], mod_ref[0, 4], mod_ref[0, 5]
    x, xp, xn = x_ref[0], xp_ref[0], xn_ref[0]
    tm = x.shape[0]
    halo = xp.shape[0]
    hp = jnp.where(t > 0, _modnorm(xp, gain, shift, scale), 0.0)
    hn = jnp.where(t < nt - 1, _modnorm(xn, gain, shift, scale), 0.0)
    h = jnp.concatenate([hp, _modnorm(x, gain, shift, scale), hn], axis=0).astype(BF16)
    rows = tm + 2 * halo

    def conv(p, col):
        w = cw_ref[:, col:col + d_ff]
        prev = pltpu.roll(p, 1, 0)[halo:halo + tm]
        nxt = pltpu.roll(p, rows - 1, 0)[halo:halo + tm]
        cur = p[halo:halo + tm]
        return ((cb_ref[:, col:col + d_ff] + prev * w[0:1]) + cur * w[1:2]) + nxt * w[2:3]

    pa = jnp.dot(h, win_ref[:, :d_ff], preferred_element_type=F32)
    pv = jnp.dot(h, win_ref[:, d_ff:], preferred_element_type=F32)
    g = _silu(conv(pa, 0)) * conv(pv, d_ff)
    y = x + gate * jnp.dot(g.astype(BF16), wout_ref[...], preferred_element_type=F32)
    y_ref[0] = _rms(y, og_ref[...]) if out_norm else y


def _conv_ffn(x, mod, ng, w_in, conv_w, conv_b, w_out, out_gain, out_norm):
    b, l, d = x.shape
    d_ff = w_out.shape[0]
    tm = _row_tile(l, 512)
    halo = SUBLANES
    per = tm // halo
    last = l // halo - 1
    return pl.pallas_call(
        functools.partial(_ffn_kernel, d_ff=d_ff, out_norm=out_norm),
        grid=(b, l // tm),
        in_specs=[pl.BlockSpec((1, halo, d), lambda i, t: (i, jnp.maximum(t * per - 1, 0), 0)),
                  pl.BlockSpec((1, tm, d), lambda i, t: (i, t, 0)),
                  pl.BlockSpec((1, halo, d), lambda i, t: (i, jnp.minimum((t + 1) * per, last), 0)),
                  pl.BlockSpec((1, 6, 1, d), lambda i, t: (i % mod.shape[0], 0, 0, 0)),
                  _resident((1, d)),
                  _resident((d, 2 * d_ff)),
                  _resident((CONV_W, 2 * d_ff)),
                  _resident((1, 2 * d_ff)),
                  _resident((d_ff, d)),
                  _resident((1, d))],
        out_specs=pl.BlockSpec((1, tm, d), lambda i, t: (i, t, 0)),
        out_shape=jax.ShapeDtypeStruct((b, l, d), F32),
        compiler_params=_params("parallel", "arbitrary"),
        name="conv_ffn",
    )(x, x, x, mod, ng, w_in, conv_w, conv_b, w_out, out_gain)


def _hgrn_proj_kernel(x_ref, mod_ref, ng_ref, w_ref, lb_ref, q_ref, v_ref, gf_ref, gb_ref, gate_ref, *, chunk):
    h = _modnorm(x_ref[0], ng_ref[...], mod_ref[0, 0], mod_ref[0, 1]).astype(BF16)
    width = q_ref.shape[2]
    scale = HGRN_DK ** -0.5
    for part in range(HGRN_PARTS):
        for j in range(width // chunk):
            c0 = j * chunk
            p = jnp.dot(h, w_ref[:, part * width + c0:part * width + c0 + chunk], preferred_element_type=F32)
            if part == 0:
                q_ref[0, :, c0:c0 + chunk] = p * scale
            elif part == 1:
                v_ref[0, :, c0:c0 + chunk] = p
            elif part == 4:
                gate_ref[0, :, c0:c0 + chunk] = p
            else:
                lb = lb_ref[:, c0:c0 + chunk]
                f = lb + (1.0 - lb) * _sigmoid(p)
                g_out = gf_ref if part == 2 else gb_ref
                g_out[0, :, c0:c0 + chunk] = jnp.log2(f)


def _hgrn_proj(x, mod, ng, w_in, lb):
    b, l, d = x.shape
    width = w_in.shape[1] // HGRN_PARTS
    tm = _row_tile(l, 512)
    out = jax.ShapeDtypeStruct((b, l, width), F32)
    ospec = pl.BlockSpec((1, tm, width), lambda i, t: (i, t, 0))
    return pl.pallas_call(
        functools.partial(_hgrn_proj_kernel, chunk=512),
        grid=(b, l // tm),
        in_specs=[pl.BlockSpec((1, tm, d), lambda i, t: (i, t, 0)),
                  pl.BlockSpec((1, 6, 1, d), lambda i, t: (i % mod.shape[0], 0, 0, 0)),
                  _resident((1, d)),
                  _resident((d, HGRN_PARTS * width)),
                  _resident((1, width))],
        out_specs=[ospec] * 5,
        out_shape=[out] * 5,
        compiler_params=_params("parallel", "parallel"),
        name="hgrn_proj",
    )(x, mod, ng, w_in, lb)


_HGRN_LEVELS = (32, 16, 8, 4)


def _hgrn_tri(reverse):
    idx = np.arange(HGRN_CHUNK)
    tri = idx[None, :] >= idx[:, None] if reverse else idx[None, :] <= idx[:, None]
    return jnp.asarray(tri.astype(np.float32), dtype=BF16)


def _hgrn_block(q, v, g, tri, st, reverse):
    c = HGRN_CHUNK
    n = q.shape[0] // c
    k = 1.0 - jnp.exp2(g)

    g_hi = g.astype(BF16)
    rest = g - g_hi.astype(F32)
    g_mid = rest.astype(BF16)
    g_lo = (rest - g_mid.astype(F32)).astype(BF16)
    pieces = [piece[i * c:(i + 1) * c] for i in range(n) for piece in (g_hi, g_mid, g_lo)]
    sums = jnp.dot(tri, jnp.concatenate(pieces, axis=1), preferred_element_type=F32)
    cum = jnp.stack([(sums[:, (3 * i) * LANES:(3 * i + 1) * LANES]
                      + sums[:, (3 * i + 1) * LANES:(3 * i + 2) * LANES])
                     + sums[:, (3 * i + 2) * LANES:(3 * i + 3) * LANES] for i in range(n)], axis=0)

    q3, k3, v3 = (a.reshape(n, c, LANES) for a in (q, k, v))
    tot = cum[:, 0:1, :] if reverse else cum[:, c - 1:c, :]
    row = lax.broadcasted_iota(jnp.int32, (c, c), 0)
    col = lax.broadcasted_iota(jnp.int32, (c, c), 1)
    early, late = (row, col) if reverse else (col, row)

    sub_row = lax.broadcasted_iota(jnp.int32, (c, LANES), 0)
    scores = jnp.zeros((n, c, c), F32)
    for half in _HGRN_LEVELS:
        blk = 2 * half
        cb = cum.reshape(n * (c // blk), blk, LANES)
        b_row = half if reverse else half - 1
        d = cb - cb[:, b_row:b_row + 1, :]
        e = jnp.exp2(_neg_abs(d)).reshape(n, c, LANES)
        query_row = ((sub_row % blk) < half) if reverse else ((sub_row % blk) >= half)
        z = (jnp.where(query_row[None], q3, k3) * e).astype(BF16)
        s_l = lax.dot_general(z, z, (((2,), (2,)), ((0,), (0,))), preferred_element_type=F32)
        pick = ((row // blk) == (col // blk)) & ((early % blk) < half) & ((late % blk) >= half)
        scores = jnp.where(pick[None], s_l, scores)

    sub = _HGRN_LEVELS[-1]
    cum2 = cum.reshape(n * c, LANES)
    prods = [q * k]
    for delta in range(1, sub):
        shift = n * c - delta if reverse else delta
        decay = jnp.exp2(jnp.minimum(cum2 - pltpu.roll(cum2, shift, 0), 0.0))
        prods.append(q * pltpu.roll(k, shift, 0) * decay)
    ones_bf = jnp.ones((LANES, LANES), BF16)
    diag = jnp.dot(jnp.concatenate(prods, axis=0).astype(BF16), ones_bf, preferred_element_type=F32)
    for delta in range(sub):
        d_t = diag[delta * n * c:(delta + 1) * n * c, 0:c].reshape(n, c, c)
        if reverse:
            hit = (col == row + delta) & ((row % sub) + delta < sub)
        else:
            hit = (col == row - delta) & ((row % sub) >= delta)
        scores = jnp.where(hit[None], d_t, scores)

    out = lax.dot_general(scores.astype(BF16), v3.astype(BF16), (((2,), (1,)), ((0,), (0,))),
                          preferred_element_type=F32)
    qe = (q3 * jnp.exp2(cum)).astype(BF16)
    kd = (k3 * jnp.exp2(tot - cum)).astype(BF16)
    e_tot = jnp.exp2(tot)
    v_t = jnp.stack([v3[i].T for i in range(n)], axis=0).astype(BF16)
    upd = lax.dot_general(v_t, kd, (((2,), (1,)), ((0,), (0,))), preferred_element_type=F32)
    states = [None] * n
    for i in (range(n - 1, -1, -1) if reverse else range(n)):
        states[i] = st
        st = st * e_tot[i] + upd[i]
    out = out + lax.dot_general(qe, jnp.stack(states, axis=0).astype(BF16), (((2,), (2,)), ((0,), (0,))),
                                preferred_element_type=F32)
    return out.reshape(n * c, LANES), st


def _hgrn_scan_kernel(qf_ref, vf_ref, gf_ref, qb_ref, vb_ref, gb_ref, tf_ref, tb_ref,
                      sf0_ref, sb0_ref, of_ref, ob_ref, sf_ref, sb_ref, st_sc):
    step = pl.program_id(2)

    @pl.when(step == 0)
    def _():
        st_sc[0] = sf0_ref[0, 0]
        st_sc[1] = sb0_ref[0, 0]

    o, st_f = _hgrn_block(qf_ref[0], vf_ref[0], gf_ref[0], tf_ref[...], st_sc[0], False)
    of_ref[0] = o
    o, st_b = _hgrn_block(qb_ref[0], vb_ref[0], gb_ref[0], tb_ref[...], st_sc[1], True)
    ob_ref[0] = o
    st_sc[0] = st_f
    st_sc[1] = st_b

    @pl.when(step == pl.num_programs(2) - 1)
    def _():
        sf_ref[0, 0] = st_f
        sb_ref[0, 0] = st_b


def _hgrn_scan(q, v, gf, gb, sf0, sb0):
    b, l, width = q.shape
    heads = width // LANES
    rb = _row_tile(l, 2048)
    nc = l // rb
    fwd = pl.BlockSpec((1, rb, LANES), lambda i, h, s: (i, s, h))
    bwd = pl.BlockSpec((1, rb, LANES), lambda i, h, s: (i, nc - 1 - s, h))
    st_spec = pl.BlockSpec((1, 1, LANES, LANES), lambda i, h, s: (i, h, 0, 0))
    o_shape = jax.ShapeDtypeStruct((b, l, width), F32)
    s_shape = jax.ShapeDtypeStruct((b, heads, LANES, LANES), F32)
    return pl.pallas_call(
        _hgrn_scan_kernel,
        grid=(b, heads, nc),
        in_specs=[fwd, fwd, fwd, bwd, bwd, bwd,
                  _resident((HGRN_CHUNK, HGRN_CHUNK)), _resident((HGRN_CHUNK, HGRN_CHUNK)), st_spec, st_spec],
        out_specs=[fwd, bwd, st_spec, st_spec],
        out_shape=[o_shape, o_shape, s_shape, s_shape],
        scratch_shapes=[pltpu.VMEM((2, LANES, LANES), F32)],
        compiler_params=_params("parallel", "parallel", "arbitrary"),
        name="hgrn_scan",
    )(q, v, gf, q, v, gb, _hgrn_tri(False), _hgrn_tri(True), sf0, sb0)


def _hgrn_merge_kernel(x_ref, of_ref, ob_ref, gate_ref, og_ref, w_ref, mod_ref, y_ref):
    width = of_ref.shape[2]
    og = og_ref[...]
    parts = []
    for hd in range(width // LANES):
        sl = slice(hd * LANES, (hd + 1) * LANES)
        o = of_ref[0, :, sl] + ob_ref[0, :, sl]
        parts.append((_rms(o, og) * _silu(gate_ref[0, :, sl])).astype(BF16))
    r = jnp.concatenate(parts, axis=1)
    y = jnp.dot(r, w_ref[...], preferred_element_type=F32)
    y_ref[0] = x_ref[0] + mod_ref[0, 2] * y


def _hgrn_merge(x, o_f, o_b, gate, o_gain, w_out, mod):
    b, l, d = x.shape
    width = o_f.shape[2]
    tm = _row_tile(l, 512)
    wide = pl.BlockSpec((1, tm, width), lambda i, t: (i, t, 0))
    return pl.pallas_call(
        _hgrn_merge_kernel,
        grid=(b, l // tm),
        in_specs=[pl.BlockSpec((1, tm, d), lambda i, t: (i, t, 0)), wide, wide, wide,
                  _resident((1, LANES)),
                  _resident((width, d)),
                  pl.BlockSpec((1, 6, 1, d), lambda i, t: (i % mod.shape[0], 0, 0, 0))],
        out_specs=pl.BlockSpec((1, tm, d), lambda i, t: (i, t, 0)),
        out_shape=jax.ShapeDtypeStruct((b, l, d), F32),
        compiler_params=_params("parallel", "parallel"),
        name="hgrn_merge",
    )(x, o_f, o_b, gate, o_gain, w_out, mod)


def _mla_rope(y, cos, sin_lo, sin_hi):
    half = MLA_ROPE // 2
    return y * cos + pltpu.roll(y, LANES - half, 1) * sin_lo + pltpu.roll(y, half, 1) * sin_hi


def _mla_proj_kernel(x_ref, mod_ref, ng_ref, w_ref, qg_ref, kvg_ref, wq_ref, wk_ref, wv_ref,
                     cos_ref, slo_ref, shi_ref, q_ref, k_ref, v_ref, *, rotate, q_lora, kv_lora, chunk):
    h = _modnorm(x_ref[0], ng_ref[...], mod_ref[0, 0], mod_ref[0, 1]).astype(BF16)
    p = jnp.dot(h, w_ref[...], preferred_element_type=F32)
    cq = _rms(p[:, :q_lora], qg_ref[...]).astype(BF16)
    ckv = _rms(p[:, q_lora:q_lora + kv_lora], kvg_ref[...]).astype(BF16)
    k_rope = p[:, q_lora + kv_lora:]
    scale = (MLA_NOPE + MLA_ROPE) ** -0.5 * LOG2_E
    if rotate:
        cos, slo, shi = cos_ref[...], slo_ref[...], shi_ref[...]
        k_rope = _mla_rope(k_rope, cos, slo, shi)
    n = wq_ref.shape[1]
    for j in range(n // chunk):
        cs = slice(j * chunk, (j + 1) * chunk)
        pq = jnp.dot(cq, wq_ref[:, cs], preferred_element_type=F32)
        pk = jnp.dot(ckv, wk_ref[:, cs], preferred_element_type=F32)
        for u in range(chunk // LANES):
            us = slice(u * LANES, (u + 1) * LANES)
            os = slice(j * chunk + u * LANES, j * chunk + (u + 1) * LANES)
            qh = pq[:, us]
            if rotate:
                qh = _mla_rope(qh, cos, slo, shi)
            q_ref[0, :, os] = (qh * scale).astype(BF16)
            k_ref[0, :, os] = (pk[:, us] + k_rope).astype(BF16)
    for j in range(wv_ref.shape[1] // chunk):
        pv = jnp.dot(ckv, wv_ref[:, j * chunk:(j + 1) * chunk], preferred_element_type=F32)
        for u in range(chunk // LANES):
            os = slice(j * chunk + u * LANES, j * chunk + (u + 1) * LANES)
            v_ref[0, 0, os, :] = pv[:, u * LANES:(u + 1) * LANES].T.astype(BF16)


def _mla_proj(x, mod, ng, w_in, qg, kvg, wq, wk, wv, cos, slo, shi, rotate):
    b, l, d = x.shape
    n = wq.shape[1]
    q_lora, kv_lora = wq.shape[0], wk.shape[0]
    tm = _row_tile(l, 512)
    kern = functools.partial(_mla_proj_kernel, rotate=rotate, q_lora=q_lora, kv_lora=kv_lora, chunk=512)
    out = jax.ShapeDtypeStruct((b, l, n), BF16)
    ospec = pl.BlockSpec((1, tm, n), lambda i, t: (i, t, 0))
    tab = pl.BlockSpec((tm, LANES), lambda i, t: (t, 0))
    return pl.pallas_call(
        kern,
        grid=(b, l // tm),
        in_specs=[pl.BlockSpec((1, tm, d), lambda i, t: (i, t, 0)),
                  pl.BlockSpec((1, 6, 1, d), lambda i, t: (i % mod.shape[0], 0, 0, 0)),
                  _resident((1, d)),
                  _resident(w_in.shape),
                  _resident((1, q_lora)),
                  _resident((1, kv_lora)),
                  _resident(wq.shape), _resident(wk.shape), _resident(wv.shape),
                  tab, tab, tab],
        out_specs=[ospec, ospec, pl.BlockSpec((1, 1, wv.shape[1], tm), lambda i, t: (i, t, 0, 0))],
        out_shape=[out, out, jax.ShapeDtypeStruct((b, l // tm, wv.shape[1], tm), BF16)],
        compiler_params=_params("parallel", "parallel"),
        name="mla_proj",
    )(x, mod, ng, w_in, qg, kvg, wq, wk, wv, cos, slo, shi)


def _axial_angles(rows, rot_dim):
    row = jnp.repeat(jnp.arange(rows, dtype=F32), GRID_W)
    col = jnp.tile(jnp.arange(GRID_W, dtype=F32), rows)
    axis_dim = rot_dim // 2
    inv_freq = jnp.power(ROPE_THETA, -jnp.arange(0, axis_dim, 2, dtype=F32) / axis_dim)
    ang = jnp.concatenate([row[:, None] * inv_freq, col[:, None] * inv_freq], axis=-1)
    return jnp.cos(ang), jnp.sin(ang)


def _gqa_tables(rows):
    cos, sin = _axial_angles(rows, GQA_HEAD_DIM)
    return jnp.concatenate([cos, cos], axis=-1), jnp.concatenate([-sin, sin], axis=-1)


def _mla_tables(rows):
    cos, sin = _axial_angles(rows, MLA_ROPE)
    s = cos.shape[0]
    half = MLA_ROPE // 2
    ones = jnp.ones((s, MLA_NOPE), F32)
    zeros = jnp.zeros((s, MLA_NOPE), F32)
    tail1 = jnp.ones((s, LANES - MLA_NOPE - MLA_ROPE), F32)
    tail0 = jnp.zeros((s, LANES - MLA_NOPE - MLA_ROPE), F32)
    zh = jnp.zeros((s, half), F32)
    c = jnp.concatenate([ones, cos, cos, tail1], axis=-1)
    s_lo = jnp.concatenate([zeros, -sin, zh, tail0], axis=-1)
    s_hi = jnp.concatenate([zeros, zh, sin, tail0], axis=-1)
    return c, s_lo, s_hi


def _mla_weights(w_in, w_qb, w_kvb, w_out):
    d = w_in.shape[0]
    q_lora, kv_lora = w_qb.shape[0], w_kvb.shape[0]
    heads = w_qb.shape[1] // (MLA_NOPE + MLA_ROPE)
    kr = jnp.zeros((d, LANES), w_in.dtype).at[:, MLA_NOPE:MLA_NOPE + MLA_ROPE].set(w_in[:, q_lora + kv_lora:])
    w_in_p = jnp.concatenate([w_in[:, :q_lora + kv_lora], kr], axis=1)
    wq = w_qb.reshape(q_lora, heads, MLA_NOPE + MLA_ROPE)
    wq = jnp.pad(wq, ((0, 0), (0, 0), (0, LANES - MLA_NOPE - MLA_ROPE))).reshape(q_lora, heads * LANES)
    wkv = w_kvb.reshape(kv_lora, heads, MLA_NOPE + MLA_V)
    wk = jnp.pad(wkv[:, :, :MLA_NOPE], ((0, 0), (0, 0), (0, LANES - MLA_NOPE))).reshape(kv_lora, heads * LANES)
    wv = jnp.pad(wkv[:, :, MLA_NOPE:], ((0, 0), (0, 0), (0, LANES - MLA_V))).reshape(kv_lora, heads * LANES)
    wo = w_out.reshape(heads, MLA_V, -1)
    wo = jnp.pad(wo, ((0, 0), (0, LANES - MLA_V), (0, 0))).reshape(heads * LANES, -1)
    return w_in_p.astype(BF16), wq.astype(BF16), wk.astype(BF16), wv.astype(BF16), wo.astype(BF16)


def kernel(x, c, ctx, c_ctx, w_ada, b_ada, norm_mix, norm_ffn, ffn_w_in, ffn_conv_w, ffn_conv_b, ffn_w_out,
           gqa_w_in, gqa_q_norm, gqa_k_norm, gqa_w_out, hgrn_w_in, hgrn_out_norm, hgrn_w_out, hgrn_lower_bounds,
           mla_w_in, mla_q_norm, mla_kv_norm, mla_w_qb, mla_w_kvb, mla_w_out, final_norm):
    batch, seq, d = x.shape
    depth = w_ada.shape[0]
    n_mixers = 3
    rows = seq // GRID_W
    assert batch + 1 <= SUBLANES

    cv = jnp.zeros((SUBLANES, d), F32).at[:batch].set(c).at[batch].set(c_ctx)
    mods = _ada_mods(cv, w_ada, b_ada).reshape(depth, SUBLANES, 6, 1, d)
    lb_all = _lower_bounds(hgrn_lower_bounds)

    cos_a, sin_a = _gqa_tables(rows)
    cos_m, slo_m, shi_m = _mla_tables(rows)

    for i in range(depth):
        last = i == depth - 1
        kind = i % n_mixers
        j = i // n_mixers
        mod = mods[i, :batch]
        mod_c = mods[i, batch:batch + 1]
        ng = norm_mix[i][None, :]

        if kind == 0:
            w_in = gqa_w_in[j].astype(BF16)
            w_out = gqa_w_out[j].astype(BF16)
            qg, kg = gqa_q_norm[j][None, :], gqa_k_norm[j][None, :]
            q, k, v = _gqa_proj(x, mod, ng, w_in, qg, kg, cos_a, sin_a, True)
            n_ctx = ctx.shape[1]
            qc, kc, vc = _gqa_proj(ctx, mod_c, ng, w_in, qg, kg, cos_a[:n_ctx], sin_a[:n_ctx], False)
            o = _attention(q, [(k, v), (kc, vc)], GQA_GROUP, True, 256)
            x = _merge(x, o, w_out, mod)
            if not last:
                oc = _attention(qc, [(kc, vc)], GQA_GROUP, True, 256)
                ctx = _merge(ctx, oc, w_out, mod_c)
        elif kind == 1:
            w_in = hgrn_w_in[j].astype(BF16)
            w_out = hgrn_w_out[j].astype(BF16)
            lb = lb_all[i][None, :]
            og = hgrn_out_norm[j][None, :]
            heads = w_out.shape[0] // HGRN_DK
            qc, vc, gfc, gbc, gatec = _hgrn_proj(ctx, mod_c, ng, w_in, lb)
            q, v, gf, gb, gate = _hgrn_proj(x, mod, ng, w_in, lb)
            s0 = jnp.zeros((batch, heads, HGRN_DK, HGRN_DK), F32)
            oc_f, oc_b, s_f, s_b = _hgrn_scan(qc, vc, gfc, gbc, s0, s0)
            o_f, o_b, _, _ = _hgrn_scan(q, v, gf, gb, s_f, s_b)
            x = _hgrn_merge(x, o_f, o_b, gate, og, w_out, mod)
            if not last:
                ctx = _hgrn_merge(ctx, oc_f, oc_b, gatec, og, w_out, mod_c)
        else:
            w_in, wq, wk, wv, w_out = _mla_weights(mla_w_in[j], mla_w_qb[j], mla_w_kvb[j], mla_w_out[j])
            qg, kvg = mla_q_norm[j][None, :], mla_kv_norm[j][None, :]
            n_ctx = ctx.shape[1]
            q, k, v = _mla_proj(x, mod, ng, w_in, qg, kvg, wq, wk, wv, cos_m, slo_m, shi_m, True)
            qc, kc, vc = _mla_proj(ctx, mod_c, ng, w_in, qg, kvg, wq, wk, wv,
                                   cos_m[:n_ctx], slo_m[:n_ctx], shi_m[:n_ctx], False)
            o = _attention(q, [(k, v), (kc, vc)], 1, True, 512)
            x = _merge(x, o, w_out, mod)
            if not last:
                oc = _attention(qc, [(kc, vc)], 1, True, 512)
                ctx = _merge(ctx, oc, w_out, mod_c)

        fg = norm_ffn[i][None, :]
        f_in = ffn_w_in[i].astype(BF16)
        f_out = ffn_w_out[i].astype(BF16)
        f_cw = ffn_conv_w[i]
        f_cb = ffn_conv_b[i][None, :]
        x = _conv_ffn(x, mod, fg, f_in, f_cw, f_cb, f_out, final_norm[None, :], last)
        if not last:
            ctx = _conv_ffn(ctx, mod_c, fg, f_in, f_cw, f_cb, f_out, final_norm[None, :], False)

    return x
```

```python
import functools

import numpy as np
import jax
import jax.numpy as jnp
from jax import lax
from jax.experimental import pallas as pl
from jax.experimental.pallas import tpu as pltpu

F32 = jnp.float32
BF16 = jnp.bfloat16
HIGHEST = lax.Precision.HIGHEST

GRID_W = 64
ROPE_THETA = 10000.0
NORM_EPS = 1e-6
CONV_W = 3

LANES = 128
SUBLANES = 8

GQA_HEAD_DIM = 128
GQA_GROUP = 2

HGRN_DK = 128
HGRN_CHUNK = 64
HGRN_PARTS = 5

MLA_NOPE = 64
MLA_ROPE = 32
MLA_V = 64

VMEM_LIMIT = 56 * 1024 * 1024

LOG2_E = 1.4426950408889634
ATTN_UNROLL = 16


def _params(*sem):
    return pltpu.CompilerParams(dimension_semantics=sem, vmem_limit_bytes=VMEM_LIMIT)


def _resident(shape):
    nd = len(shape)
    return pl.BlockSpec(shape, lambda *_: (0,) * nd, pipeline_mode=pl.Buffered(1))


def _silu(x):
    return x / (1.0 + jnp.exp2(x * -LOG2_E))


def _sigmoid(x):
    return 1.0 / (1.0 + jnp.exp2(x * -LOG2_E))


def _neg_abs(x):
    bits = lax.bitcast_convert_type(x, jnp.uint32) | jnp.uint32(0x80000000)
    return lax.bitcast_convert_type(bits, F32)


def _rms(x, gain):
    return x * lax.rsqrt(jnp.mean(x * x, axis=-1, keepdims=True) + NORM_EPS) * gain


def _head_rms(p, gain):
    ones_bf = jnp.ones((LANES, LANES), BF16)
    ss = jnp.dot((p * p).astype(BF16), ones_bf, preferred_element_type=F32)
    return p * lax.rsqrt(ss * (1.0 / LANES) + NORM_EPS) * gain


def _modnorm(x, gain, shift, scale):
    return _rms(x, gain) * (1.0 + scale) + shift


def _aligned_ds(start, size):
    if isinstance(start, int):
        return pl.ds(start, size)
    return pl.ds(pl.multiple_of(start, size), size)


def _row_tile(n, want):
    t = min(n, want)
    while n % t:
        t -= SUBLANES
    assert t > 0 and t % SUBLANES == 0, (n, want)
    return t


def _ada_kernel(cv_ref, w_ref, b_ref, o_ref):
    s = _silu(cv_ref[...])
    o_ref[0] = jnp.dot(s, w_ref[0], precision=HIGHEST, preferred_element_type=F32) + b_ref[0]


def _ada_mods(cv, w_ada, b_ada):
    depth, d, n = w_ada.shape
    tn = 1536
    return pl.pallas_call(
        _ada_kernel,
        grid=(depth, n // tn),
        in_specs=[pl.BlockSpec((SUBLANES, d), lambda i, j: (0, 0)),
                  pl.BlockSpec((1, d, tn), lambda i, j: (i, 0, j)),
                  pl.BlockSpec((1, 1, tn), lambda i, j: (i, 0, j))],
        out_specs=pl.BlockSpec((1, SUBLANES, tn), lambda i, j: (i, 0, j)),
        out_shape=jax.ShapeDtypeStruct((depth, SUBLANES, n), F32),
        compiler_params=_params("arbitrary", "arbitrary"),
        name="ada_mods",
    )(cv, w_ada, b_ada.reshape(depth, 1, n))


def _lb_kernel(x_ref, o_ref):
    depth = x_ref.shape[0]
    rows = [x_ref[i:i + 1, :] for i in range(depth)]
    m = rows[0]
    for r in rows[1:]:
        m = jnp.maximum(m, r)
    e = [jnp.exp(r - m) for r in rows]
    tot = e[0]
    for r in e[1:]:
        tot = tot + r
    p = [r / tot for r in e]
    cum = p[0]
    o_ref[0:1, :] = cum - p[0]
    for i in range(1, depth):
        cum = cum + p[i]
        o_ref[i:i + 1, :] = cum - p[0]


def _lower_bounds(lb_raw):
    return pl.pallas_call(
        _lb_kernel,
        out_shape=jax.ShapeDtypeStruct(lb_raw.shape, F32),
        name="hgrn_lower_bounds",
    )(lb_raw.astype(F32))


def _gqa_proj_kernel(x_ref, mod_ref, ng_ref, w_ref, qg_ref, kg_ref, cos_ref, sin_ref,
                     q_ref, k_ref, v_ref, *, rotate, qd, kd, chunk):
    h = _modnorm(x_ref[0], ng_ref[...], mod_ref[0, 0], mod_ref[0, 1]).astype(BF16)
    scale = GQA_HEAD_DIM ** -0.5 * LOG2_E
    qg = qg_ref[...] * scale
    kg = kg_ref[...]
    if rotate:
        cos = cos_ref[...]
        sin = sin_ref[...]

    def head(p, gain):
        y = _head_rms(p, gain)
        if rotate:
            y = y * cos + pltpu.roll(y, GQA_HEAD_DIM // 2, 1) * sin
        return y

    n = w_ref.shape[1]
    for j in range(n // chunk):
        p = jnp.dot(h, w_ref[:, j * chunk:(j + 1) * chunk], preferred_element_type=F32)
        for u in range(chunk // LANES):
            col = j * chunk + u * LANES
            ph = p[:, u * LANES:(u + 1) * LANES]
            if col < qd:
                q_ref[0, :, col:col + LANES] = head(ph, qg).astype(BF16)
            elif col < qd + kd:
                k_ref[0, :, col - qd:col - qd + LANES] = head(ph, kg).astype(BF16)
            else:
                c0 = col - qd - kd
                v_ref[0, 0, c0:c0 + LANES, :] = ph.T.astype(BF16)


def _gqa_proj(x, mod, ng, w_in, qg, kg, cos, sin, rotate):
    b, l, d = x.shape
    n = w_in.shape[1]
    kd = n // 4
    qd = n - 2 * kd
    tm = _row_tile(l, 512)
    kern = functools.partial(_gqa_proj_kernel, rotate=rotate, qd=qd, kd=kd, chunk=512)
    return pl.pallas_call(
        kern,
        grid=(b, l // tm),
        in_specs=[pl.BlockSpec((1, tm, d), lambda i, t: (i, t, 0)),
                  pl.BlockSpec((1, 6, 1, d), lambda i, t: (i % mod.shape[0], 0, 0, 0)),
                  _resident((1, d)),
                  _resident((d, n)),
                  _resident((1, LANES)),
                  _resident((1, LANES)),
                  pl.BlockSpec((tm, LANES), lambda i, t: (t, 0)),
                  pl.BlockSpec((tm, LANES), lambda i, t: (t, 0))],
        out_specs=[pl.BlockSpec((1, tm, qd), lambda i, t: (i, t, 0)),
                   pl.BlockSpec((1, tm, kd), lambda i, t: (i, t, 0)),
                   pl.BlockSpec((1, 1, kd, tm), lambda i, t: (i, t, 0, 0))],
        out_shape=[jax.ShapeDtypeStruct((b, l, qd), BF16),
                   jax.ShapeDtypeStruct((b, l, kd), BF16),
                   jax.ShapeDtypeStruct((b, l // tm, kd, tm), BF16)],
        compiler_params=_params("parallel", "parallel"),
        name="gqa_proj",
    )(x, mod, ng, w_in, qg, kg, cos, sin)


def _attn_kernel(*refs, group, shared, n_src):
    q_ref = refs[0]
    kv_refs = refs[1:1 + 2 * n_src]
    o_ref = refs[1 + 2 * n_src]
    s_sc, m_sc = refs[2 + 2 * n_src:]
    tq = q_ref.shape[1]
    nq = group * tq
    dv = LANES if shared else LANES // group
    step = pl.program_id(2)

    def fold(a):
        return a.reshape(a.shape[0] // SUBLANES, SUBLANES, nq)

    def run(do_scores, do_weigh):
        carry = {}
        if do_scores:
            q = q_ref[0]
            q_heads = [q[:, g * LANES:(g + 1) * LANES] for g in range(group)]
            q_all = jnp.concatenate(q_heads, axis=0)
            carry["m8"] = jnp.full((SUBLANES, nq), -jnp.inf, F32)
        if do_weigh:
            m = jnp.max(m_sc[...], axis=0, keepdims=True)
            carry["l8"] = jnp.zeros((SUBLANES, nq), F32)
            carry["acc"] = jnp.zeros((dv, nq), F32)
        row0 = 0
        for i in range(n_src):
            k_ref, vt_ref = kv_refs[2 * i], kv_refs[2 * i + 1]
            n_chunks, _, c = vt_ref.shape[1:]

            def body(j, carry, k_ref=k_ref, vt_ref=vt_ref, c=c, row0=row0):
                carry = dict(carry)
                rows = _aligned_ds(row0 + j * c, c)
                if do_weigh:
                    s_old = s_sc[rows, :]
                if do_scores:
                    kc = k_ref[0, _aligned_ds(j * c, c), :]
                    nt_dims = (((1,), (1,)), ((), ()))
                    if shared:
                        s = lax.dot_general(kc, q_all, nt_dims, preferred_element_type=F32)
                    else:
                        s = jnp.concatenate(
                            [lax.dot_general(kc[:, g * LANES:(g + 1) * LANES], q_heads[g], nt_dims,
                                             preferred_element_type=F32) for g in range(group)], axis=1)
                    s_sc[rows, :] = s
                    carry["m8"] = jnp.maximum(carry["m8"], jnp.max(fold(s), axis=0))
                if do_weigh:
                    p = jnp.exp2(s_old - m)
                    carry["l8"] = carry["l8"] + jnp.sum(fold(p), axis=0)
                    pb = p.astype(BF16)
                    vt = vt_ref[0, j]
                    if shared:
                        upd = jnp.dot(vt, pb, preferred_element_type=F32)
                    else:
                        upd = jnp.concatenate(
                            [jnp.dot(vt[g * dv:(g + 1) * dv], pb[:, g * tq:(g + 1) * tq],
                                     preferred_element_type=F32) for g in range(group)], axis=1)
                    carry["acc"] = carry["acc"] + upd
                return carry

            if n_chunks == 1:
                carry = body(0, carry)
            else:
                carry = lax.fori_loop(0, n_chunks, body, carry, unroll=ATTN_UNROLL)
            row0 += n_chunks * c
        if do_scores:
            m_sc[...] = carry["m8"]
        if do_weigh:
            o = carry["acc"] / jnp.sum(carry["l8"], axis=0, keepdims=True)
            o_ref[0] = jnp.concatenate([o[:, g * tq:(g + 1) * tq].T for g in range(group)],
                                       axis=1).astype(o_ref.dtype)

    n_tiles = pl.num_programs(2) - 1

    @pl.when(step == 0)
    def _():
        run(True, False)

    @pl.when((step > 0) & (step < n_tiles))
    def _():
        run(True, True)

    @pl.when(step == n_tiles)
    def _():
        run(False, True)


def _attention(q, kv_sources, group, shared, tq):
    b, nq, hd = q.shape
    hkv = hd // (group * LANES)
    kw = LANES if shared else group * LANES
    ow = group * LANES if shared else LANES
    tq = _row_tile(nq, tq)
    nt = nq // tq
    n_src = len(kv_sources)
    in_specs = [pl.BlockSpec((1, tq, group * LANES), lambda i, h, t: (i, jnp.minimum(t, nt - 1), h))]
    args = [q]
    nk_total = 0
    for k, vt in kv_sources:
        nk = k.shape[1]
        n_chunks, _, c = vt.shape[1:]
        assert n_chunks * c == nk and nk_total % c == 0
        nk_total += nk
        in_specs.append(pl.BlockSpec((1, nk, kw), lambda i, h, t: (i, 0, h)))
        in_specs.append(pl.BlockSpec((1, n_chunks, LANES, c), lambda i, h, t: (i, 0, h, 0)))
        args += [k, vt]
    return pl.pallas_call(
        functools.partial(_attn_kernel, group=group, shared=shared, n_src=n_src),
        grid=(b, hkv, nt + 1),
        in_specs=in_specs,
        out_specs=pl.BlockSpec((1, tq, ow), lambda i, h, t: (i, jnp.maximum(t - 1, 0), h)),
        out_shape=jax.ShapeDtypeStruct((b, nq, hkv * ow), BF16),
        scratch_shapes=[pltpu.VMEM((nk_total, group * tq), F32),
                        pltpu.VMEM((SUBLANES, group * tq), F32)],
        compiler_params=_params("parallel", "parallel", "arbitrary"),
        name="flash_attention",
    )(*args)


def _merge_kernel(x_ref, o_ref, w_ref, mod_ref, y_ref):
    y = jnp.dot(o_ref[0], w_ref[...], preferred_element_type=F32)
    y_ref[0] = x_ref[0] + mod_ref[0, 2] * y


def _merge(x, o, w_out, mod):
    b, l, d = x.shape
    ko = o.shape[2]
    tm = _row_tile(l, 1024)
    return pl.pallas_call(
        _merge_kernel,
        grid=(b, l // tm),
        in_specs=[pl.BlockSpec((1, tm, d), lambda i, t: (i, t, 0)),
                  pl.BlockSpec((1, tm, ko), lambda i, t: (i, t, 0)),
                  _resident((ko, d)),
                  pl.BlockSpec((1, 6, 1, d), lambda i, t: (i % mod.shape[0], 0, 0, 0))],
        out_specs=pl.BlockSpec((1, tm, d), lambda i, t: (i, t, 0)),
        out_shape=jax.ShapeDtypeStruct((b, l, d), F32),
        compiler_params=_params("parallel", "parallel"),
        name="merge_residual",
    )(x, o, w_out, mod)


def _ffn_kernel(xp_ref, x_ref, xn_ref, mod_ref, ng_ref, win_ref, cw_ref, cb_ref, wout_ref, og_ref, y_ref,
                *, d_ff, out_norm):
    t = pl.program_id(1)
    nt = pl.num_programs(1)
    gain = ng_ref[...]
    shift, scale, gate = mod_ref[0, 3], mod_ref[0, 4], mod_ref[0, 5]
    x, xp, xn = x_ref[0], xp_ref[0], xn_ref[0]
    tm = x.shape[0]
    halo = xp.shape[0]
    hp = jnp.where(t > 0, _modnorm(xp, gain, shift, scale), 0.0)
    hn = jnp.where(t < nt - 1, _modnorm(xn, gain, shift, scale), 0.0)
    h = jnp.concatenate([hp, _modnorm(x, gain, shift, scale), hn], axis=0).astype(BF16)
    rows = tm + 2 * halo

    def conv(p, col):
        w = cw_ref[:, col:col + d_ff]
        prev = pltpu.roll(p, 1, 0)[halo:halo + tm]
        nxt = pltpu.roll(p, rows - 1, 0)[halo:halo + tm]
        cur = p[halo:halo + tm]
        return ((cb_ref[:, col:col + d_ff] + prev * w[0:1]) + cur * w[1:2]) + nxt * w[2:3]

    pa = jnp.dot(h, win_ref[:, :d_ff], preferred_element_type=F32)
    pv = jnp.dot(h, win_ref[:, d_ff:], preferred_element_type=F32)
    g = _silu(conv(pa, 0)) * conv(pv, d_ff)
    y = x + gate * jnp.dot(g.astype(BF16), wout_ref[...], preferred_element_type=F32)
    y_ref[0] = _rms(y, og_ref[...]) if out_norm else y


def _conv_ffn(x, mod, ng, w_in, conv_w, conv_b, w_out, out_gain, out_norm):
    b, l, d = x.shape
    d_ff = w_out.shape[0]
    tm = _row_tile(l, 512)
    halo = SUBLANES
    per = tm // halo
    last = l // halo - 1
    return pl.pallas_call(
        functools.partial(_ffn_kernel, d_ff=d_ff, out_norm=out_norm),
        grid=(b, l // tm),
        in_specs=[pl.BlockSpec((1, halo, d), lambda i, t: (i, jnp.maximum(t * per - 1, 0), 0)),
                  pl.BlockSpec((1, tm, d), lambda i, t: (i, t, 0)),
                  pl.BlockSpec((1, halo, d), lambda i, t: (i, jnp.minimum((t + 1) * per, last), 0)),
                  pl.BlockSpec((1, 6, 1, d), lambda i, t: (i % mod.shape[0], 0, 0, 0)),
                  _resident((1, d)),
                  _resident((d, 2 * d_ff)),
                  _resident((CONV_W, 2 * d_ff)),
                  _resident((1, 2 * d_ff)),
                  _resident((d_ff, d)),
                  _resident((1, d))],
        out_specs=pl.BlockSpec((1, tm, d), lambda i, t: (i, t, 0)),
        out_shape=jax.ShapeDtypeStruct((b, l, d), F32),
        compiler_params=_params("parallel", "arbitrary"),
        name="conv_ffn",
    )(x, x, x, mod, ng, w_in, conv_w, conv_b, w_out, out_gain)


def _hgrn_proj_kernel(x_ref, mod_ref, ng_ref, w_ref, lb_ref, q_ref, v_ref, gf_ref, gb_ref, gate_ref, *, chunk):
    h = _modnorm(x_ref[0], ng_ref[...], mod_ref[0, 0], mod_ref[0, 1]).astype(BF16)
    width = q_ref.shape[2]
    scale = HGRN_DK ** -0.5
    for part in range(HGRN_PARTS):
        for j in range(width // chunk):
            c0 = j * chunk
            p = jnp.dot(h, w_ref[:, part * width + c0:part * width + c0 + chunk], preferred_element_type=F32)
            if part == 0:
                q_ref[0, :, c0:c0 + chunk] = p * scale
            elif part == 1:
                v_ref[0, :, c0:c0 + chunk] = p
            elif part == 4:
                gate_ref[0, :, c0:c0 + chunk] = p
            else:
                lb = lb_ref[:, c0:c0 + chunk]
                f = lb + (1.0 - lb) * _sigmoid(p)
                g_out = gf_ref if part == 2 else gb_ref
                g_out[0, :, c0:c0 + chunk] = jnp.log2(f)


def _hgrn_proj(x, mod, ng, w_in, lb):
    b, l, d = x.shape
    width = w_in.shape[1] // HGRN_PARTS
    tm = _row_tile(l, 512)
    out = jax.ShapeDtypeStruct((b, l, width), F32)
    ospec = pl.BlockSpec((1, tm, width), lambda i, t: (i, t, 0))
    return pl.pallas_call(
        functools.partial(_hgrn_proj_kernel, chunk=512),
        grid=(b, l // tm),
        in_specs=[pl.BlockSpec((1, tm, d), lambda i, t: (i, t, 0)),
                  pl.BlockSpec((1, 6, 1, d), lambda i, t: (i % mod.shape[0], 0, 0, 0)),
                  _resident((1, d)),
                  _resident((d, HGRN_PARTS * width)),
                  _resident((1, width))],
        out_specs=[ospec] * 5,
        out_shape=[out] * 5,
        compiler_params=_params("parallel", "parallel"),
        name="hgrn_proj",
    )(x, mod, ng, w_in, lb)


_HGRN_LEVELS = (32, 16, 8, 4)


def _hgrn_tri(reverse):
    idx = np.arange(HGRN_CHUNK)
    tri = idx[None, :] >= idx[:, None] if reverse else idx[None, :] <= idx[:, None]
    return jnp.asarray(tri.astype(np.float32), dtype=BF16)


def _hgrn_block(q, v, g, tri, st, reverse):
    c = HGRN_CHUNK
    n = q.shape[0] // c
    k = 1.0 - jnp.exp2(g)

    g_hi = g.astype(BF16)
    rest = g - g_hi.astype(F32)
    g_mid = rest.astype(BF16)
    g_lo = (rest - g_mid.astype(F32)).astype(BF16)
    pieces = [piece[i * c:(i + 1) * c] for i in range(n) for piece in (g_hi, g_mid, g_lo)]
    sums = jnp.dot(tri, jnp.concatenate(pieces, axis=1), preferred_element_type=F32)
    cum = jnp.stack([(sums[:, (3 * i) * LANES:(3 * i + 1) * LANES]
                      + sums[:, (3 * i + 1) * LANES:(3 * i + 2) * LANES])
                     + sums[:, (3 * i + 2) * LANES:(3 * i + 3) * LANES] for i in range(n)], axis=0)

    q3, k3, v3 = (a.reshape(n, c, LANES) for a in (q, k, v))
    tot = cum[:, 0:1, :] if reverse else cum[:, c - 1:c, :]
    row = lax.broadcasted_iota(jnp.int32, (c, c), 0)
    col = lax.broadcasted_iota(jnp.int32, (c, c), 1)
    early, late = (row, col) if reverse else (col, row)

    sub_row = lax.broadcasted_iota(jnp.int32, (c, LANES), 0)
    scores = jnp.zeros((n, c, c), F32)
    for half in _HGRN_LEVELS:
        blk = 2 * half
        cb = cum.reshape(n * (c // blk), blk, LANES)
        b_row = half if reverse else half - 1
        d = cb - cb[:, b_row:b_row + 1, :]
        e = jnp.exp2(_neg_abs(d)).reshape(n, c, LANES)
        query_row = ((sub_row % blk) < half) if reverse else ((sub_row % blk) >= half)
        z = (jnp.where(query_row[None], q3, k3) * e).astype(BF16)
        s_l = lax.dot_general(z, z, (((2,), (2,)), ((0,), (0,))), preferred_element_type=F32)
        pick = ((row // blk) == (col // blk)) & ((early % blk) < half) & ((late % blk) >= half)
        scores = jnp.where(pick[None], s_l, scores)

    sub = _HGRN_LEVELS[-1]
    cum2 = cum.reshape(n * c, LANES)
    prods = [q * k]
    for delta in range(1, sub):
        shift = n * c - delta if reverse else delta
        decay = jnp.exp2(jnp.minimum(cum2 - pltpu.roll(cum2, shift, 0), 0.0))
        prods.append(q * pltpu.roll(k, shift, 0) * decay)
    ones_bf = jnp.ones((LANES, LANES), BF16)
    diag = jnp.dot(jnp.concatenate(prods, axis=0).astype(BF16), ones_bf, preferred_element_type=F32)
    for delta in range(sub):
        d_t = diag[delta * n * c:(delta + 1) * n * c, 0:c].reshape(n, c, c)
        if reverse:
            hit = (col == row + delta) & ((row % sub) + delta < sub)
        else:
            hit = (col == row - delta) & ((row % sub) >= delta)
        scores = jnp.where(hit[None], d_t, scores)

    out = lax.dot_general(scores.astype(BF16), v3.astype(BF16), (((2,), (1,)), ((0,), (0,))),
                          preferred_element_type=F32)
    qe = (q3 * jnp.exp2(cum)).astype(BF16)
    kd = (k3 * jnp.exp2(tot - cum)).astype(BF16)
    e_tot = jnp.exp2(tot)
    v_t = jnp.stack([v3[i].T for i in range(n)], axis=0).astype(BF16)
    upd = lax.dot_general(v_t, kd, (((2,), (1,)), ((0,), (0,))), preferred_element_type=F32)
    states = [None] * n
    for i in (range(n - 1, -1, -1) if reverse else range(n)):
        states[i] = st
        st = st * e_tot[i] + upd[i]
    out = out + lax.dot_general(qe, jnp.stack(states, axis=0).astype(BF16), (((2,), (2,)), ((0,), (0,))),
                                preferred_element_type=F32)
    return out.reshape(n * c, LANES), st


def _hgrn_scan_kernel(qf_ref, vf_ref, gf_ref, qb_ref, vb_ref, gb_ref, tf_ref, tb_ref,
                      sf0_ref, sb0_ref, of_ref, ob_ref, sf_ref, sb_ref, st_sc):
    step = pl.program_id(2)

    @pl.when(step == 0)
    def _():
        st_sc[0] = sf0_ref[0, 0]
        st_sc[1] = sb0_ref[0, 0]

    o, st_f = _hgrn_block(qf_ref[0], vf_ref[0], gf_ref[0], tf_ref[...], st_sc[0], False)
    of_ref[0] = o
    o, st_b = _hgrn_block(qb_ref[0], vb_ref[0], gb_ref[0], tb_ref[...], st_sc[1], True)
    ob_ref[0] = o
    st_sc[0] = st_f
    st_sc[1] = st_b

    @pl.when(step == pl.num_programs(2) - 1)
    def _():
        sf_ref[0, 0] = st_f
        sb_ref[0, 0] = st_b


def _hgrn_scan(q, v, gf, gb, sf0, sb0):
    b, l, width = q.shape
    heads = width // LANES
    rb = _row_tile(l, 2048)
    nc = l // rb
    fwd = pl.BlockSpec((1, rb, LANES), lambda i, h, s: (i, s, h))
    bwd = pl.BlockSpec((1, rb, LANES), lambda i, h, s: (i, nc - 1 - s, h))
    st_spec = pl.BlockSpec((1, 1, LANES, LANES), lambda i, h, s: (i, h, 0, 0))
    o_shape = jax.ShapeDtypeStruct((b, l, width), F32)
    s_shape = jax.ShapeDtypeStruct((b, heads, LANES, LANES), F32)
    return pl.pallas_call(
        _hgrn_scan_kernel,
        grid=(b, heads, nc),
        in_specs=[fwd, fwd, fwd, bwd, bwd, bwd,
                  _resident((HGRN_CHUNK, HGRN_CHUNK)), _resident((HGRN_CHUNK, HGRN_CHUNK)), st_spec, st_spec],
        out_specs=[fwd, bwd, st_spec, st_spec],
        out_shape=[o_shape, o_shape, s_shape, s_shape],
        scratch_shapes=[pltpu.VMEM((2, LANES, LANES), F32)],
        compiler_params=_params("parallel", "parallel", "arbitrary"),
        name="hgrn_scan",
    )(q, v, gf, q, v, gb, _hgrn_tri(False), _hgrn_tri(True), sf0, sb0)


def _hgrn_merge_kernel(x_ref, of_ref, ob_ref, gate_ref, og_ref, w_ref, mod_ref, y_ref):
    width = of_ref.shape[2]
    og = og_ref[...]
    parts = []
    for hd in range(width // LANES):
        sl = slice(hd * LANES, (hd + 1) * LANES)
        o = of_ref[0, :, sl] + ob_ref[0, :, sl]
        parts.append((_rms(o, og) * _silu(gate_ref[0, :, sl])).astype(BF16))
    r = jnp.concatenate(parts, axis=1)
    y = jnp.dot(r, w_ref[...], preferred_element_type=F32)
    y_ref[0] = x_ref[0] + mod_ref[0, 2] * y


def _hgrn_merge(x, o_f, o_b, gate, o_gain, w_out, mod):
    b, l, d = x.shape
    width = o_f.shape[2]
    tm = _row_tile(l, 512)
    wide = pl.BlockSpec((1, tm, width), lambda i, t: (i, t, 0))
    return pl.pallas_call(
        _hgrn_merge_kernel,
        grid=(b, l // tm),
        in_specs=[pl.BlockSpec((1, tm, d), lambda i, t: (i, t, 0)), wide, wide, wide,
                  _resident((1, LANES)),
                  _resident((width, d)),
                  pl.BlockSpec((1, 6, 1, d), lambda i, t: (i % mod.shape[0], 0, 0, 0))],
        out_specs=pl.BlockSpec((1, tm, d), lambda i, t: (i, t, 0)),
        out_shape=jax.ShapeDtypeStruct((b, l, d), F32),
        compiler_params=_params("parallel", "parallel"),
        name="hgrn_merge",
    )(x, o_f, o_b, gate, o_gain, w_out, mod)


def _mla_rope(y, cos, sin_lo, sin_hi):
    half = MLA_ROPE // 2
    return y * cos + pltpu.roll(y, LANES - half, 1) * sin_lo + pltpu.roll(y, half, 1) * sin_hi


def _mla_proj_kernel(x_ref, mod_ref, ng_ref, w_ref, qg_ref, kvg_ref, wq_ref, wk_ref, wv_ref,
                     cos_ref, slo_ref, shi_ref, q_ref, k_ref, v_ref, *, rotate, q_lora, kv_lora, chunk):
    h = _modnorm(x_ref[0], ng_ref[...], mod_ref[0, 0], mod_ref[0, 1]).astype(BF16)
    p = jnp.dot(h, w_ref[...], preferred_element_type=F32)
    cq = _rms(p[:, :q_lora], qg_ref[...]).astype(BF16)
    ckv = _rms(p[:, q_lora:q_lora + kv_lora], kvg_ref[...]).astype(BF16)
    k_rope = p[:, q_lora + kv_lora:]
    scale = (MLA_NOPE + MLA_ROPE) ** -0.5 * LOG2_E
    if rotate:
        cos, slo, shi = cos_ref[...], slo_ref[...], shi_ref[...]
        k_rope = _mla_rope(k_rope, cos, slo, shi)
    n = wq_ref.shape[1]
    for j in range(n // chunk):
        cs = slice(j * chunk, (j + 1) * chunk)
        pq = jnp.dot(cq, wq_ref[:, cs], preferred_element_type=F32)
        pk = jnp.dot(ckv, wk_ref[:, cs], preferred_element_type=F32)
        for u in range(chunk // LANES):
            us = slice(u * LANES, (u + 1) * LANES)
            os = slice(j * chunk + u * LANES, j * chunk + (u + 1) * LANES)
            qh = pq[:, us]
            if rotate:
                qh = _mla_rope(qh, cos, slo, shi)
            q_ref[0, :, os] = (qh * scale).astype(BF16)
            k_ref[0, :, os] = (pk[:, us] + k_rope).astype(BF16)
    for j in range(wv_ref.shape[1] // chunk):
        pv = jnp.dot(ckv, wv_ref[:, j * chunk:(j + 1) * chunk], preferred_element_type=F32)
        for u in range(chunk // LANES):
            os = slice(j * chunk + u * LANES, j * chunk + (u + 1) * LANES)
            v_ref[0, 0, os, :] = pv[:, u * LANES:(u + 1) * LANES].T.astype(BF16)


def _mla_proj(x, mod, ng, w_in, qg, kvg, wq, wk, wv, cos, slo, shi, rotate):
    b, l, d = x.shape
    n = wq.shape[1]
    q_lora, kv_lora = wq.shape[0], wk.shape[0]
    tm = _row_tile(l, 512)
    kern = functools.partial(_mla_proj_kernel, rotate=rotate, q_lora=q_lora, kv_lora=kv_lora, chunk=512)
    out = jax.ShapeDtypeStruct((b, l, n), BF16)
    ospec = pl.BlockSpec((1, tm, n), lambda i, t: (i, t, 0))
    tab = pl.BlockSpec((tm, LANES), lambda i, t: (t, 0))
    return pl.pallas_call(
        kern,
        grid=(b, l // tm),
        in_specs=[pl.BlockSpec((1, tm, d), lambda i, t: (i, t, 0)),
                  pl.BlockSpec((1, 6, 1, d), lambda i, t: (i % mod.shape[0], 0, 0, 0)),
                  _resident((1, d)),
                  _resident(w_in.shape),
                  _resident((1, q_lora)),
                  _resident((1, kv_lora)),
                  _resident(wq.shape), _resident(wk.shape), _resident(wv.shape),
                  tab, tab, tab],
        out_specs=[ospec, ospec, pl.BlockSpec((1, 1, wv.shape[1], tm), lambda i, t: (i, t, 0, 0))],
        out_shape=[out, out, jax.ShapeDtypeStruct((b, l // tm, wv.shape[1], tm), BF16)],
        compiler_params=_params("parallel", "parallel"),
        name="mla_proj",
    )(x, mod, ng, w_in, qg, kvg, wq, wk, wv, cos, slo, shi)


def _axial_angles(rows, rot_dim):
    row = jnp.repeat(jnp.arange(rows, dtype=F32), GRID_W)
    col = jnp.tile(jnp.arange(GRID_W, dtype=F32), rows)
    axis_dim = rot_dim // 2
    inv_freq = jnp.power(ROPE_THETA, -jnp.arange(0, axis_dim, 2, dtype=F32) / axis_dim)
    ang = jnp.concatenate([row[:, None] * inv_freq, col[:, None] * inv_freq], axis=-1)
    return jnp.cos(ang), jnp.sin(ang)


def _gqa_tables(rows):
    cos, sin = _axial_angles(rows, GQA_HEAD_DIM)
    return jnp.concatenate([cos, cos], axis=-1), jnp.concatenate([-sin, sin], axis=-1)


def _mla_tables(rows):
    cos, sin = _axial_angles(rows, MLA_ROPE)
    s = cos.shape[0]
    half = MLA_ROPE // 2
    ones = jnp.ones((s, MLA_NOPE), F32)
    zeros = jnp.zeros((s, MLA_NOPE), F32)
    tail1 = jnp.ones((s, LANES - MLA_NOPE - MLA_ROPE), F32)
    tail0 = jnp.zeros((s, LANES - MLA_NOPE - MLA_ROPE), F32)
    zh = jnp.zeros((s, half), F32)
    c = jnp.concatenate([ones, cos, cos, tail1], axis=-1)
    s_lo = jnp.concatenate([zeros, -sin, zh, tail0], axis=-1)
    s_hi = jnp.concatenate([zeros, zh, sin, tail0], axis=-1)
    return c, s_lo, s_hi


def _mla_weights(w_in, w_qb, w_kvb, w_out):
    d = w_in.shape[0]
    q_lora, kv_lora = w_qb.shape[0], w_kvb.shape[0]
    heads = w_qb.shape[1] // (MLA_NOPE + MLA_ROPE)
    kr = jnp.zeros((d, LANES), w_in.dtype).at[:, MLA_NOPE:MLA_NOPE + MLA_ROPE].set(w_in[:, q_lora + kv_lora:])
    w_in_p = jnp.concatenate([w_in[:, :q_lora + kv_lora], kr], axis=1)
    wq = w_qb.reshape(q_lora, heads, MLA_NOPE + MLA_ROPE)
    wq = jnp.pad(wq, ((0, 0), (0, 0), (0, LANES - MLA_NOPE - MLA_ROPE))).reshape(q_lora, heads * LANES)
    wkv = w_kvb.reshape(kv_lora, heads, MLA_NOPE + MLA_V)
    wk = jnp.pad(wkv[:, :, :MLA_NOPE], ((0, 0), (0, 0), (0, LANES - MLA_NOPE))).reshape(kv_lora, heads * LANES)
    wv = jnp.pad(wkv[:, :, MLA_NOPE:], ((0, 0), (0, 0), (0, LANES - MLA_V))).reshape(kv_lora, heads * LANES)
    wo = w_out.reshape(heads, MLA_V, -1)
    wo = jnp.pad(wo, ((0, 0), (0, LANES - MLA_V), (0, 0))).reshape(heads * LANES, -1)
    return w_in_p.astype(BF16), wq.astype(BF16), wk.astype(BF16), wv.astype(BF16), wo.astype(BF16)


def kernel(x, c, ctx, c_ctx, w_ada, b_ada, norm_mix, norm_ffn, ffn_w_in, ffn_conv_w, ffn_conv_b, ffn_w_out,
           gqa_w_in, gqa_q_norm, gqa_k_norm, gqa_w_out, hgrn_w_in, hgrn_out_norm, hgrn_w_out, hgrn_lower_bounds,
           mla_w_in, mla_q_norm, mla_kv_norm, mla_w_qb, mla_w_kvb, mla_w_out, final_norm):
    batch, seq, d = x.shape
    depth = w_ada.shape[0]
    n_mixers = 3
    rows = seq // GRID_W
    assert batch + 1 <= SUBLANES

    cv = jnp.zeros((SUBLANES, d), F32).at[:batch].set(c).at[batch].set(c_ctx)
    mods = _ada_mods(cv, w_ada, b_ada).reshape(depth, SUBLANES, 6, 1, d)
    lb_all = _lower_bounds(hgrn_lower_bounds)

    cos_a, sin_a = _gqa_tables(rows)
    cos_m, slo_m, shi_m = _mla_tables(rows)

    for i in range(depth):
        last = i == depth - 1
        kind = i % n_mixers
        j = i // n_mixers
        mod = mods[i, :batch]
        mod_c = mods[i, batch:batch + 1]
        ng = norm_mix[i][None, :]

        if kind == 0:
            w_in = gqa_w_in[j].astype(BF16)
            w_out = gqa_w_out[j].astype(BF16)
            qg, kg = gqa_q_norm[j][None, :], gqa_k_norm[j][None, :]
            q, k, v = _gqa_proj(x, mod, ng, w_in, qg, kg, cos_a, sin_a, True)
            n_ctx = ctx.shape[1]
            qc, kc, vc = _gqa_proj(ctx, mod_c, ng, w_in, qg, kg, cos_a[:n_ctx], sin_a[:n_ctx], False)
            o = _attention(q, [(k, v), (kc, vc)], GQA_GROUP, True, 256)
            x = _merge(x, o, w_out, mod)
            if not last:
                oc = _attention(qc, [(kc, vc)], GQA_GROUP, True, 256)
                ctx = _merge(ctx, oc, w_out, mod_c)
        elif kind == 1:
            w_in = hgrn_w_in[j].astype(BF16)
            w_out = hgrn_w_out[j].astype(BF16)
            lb = lb_all[i][None, :]
            og = hgrn_out_norm[j][None, :]
            heads = w_out.shape[0] // HGRN_DK
            qc, vc, gfc, gbc, gatec = _hgrn_proj(ctx, mod_c, ng, w_in, lb)
            q, v, gf, gb, gate = _hgrn_proj(x, mod, ng, w_in, lb)
            s0 = jnp.zeros((batch, heads, HGRN_DK, HGRN_DK), F32)
            oc_f, oc_b, s_f, s_b = _hgrn_scan(qc, vc, gfc, gbc, s0, s0)
            o_f, o_b, _, _ = _hgrn_scan(q, v, gf, gb, s_f, s_b)
            x = _hgrn_merge(x, o_f, o_b, gate, og, w_out, mod)
            if not last:
                ctx = _hgrn_merge(ctx, oc_f, oc_b, gatec, og, w_out, mod_c)
        else:
            w_in, wq, wk, wv, w_out = _mla_weights(mla_w_in[j], mla_w_qb[j], mla_w_kvb[j], mla_w_out[j])
            qg, kvg = mla_q_norm[j][None, :], mla_kv_norm[j][None, :]
            n_ctx = ctx.shape[1]
            q, k, v = _mla_proj(x, mod, ng, w_in, qg, kvg, wq, wk, wv, cos_m, slo_m, shi_m, True)
            qc, kc, vc = _mla_proj(ctx, mod_c, ng, w_in, qg, kvg, wq, wk, wv,
                                   cos_m[:n_ctx], slo_m[:n_ctx], shi_m[:n_ctx], False)
            o = _attention(q, [(k, v), (kc, vc)], 1, True, 512)
            x = _merge(x, o, w_out, mod)
            if not last:
                oc = _attention(qc, [(kc, vc)], 1, True, 512)
                ctx = _merge(ctx, oc, w_out, mod_c)

        fg = norm_ffn[i][None, :]
        f_in = ffn_w_in[i].astype(BF16)
        f_out = ffn_w_out[i].astype(BF16)
        f_cw = ffn_conv_w[i]
        f_cb = ffn_conv_b[i][None, :]
        x = _conv_ffn(x, mod, fg, f_in, f_cw, f_cb, f_out, final_norm[None, :], last)
        if not last:
            ctx = _conv_ffn(ctx, mod_c, fg, f_in, f_cw, f_cb, f_out, final_norm[None, :], False)

    return x
```

```python
import functools

import numpy as np
import jax
import jax.numpy as jnp
from jax import lax
from jax.experimental import pallas as pl
from jax.experimental.pallas import tpu as pltpu

F32 = jnp.float32
BF16 = jnp.bfloat16
HIGHEST = lax.Precision.HIGHEST

GRID_W = 64
ROPE_THETA = 10000.0
NORM_EPS = 1e-6
CONV_W = 3

LANES = 128
SUBLANES = 8

GQA_HEAD_DIM = 128
GQA_GROUP = 2

HGRN_DK = 128
HGRN_CHUNK = 64
HGRN_PARTS = 5

MLA_NOPE = 64
MLA_ROPE = 32
MLA_V = 64

VMEM_LIMIT = 56 * 1024 * 1024

LOG2_E = 1.4426950408889634
ATTN_UNROLL = 16


def _params(*sem):
    return pltpu.CompilerParams(dimension_semantics=sem, vmem_limit_bytes=VMEM_LIMIT)


def _resident(shape):
    nd = len(shape)
    return pl.BlockSpec(shape, lambda *_: (0,) * nd, pipeline_mode=pl.Buffered(1))


def _silu(x):
    return x / (1.0 + jnp.exp2(x * -LOG2_E))


def _sigmoid(x):
    return 1.0 / (1.0 + jnp.exp2(x * -LOG2_E))


def _neg_abs(x):
    bits = lax.bitcast_convert_type(x, jnp.uint32) | jnp.uint32(0x80000000)
    return lax.bitcast_convert_type(bits, F32)


def _rms(x, gain):
    return x * lax.rsqrt(jnp.mean(x * x, axis=-1, keepdims=True) + NORM_EPS) * gain


def _head_rms(p, gain):
    ones_bf = jnp.ones((LANES, LANES), BF16)
    ss = jnp.dot((p * p).astype(BF16), ones_bf, preferred_element_type=F32)
    return p * lax.rsqrt(ss * (1.0 / LANES) + NORM_EPS) * gain


def _modnorm(x, gain, shift, scale):
    return _rms(x, gain) * (1.0 + scale) + shift


def _aligned_ds(start, size):
    if isinstance(start, int):
        return pl.ds(start, size)
    return pl.ds(pl.multiple_of(start, size), size)


def _row_tile(n, want):
    t = min(n, want)
    while n % t:
        t -= SUBLANES
    assert t > 0 and t % SUBLANES == 0, (n, want)
    return t


def _ada_kernel(cv_ref, w_ref, b_ref, o_ref):
    s = _silu(cv_ref[...])
    o_ref[0] = jnp.dot(s, w_ref[0], precision=HIGHEST, preferred_element_type=F32) + b_ref[0]


def _ada_mods(cv, w_ada, b_ada):
    depth, d, n = w_ada.shape
    tn = 1536
    return pl.pallas_call(
        _ada_kernel,
        grid=(depth, n // tn),
        in_specs=[pl.BlockSpec((SUBLANES, d), lambda i, j: (0, 0)),
                  pl.BlockSpec((1, d, tn), lambda i, j: (i, 0, j)),
                  pl.BlockSpec((1, 1, tn), lambda i, j: (i, 0, j))],
        out_specs=pl.BlockSpec((1, SUBLANES, tn), lambda i, j: (i, 0, j)),
        out_shape=jax.ShapeDtypeStruct((depth, SUBLANES, n), F32),
        compiler_params=_params("arbitrary", "arbitrary"),
        name="ada_mods",
    )(cv, w_ada, b_ada.reshape(depth, 1, n))


def _lb_kernel(x_ref, o_ref):
    depth = x_ref.shape[0]
    rows = [x_ref[i:i + 1, :] for i in range(depth)]
    m = rows[0]
    for r in rows[1:]:
        m = jnp.maximum(m, r)
    e = [jnp.exp(r - m) for r in rows]
    tot = e[0]
    for r in e[1:]:
        tot = tot + r
    p = [r / tot for r in e]
    cum = p[0]
    o_ref[0:1, :] = cum - p[0]
    for i in range(1, depth):
        cum = cum + p[i]
        o_ref[i:i + 1, :] = cum - p[0]


def _lower_bounds(lb_raw):
    return pl.pallas_call(
        _lb_kernel,
        out_shape=jax.ShapeDtypeStruct(lb_raw.shape, F32),
        name="hgrn_lower_bounds",
    )(lb_raw.astype(F32))


def _gqa_proj_kernel(x_ref, mod_ref, ng_ref, w_ref, qg_ref, kg_ref, cos_ref, sin_ref,
                     q_ref, k_ref, v_ref, *, rotate, qd, kd, chunk):
    h = _modnorm(x_ref[0], ng_ref[...], mod_ref[0, 0], mod_ref[0, 1]).astype(BF16)
    scale = GQA_HEAD_DIM ** -0.5 * LOG2_E
    qg = qg_ref[...] * scale
    kg = kg_ref[...]
    if rotate:
        cos = cos_ref[...]
        sin = sin_ref[...]

    def head(p, gain):
        y = _head_rms(p, gain)
        if rotate:
            y = y * cos + pltpu.roll(y, GQA_HEAD_DIM // 2, 1) * sin
        return y

    n = w_ref.shape[1]
    for j in range(n // chunk):
        p = jnp.dot(h, w_ref[:, j * chunk:(j + 1) * chunk], preferred_element_type=F32)
        for u in range(chunk // LANES):
            col = j * chunk + u * LANES
            ph = p[:, u * LANES:(u + 1) * LANES]
            if col < qd:
                q_ref[0, :, col:col + LANES] = head(ph, qg).astype(BF16)
            elif col < qd + kd:
                k_ref[0, :, col - qd:col - qd + LANES] = head(ph, kg).astype(BF16)
            else:
                c0 = col - qd - kd
                v_ref[0, 0, c0:c0 + LANES, :] = ph.T.astype(BF16)


def _gqa_proj(x, mod, ng, w_in, qg, kg, cos, sin, rotate):
    b, l, d = x.shape
    n = w_in.shape[1]
    kd = n // 4
    qd = n - 2 * kd
    tm = _row_tile(l, 512)
    kern = functools.partial(_gqa_proj_kernel, rotate=rotate, qd=qd, kd=kd, chunk=512)
    return pl.pallas_call(
        kern,
        grid=(b, l // tm),
        in_specs=[pl.BlockSpec((1, tm, d), lambda i, t: (i, t, 0)),
                  pl.BlockSpec((1, 6, 1, d), lambda i, t: (i % mod.shape[0], 0, 0, 0)),
                  _resident((1, d)),
                  _resident((d, n)),
                  _resident((1, LANES)),
                  _resident((1, LANES)),
                  pl.BlockSpec((tm, LANES), lambda i, t: (t, 0)),
                  pl.BlockSpec((tm, LANES), lambda i, t: (t, 0))],
        out_specs=[pl.BlockSpec((1, tm, qd), lambda i, t: (i, t, 0)),
                   pl.BlockSpec((1, tm, kd), lambda i, t: (i, t, 0)),
                   pl.BlockSpec((1, 1, kd, tm), lambda i, t: (i, t, 0, 0))],
        out_shape=[jax.ShapeDtypeStruct((b, l, qd), BF16),
                   jax.ShapeDtypeStruct((b, l, kd), BF16),
                   jax.ShapeDtypeStruct((b, l // tm, kd, tm), BF16)],
        compiler_params=_params("parallel", "parallel"),
        name="gqa_proj",
    )(x, mod, ng, w_in, qg, kg, cos, sin)


def _attn_kernel(*refs, group, shared, n_src):
    q_ref = refs[0]
    kv_refs = refs[1:1 + 2 * n_src]
    o_ref = refs[1 + 2 * n_src]
    s_sc, m_sc = refs[2 + 2 * n_src:]
    tq = q_ref.shape[1]
    nq = group * tq
    dv = LANES if shared else LANES // group
    step = pl.program_id(2)

    def fold(a):
        return a.reshape(a.shape[0] // SUBLANES, SUBLANES, nq)

    def run(do_scores, do_weigh):
        carry = {}
        if do_scores:
            q = q_ref[0]
            q_heads = [q[:, g * LANES:(g + 1) * LANES] for g in range(group)]
            q_all = jnp.concatenate(q_heads, axis=0)
            carry["m8"] = jnp.full((SUBLANES, nq), -jnp.inf, F32)
        if do_weigh:
            m = jnp.max(m_sc[...], axis=0, keepdims=True)
            carry["l8"] = jnp.zeros((SUBLANES, nq), F32)
            carry["acc"] = jnp.zeros((dv, nq), F32)
        row0 = 0
        for i in range(n_src):
            k_ref, vt_ref = kv_refs[2 * i], kv_refs[2 * i + 1]
            n_chunks, _, c = vt_ref.shape[1:]

            def body(j, carry, k_ref=k_ref, vt_ref=vt_ref, c=c, row0=row0):
                carry = dict(carry)
                rows = _aligned_ds(row0 + j * c, c)
                if do_weigh:
                    s_old = s_sc[rows, :]
                if do_scores:
                    kc = k_ref[0, _aligned_ds(j * c, c), :]
                    nt_dims = (((1,), (1,)), ((), ()))
                    if shared:
                        s = lax.dot_general(kc, q_all, nt_dims, preferred_element_type=F32)
                    else:
                        s = jnp.concatenate(
                            [lax.dot_general(kc[:, g * LANES:(g + 1) * LANES], q_heads[g], nt_dims,
                                             preferred_element_type=F32) for g in range(group)], axis=1)
                    s_sc[rows, :] = s
                    carry["m8"] = jnp.maximum(carry["m8"], jnp.max(fold(s), axis=0))
                if do_weigh:
                    p = jnp.exp2(s_old - m)
                    carry["l8"] = carry["l8"] + jnp.sum(fold(p), axis=0)
                    pb = p.astype(BF16)
                    vt = vt_ref[0, j]
                    if shared:
                        upd = jnp.dot(vt, pb, preferred_element_type=F32)
                    else:
                        upd = jnp.concatenate(
                            [jnp.dot(vt[g * dv:(g + 1) * dv], pb[:, g * tq:(g + 1) * tq],
                                     preferred_element_type=F32) for g in range(group)], axis=1)
                    carry["acc"] = carry["acc"] + upd
                return carry

            if n_chunks == 1:
                carry = body(0, carry)
            else:
                carry = lax.fori_loop(0, n_chunks, body, carry, unroll=ATTN_UNROLL)
            row0 += n_chunks * c
        if do_scores:
            m_sc[...] = carry["m8"]
        if do_weigh:
            o = carry["acc"] / jnp.sum(carry["l8"], axis=0, keepdims=True)
            o_ref[0] = jnp.concatenate([o[:, g * tq:(g + 1) * tq].T for g in range(group)],
                                       axis=1).astype(o_ref.dtype)

    n_tiles = pl.num_programs(2) - 1

    @pl.when(step == 0)
    def _():
        run(True, False)

    @pl.when((step > 0) & (step < n_tiles))
    def _():
        run(True, True)

    @pl.when(step == n_tiles)
    def _():
        run(False, True)


def _attention(q, kv_sources, group, shared, tq):
    b, nq, hd = q.shape
    hkv = hd // (group * LANES)
    kw = LANES if shared else group * LANES
    ow = group * LANES if shared else LANES
    tq = _row_tile(nq, tq)
    nt = nq // tq
    n_src = len(kv_sources)
    in_specs = [pl.BlockSpec((1, tq, group * LANES), lambda i, h, t: (i, jnp.minimum(t, nt - 1), h))]
    args = [q]
    nk_total = 0
    for k, vt in kv_sources:
        nk = k.shape[1]
        n_chunks, _, c = vt.shape[1:]
        assert n_chunks * c == nk and nk_total % c == 0
        nk_total += nk
        in_specs.append(pl.BlockSpec((1, nk, kw), lambda i, h, t: (i, 0, h)))
        in_specs.append(pl.BlockSpec((1, n_chunks, LANES, c), lambda i, h, t: (i, 0, h, 0)))
        args += [k, vt]
    return pl.pallas_call(
        functools.partial(_attn_kernel, group=group, shared=shared, n_src=n_src),
        grid=(b, hkv, nt + 1),
        in_specs=in_specs,
        out_specs=pl.BlockSpec((1, tq, ow), lambda i, h, t: (i, jnp.maximum(t - 1, 0), h)),
        out_shape=jax.ShapeDtypeStruct((b, nq, hkv * ow), BF16),
        scratch_shapes=[pltpu.VMEM((nk_total, group * tq), F32),
                        pltpu.VMEM((SUBLANES, group * tq), F32)],
        compiler_params=_params("parallel", "parallel", "arbitrary"),
        name="flash_attention",
    )(*args)


def _merge_kernel(x_ref, o_ref, w_ref, mod_ref, y_ref):
    y = jnp.dot(o_ref[0], w_ref[...], preferred_element_type=F32)
    y_ref[0] = x_ref[0] + mod_ref[0, 2] * y


def _merge(x, o, w_out, mod):
    b, l, d = x.shape
    ko = o.shape[2]
    tm = _row_tile(l, 1024)
    return pl.pallas_call(
        _merge_kernel,
        grid=(b, l // tm),
        in_specs=[pl.BlockSpec((1, tm, d), lambda i, t: (i, t, 0)),
                  pl.BlockSpec((1, tm, ko), lambda i, t: (i, t, 0)),
                  _resident((ko, d)),
                  pl.BlockSpec((1, 6, 1, d), lambda i, t: (i % mod.shape[0], 0, 0, 0))],
        out_specs=pl.BlockSpec((1, tm, d), lambda i, t: (i, t, 0)),
        out_shape=jax.ShapeDtypeStruct((b, l, d), F32),
        compiler_params=_params("parallel", "parallel"),
        name="merge_residual",
    )(x, o, w_out, mod)


def _ffn_kernel(xp_ref, x_ref, xn_ref, mod_ref, ng_ref, win_ref, cw_ref, cb_ref, wout_ref, og_ref, y_ref,
                *, d_ff, out_norm):
    t = pl.program_id(1)
    nt = pl.num_programs(1)
    gain = ng_ref[...]
    shift, scale, gate = mod_ref[0, 3], mod_ref[0, 4], mod_ref[0, 5]
    x, xp, xn = x_ref[0], xp_ref[0], xn_ref[0]
    tm = x.shape[0]
    halo = xp.shape[0]
    hp = jnp.where(t > 0, _modnorm(xp, gain, shift, scale), 0.0)
    hn = jnp.where(t < nt - 1, _modnorm(xn, gain, shift, scale), 0.0)
    h = jnp.concatenate([hp, _modnorm(x, gain, shift, scale), hn], axis=0).astype(BF16)
    rows = tm + 2 * halo

    def conv(p, col):
        w = cw_ref[:, col:col + d_ff]
        prev = pltpu.roll(p, 1, 0)[halo:halo + tm]
        nxt = pltpu.roll(p, rows - 1, 0)[halo:halo + tm]
        cur = p[halo:halo + tm]
        return ((cb_ref[:, col:col + d_ff] + prev * w[0:1]) + cur * w[1:2]) + nxt * w[2:3]

    pa = jnp.dot(h, win_ref[:, :d_ff], preferred_element_type=F32)
    pv = jnp.dot(h, win_ref[:, d_ff:], preferred_element_type=F32)
    g = _silu(conv(pa, 0)) * conv(pv, d_ff)
    y = x + gate * jnp.dot(g.astype(BF16), wout_ref[...], preferred_element_type=F32)
    y_ref[0] = _rms(y, og_ref[...]) if out_norm else y


def _conv_ffn(x, mod, ng, w_in, conv_w, conv_b, w_out, out_gain, out_norm):
    b, l, d = x.shape
    d_ff = w_out.shape[0]
    tm = _row_tile(l, 512)
    halo = SUBLANES
    per = tm // halo
    last = l // halo - 1
    return pl.pallas_call(
        functools.partial(_ffn_kernel, d_ff=d_ff, out_norm=out_norm),
        grid=(b, l // tm),
        in_specs=[pl.BlockSpec((1, halo, d), lambda i, t: (i, jnp.maximum(t * per - 1, 0), 0)),
                  pl.BlockSpec((1, tm, d), lambda i, t: (i, t, 0)),
                  pl.BlockSpec((1, halo, d), lambda i, t: (i, jnp.minimum((t + 1) * per, last), 0)),
                  pl.BlockSpec((1, 6, 1, d), lambda i, t: (i % mod.shape[0], 0, 0, 0)),
                  _resident((1, d)),
                  _resident((d, 2 * d_ff)),
                  _resident((CONV_W, 2 * d_ff)),
                  _resident((1, 2 * d_ff)),
                  _resident((d_ff, d)),
                  _resident((1, d))],
        out_specs=pl.BlockSpec((1, tm, d), lambda i, t: (i, t, 0)),
        out_shape=jax.ShapeDtypeStruct((b, l, d), F32),
        compiler_params=_params("parallel", "arbitrary"),
        name="conv_ffn",
    )(x, x, x, mod, ng, w_in, conv_w, conv_b, w_out, out_gain)


def _hgrn_proj_kernel(x_ref, mod_ref, ng_ref, w_ref, lb_ref, q_ref, v_ref, gf_ref, gb_ref, gate_ref, *, chunk):
    h = _modnorm(x_ref[0], ng_ref[...], mod_ref[0, 0], mod_ref[0, 1]).astype(BF16)
    width = q_ref.shape[2]
    scale = HGRN_DK ** -0.5
    for part in range(HGRN_PARTS):
        for j in range(width // chunk):
            c0 = j * chunk
            p = jnp.dot(h, w_ref[:, part * width + c0:part * width + c0 + chunk], preferred_element_type=F32)
            if part == 0:
                q_ref[0, :, c0:c0 + chunk] = p * scale
            elif part == 1:
                v_ref[0, :, c0:c0 + chunk] = p
            elif part == 4:
                gate_ref[0, :, c0:c0 + chunk] = p
            else:
                lb = lb_ref[:, c0:c0 + chunk]
                f = lb + (1.0 - lb) * _sigmoid(p)
                g_out = gf_ref if part == 2 else gb_ref
                g_out[0, :, c0:c0 + chunk] = jnp.log2(f)


def _hgrn_proj(x, mod, ng, w_in, lb):
    b, l, d = x.shape
    width = w_in.shape[1] // HGRN_PARTS
    tm = _row_tile(l, 512)
    out = jax.ShapeDtypeStruct((b, l, width), F32)
    ospec = pl.BlockSpec((1, tm, width), lambda i, t: (i, t, 0))
    return pl.pallas_call(
        functools.partial(_hgrn_proj_kernel, chunk=512),
        grid=(b, l // tm),
        in_specs=[pl.BlockSpec((1, tm, d), lambda i, t: (i, t, 0)),
                  pl.BlockSpec((1, 6, 1, d), lambda i, t: (i % mod.shape[0], 0, 0, 0)),
                  _resident((1, d)),
                  _resident((d, HGRN_PARTS * width)),
                  _resident((1, width))],
        out_specs=[ospec] * 5,
        out_shape=[out] * 5,
        compiler_params=_params("parallel", "parallel"),
        name="hgrn_proj",
    )(x, mod, ng, w_in, lb)


_HGRN_LEVELS = (32, 16, 8, 4)


def _hgrn_tri(reverse):
    idx = np.arange(HGRN_CHUNK)
    tri = idx[None, :] >= idx[:, None] if reverse else idx[None, :] <= idx[:, None]
    return jnp.asarray(tri.astype(np.float32), dtype=BF16)


def _hgrn_block(q, v, g, tri, st, reverse):
    c = HGRN_CHUNK
    n = q.shape[0] // c
    k = 1.0 - jnp.exp2(g)

    g_hi = g.astype(BF16)
    rest = g - g_hi.astype(F32)
    g_mid = rest.astype(BF16)
    g_lo = (rest - g_mid.astype(F32)).astype(BF16)
    pieces = [piece[i * c:(i + 1) * c] for i in range(n) for piece in (g_hi, g_mid, g_lo)]
    sums = jnp.dot(tri, jnp.concatenate(pieces, axis=1), preferred_element_type=F32)
    cum = jnp.stack([(sums[:, (3 * i) * LANES:(3 * i + 1) * LANES]
                      + sums[:, (3 * i + 1) * LANES:(3 * i + 2) * LANES])
                     + sums[:, (3 * i + 2) * LANES:(3 * i + 3) * LANES] for i in range(n)], axis=0)

    q3, k3, v3 = (a.reshape(n, c, LANES) for a in (q, k, v))
    tot = cum[:, 0:1, :] if reverse else cum[:, c - 1:c, :]
    row = lax.broadcasted_iota(jnp.int32, (c, c), 0)
    col = lax.broadcasted_iota(jnp.int32, (c, c), 1)
    early, late = (row, col) if reverse else (col, row)

    sub_row = lax.broadcasted_iota(jnp.int32, (c, LANES), 0)
    scores = jnp.zeros((n, c, c), F32)
    for half in _HGRN_LEVELS:
        blk = 2 * half
        cb = cum.reshape(n * (c // blk), blk, LANES)
        b_row = half if reverse else half - 1
        d = cb - cb[:, b_row:b_row + 1, :]
        e = jnp.exp2(_neg_abs(d)).reshape(n, c, LANES)
        query_row = ((sub_row % blk) < half) if reverse else ((sub_row % blk) >= half)
        z = (jnp.where(query_row[None], q3, k3) * e).astype(BF16)
        s_l = lax.dot_general(z, z, (((2,), (2,)), ((0,), (0,))), preferred_element_type=F32)
        pick = ((row // blk) == (col // blk)) & ((early % blk) < half) & ((late % blk) >= half)
        scores = jnp.where(pick[None], s_l, scores)

    sub = _HGRN_LEVELS[-1]
    cum2 = cum.reshape(n * c, LANES)
    prods = [q * k]
    for delta in range(1, sub):
        shift = n * c - delta if reverse else delta
        decay = jnp.exp2(jnp.minimum(cum2 - pltpu.roll(cum2, shift, 0), 0.0))
        prods.append(q * pltpu.roll(k, shift, 0) * decay)
    ones_bf = jnp.ones((LANES, LANES), BF16)
    diag = jnp.dot(jnp.concatenate(prods, axis=0).astype(BF16), ones_bf, preferred_element_type=F32)
    for delta in range(sub):
        d_t = diag[delta * n * c:(delta + 1) * n * c, 0:c].reshape(n, c, c)
        if reverse:
            hit = (col == row + delta) & ((row % sub) + delta < sub)
        else:
            hit = (col == row - delta) & ((row % sub) >= delta)
        scores = jnp.where(hit[None], d_t, scores)

    out = lax.dot_general(scores.astype(BF16), v3.astype(BF16), (((2,), (1,)), ((0,), (0,))),
                          preferred_element_type=F32)
    qe = (q3 * jnp.exp2(cum)).astype(BF16)
    kd = (k3 * jnp.exp2(tot - cum)).astype(BF16)
    e_tot = jnp.exp2(tot)
    v_t = jnp.stack([v3[i].T for i in range(n)], axis=0).astype(BF16)
    upd = lax.dot_general(v_t, kd, (((2,), (1,)), ((0,), (0,))), preferred_element_type=F32)
    states = [None] * n
    for i in (range(n - 1, -1, -1) if reverse else range(n)):
        states[i] = st
        st = st * e_tot[i] + upd[i]
    out = out + lax.dot_general(qe, jnp.stack(states, axis=0).astype(BF16), (((2,), (2,)), ((0,), (0,))),
                                preferred_element_type=F32)
    return out.reshape(n * c, LANES), st


def _hgrn_scan_kernel(qf_ref, vf_ref, gf_ref, qb_ref, vb_ref, gb_ref, tf_ref, tb_ref,
                      sf0_ref, sb0_ref, of_ref, ob_ref, sf_ref, sb_ref, st_sc):
    step = pl.program_id(2)

    @pl.when(step == 0)
    def _():
        st_sc[0] = sf0_ref[0, 0]
        st_sc[1] = sb0_ref[0, 0]

    o, st_f = _hgrn_block(qf_ref[0], vf_ref[0], gf_ref[0], tf_ref[...], st_sc[0], False)
    of_ref[0] = o
    o, st_b = _hgrn_block(qb_ref[0], vb_ref[0], gb_ref[0], tb_ref[...], st_sc[1], True)
    ob_ref[0] = o
    st_sc[0] = st_f
    st_sc[1] = st_b

    @pl.when(step == pl.num_programs(2) - 1)
    def _():
        sf_ref[0, 0] = st_f
        sb_ref[0, 0] = st_b


def _hgrn_scan(q, v, gf, gb, sf0, sb0):
    b, l, width = q.shape
    heads = width // LANES
    rb = _row_tile(l, 2048)
    nc = l // rb
    fwd = pl.BlockSpec((1, rb, LANES), lambda i, h, s: (i, s, h))
    bwd = pl.BlockSpec((1, rb, LANES), lambda i, h, s: (i, nc - 1 - s, h))
    st_spec = pl.BlockSpec((1, 1, LANES, LANES), lambda i, h, s: (i, h, 0, 0))
    o_shape = jax.ShapeDtypeStruct((b, l, width), F32)
    s_shape = jax.ShapeDtypeStruct((b, heads, LANES, LANES), F32)
    return pl.pallas_call(
        _hgrn_scan_kernel,
        grid=(b, heads, nc),
        in_specs=[fwd, fwd, fwd, bwd, bwd, bwd,
                  _resident((HGRN_CHUNK, HGRN_CHUNK)), _resident((HGRN_CHUNK, HGRN_CHUNK)), st_spec, st_spec],
        out_specs=[fwd, bwd, st_spec, st_spec],
        out_shape=[o_shape, o_shape, s_shape, s_shape],
        scratch_shapes=[pltpu.VMEM((2, LANES, LANES), F32)],
        compiler_params=_params("parallel", "parallel", "arbitrary"),
        name="hgrn_scan",
    )(q, v, gf, q, v, gb, _hgrn_tri(False), _hgrn_tri(True), sf0, sb0)


def _hgrn_merge_kernel(x_ref, of_ref, ob_ref, gate_ref, og_ref, w_ref, mod_ref, y_ref):
    width = of_ref.shape[2]
    og = og_ref[...]
    parts = []
    for hd in range(width // LANES):
        sl = slice(hd * LANES, (hd + 1) * LANES)
        o = of_ref[0, :, sl] + ob_ref[0, :, sl]
        parts.append((_rms(o, og) * _silu(gate_ref[0, :, sl])).astype(BF16))
    r = jnp.concatenate(parts, axis=1)
    y = jnp.dot(r, w_ref[...], preferred_element_type=F32)
    y_ref[0] = x_ref[0] + mod_ref[0, 2] * y


def _hgrn_merge(x, o_f, o_b, gate, o_gain, w_out, mod):
    b, l, d = x.shape
    width = o_f.shape[2]
    tm = _row_tile(l, 1024)
    wide = pl.BlockSpec((1, tm, width), lambda i, t: (i, t, 0))
    return pl.pallas_call(
        _hgrn_merge_kernel,
        grid=(b, l // tm),
        in_specs=[pl.BlockSpec((1, tm, d), lambda i, t: (i, t, 0)), wide, wide, wide,
                  _resident((1, LANES)),
                  _resident((width, d)),
                  pl.BlockSpec((1, 6, 1, d), lambda i, t: (i % mod.shape[0], 0, 0, 0))],
        out_specs=pl.BlockSpec((1, tm, d), lambda i, t: (i, t, 0)),
        out_shape=jax.ShapeDtypeStruct((b, l, d), F32),
        compiler_params=_params("parallel", "parallel"),
        name="hgrn_merge",
    )(x, o_f, o_b, gate, o_gain, w_out, mod)


def _mla_rope(y, cos, sin_lo, sin_hi):
    half = MLA_ROPE // 2
    return y * cos + pltpu.roll(y, LANES - half, 1) * sin_lo + pltpu.roll(y, half, 1) * sin_hi


def _mla_proj_kernel(x_ref, mod_ref, ng_ref, w_ref, qg_ref, kvg_ref, wq_ref, wk_ref, wv_ref,
                     cos_ref, slo_ref, shi_ref, q_ref, k_ref, v_ref, *, rotate, q_lora, kv_lora, chunk):
    h = _modnorm(x_ref[0], ng_ref[...], mod_ref[0, 0], mod_ref[0, 1]).astype(BF16)
    p = jnp.dot(h, w_ref[...], preferred_element_type=F32)
    cq = _rms(p[:, :q_lora], qg_ref[...]).astype(BF16)
    ckv = _rms(p[:, q_lora:q_lora + kv_lora], kvg_ref[...]).astype(BF16)
    k_rope = p[:, q_lora + kv_lora:]
    scale = (MLA_NOPE + MLA_ROPE) ** -0.5 * LOG2_E
    if rotate:
        cos, slo, shi = cos_ref[...], slo_ref[...], shi_ref[...]
        k_rope = _mla_rope(k_rope, cos, slo, shi)
    n = wq_ref.shape[1]
    for j in range(n // chunk):
        cs = slice(j * chunk, (j + 1) * chunk)
        pq = jnp.dot(cq, wq_ref[:, cs], preferred_element_type=F32)
        pk = jnp.dot(ckv, wk_ref[:, cs], preferred_element_type=F32)
        for u in range(chunk // LANES):
            us = slice(u * LANES, (u + 1) * LANES)
            os = slice(j * chunk + u * LANES, j * chunk + (u + 1) * LANES)
            qh = pq[:, us]
            if rotate:
                qh = _mla_rope(qh, cos, slo, shi)
            q_ref[0, :, os] = (qh * scale).astype(BF16)
            k_ref[0, :, os] = (pk[:, us] + k_rope).astype(BF16)
    for j in range(wv_ref.shape[1] // chunk):
        pv = jnp.dot(ckv, wv_ref[:, j * chunk:(j + 1) * chunk], preferred_element_type=F32)
        for u in range(chunk // LANES):
            os = slice(j * chunk + u * LANES, j * chunk + (u + 1) * LANES)
            v_ref[0, 0, os, :] = pv[:, u * LANES:(u + 1) * LANES].T.astype(BF16)


def _mla_proj(x, mod, ng, w_in, qg, kvg, wq, wk, wv, cos, slo, shi, rotate):
    b, l, d = x.shape
    n = wq.shape[1]
    q_lora, kv_lora = wq.shape[0], wk.shape[0]
    tm = _row_tile(l, 512)
    kern = functools.partial(_mla_proj_kernel, rotate=rotate, q_lora=q_lora, kv_lora=kv_lora, chunk=512)
    out = jax.ShapeDtypeStruct((b, l, n), BF16)
    ospec = pl.BlockSpec((1, tm, n), lambda i, t: (i, t, 0))
    tab = pl.BlockSpec((tm, LANES), lambda i, t: (t, 0))
    return pl.pallas_call(
        kern,
        grid=(b, l // tm),
        in_specs=[pl.BlockSpec((1, tm, d), lambda i, t: (i, t, 0)),
                  pl.BlockSpec((1, 6, 1, d), lambda i, t: (i % mod.shape[0], 0, 0, 0)),
                  _resident((1, d)),
                  _resident(w_in.shape),
                  _resident((1, q_lora)),
                  _resident((1, kv_lora)),
                  _resident(wq.shape), _resident(wk.shape), _resident(wv.shape),
                  tab, tab, tab],
        out_specs=[ospec, ospec, pl.BlockSpec((1, 1, wv.shape[1], tm), lambda i, t: (i, t, 0, 0))],
        out_shape=[out, out, jax.ShapeDtypeStruct((b, l // tm, wv.shape[1], tm), BF16)],
        compiler_params=_params("parallel", "parallel"),
        name="mla_proj",
    )(x, mod, ng, w_in, qg, kvg, wq, wk, wv, cos, slo, shi)


def _axial_angles(rows, rot_dim):
    row = jnp.repeat(jnp.arange(rows, dtype=F32), GRID_W)
    col = jnp.tile(jnp.arange(GRID_W, dtype=F32), rows)
    axis_dim = rot_dim // 2
    inv_freq = jnp.power(ROPE_THETA, -jnp.arange(0, axis_dim, 2, dtype=F32) / axis_dim)
    ang = jnp.concatenate([row[:, None] * inv_freq, col[:, None] * inv_freq], axis=-1)
    return jnp.cos(ang), jnp.sin(ang)


def _gqa_tables(rows):
    cos, sin = _axial_angles(rows, GQA_HEAD_DIM)
    return jnp.concatenate([cos, cos], axis=-1), jnp.concatenate([-sin, sin], axis=-1)


def _mla_tables(rows):
    cos, sin = _axial_angles(rows, MLA_ROPE)
    s = cos.shape[0]
    half = MLA_ROPE // 2
    ones = jnp.ones((s, MLA_NOPE), F32)
    zeros = jnp.zeros((s, MLA_NOPE), F32)
    tail1 = jnp.ones((s, LANES - MLA_NOPE - MLA_ROPE), F32)
    tail0 = jnp.zeros((s, LANES - MLA_NOPE - MLA_ROPE), F32)
    zh = jnp.zeros((s, half), F32)
    c = jnp.concatenate([ones, cos, cos, tail1], axis=-1)
    s_lo = jnp.concatenate([zeros, -sin, zh, tail0], axis=-1)
    s_hi = jnp.concatenate([zeros, zh, sin, tail0], axis=-1)
    return c, s_lo, s_hi


def _mla_weights(w_in, w_qb, w_kvb, w_out):
    d = w_in.shape[0]
    q_lora, kv_lora = w_qb.shape[0], w_kvb.shape[0]
    heads = w_qb.shape[1] // (MLA_NOPE + MLA_ROPE)
    kr = jnp.zeros((d, LANES), w_in.dtype).at[:, MLA_NOPE:MLA_NOPE + MLA_ROPE].set(w_in[:, q_lora + kv_lora:])
    w_in_p = jnp.concatenate([w_in[:, :q_lora + kv_lora], kr], axis=1)
    wq = w_qb.reshape(q_lora, heads, MLA_NOPE + MLA_ROPE)
    wq = jnp.pad(wq, ((0, 0), (0, 0), (0, LANES - MLA_NOPE - MLA_ROPE))).reshape(q_lora, heads * LANES)
    wkv = w_kvb.reshape(kv_lora, heads, MLA_NOPE + MLA_V)
    wk = jnp.pad(wkv[:, :, :MLA_NOPE], ((0, 0), (0, 0), (0, LANES - MLA_NOPE))).reshape(kv_lora, heads * LANES)
    wv = jnp.pad(wkv[:, :, MLA_NOPE:], ((0, 0), (0, 0), (0, LANES - MLA_V))).reshape(kv_lora, heads * LANES)
    wo = w_out.reshape(heads, MLA_V, -1)
    wo = jnp.pad(wo, ((0, 0), (0, LANES - MLA_V), (0, 0))).reshape(heads * LANES, -1)
    return w_in_p.astype(BF16), wq.astype(BF16), wk.astype(BF16), wv.astype(BF16), wo.astype(BF16)


def kernel(x, c, ctx, c_ctx, w_ada, b_ada, norm_mix, norm_ffn, ffn_w_in, ffn_conv_w, ffn_conv_b, ffn_w_out,
           gqa_w_in, gqa_q_norm, gqa_k_norm, gqa_w_out, hgrn_w_in, hgrn_out_norm, hgrn_w_out, hgrn_lower_bounds,
           mla_w_in, mla_q_norm, mla_kv_norm, mla_w_qb, mla_w_kvb, mla_w_out, final_norm):
    batch, seq, d = x.shape
    depth = w_ada.shape[0]
    n_mixers = 3
    rows = seq // GRID_W
    assert batch + 1 <= SUBLANES

    cv = jnp.zeros((SUBLANES, d), F32).at[:batch].set(c).at[batch].set(c_ctx)
    mods = _ada_mods(cv, w_ada, b_ada).reshape(depth, SUBLANES, 6, 1, d)
    lb_all = _lower_bounds(hgrn_lower_bounds)

    cos_a, sin_a = _gqa_tables(rows)
    cos_m, slo_m, shi_m = _mla_tables(rows)

    for i in range(depth):
        last = i == depth - 1
        kind = i % n_mixers
        j = i // n_mixers
        mod = mods[i, :batch]
        mod_c = mods[i, batch:batch + 1]
        ng = norm_mix[i][None, :]

        if kind == 0:
            w_in = gqa_w_in[j].astype(BF16)
            w_out = gqa_w_out[j].astype(BF16)
            qg, kg = gqa_q_norm[j][None, :], gqa_k_norm[j][None, :]
            q, k, v = _gqa_proj(x, mod, ng, w_in, qg, kg, cos_a, sin_a, True)
            n_ctx = ctx.shape[1]
            qc, kc, vc = _gqa_proj(ctx, mod_c, ng, w_in, qg, kg, cos_a[:n_ctx], sin_a[:n_ctx], False)
            o = _attention(q, [(k, v), (kc, vc)], GQA_GROUP, True, 512)
            x = _merge(x, o, w_out, mod)
            if not last:
                oc = _attention(qc, [(kc, vc)], GQA_GROUP, True, 512)
                ctx = _merge(ctx, oc, w_out, mod_c)
        elif kind == 1:
            w_in = hgrn_w_in[j].astype(BF16)
            w_out = hgrn_w_out[j].astype(BF16)
            lb = lb_all[i][None, :]
            og = hgrn_out_norm[j][None, :]
            heads = w_out.shape[0] // HGRN_DK
            qc, vc, gfc, gbc, gatec = _hgrn_proj(ctx, mod_c, ng, w_in, lb)
            q, v, gf, gb, gate = _hgrn_proj(x, mod, ng, w_in, lb)
            s0 = jnp.zeros((batch, heads, HGRN_DK, HGRN_DK), F32)
            oc_f, oc_b, s_f, s_b = _hgrn_scan(qc, vc, gfc, gbc, s0, s0)
            o_f, o_b, _, _ = _hgrn_scan(q, v, gf, gb, s_f, s_b)
            x = _hgrn_merge(x, o_f, o_b, gate, og, w_out, mod)
            if not last:
                ctx = _hgrn_merge(ctx, oc_f, oc_b, gatec, og, w_out, mod_c)
        else:
            w_in, wq, wk, wv, w_out = _mla_weights(mla_w_in[j], mla_w_qb[j], mla_w_kvb[j], mla_w_out[j])
            qg, kvg = mla_q_norm[j][None, :], mla_kv_norm[j][None, :]
            n_ctx = ctx.shape[1]
            q, k, v = _mla_proj(x, mod, ng, w_in, qg, kvg, wq, wk, wv, cos_m, slo_m, shi_m, True)
            qc, kc, vc = _mla_proj(ctx, mod_c, ng, w_in, qg, kvg, wq, wk, wv,
                                   cos_m[:n_ctx], slo_m[:n_ctx], shi_m[:n_ctx], False)
            o = _attention(q, [(k, v), (kc, vc)], 1, True, 1024)
            x = _merge(x, o, w_out, mod)
            if not last:
                oc = _attention(qc, [(kc, vc)], 1, True, 1024)
                ctx = _merge(ctx, oc, w_out, mod_c)

        fg = norm_ffn[i][None, :]
        f_in = ffn_w_in[i].astype(BF16)
        f_out = ffn_w_out[i].astype(BF16)
        f_cw = ffn_conv_w[i]
        f_cb = ffn_conv_b[i][None, :]
        x = _conv_ffn(x, mod, fg, f_in, f_cw, f_cb, f_out, final_norm[None, :], last)
        if not last:
            ctx = _conv_ffn(ctx, mod_c, fg, f_in, f_cw, f_cb, f_out, final_norm[None, :], False)

    return x
```

```python
import functools

import numpy as np
import jax
import jax.numpy as jnp
from jax import lax
from jax.experimental import pallas as pl
from jax.experimental.pallas import tpu as pltpu

F32 = jnp.float32
BF16 = jnp.bfloat16

GRID_W = 64
ROPE_THETA = 10000.0
NORM_EPS = 1e-6
CONV_W = 3

LANES = 128
SUBLANES = 8

GQA_HEAD_DIM = 128
GQA_GROUP = 2

HGRN_DK = 128
HGRN_CHUNK = 64
HGRN_PARTS = 5

MLA_NOPE = 64
MLA_ROPE = 32
MLA_V = 64

VMEM_LIMIT = 56 * 1024 * 1024

LOG2_E = 1.4426950408889634
ATTN_UNROLL = 16


def _params(*sem):
    return pltpu.CompilerParams(dimension_semantics=sem, vmem_limit_bytes=VMEM_LIMIT)


def _resident(shape):
    nd = len(shape)
    return pl.BlockSpec(shape, lambda *_: (0,) * nd, pipeline_mode=pl.Buffered(1))


def _silu(x):
    return x / (1.0 + jnp.exp2(x * -LOG2_E))


def _sigmoid(x):
    return 1.0 / (1.0 + jnp.exp2(x * -LOG2_E))


def _neg_abs(x):
    bits = lax.bitcast_convert_type(x, jnp.uint32) | jnp.uint32(0x80000000)
    return lax.bitcast_convert_type(bits, F32)


def _rms(x, gain):
    return x * lax.rsqrt(jnp.mean(x * x, axis=-1, keepdims=True) + NORM_EPS) * gain


def _head_rms(p, gain):
    ones_bf = jnp.ones((LANES, LANES), BF16)
    ss = jnp.dot((p * p).astype(BF16), ones_bf, preferred_element_type=F32)
    return p * lax.rsqrt(ss * (1.0 / LANES) + NORM_EPS) * gain


def _modnorm(x, gain, shift, scale):
    return _rms(x, gain) * (1.0 + scale) + shift


def _aligned_ds(start, size):
    if isinstance(start, int):
        return pl.ds(start, size)
    return pl.ds(pl.multiple_of(start, size), size)


def _row_tile(n, want):
    t = min(n, want)
    while n % t:
        t -= SUBLANES
    assert t > 0 and t % SUBLANES == 0, (n, want)
    return t


def _split_bf16(a):
    hi = a.astype(BF16)
    return hi, (a - hi.astype(F32)).astype(BF16)


def _ada_kernel(cv_ref, w_ref, b_ref, o_ref):
    rows = cv_ref.shape[0]
    s_hi, s_lo = _split_bf16(_silu(cv_ref[...]))
    w_hi, w_lo = _split_bf16(w_ref[0])
    a = jnp.dot(jnp.concatenate([s_hi, s_lo], axis=0), w_hi, preferred_element_type=F32)
    b = jnp.dot(s_hi, w_lo, preferred_element_type=F32)
    o_ref[0] = ((a[:rows] + a[rows:]) + b) + b_ref[0]


def _ada_mods(cv, w_ada, b_ada):
    depth, d, n = w_ada.shape
    tn = 1536
    return pl.pallas_call(
        _ada_kernel,
        grid=(depth, n // tn),
        in_specs=[pl.BlockSpec((SUBLANES, d), lambda i, j: (0, 0)),
                  pl.BlockSpec((1, d, tn), lambda i, j: (i, 0, j)),
                  pl.BlockSpec((1, 1, tn), lambda i, j: (i, 0, j))],
        out_specs=pl.BlockSpec((1, SUBLANES, tn), lambda i, j: (i, 0, j)),
        out_shape=jax.ShapeDtypeStruct((depth, SUBLANES, n), F32),
        compiler_params=_params("arbitrary", "arbitrary"),
        name="ada_mods",
    )(cv, w_ada, b_ada.reshape(depth, 1, n))


def _lb_kernel(x_ref, o_ref):
    depth = x_ref.shape[0]
    rows = [x_ref[i:i + 1, :] for i in range(depth)]
    m = rows[0]
    for r in rows[1:]:
        m = jnp.maximum(m, r)
    e = [jnp.exp(r - m) for r in rows]
    tot = e[0]
    for r in e[1:]:
        tot = tot + r
    p = [r / tot for r in e]
    cum = p[0]
    o_ref[0:1, :] = cum - p[0]
    for i in range(1, depth):
        cum = cum + p[i]
        o_ref[i:i + 1, :] = cum - p[0]


def _lower_bounds(lb_raw):
    return pl.pallas_call(
        _lb_kernel,
        out_shape=jax.ShapeDtypeStruct(lb_raw.shape, F32),
        name="hgrn_lower_bounds",
    )(lb_raw.astype(F32))


def _gqa_proj_kernel(x_ref, mod_ref, ng_ref, w_ref, qg_ref, kg_ref, cos_ref, sin_ref,
                     q_ref, k_ref, v_ref, *, rotate, qd, kd, chunk):
    h = _modnorm(x_ref[0], ng_ref[...], mod_ref[0, 0], mod_ref[0, 1]).astype(BF16)
    scale = GQA_HEAD_DIM ** -0.5 * LOG2_E
    qg = qg_ref[...] * scale
    kg = kg_ref[...]
    if rotate:
        cos = cos_ref[...]
        sin = sin_ref[...]

    def head(p, gain):
        y = _head_rms(p, gain)
        if rotate:
            y = y * cos + pltpu.roll(y, GQA_HEAD_DIM // 2, 1) * sin
        return y

    n = w_ref.shape[1]
    for j in range(n // chunk):
        p = jnp.dot(h, w_ref[:, j * chunk:(j + 1) * chunk], preferred_element_type=F32)
        for u in range(chunk // LANES):
            col = j * chunk + u * LANES
            ph = p[:, u * LANES:(u + 1) * LANES]
            if col < qd:
                q_ref[0, :, col:col + LANES] = head(ph, qg).astype(BF16)
            elif col < qd + kd:
                k_ref[0, :, col - qd:col - qd + LANES] = head(ph, kg).astype(BF16)
            else:
                c0 = col - qd - kd
                v_ref[0, 0, c0:c0 + LANES, :] = ph.T.astype(BF16)


def _gqa_proj(x, mod, ng, w_in, qg, kg, cos, sin, rotate):
    b, l, d = x.shape
    n = w_in.shape[1]
    kd = n // 4
    qd = n - 2 * kd
    tm = _row_tile(l, 512)
    kern = functools.partial(_gqa_proj_kernel, rotate=rotate, qd=qd, kd=kd, chunk=512)
    return pl.pallas_call(
        kern,
        grid=(b, l // tm),
        in_specs=[pl.BlockSpec((1, tm, d), lambda i, t: (i, t, 0)),
                  pl.BlockSpec((1, 6, 1, d), lambda i, t: (i % mod.shape[0], 0, 0, 0)),
                  _resident((1, d)),
                  _resident((d, n)),
                  _resident((1, LANES)),
                  _resident((1, LANES)),
                  pl.BlockSpec((tm, LANES), lambda i, t: (t, 0)),
                  pl.BlockSpec((tm, LANES), lambda i, t: (t, 0))],
        out_specs=[pl.BlockSpec((1, tm, qd), lambda i, t: (i, t, 0)),
                   pl.BlockSpec((1, tm, kd), lambda i, t: (i, t, 0)),
                   pl.BlockSpec((1, 1, kd, tm), lambda i, t: (i, t, 0, 0))],
        out_shape=[jax.ShapeDtypeStruct((b, l, qd), BF16),
                   jax.ShapeDtypeStruct((b, l, kd), BF16),
                   jax.ShapeDtypeStruct((b, l // tm, kd, tm), BF16)],
        compiler_params=_params("parallel", "parallel"),
        name="gqa_proj",
    )(x, mod, ng, w_in, qg, kg, cos, sin)


def _attn_kernel(*refs, group, shared, n_src):
    q_ref = refs[0]
    kv_refs = refs[1:1 + 2 * n_src]
    o_ref = refs[1 + 2 * n_src]
    s_sc, m_sc = refs[2 + 2 * n_src:]
    tq = q_ref.shape[1]
    nq = group * tq
    dv = LANES if shared else LANES // group
    step = pl.program_id(2)

    def fold(a):
        return a.reshape(a.shape[0] // SUBLANES, SUBLANES, nq)

    def run(do_scores, do_weigh):
        carry = {}
        if do_scores:
            q = q_ref[0]
            q_heads = [q[:, g * LANES:(g + 1) * LANES] for g in range(group)]
            q_all = jnp.concatenate(q_heads, axis=0)
            carry["m8"] = jnp.full((SUBLANES, nq), -jnp.inf, F32)
        if do_weigh:
            m = jnp.max(m_sc[...], axis=0, keepdims=True)
            carry["l8"] = jnp.zeros((SUBLANES, nq), F32)
            carry["acc"] = jnp.zeros((dv, nq), F32)
        row0 = 0
        for i in range(n_src):
            k_ref, vt_ref = kv_refs[2 * i], kv_refs[2 * i + 1]
            n_chunks, _, c = vt_ref.shape[1:]

            def body(j, carry, k_ref=k_ref, vt_ref=vt_ref, c=c, row0=row0):
                carry = dict(carry)
                rows = _aligned_ds(row0 + j * c, c)
                if do_weigh:
                    s_old = s_sc[rows, :]
                if do_scores:
                    kc = k_ref[0, _aligned_ds(j * c, c), :]
                    nt_dims = (((1,), (1,)), ((), ()))
                    if shared:
                        s = lax.dot_general(kc, q_all, nt_dims, preferred_element_type=F32)
                    else:
                        s = jnp.concatenate(
                            [lax.dot_general(kc[:, g * LANES:(g + 1) * LANES], q_heads[g], nt_dims,
                                             preferred_element_type=F32) for g in range(group)], axis=1)
                    s_sc[rows, :] = s
                    carry["m8"] = jnp.maximum(carry["m8"], jnp.max(fold(s), axis=0))
                if do_weigh:
                    p = jnp.exp2(s_old - m)
                    carry["l8"] = carry["l8"] + jnp.sum(fold(p), axis=0)
                    pb = p.astype(BF16)
                    vt = vt_ref[0, j]
                    if shared:
                        upd = jnp.dot(vt, pb, preferred_element_type=F32)
                    else:
                        upd = jnp.concatenate(
                            [jnp.dot(vt[g * dv:(g + 1) * dv], pb[:, g * tq:(g + 1) * tq],
                                     preferred_element_type=F32) for g in range(group)], axis=1)
                    carry["acc"] = carry["acc"] + upd
                return carry

            if n_chunks == 1:
                carry = body(0, carry)
            else:
                carry = lax.fori_loop(0, n_chunks, body, carry, unroll=ATTN_UNROLL)
            row0 += n_chunks * c
        if do_scores:
            m_sc[...] = carry["m8"]
        if do_weigh:
            o = carry["acc"] / jnp.sum(carry["l8"], axis=0, keepdims=True)
            o_ref[0] = jnp.concatenate([o[:, g * tq:(g + 1) * tq].T for g in range(group)],
                                       axis=1).astype(o_ref.dtype)

    n_tiles = pl.num_programs(2) - 1

    @pl.when(step == 0)
    def _():
        run(True, False)

    @pl.when((step > 0) & (step < n_tiles))
    def _():
        run(True, True)

    @pl.when(step == n_tiles)
    def _():
        run(False, True)


def _attention(q, kv_sources, group, shared, tq):
    b, nq, hd = q.shape
    hkv = hd // (group * LANES)
    kw = LANES if shared else group * LANES
    ow = group * LANES if shared else LANES
    tq = _row_tile(nq, tq)
    nt = nq // tq
    n_src = len(kv_sources)
    in_specs = [pl.BlockSpec((1, tq, group * LANES), lambda i, h, t: (i, jnp.minimum(t, nt - 1), h))]
    args = [q]
    nk_total = 0
    for k, vt in kv_sources:
        nk = k.shape[1]
        n_chunks, _, c = vt.shape[1:]
        assert n_chunks * c == nk and nk_total % c == 0
        nk_total += nk
        in_specs.append(pl.BlockSpec((1, nk, kw), lambda i, h, t: (i, 0, h)))
        in_specs.append(pl.BlockSpec((1, n_chunks, LANES, c), lambda i, h, t: (i, 0, h, 0)))
        args += [k, vt]
    return pl.pallas_call(
        functools.partial(_attn_kernel, group=group, shared=shared, n_src=n_src),
        grid=(b, hkv, nt + 1),
        in_specs=in_specs,
        out_specs=pl.BlockSpec((1, tq, ow), lambda i, h, t: (i, jnp.maximum(t - 1, 0), h)),
        out_shape=jax.ShapeDtypeStruct((b, nq, hkv * ow), BF16),
        scratch_shapes=[pltpu.VMEM((nk_total, group * tq), F32),
                        pltpu.VMEM((SUBLANES, group * tq), F32)],
        compiler_params=_params("parallel", "parallel", "arbitrary"),
        name="flash_attention",
    )(*args)


def _merge_kernel(x_ref, o_ref, w_ref, mod_ref, y_ref):
    y = jnp.dot(o_ref[0], w_ref[...], preferred_element_type=F32)
    y_ref[0] = x_ref[0] + mod_ref[0, 2] * y


def _merge(x, o, w_out, mod):
    b, l, d = x.shape
    ko = o.shape[2]
    tm = _row_tile(l, 1024)
    return pl.pallas_call(
        _merge_kernel,
        grid=(b, l // tm),
        in_specs=[pl.BlockSpec((1, tm, d), lambda i, t: (i, t, 0)),
                  pl.BlockSpec((1, tm, ko), lambda i, t: (i, t, 0)),
                  _resident((ko, d)),
                  pl.BlockSpec((1, 6, 1, d), lambda i, t: (i % mod.shape[0], 0, 0, 0))],
        out_specs=pl.BlockSpec((1, tm, d), lambda i, t: (i, t, 0)),
        out_shape=jax.ShapeDtypeStruct((b, l, d), F32),
        compiler_params=_params("parallel", "parallel"),
        name="merge_residual",
    )(x, o, w_out, mod)


def _ffn_kernel(xp_ref, x_ref, xn_ref, mod_ref, ng_ref, win_ref, cw_ref, cb_ref, wout_ref, og_ref, y_ref,
                *, d_ff, out_norm):
    t = pl.program_id(1)
    nt = pl.num_programs(1)
    gain = ng_ref[...]
    shift, scale, gate = mod_ref[0, 3], mod_ref[0, 4], mod_ref[0, 5]
    x, xp, xn = x_ref[0], xp_ref[0], xn_ref[0]
    tm = x.shape[0]
    halo = xp.shape[0]
    hp = jnp.where(t > 0, _modnorm(xp, gain, shift, scale), 0.0)
    hn = jnp.where(t < nt - 1, _modnorm(xn, gain, shift, scale), 0.0)
    h = jnp.concatenate([hp, _modnorm(x, gain, shift, scale), hn], axis=0).astype(BF16)
    rows = tm + 2 * halo

    def conv(p, col):
        w = cw_ref[:, col:col + d_ff]
        prev = pltpu.roll(p, 1, 0)[halo:halo + tm]
        nxt = pltpu.roll(p, rows - 1, 0)[halo:halo + tm]
        cur = p[halo:halo + tm]
        return ((cb_ref[:, col:col + d_ff] + prev * w[0:1]) + cur * w[1:2]) + nxt * w[2:3]

    pa = jnp.dot(h, win_ref[:, :d_ff], preferred_element_type=F32)
    pv = jnp.dot(h, win_ref[:, d_ff:], preferred_element_type=F32)
    g = _silu(conv(pa, 0)) * conv(pv, d_ff)
    y = x + gate * jnp.dot(g.astype(BF16), wout_ref[...], preferred_element_type=F32)
    y_ref[0] = _rms(y, og_ref[...]) if out_norm else y


def _conv_ffn(x, mod, ng, w_in, conv_w, conv_b, w_out, out_gain, out_norm):
    b, l, d = x.shape
    d_ff = w_out.shape[0]
    tm = _row_tile(l, 512)
    halo = SUBLANES
    per = tm // halo
    last = l // halo - 1
    return pl.pallas_call(
        functools.partial(_ffn_kernel, d_ff=d_ff, out_norm=out_norm),
        grid=(b, l // tm),
        in_specs=[pl.BlockSpec((1, halo, d), lambda i, t: (i, jnp.maximum(t * per - 1, 0), 0)),
                  pl.BlockSpec((1, tm, d), lambda i, t: (i, t, 0)),
                  pl.BlockSpec((1, halo, d), lambda i, t: (i, jnp.minimum((t + 1) * per, last), 0)),
                  pl.BlockSpec((1, 6, 1, d), lambda i, t: (i % mod.shape[0], 0, 0, 0)),
                  _resident((1, d)),
                  _resident((d, 2 * d_ff)),
                  _resident((CONV_W, 2 * d_ff)),
                  _resident((1, 2 * d_ff)),
                  _resident((d_ff, d)),
                  _resident((1, d))],
        out_specs=pl.BlockSpec((1, tm, d), lambda i, t: (i, t, 0)),
        out_shape=jax.ShapeDtypeStruct((b, l, d), F32),
        compiler_params=_params("parallel", "arbitrary"),
        name="conv_ffn",
    )(x, x, x, mod, ng, w_in, conv_w, conv_b, w_out, out_gain)


def _hgrn_proj_kernel(x_ref, mod_ref, ng_ref, w_ref, lb_ref, q_ref, v_ref, gf_ref, gb_ref, gate_ref, *, chunk):
    h = _modnorm(x_ref[0], ng_ref[...], mod_ref[0, 0], mod_ref[0, 1]).astype(BF16)
    width = q_ref.shape[2]
    scale = HGRN_DK ** -0.5
    for part in range(HGRN_PARTS):
        for j in range(width // chunk):
            c0 = j * chunk
            p = jnp.dot(h, w_ref[:, part * width + c0:part * width + c0 + chunk], preferred_element_type=F32)
            if part == 0:
                q_ref[0, :, c0:c0 + chunk] = p * scale
            elif part == 1:
                v_ref[0, :, c0:c0 + chunk] = p
            elif part == 4:
                gate_ref[0, :, c0:c0 + chunk] = p
            else:
                lb = lb_ref[:, c0:c0 + chunk]
                f = lb + (1.0 - lb) * _sigmoid(p)
                g_out = gf_ref if part == 2 else gb_ref
                g_out[0, :, c0:c0 + chunk] = jnp.log2(f)


def _hgrn_proj(x, mod, ng, w_in, lb):
    b, l, d = x.shape
    width = w_in.shape[1] // HGRN_PARTS
    tm = _row_tile(l, 512)
    out = jax.ShapeDtypeStruct((b, l, width), F32)
    ospec = pl.BlockSpec((1, tm, width), lambda i, t: (i, t, 0))
    return pl.pallas_call(
        functools.partial(_hgrn_proj_kernel, chunk=512),
        grid=(b, l // tm),
        in_specs=[pl.BlockSpec((1, tm, d), lambda i, t: (i, t, 0)),
                  pl.BlockSpec((1, 6, 1, d), lambda i, t: (i % mod.shape[0], 0, 0, 0)),
                  _resident((1, d)),
                  _resident((d, HGRN_PARTS * width)),
                  _resident((1, width))],
        out_specs=[ospec] * 5,
        out_shape=[out] * 5,
        compiler_params=_params("parallel", "parallel"),
        name="hgrn_proj",
    )(x, mod, ng, w_in, lb)


_HGRN_LEVELS = (32, 16, 8, 4)


def _hgrn_tri(reverse):
    idx = np.arange(HGRN_CHUNK)
    tri = idx[None, :] >= idx[:, None] if reverse else idx[None, :] <= idx[:, None]
    return jnp.asarray(tri.astype(np.float32), dtype=BF16)


def _hgrn_block(q, v, g, tri, st, reverse):
    c = HGRN_CHUNK
    n = q.shape[0] // c
    k = 1.0 - jnp.exp2(g)

    g_hi = g.astype(BF16)
    rest = g - g_hi.astype(F32)
    g_mid = rest.astype(BF16)
    g_lo = (rest - g_mid.astype(F32)).astype(BF16)
    pieces = [piece[i * c:(i + 1) * c] for i in range(n) for piece in (g_hi, g_mid, g_lo)]
    sums = jnp.dot(tri, jnp.concatenate(pieces, axis=1), preferred_element_type=F32)
    cum = jnp.stack([(sums[:, (3 * i) * LANES:(3 * i + 1) * LANES]
                      + sums[:, (3 * i + 1) * LANES:(3 * i + 2) * LANES])
                     + sums[:, (3 * i + 2) * LANES:(3 * i + 3) * LANES] for i in range(n)], axis=0)

    q3, k3, v3 = (a.reshape(n, c, LANES) for a in (q, k, v))
    tot = cum[:, 0:1, :] if reverse else cum[:, c - 1:c, :]
    row = lax.broadcasted_iota(jnp.int32, (c, c), 0)
    col = lax.broadcasted_iota(jnp.int32, (c, c), 1)
    early, late = (row, col) if reverse else (col, row)

    sub_row = lax.broadcasted_iota(jnp.int32, (c, LANES), 0)
    scores = jnp.zeros((n, c, c), F32)
    for half in _HGRN_LEVELS:
        blk = 2 * half
        cb = cum.reshape(n * (c // blk), blk, LANES)
        b_row = half if reverse else half - 1
        d = cb - cb[:, b_row:b_row + 1, :]
        e = jnp.exp2(_neg_abs(d)).reshape(n, c, LANES)
        query_row = ((sub_row % blk) < half) if reverse else ((sub_row % blk) >= half)
        z = (jnp.where(query_row[None], q3, k3) * e).astype(BF16)
        s_l = lax.dot_general(z, z, (((2,), (2,)), ((0,), (0,))), preferred_element_type=F32)
        pick = ((row // blk) == (col // blk)) & ((early % blk) < half) & ((late % blk) >= half)
        scores = jnp.where(pick[None], s_l, scores)

    sub = _HGRN_LEVELS[-1]
    cum2 = cum.reshape(n * c, LANES)
    prods = [q * k]
    for delta in range(1, sub):
        shift = n * c - delta if reverse else delta
        decay = jnp.exp2(jnp.minimum(cum2 - pltpu.roll(cum2, shift, 0), 0.0))
        prods.append(q * pltpu.roll(k, shift, 0) * decay)
    ones_bf = jnp.ones((LANES, LANES), BF16)
    diag = jnp.dot(jnp.concatenate(prods, axis=0).astype(BF16), ones_bf, preferred_element_type=F32)
    for delta in range(sub):
        d_t = diag[delta * n * c:(delta + 1) * n * c, 0:c].reshape(n, c, c)
        if reverse:
            hit = (col == row + delta) & ((row % sub) + delta < sub)
        else:
            hit = (col == row - delta) & ((row % sub) >= delta)
        scores = jnp.where(hit[None], d_t, scores)

    out = lax.dot_general(scores.astype(BF16), v3.astype(BF16), (((2,), (1,)), ((0,), (0,))),
                          preferred_element_type=F32)
    qe = (q3 * jnp.exp2(cum)).astype(BF16)
    kd = (k3 * jnp.exp2(tot - cum)).astype(BF16)
    e_tot = jnp.exp2(tot)
    v_t = jnp.stack([v3[i].T for i in range(n)], axis=0).astype(BF16)
    upd = lax.dot_general(v_t, kd, (((2,), (1,)), ((0,), (0,))), preferred_element_type=F32)
    states = [None] * n
    for i in (range(n - 1, -1, -1) if reverse else range(n)):
        states[i] = st
        st = st * e_tot[i] + upd[i]
    out = out + lax.dot_general(qe, jnp.stack(states, axis=0).astype(BF16), (((2,), (2,)), ((0,), (0,))),
                                preferred_element_type=F32)
    return out.reshape(n * c, LANES), st


def _hgrn_scan_kernel(qf_ref, vf_ref, gf_ref, qb_ref, vb_ref, gb_ref, tf_ref, tb_ref,
                      sf0_ref, sb0_ref, of_ref, ob_ref, sf_ref, sb_ref, st_sc):
    step = pl.program_id(2)

    @pl.when(step == 0)
    def _():
        st_sc[0] = sf0_ref[0, 0]
        st_sc[1] = sb0_ref[0, 0]

    o, st_f = _hgrn_block(qf_ref[0], vf_ref[0], gf_ref[0], tf_ref[...], st_sc[0], False)
    of_ref[0] = o
    o, st_b = _hgrn_block(qb_ref[0], vb_ref[0], gb_ref[0], tb_ref[...], st_sc[1], True)
    ob_ref[0] = o
    st_sc[0] = st_f
    st_sc[1] = st_b

    @pl.when(step == pl.num_programs(2) - 1)
    def _():
        sf_ref[0, 0] = st_f
        sb_ref[0, 0] = st_b


def _hgrn_scan(q, v, gf, gb, sf0, sb0):
    b, l, width = q.shape
    heads = width // LANES
    rb = _row_tile(l, 2048)
    nc = l // rb
    fwd = pl.BlockSpec((1, rb, LANES), lambda i, h, s: (i, s, h))
    bwd = pl.BlockSpec((1, rb, LANES), lambda i, h, s: (i, nc - 1 - s, h))
    st_spec = pl.BlockSpec((1, 1, LANES, LANES), lambda i, h, s: (i, h, 0, 0))
    o_shape = jax.ShapeDtypeStruct((b, l, width), F32)
    s_shape = jax.ShapeDtypeStruct((b, heads, LANES, LANES), F32)
    return pl.pallas_call(
        _hgrn_scan_kernel,
        grid=(b, heads, nc),
        in_specs=[fwd, fwd, fwd, bwd, bwd, bwd,
                  _resident((HGRN_CHUNK, HGRN_CHUNK)), _resident((HGRN_CHUNK, HGRN_CHUNK)), st_spec, st_spec],
        out_specs=[fwd, bwd, st_spec, st_spec],
        out_shape=[o_shape, o_shape, s_shape, s_shape],
        scratch_shapes=[pltpu.VMEM((2, LANES, LANES), F32)],
        compiler_params=_params("parallel", "parallel", "arbitrary"),
        name="hgrn_scan",
    )(q, v, gf, q, v, gb, _hgrn_tri(False), _hgrn_tri(True), sf0, sb0)


def _hgrn_merge_kernel(x_ref, of_ref, ob_ref, gate_ref, og_ref, w_ref, mod_ref, y_ref):
    width = of_ref.shape[2]
    og = og_ref[...]
    parts = []
    for hd in range(width // LANES):
        sl = slice(hd * LANES, (hd + 1) * LANES)
        o = of_ref[0, :, sl] + ob_ref[0, :, sl]
        parts.append((_rms(o, og) * _silu(gate_ref[0, :, sl])).astype(BF16))
    r = jnp.concatenate(parts, axis=1)
    y = jnp.dot(r, w_ref[...], preferred_element_type=F32)
    y_ref[0] = x_ref[0] + mod_ref[0, 2] * y


def _hgrn_merge(x, o_f, o_b, gate, o_gain, w_out, mod):
    b, l, d = x.shape
    width = o_f.shape[2]
    tm = _row_tile(l, 512)
    wide = pl.BlockSpec((1, tm, width), lambda i, t: (i, t, 0))
    return pl.pallas_call(
        _hgrn_merge_kernel,
        grid=(b, l // tm),
        in_specs=[pl.BlockSpec((1, tm, d), lambda i, t: (i, t, 0)), wide, wide, wide,
                  _resident((1, LANES)),
                  _resident((width, d)),
                  pl.BlockSpec((1, 6, 1, d), lambda i, t: (i % mod.shape[0], 0, 0, 0))],
        out_specs=pl.BlockSpec((1, tm, d), lambda i, t: (i, t, 0)),
        out_shape=jax.ShapeDtypeStruct((b, l, d), F32),
        compiler_params=_params("parallel", "parallel"),
        name="hgrn_merge",
    )(x, o_f, o_b, gate, o_gain, w_out, mod)


def _mla_rope(y, cos, sin_lo, sin_hi):
    half = MLA_ROPE // 2
    return y * cos + pltpu.roll(y, LANES - half, 1) * sin_lo + pltpu.roll(y, half, 1) * sin_hi


def _mla_proj_kernel(x_ref, mod_ref, ng_ref, w_ref, qg_ref, kvg_ref, wq_ref, wk_ref, wv_ref,
                     cos_ref, slo_ref, shi_ref, q_ref, k_ref, v_ref, *, rotate, q_lora, kv_lora, chunk):
    h = _modnorm(x_ref[0], ng_ref[...], mod_ref[0, 0], mod_ref[0, 1]).astype(BF16)
    p = jnp.dot(h, w_ref[...], preferred_element_type=F32)
    cq = _rms(p[:, :q_lora], qg_ref[...]).astype(BF16)
    ckv = _rms(p[:, q_lora:q_lora + kv_lora], kvg_ref[...]).astype(BF16)
    k_rope = p[:, q_lora + kv_lora:]
    scale = (MLA_NOPE + MLA_ROPE) ** -0.5 * LOG2_E
    if rotate:
        cos, slo, shi = cos_ref[...], slo_ref[...], shi_ref[...]
        k_rope = _mla_rope(k_rope, cos, slo, shi)
    n = wq_ref.shape[1]
    for j in range(n // chunk):
        cs = slice(j * chunk, (j + 1) * chunk)
        pq = jnp.dot(cq, wq_ref[:, cs], preferred_element_type=F32)
        pk = jnp.dot(ckv, wk_ref[:, cs], preferred_element_type=F32)
        for u in range(chunk // LANES):
            us = slice(u * LANES, (u + 1) * LANES)
            os = slice(j * chunk + u * LANES, j * chunk + (u + 1) * LANES)
            qh = pq[:, us]
            if rotate:
                qh = _mla_rope(qh, cos, slo, shi)
            q_ref[0, :, os] = (qh * scale).astype(BF16)
            k_ref[0, :, os] = (pk[:, us] + k_rope).astype(BF16)
    for j in range(wv_ref.shape[1] // chunk):
        pv = jnp.dot(ckv, wv_ref[:, j * chunk:(j + 1) * chunk], preferred_element_type=F32)
        for u in range(chunk // LANES):
            os = slice(j * chunk + u * LANES, j * chunk + (u + 1) * LANES)
            v_ref[0, 0, os, :] = pv[:, u * LANES:(u + 1) * LANES].T.astype(BF16)


def _mla_proj(x, mod, ng, w_in, qg, kvg, wq, wk, wv, cos, slo, shi, rotate):
    b, l, d = x.shape
    n = wq.shape[1]
    q_lora, kv_lora = wq.shape[0], wk.shape[0]
    tm = _row_tile(l, 512)
    kern = functools.partial(_mla_proj_kernel, rotate=rotate, q_lora=q_lora, kv_lora=kv_lora, chunk=512)
    out = jax.ShapeDtypeStruct((b, l, n), BF16)
    ospec = pl.BlockSpec((1, tm, n), lambda i, t: (i, t, 0))
    tab = pl.BlockSpec((tm, LANES), lambda i, t: (t, 0))
    return pl.pallas_call(
        kern,
        grid=(b, l // tm),
        in_specs=[pl.BlockSpec((1, tm, d), lambda i, t: (i, t, 0)),
                  pl.BlockSpec((1, 6, 1, d), lambda i, t: (i % mod.shape[0], 0, 0, 0)),
                  _resident((1, d)),
                  _resident(w_in.shape),
                  _resident((1, q_lora)),
                  _resident((1, kv_lora)),
                  _resident(wq.shape), _resident(wk.shape), _resident(wv.shape),
                  tab, tab, tab],
        out_specs=[ospec, ospec, pl.BlockSpec((1, 1, wv.shape[1], tm), lambda i, t: (i, t, 0, 0))],
        out_shape=[out, out, jax.ShapeDtypeStruct((b, l // tm, wv.shape[1], tm), BF16)],
        compiler_params=_params("parallel", "parallel"),
        name="mla_proj",
    )(x, mod, ng, w_in, qg, kvg, wq, wk, wv, cos, slo, shi)


def _axial_angles(rows, rot_dim):
    row = jnp.repeat(jnp.arange(rows, dtype=F32), GRID_W)
    col = jnp.tile(jnp.arange(GRID_W, dtype=F32), rows)
    axis_dim = rot_dim // 2
    inv_freq = jnp.power(ROPE_THETA, -jnp.arange(0, axis_dim, 2, dtype=F32) / axis_dim)
    ang = jnp.concatenate([row[:, None] * inv_freq, col[:, None] * inv_freq], axis=-1)
    return jnp.cos(ang), jnp.sin(ang)


def _gqa_tables(rows):
    cos, sin = _axial_angles(rows, GQA_HEAD_DIM)
    return jnp.concatenate([cos, cos], axis=-1), jnp.concatenate([-sin, sin], axis=-1)


def _mla_tables(rows):
    cos, sin = _axial_angles(rows, MLA_ROPE)
    s = cos.shape[0]
    half = MLA_ROPE // 2
    ones = jnp.ones((s, MLA_NOPE), F32)
    zeros = jnp.zeros((s, MLA_NOPE), F32)
    tail1 = jnp.ones((s, LANES - MLA_NOPE - MLA_ROPE), F32)
    tail0 = jnp.zeros((s, LANES - MLA_NOPE - MLA_ROPE), F32)
    zh = jnp.zeros((s, half), F32)
    c = jnp.concatenate([ones, cos, cos, tail1], axis=-1)
    s_lo = jnp.concatenate([zeros, -sin, zh, tail0], axis=-1)
    s_hi = jnp.concatenate([zeros, zh, sin, tail0], axis=-1)
    return c, s_lo, s_hi


def _mla_weights(w_in, w_qb, w_kvb, w_out):
    d = w_in.shape[0]
    q_lora, kv_lora = w_qb.shape[0], w_kvb.shape[0]
    heads = w_qb.shape[1] // (MLA_NOPE + MLA_ROPE)
    kr = jnp.zeros((d, LANES), w_in.dtype).at[:, MLA_NOPE:MLA_NOPE + MLA_ROPE].set(w_in[:, q_lora + kv_lora:])
    w_in_p = jnp.concatenate([w_in[:, :q_lora + kv_lora], kr], axis=1)
    wq = w_qb.reshape(q_lora, heads, MLA_NOPE + MLA_ROPE)
    wq = jnp.pad(wq, ((0, 0), (0, 0), (0, LANES - MLA_NOPE - MLA_ROPE))).reshape(q_lora, heads * LANES)
    wkv = w_kvb.reshape(kv_lora, heads, MLA_NOPE + MLA_V)
    wk = jnp.pad(wkv[:, :, :MLA_NOPE], ((0, 0), (0, 0), (0, LANES - MLA_NOPE))).reshape(kv_lora, heads * LANES)
    wv = jnp.pad(wkv[:, :, MLA_NOPE:], ((0, 0), (0, 0), (0, LANES - MLA_V))).reshape(kv_lora, heads * LANES)
    wo = w_out.reshape(heads, MLA_V, -1)
    wo = jnp.pad(wo, ((0, 0), (0, LANES - MLA_V), (0, 0))).reshape(heads * LANES, -1)
    return w_in_p.astype(BF16), wq.astype(BF16), wk.astype(BF16), wv.astype(BF16), wo.astype(BF16)


def kernel(x, c, ctx, c_ctx, w_ada, b_ada, norm_mix, norm_ffn, ffn_w_in, ffn_conv_w, ffn_conv_b, ffn_w_out,
           gqa_w_in, gqa_q_norm, gqa_k_norm, gqa_w_out, hgrn_w_in, hgrn_out_norm, hgrn_w_out, hgrn_lower_bounds,
           mla_w_in, mla_q_norm, mla_kv_norm, mla_w_qb, mla_w_kvb, mla_w_out, final_norm):
    batch, seq, d = x.shape
    depth = w_ada.shape[0]
    n_mixers = 3
    rows = seq // GRID_W
    assert batch + 1 <= SUBLANES

    cv = jnp.zeros((SUBLANES, d), F32).at[:batch].set(c).at[batch].set(c_ctx)
    mods = _ada_mods(cv, w_ada, b_ada).reshape(depth, SUBLANES, 6, 1, d)
    lb_all = _lower_bounds(hgrn_lower_bounds)

    cos_a, sin_a = _gqa_tables(rows)
    cos_m, slo_m, shi_m = _mla_tables(rows)

    for i in range(depth):
        last = i == depth - 1
        kind = i % n_mixers
        j = i // n_mixers
        mod = mods[i, :batch]
        mod_c = mods[i, batch:batch + 1]
        ng = norm_mix[i][None, :]

        if kind == 0:
            w_in = gqa_w_in[j].astype(BF16)
            w_out = gqa_w_out[j].astype(BF16)
            qg, kg = gqa_q_norm[j][None, :], gqa_k_norm[j][None, :]
            q, k, v = _gqa_proj(x, mod, ng, w_in, qg, kg, cos_a, sin_a, True)
            n_ctx = ctx.shape[1]
            qc, kc, vc = _gqa_proj(ctx, mod_c, ng, w_in, qg, kg, cos_a[:n_ctx], sin_a[:n_ctx], False)
            o = _attention(q, [(k, v), (kc, vc)], GQA_GROUP, True, 256)
            x = _merge(x, o, w_out, mod)
            if not last:
                oc = _attention(qc, [(kc, vc)], GQA_GROUP, True, 256)
                ctx = _merge(ctx, oc, w_out, mod_c)
        elif kind == 1:
            w_in = hgrn_w_in[j].astype(BF16)
            w_out = hgrn_w_out[j].astype(BF16)
            lb = lb_all[i][None, :]
            og = hgrn_out_norm[j][None, :]
            heads = w_out.shape[0] // HGRN_DK
            qc, vc, gfc, gbc, gatec = _hgrn_proj(ctx, mod_c, ng, w_in, lb)
            q, v, gf, gb, gate = _hgrn_proj(x, mod, ng, w_in, lb)
            s0 = jnp.zeros((batch, heads, HGRN_DK, HGRN_DK), F32)
            oc_f, oc_b, s_f, s_b = _hgrn_scan(qc, vc, gfc, gbc, s0, s0)
            o_f, o_b, _, _ = _hgrn_scan(q, v, gf, gb, s_f, s_b)
            x = _hgrn_merge(x, o_f, o_b, gate, og, w_out, mod)
            if not last:
                ctx = _hgrn_merge(ctx, oc_f, oc_b, gatec, og, w_out, mod_c)
        else:
            w_in, wq, wk, wv, w_out = _mla_weights(mla_w_in[j], mla_w_qb[j], mla_w_kvb[j], mla_w_out[j])
            qg, kvg = mla_q_norm[j][None, :], mla_kv_norm[j][None, :]
            n_ctx = ctx.shape[1]
            q, k, v = _mla_proj(x, mod, ng, w_in, qg, kvg, wq, wk, wv, cos_m, slo_m, shi_m, True)
            qc, kc, vc = _mla_proj(ctx, mod_c, ng, w_in, qg, kvg, wq, wk, wv,
                                   cos_m[:n_ctx], slo_m[:n_ctx], shi_m[:n_ctx], False)
            o = _attention(q, [(k, v), (kc, vc)], 1, True, 512)
            x = _merge(x, o, w_out, mod)
            if not last:
                oc = _attention(qc, [(kc, vc)], 1, True, 512)
                ctx = _merge(ctx, oc, w_out, mod_c)

        fg = norm_ffn[i][None, :]
        f_in = ffn_w_in[i].astype(BF16)
        f_out = ffn_w_out[i].astype(BF16)
        f_cw = ffn_conv_w[i]
        f_cb = ffn_conv_b[i][None, :]
        x = _conv_ffn(x, mod, fg, f_in, f_cw, f_cb, f_out, final_norm[None, :], last)
        if not last:
            ctx = _conv_ffn(ctx, mod_c, fg, f_in, f_cw, f_cb, f_out, final_norm[None, :], False)

    return x
```

```python
import functools

import numpy as np
import jax
import jax.numpy as jnp
from jax import lax
from jax.experimental import pallas as pl
from jax.experimental.pallas import tpu as pltpu

F32 = jnp.float32
BF16 = jnp.bfloat16
HIGHEST = lax.Precision.HIGHEST

GRID_W = 64
ROPE_THETA = 10000.0
NORM_EPS = 1e-6
CONV_W = 3

LANES = 128
SUBLANES = 8

GQA_HEAD_DIM = 128
GQA_GROUP = 2

HGRN_DK = 128
HGRN_CHUNK = 64
HGRN_PARTS = 5

MLA_NOPE = 64
MLA_ROPE = 32
MLA_V = 64

VMEM_LIMIT = 56 * 1024 * 1024

LOG2_E = 1.4426950408889634
FFN_CHUNKS = 2
ATTN_UNROLL = 16


def _params(*sem):
    return pltpu.CompilerParams(dimension_semantics=sem, vmem_limit_bytes=VMEM_LIMIT)


def _resident(shape):
    nd = len(shape)
    return pl.BlockSpec(shape, lambda *_: (0,) * nd, pipeline_mode=pl.Buffered(1))


def _silu(x):
    return x / (1.0 + jnp.exp2(x * -LOG2_E))


def _sigmoid(x):
    return 1.0 / (1.0 + jnp.exp2(x * -LOG2_E))


def _neg_abs(x):
    bits = lax.bitcast_convert_type(x, jnp.uint32) | jnp.uint32(0x80000000)
    return lax.bitcast_convert_type(bits, F32)


def _rms(x, gain):
    return x * lax.rsqrt(jnp.mean(x * x, axis=-1, keepdims=True) + NORM_EPS) * gain


def _head_rms(p, gain):
    ones_bf = jnp.ones((LANES, LANES), BF16)
    ss = jnp.dot((p * p).astype(BF16), ones_bf, preferred_element_type=F32)
    return p * lax.rsqrt(ss * (1.0 / LANES) + NORM_EPS) * gain


def _modnorm(x, gain, shift, scale):
    return _rms(x, gain) * (1.0 + scale) + shift


def _aligned_ds(start, size):
    if isinstance(start, int):
        return pl.ds(start, size)
    return pl.ds(pl.multiple_of(start, size), size)


def _row_tile(n, want):
    t = min(n, want)
    while n % t:
        t -= SUBLANES
    assert t > 0 and t % SUBLANES == 0, (n, want)
    return t


def _ada_kernel(cv_ref, w_ref, b_ref, o_ref):
    s = _silu(cv_ref[...])
    o_ref[0] = jnp.dot(s, w_ref[0], precision=HIGHEST, preferred_element_type=F32) + b_ref[0]


def _ada_mods(cv, w_ada, b_ada):
    depth, d, n = w_ada.shape
    tn = 1536
    return pl.pallas_call(
        _ada_kernel,
        grid=(depth, n // tn),
        in_specs=[pl.BlockSpec((SUBLANES, d), lambda i, j: (0, 0)),
                  pl.BlockSpec((1, d, tn), lambda i, j: (i, 0, j)),
                  pl.BlockSpec((1, 1, tn), lambda i, j: (i, 0, j))],
        out_specs=pl.BlockSpec((1, SUBLANES, tn), lambda i, j: (i, 0, j)),
        out_shape=jax.ShapeDtypeStruct((depth, SUBLANES, n), F32),
        compiler_params=_params("arbitrary", "arbitrary"),
        name="ada_mods",
    )(cv, w_ada, b_ada.reshape(depth, 1, n))


def _lb_kernel(x_ref, o_ref):
    depth = x_ref.shape[0]
    rows = [x_ref[i:i + 1, :] for i in range(depth)]
    m = rows[0]
    for r in rows[1:]:
        m = jnp.maximum(m, r)
    e = [jnp.exp(r - m) for r in rows]
    tot = e[0]
    for r in e[1:]:
        tot = tot + r
    p = [r / tot for r in e]
    cum = p[0]
    o_ref[0:1, :] = cum - p[0]
    for i in range(1, depth):
        cum = cum + p[i]
        o_ref[i:i + 1, :] = cum - p[0]


def _lower_bounds(lb_raw):
    return pl.pallas_call(
        _lb_kernel,
        out_shape=jax.ShapeDtypeStruct(lb_raw.shape, F32),
        name="hgrn_lower_bounds",
    )(lb_raw.astype(F32))


def _gqa_proj_kernel(x_ref, mod_ref, ng_ref, w_ref, qg_ref, kg_ref, cos_ref, sin_ref,
                     q_ref, k_ref, v_ref, *, rotate, qd, kd, chunk):
    h = _modnorm(x_ref[0], ng_ref[...], mod_ref[0, 0], mod_ref[0, 1]).astype(BF16)
    scale = GQA_HEAD_DIM ** -0.5 * LOG2_E
    qg = qg_ref[...] * scale
    kg = kg_ref[...]
    if rotate:
        cos = cos_ref[...]
        sin = sin_ref[...]

    def head(p, gain):
        y = _head_rms(p, gain)
        if rotate:
            y = y * cos + pltpu.roll(y, GQA_HEAD_DIM // 2, 1) * sin
        return y

    n = w_ref.shape[1]
    for j in range(n // chunk):
        p = jnp.dot(h, w_ref[:, j * chunk:(j + 1) * chunk], preferred_element_type=F32)
        for u in range(chunk // LANES):
            col = j * chunk + u * LANES
            ph = p[:, u * LANES:(u + 1) * LANES]
            if col < qd:
                q_ref[0, :, col:col + LANES] = head(ph, qg).astype(BF16)
            elif col < qd + kd:
                k_ref[0, :, col - qd:col - qd + LANES] = head(ph, kg).astype(BF16)
            else:
                c0 = col - qd - kd
                v_ref[0, 0, c0:c0 + LANES, :] = ph.T.astype(BF16)


def _gqa_proj(x, mod, ng, w_in, qg, kg, cos, sin, rotate):
    b, l, d = x.shape
    n = w_in.shape[1]
    kd = n // 4
    qd = n - 2 * kd
    tm = _row_tile(l, 512)
    kern = functools.partial(_gqa_proj_kernel, rotate=rotate, qd=qd, kd=kd, chunk=512)
    return pl.pallas_call(
        kern,
        grid=(b, l // tm),
        in_specs=[pl.BlockSpec((1, tm, d), lambda i, t: (i, t, 0)),
                  pl.BlockSpec((1, 6, 1, d), lambda i, t: (i % mod.shape[0], 0, 0, 0)),
                  _resident((1, d)),
                  _resident((d, n)),
                  _resident((1, LANES)),
                  _resident((1, LANES)),
                  pl.BlockSpec((tm, LANES), lambda i, t: (t, 0)),
                  pl.BlockSpec((tm, LANES), lambda i, t: (t, 0))],
        out_specs=[pl.BlockSpec((1, tm, qd), lambda i, t: (i, t, 0)),
                   pl.BlockSpec((1, tm, kd), lambda i, t: (i, t, 0)),
                   pl.BlockSpec((1, 1, kd, tm), lambda i, t: (i, t, 0, 0))],
        out_shape=[jax.ShapeDtypeStruct((b, l, qd), BF16),
                   jax.ShapeDtypeStruct((b, l, kd), BF16),
                   jax.ShapeDtypeStruct((b, l // tm, kd, tm), BF16)],
        compiler_params=_params("parallel", "parallel"),
        name="gqa_proj",
    )(x, mod, ng, w_in, qg, kg, cos, sin)


def _attn_kernel(*refs, group, shared, n_src):
    q_ref = refs[0]
    kv_refs = refs[1:1 + 2 * n_src]
    o_ref = refs[1 + 2 * n_src]
    s_sc, m_sc = refs[2 + 2 * n_src:]
    tq = q_ref.shape[1]
    nq = group * tq
    dv = LANES if shared else LANES // group
    step = pl.program_id(2)

    def fold(a):
        return a.reshape(a.shape[0] // SUBLANES, SUBLANES, nq)

    def run(do_scores, do_weigh):
        carry = {}
        if do_scores:
            q = q_ref[0]
            q_heads = [q[:, g * LANES:(g + 1) * LANES] for g in range(group)]
            q_all = jnp.concatenate(q_heads, axis=0)
            carry["m8"] = jnp.full((SUBLANES, nq), -jnp.inf, F32)
        if do_weigh:
            m = jnp.max(m_sc[...], axis=0, keepdims=True)
            carry["l8"] = jnp.zeros((SUBLANES, nq), F32)
            carry["acc"] = jnp.zeros((dv, nq), F32)
        row0 = 0
        for i in range(n_src):
            k_ref, vt_ref = kv_refs[2 * i], kv_refs[2 * i + 1]
            n_chunks, _, c = vt_ref.shape[1:]

            def body(j, carry, k_ref=k_ref, vt_ref=vt_ref, c=c, row0=row0):
                carry = dict(carry)
                rows = _aligned_ds(row0 + j * c, c)
                if do_weigh:
                    s_old = s_sc[rows, :]
                if do_scores:
                    kc = k_ref[0, _aligned_ds(j * c, c), :]
                    nt_dims = (((1,), (1,)), ((), ()))
                    if shared:
                        s = lax.dot_general(kc, q_all, nt_dims, preferred_element_type=F32)
                    else:
                        s = jnp.concatenate(
                            [lax.dot_general(kc[:, g * LANES:(g + 1) * LANES], q_heads[g], nt_dims,
                                             preferred_element_type=F32) for g in range(group)], axis=1)
                    s_sc[rows, :] = s
                    carry["m8"] = jnp.maximum(carry["m8"], jnp.max(fold(s), axis=0))
                if do_weigh:
                    p = jnp.exp2(s_old - m)
                    carry["l8"] = carry["l8"] + jnp.sum(fold(p), axis=0)
                    pb = p.astype(BF16)
                    vt = vt_ref[0, j]
                    if shared:
                        upd = jnp.dot(vt, pb, preferred_element_type=F32)
                    else:
                        upd = jnp.concatenate(
                            [jnp.dot(vt[g * dv:(g + 1) * dv], pb[:, g * tq:(g + 1) * tq],
                                     preferred_element_type=F32) for g in range(group)], axis=1)
                    carry["acc"] = carry["acc"] + upd
                return carry

            if n_chunks == 1:
                carry = body(0, carry)
            else:
                carry = lax.fori_loop(0, n_chunks, body, carry, unroll=ATTN_UNROLL)
            row0 += n_chunks * c
        if do_scores:
            m_sc[...] = carry["m8"]
        if do_weigh:
            o = carry["acc"] / jnp.sum(carry["l8"], axis=0, keepdims=True)
            o_ref[0] = jnp.concatenate([o[:, g * tq:(g + 1) * tq].T for g in range(group)],
                                       axis=1).astype(o_ref.dtype)

    n_tiles = pl.num_programs(2) - 1

    @pl.when(step == 0)
    def _():
        run(True, False)

    @pl.when((step > 0) & (step < n_tiles))
    def _():
        run(True, True)

    @pl.when(step == n_tiles)
    def _():
        run(False, True)


def _attention(q, kv_sources, group, shared, tq):
    b, nq, hd = q.shape
    hkv = hd // (group * LANES)
    kw = LANES if shared else group * LANES
    ow = group * LANES if shared else LANES
    tq = _row_tile(nq, tq)
    nt = nq // tq
    n_src = len(kv_sources)
    in_specs = [pl.BlockSpec((1, tq, group * LANES), lambda i, h, t: (i, jnp.minimum(t, nt - 1), h))]
    args = [q]
    nk_total = 0
    for k, vt in kv_sources:
        nk = k.shape[1]
        n_chunks, _, c = vt.shape[1:]
        assert n_chunks * c == nk and nk_total % c == 0
        nk_total += nk
        in_specs.append(pl.BlockSpec((1, nk, kw), lambda i, h, t: (i, 0, h)))
        in_specs.append(pl.BlockSpec((1, n_chunks, LANES, c), lambda i, h, t: (i, 0, h, 0)))
        args += [k, vt]
    return pl.pallas_call(
        functools.partial(_attn_kernel, group=group, shared=shared, n_src=n_src),
        grid=(b, hkv, nt + 1),
        in_specs=in_specs,
        out_specs=pl.BlockSpec((1, tq, ow), lambda i, h, t: (i, jnp.maximum(t - 1, 0), h)),
        out_shape=jax.ShapeDtypeStruct((b, nq, hkv * ow), BF16),
        scratch_shapes=[pltpu.VMEM((nk_total, group * tq), F32),
                        pltpu.VMEM((SUBLANES, group * tq), F32)],
        compiler_params=_params("parallel", "parallel", "arbitrary"),
        name="flash_attention",
    )(*args)


def _merge_kernel(x_ref, o_ref, w_ref, mod_ref, y_ref):
    y = jnp.dot(o_ref[0], w_ref[...], preferred_element_type=F32)
    y_ref[0] = x_ref[0] + mod_ref[0, 2] * y


def _merge(x, o, w_out, mod):
    b, l, d = x.shape
    ko = o.shape[2]
    tm = _row_tile(l, 1024)
    return pl.pallas_call(
        _merge_kernel,
        grid=(b, l // tm),
        in_specs=[pl.BlockSpec((1, tm, d), lambda i, t: (i, t, 0)),
                  pl.BlockSpec((1, tm, ko), lambda i, t: (i, t, 0)),
                  _resident((ko, d)),
                  pl.BlockSpec((1, 6, 1, d), lambda i, t: (i % mod.shape[0], 0, 0, 0))],
        out_specs=pl.BlockSpec((1, tm, d), lambda i, t: (i, t, 0)),
        out_shape=jax.ShapeDtypeStruct((b, l, d), F32),
        compiler_params=_params("parallel", "parallel"),
        name="merge_residual",
    )(x, o, w_out, mod)


def _ffn_kernel(xp_ref, x_ref, xn_ref, mod_ref, ng_ref, win_ref, cw_ref, cb_ref, wout_ref, og_ref, y_ref,
                *, d_ff, out_norm):
    t = pl.program_id(1)
    nt = pl.num_programs(1)
    gain = ng_ref[...]
    shift, scale, gate = mod_ref[0, 3], mod_ref[0, 4], mod_ref[0, 5]
    x, xp, xn = x_ref[0], xp_ref[0], xn_ref[0]
    tm = x.shape[0]
    halo = xp.shape[0]
    hp = jnp.where(t > 0, _modnorm(xp, gain, shift, scale), 0.0)
    hn = jnp.where(t < nt - 1, _modnorm(xn, gain, shift, scale), 0.0)
    h = jnp.concatenate([hp, _modnorm(x, gain, shift, scale), hn], axis=0).astype(BF16)
    rows = tm + 2 * halo

    chunk = d_ff // FFN_CHUNKS

    def conv(p, col):
        w = cw_ref[:, col:col + chunk]
        prev = pltpu.roll(p, 1, 0)[halo:halo + tm]
        nxt = pltpu.roll(p, rows - 1, 0)[halo:halo + tm]
        cur = p[halo:halo + tm]
        return ((cb_ref[:, col:col + chunk] + prev * w[0:1]) + cur * w[1:2]) + nxt * w[2:3]

    acc = jnp.zeros(x.shape, F32)
    for c in range(FFN_CHUNKS):
        lo = c * chunk
        pa = jnp.dot(h, win_ref[:, lo:lo + chunk], preferred_element_type=F32)
        pv = jnp.dot(h, win_ref[:, d_ff + lo:d_ff + lo + chunk], preferred_element_type=F32)
        g = _silu(conv(pa, lo)) * conv(pv, d_ff + lo)
        acc = acc + jnp.dot(g.astype(BF16), wout_ref[lo:lo + chunk, :], preferred_element_type=F32)
    y = x + gate * acc
    y_ref[0] = _rms(y, og_ref[...]) if out_norm else y


def _conv_ffn(x, mod, ng, w_in, conv_w, conv_b, w_out, out_gain, out_norm):
    b, l, d = x.shape
    d_ff = w_out.shape[0]
    tm = _row_tile(l, 512)
    halo = SUBLANES
    per = tm // halo
    last = l // halo - 1
    return pl.pallas_call(
        functools.partial(_ffn_kernel, d_ff=d_ff, out_norm=out_norm),
        grid=(b, l // tm),
        in_specs=[pl.BlockSpec((1, halo, d), lambda i, t: (i, jnp.maximum(t * per - 1, 0), 0)),
                  pl.BlockSpec((1, tm, d), lambda i, t: (i, t, 0)),
                  pl.BlockSpec((1, halo, d), lambda i, t: (i, jnp.minimum((t + 1) * per, last), 0)),
                  pl.BlockSpec((1, 6, 1, d), lambda i, t: (i % mod.shape[0], 0, 0, 0)),
                  _resident((1, d)),
                  _resident((d, 2 * d_ff)),
                  _resident((CONV_W, 2 * d_ff)),
                  _resident((1, 2 * d_ff)),
                  _resident((d_ff, d)),
                  _resident((1, d))],
        out_specs=pl.BlockSpec((1, tm, d), lambda i, t: (i, t, 0)),
        out_shape=jax.ShapeDtypeStruct((b, l, d), F32),
        compiler_params=_params("parallel", "arbitrary"),
        name="conv_ffn",
    )(x, x, x, mod, ng, w_in, conv_w, conv_b, w_out, out_gain)


def _hgrn_proj_kernel(x_ref, mod_ref, ng_ref, w_ref, lb_ref, q_ref, v_ref, gf_ref, gb_ref, gate_ref, *, chunk):
    h = _modnorm(x_ref[0], ng_ref[...], mod_ref[0, 0], mod_ref[0, 1]).astype(BF16)
    width = q_ref.shape[2]
    scale = HGRN_DK ** -0.5
    for part in range(HGRN_PARTS):
        for j in range(width // chunk):
            c0 = j * chunk
            p = jnp.dot(h, w_ref[:, part * width + c0:part * width + c0 + chunk], preferred_element_type=F32)
            if part == 0:
                q_ref[0, :, c0:c0 + chunk] = p * scale
            elif part == 1:
                v_ref[0, :, c0:c0 + chunk] = p
            elif part == 4:
                gate_ref[0, :, c0:c0 + chunk] = p
            else:
                lb = lb_ref[:, c0:c0 + chunk]
                f = lb + (1.0 - lb) * _sigmoid(p)
                g_out = gf_ref if part == 2 else gb_ref
                g_out[0, :, c0:c0 + chunk] = jnp.log2(f)


def _hgrn_proj(x, mod, ng, w_in, lb):
    b, l, d = x.shape
    width = w_in.shape[1] // HGRN_PARTS
    tm = _row_tile(l, 512)
    out = jax.ShapeDtypeStruct((b, l, width), F32)
    ospec = pl.BlockSpec((1, tm, width), lambda i, t: (i, t, 0))
    return pl.pallas_call(
        functools.partial(_hgrn_proj_kernel, chunk=512),
        grid=(b, l // tm),
        in_specs=[pl.BlockSpec((1, tm, d), lambda i, t: (i, t, 0)),
                  pl.BlockSpec((1, 6, 1, d), lambda i, t: (i % mod.shape[0], 0, 0, 0)),
                  _resident((1, d)),
                  _resident((d, HGRN_PARTS * width)),
                  _resident((1, width))],
        out_specs=[ospec] * 5,
        out_shape=[out] * 5,
        compiler_params=_params("parallel", "parallel"),
        name="hgrn_proj",
    )(x, mod, ng, w_in, lb)


_HGRN_LEVELS = (32, 16, 8, 4)


def _hgrn_tri(reverse):
    idx = np.arange(HGRN_CHUNK)
    tri = idx[None, :] >= idx[:, None] if reverse else idx[None, :] <= idx[:, None]
    return jnp.asarray(tri.astype(np.float32), dtype=BF16)


def _hgrn_block(q, v, g, tri, st, reverse):
    c = HGRN_CHUNK
    n = q.shape[0] // c
    k = 1.0 - jnp.exp2(g)

    g_hi = g.astype(BF16)
    rest = g - g_hi.astype(F32)
    g_mid = rest.astype(BF16)
    g_lo = (rest - g_mid.astype(F32)).astype(BF16)
    pieces = [piece[i * c:(i + 1) * c] for i in range(n) for piece in (g_hi, g_mid, g_lo)]
    sums = jnp.dot(tri, jnp.concatenate(pieces, axis=1), preferred_element_type=F32)
    cum = jnp.stack([(sums[:, (3 * i) * LANES:(3 * i + 1) * LANES]
                      + sums[:, (3 * i + 1) * LANES:(3 * i + 2) * LANES])
                     + sums[:, (3 * i + 2) * LANES:(3 * i + 3) * LANES] for i in range(n)], axis=0)

    q3, k3, v3 = (a.reshape(n, c, LANES) for a in (q, k, v))
    tot = cum[:, 0:1, :] if reverse else cum[:, c - 1:c, :]
    row = lax.broadcasted_iota(jnp.int32, (c, c), 0)
    col = lax.broadcasted_iota(jnp.int32, (c, c), 1)
    early, late = (row, col) if reverse else (col, row)

    sub_row = lax.broadcasted_iota(jnp.int32, (c, LANES), 0)
    scores = jnp.zeros((n, c, c), F32)
    for half in _HGRN_LEVELS:
        blk = 2 * half
        cb = cum.reshape(n * (c // blk), blk, LANES)
        b_row = half if reverse else half - 1
        d = cb - cb[:, b_row:b_row + 1, :]
        e = jnp.exp2(_neg_abs(d)).reshape(n, c, LANES)
        query_row = ((sub_row % blk) < half) if reverse else ((sub_row % blk) >= half)
        z = (jnp.where(query_row[None], q3, k3) * e).astype(BF16)
        s_l = lax.dot_general(z, z, (((2,), (2,)), ((0,), (0,))), preferred_element_type=F32)
        pick = ((row // blk) == (col // blk)) & ((early % blk) < half) & ((late % blk) >= half)
        scores = jnp.where(pick[None], s_l, scores)

    sub = _HGRN_LEVELS[-1]
    cum2 = cum.reshape(n * c, LANES)
    prods = [q * k]
    for delta in range(1, sub):
        shift = n * c - delta if reverse else delta
        decay = jnp.exp2(jnp.minimum(cum2 - pltpu.roll(cum2, shift, 0), 0.0))
        prods.append(q * pltpu.roll(k, shift, 0) * decay)
    ones_bf = jnp.ones((LANES, LANES), BF16)
    diag = jnp.dot(jnp.concatenate(prods, axis=0).astype(BF16), ones_bf, preferred_element_type=F32)
    for delta in range(sub):
        d_t = diag[delta * n * c:(delta + 1) * n * c, 0:c].reshape(n, c, c)
        if reverse:
            hit = (col == row + delta) & ((row % sub) + delta < sub)
        else:
            hit = (col == row - delta) & ((row % sub) >= delta)
        scores = jnp.where(hit[None], d_t, scores)

    out = lax.dot_general(scores.astype(BF16), v3.astype(BF16), (((2,), (1,)), ((0,), (0,))),
                          preferred_element_type=F32)
    qe = (q3 * jnp.exp2(cum)).astype(BF16)
    kd = (k3 * jnp.exp2(tot - cum)).astype(BF16)
    e_tot = jnp.exp2(tot)
    v_t = jnp.stack([v3[i].T for i in range(n)], axis=0).astype(BF16)
    upd = lax.dot_general(v_t, kd, (((2,), (1,)), ((0,), (0,))), preferred_element_type=F32)
    states = [None] * n
    for i in (range(n - 1, -1, -1) if reverse else range(n)):
        states[i] = st
        st = st * e_tot[i] + upd[i]
    out = out + lax.dot_general(qe, jnp.stack(states, axis=0).astype(BF16), (((2,), (2,)), ((0,), (0,))),
                                preferred_element_type=F32)
    return out.reshape(n * c, LANES), st


def _hgrn_scan_kernel(qf_ref, vf_ref, gf_ref, qb_ref, vb_ref, gb_ref, tf_ref, tb_ref,
                      sf0_ref, sb0_ref, of_ref, ob_ref, sf_ref, sb_ref, st_sc):
    step = pl.program_id(2)

    @pl.when(step == 0)
    def _():
        st_sc[0] = sf0_ref[0, 0]
        st_sc[1] = sb0_ref[0, 0]

    o, st_f = _hgrn_block(qf_ref[0], vf_ref[0], gf_ref[0], tf_ref[...], st_sc[0], False)
    of_ref[0] = o
    o, st_b = _hgrn_block(qb_ref[0], vb_ref[0], gb_ref[0], tb_ref[...], st_sc[1], True)
    ob_ref[0] = o
    st_sc[0] = st_f
    st_sc[1] = st_b

    @pl.when(step == pl.num_programs(2) - 1)
    def _():
        sf_ref[0, 0] = st_f
        sb_ref[0, 0] = st_b


def _hgrn_scan(q, v, gf, gb, sf0, sb0):
    b, l, width = q.shape
    heads = width // LANES
    rb = _row_tile(l, 2048)
    nc = l // rb
    fwd = pl.BlockSpec((1, rb, LANES), lambda i, h, s: (i, s, h))
    bwd = pl.BlockSpec((1, rb, LANES), lambda i, h, s: (i, nc - 1 - s, h))
    st_spec = pl.BlockSpec((1, 1, LANES, LANES), lambda i, h, s: (i, h, 0, 0))
    o_shape = jax.ShapeDtypeStruct((b, l, width), F32)
    s_shape = jax.ShapeDtypeStruct((b, heads, LANES, LANES), F32)
    return pl.pallas_call(
        _hgrn_scan_kernel,
        grid=(b, heads, nc),
        in_specs=[fwd, fwd, fwd, bwd, bwd, bwd,
                  _resident((HGRN_CHUNK, HGRN_CHUNK)), _resident((HGRN_CHUNK, HGRN_CHUNK)), st_spec, st_spec],
        out_specs=[fwd, bwd, st_spec, st_spec],
        out_shape=[o_shape, o_shape, s_shape, s_shape],
        scratch_shapes=[pltpu.VMEM((2, LANES, LANES), F32)],
        compiler_params=_params("parallel", "parallel", "arbitrary"),
        name="hgrn_scan",
    )(q, v, gf, q, v, gb, _hgrn_tri(False), _hgrn_tri(True), sf0, sb0)


def _hgrn_merge_kernel(x_ref, of_ref, ob_ref, gate_ref, og_ref, w_ref, mod_ref, y_ref):
    width = of_ref.shape[2]
    og = og_ref[...]
    parts = []
    for hd in range(width // LANES):
        sl = slice(hd * LANES, (hd + 1) * LANES)
        o = of_ref[0, :, sl] + ob_ref[0, :, sl]
        parts.append((_rms(o, og) * _silu(gate_ref[0, :, sl])).astype(BF16))
    r = jnp.concatenate(parts, axis=1)
    y = jnp.dot(r, w_ref[...], preferred_element_type=F32)
    y_ref[0] = x_ref[0] + mod_ref[0, 2] * y


def _hgrn_merge(x, o_f, o_b, gate, o_gain, w_out, mod):
    b, l, d = x.shape
    width = o_f.shape[2]
    tm = _row_tile(l, 512)
    wide = pl.BlockSpec((1, tm, width), lambda i, t: (i, t, 0))
    return pl.pallas_call(
        _hgrn_merge_kernel,
        grid=(b, l // tm),
        in_specs=[pl.BlockSpec((1, tm, d), lambda i, t: (i, t, 0)), wide, wide, wide,
                  _resident((1, LANES)),
                  _resident((width, d)),
                  pl.BlockSpec((1, 6, 1, d), lambda i, t: (i % mod.shape[0], 0, 0, 0))],
        out_specs=pl.BlockSpec((1, tm, d), lambda i, t: (i, t, 0)),
        out_shape=jax.ShapeDtypeStruct((b, l, d), F32),
        compiler_params=_params("parallel", "parallel"),
        name="hgrn_merge",
    )(x, o_f, o_b, gate, o_gain, w_out, mod)


def _mla_rope(y, cos, sin_lo, sin_hi):
    half = MLA_ROPE // 2
    return y * cos + pltpu.roll(y, LANES - half, 1) * sin_lo + pltpu.roll(y, half, 1) * sin_hi


def _mla_proj_kernel(x_ref, mod_ref, ng_ref, w_ref, qg_ref, kvg_ref, wq_ref, wk_ref, wv_ref,
                     cos_ref, slo_ref, shi_ref, q_ref, k_ref, v_ref, *, rotate, q_lora, kv_lora, chunk):
    h = _modnorm(x_ref[0], ng_ref[...], mod_ref[0, 0], mod_ref[0, 1]).astype(BF16)
    p = jnp.dot(h, w_ref[...], preferred_element_type=F32)
    cq = _rms(p[:, :q_lora], qg_ref[...]).astype(BF16)
    ckv = _rms(p[:, q_lora:q_lora + kv_lora], kvg_ref[...]).astype(BF16)
    k_rope = p[:, q_lora + kv_lora:]
    scale = (MLA_NOPE + MLA_ROPE) ** -0.5 * LOG2_E
    if rotate:
        cos, slo, shi = cos_ref[...], slo_ref[...], shi_ref[...]
        k_rope = _mla_rope(k_rope, cos, slo, shi)
    n = wq_ref.shape[1]
    for j in range(n // chunk):
        cs = slice(j * chunk, (j + 1) * chunk)
        pq = jnp.dot(cq, wq_ref[:, cs], preferred_element_type=F32)
        pk = jnp.dot(ckv, wk_ref[:, cs], preferred_element_type=F32)
        for u in range(chunk // LANES):
            us = slice(u * LANES, (u + 1) * LANES)
            os = slice(j * chunk + u * LANES, j * chunk + (u + 1) * LANES)
            qh = pq[:, us]
            if rotate:
                qh = _mla_rope(qh, cos, slo, shi)
            q_ref[0, :, os] = (qh * scale).astype(BF16)
            k_ref[0, :, os] = (pk[:, us] + k_rope).astype(BF16)
    for j in range(wv_ref.shape[1] // chunk):
        pv = jnp.dot(ckv, wv_ref[:, j * chunk:(j + 1) * chunk], preferred_element_type=F32)
        for u in range(chunk // LANES):
            os = slice(j * chunk + u * LANES, j * chunk + (u + 1) * LANES)
            v_ref[0, 0, os, :] = pv[:, u * LANES:(u + 1) * LANES].T.astype(BF16)


def _mla_proj(x, mod, ng, w_in, qg, kvg, wq, wk, wv, cos, slo, shi, rotate):
    b, l, d = x.shape
    n = wq.shape[1]
    q_lora, kv_lora = wq.shape[0], wk.shape[0]
    tm = _row_tile(l, 512)
    kern = functools.partial(_mla_proj_kernel, rotate=rotate, q_lora=q_lora, kv_lora=kv_lora, chunk=512)
    out = jax.ShapeDtypeStruct((b, l, n), BF16)
    ospec = pl.BlockSpec((1, tm, n), lambda i, t: (i, t, 0))
    tab = pl.BlockSpec((tm, LANES), lambda i, t: (t, 0))
    return pl.pallas_call(
        kern,
        grid=(b, l // tm),
        in_specs=[pl.BlockSpec((1, tm, d), lambda i, t: (i, t, 0)),
                  pl.BlockSpec((1, 6, 1, d), lambda i, t: (i % mod.shape[0], 0, 0, 0)),
                  _resident((1, d)),
                  _resident(w_in.shape),
                  _resident((1, q_lora)),
                  _resident((1, kv_lora)),
                  _resident(wq.shape), _resident(wk.shape), _resident(wv.shape),
                  tab, tab, tab],
        out_specs=[ospec, ospec, pl.BlockSpec((1, 1, wv.shape[1], tm), lambda i, t: (i, t, 0, 0))],
        out_shape=[out, out, jax.ShapeDtypeStruct((b, l // tm, wv.shape[1], tm), BF16)],
        compiler_params=_params("parallel", "parallel"),
        name="mla_proj",
    )(x, mod, ng, w_in, qg, kvg, wq, wk, wv, cos, slo, shi)


def _axial_angles(rows, rot_dim):
    row = jnp.repeat(jnp.arange(rows, dtype=F32), GRID_W)
    col = jnp.tile(jnp.arange(GRID_W, dtype=F32), rows)
    axis_dim = rot_dim // 2
    inv_freq = jnp.power(ROPE_THETA, -jnp.arange(0, axis_dim, 2, dtype=F32) / axis_dim)
    ang = jnp.concatenate([row[:, None] * inv_freq, col[:, None] * inv_freq], axis=-1)
    return jnp.cos(ang), jnp.sin(ang)


def _gqa_tables(rows):
    cos, sin = _axial_angles(rows, GQA_HEAD_DIM)
    return jnp.concatenate([cos, cos], axis=-1), jnp.concatenate([-sin, sin], axis=-1)


def _mla_tables(rows):
    cos, sin = _axial_angles(rows, MLA_ROPE)
    s = cos.shape[0]
    half = MLA_ROPE // 2
    ones = jnp.ones((s, MLA_NOPE), F32)
    zeros = jnp.zeros((s, MLA_NOPE), F32)
    tail1 = jnp.ones((s, LANES - MLA_NOPE - MLA_ROPE), F32)
    tail0 = jnp.zeros((s, LANES - MLA_NOPE - MLA_ROPE), F32)
    zh = jnp.zeros((s, half), F32)
    c = jnp.concatenate([ones, cos, cos, tail1], axis=-1)
    s_lo = jnp.concatenate([zeros, -sin, zh, tail0], axis=-1)
    s_hi = jnp.concatenate([zeros, zh, sin, tail0], axis=-1)
    return c, s_lo, s_hi


def _mla_weights(w_in, w_qb, w_kvb, w_out):
    d = w_in.shape[0]
    q_lora, kv_lora = w_qb.shape[0], w_kvb.shape[0]
    heads = w_qb.shape[1] // (MLA_NOPE + MLA_ROPE)
    kr = jnp.zeros((d, LANES), w_in.dtype).at[:, MLA_NOPE:MLA_NOPE + MLA_ROPE].set(w_in[:, q_lora + kv_lora:])
    w_in_p = jnp.concatenate([w_in[:, :q_lora + kv_lora], kr], axis=1)
    wq = w_qb.reshape(q_lora, heads, MLA_NOPE + MLA_ROPE)
    wq = jnp.pad(wq, ((0, 0), (0, 0), (0, LANES - MLA_NOPE - MLA_ROPE))).reshape(q_lora, heads * LANES)
    wkv = w_kvb.reshape(kv_lora, heads, MLA_NOPE + MLA_V)
    wk = jnp.pad(wkv[:, :, :MLA_NOPE], ((0, 0), (0, 0), (0, LANES - MLA_NOPE))).reshape(kv_lora, heads * LANES)
    wv = jnp.pad(wkv[:, :, MLA_NOPE:], ((0, 0), (0, 0), (0, LANES - MLA_V))).reshape(kv_lora, heads * LANES)
    wo = w_out.reshape(heads, MLA_V, -1)
    wo = jnp.pad(wo, ((0, 0), (0, LANES - MLA_V), (0, 0))).reshape(heads * LANES, -1)
    return w_in_p.astype(BF16), wq.astype(BF16), wk.astype(BF16), wv.astype(BF16), wo.astype(BF16)


def kernel(x, c, ctx, c_ctx, w_ada, b_ada, norm_mix, norm_ffn, ffn_w_in, ffn_conv_w, ffn_conv_b, ffn_w_out,
           gqa_w_in, gqa_q_norm, gqa_k_norm, gqa_w_out, hgrn_w_in, hgrn_out_norm, hgrn_w_out, hgrn_lower_bounds,
           mla_w_in, mla_q_norm, mla_kv_norm, mla_w_qb, mla_w_kvb, mla_w_out, final_norm):
    batch, seq, d = x.shape
    depth = w_ada.shape[0]
    n_mixers = 3
    rows = seq // GRID_W
    assert batch + 1 <= SUBLANES

    cv = jnp.zeros((SUBLANES, d), F32).at[:batch].set(c).at[batch].set(c_ctx)
    mods = _ada_mods(cv, w_ada, b_ada).reshape(depth, SUBLANES, 6, 1, d)
    lb_all = _lower_bounds(hgrn_lower_bounds)

    cos_a, sin_a = _gqa_tables(rows)
    cos_m, slo_m, shi_m = _mla_tables(rows)

    for i in range(depth):
        last = i == depth - 1
        kind = i % n_mixers
        j = i // n_mixers
        mod = mods[i, :batch]
        mod_c = mods[i, batch:batch + 1]
        ng = norm_mix[i][None, :]

        if kind == 0:
            w_in = gqa_w_in[j].astype(BF16)
            w_out = gqa_w_out[j].astype(BF16)
            qg, kg = gqa_q_norm[j][None, :], gqa_k_norm[j][None, :]
            q, k, v = _gqa_proj(x, mod, ng, w_in, qg, kg, cos_a, sin_a, True)
            n_ctx = ctx.shape[1]
            qc, kc, vc = _gqa_proj(ctx, mod_c, ng, w_in, qg, kg, cos_a[:n_ctx], sin_a[:n_ctx], False)
            o = _attention(q, [(k, v), (kc, vc)], GQA_GROUP, True, 256)
            x = _merge(x, o, w_out, mod)
            if not last:
                oc = _attention(qc, [(kc, vc)], GQA_GROUP, True, 256)
                ctx = _merge(ctx, oc, w_out, mod_c)
        elif kind == 1:
            w_in = hgrn_w_in[j].astype(BF16)
            w_out = hgrn_w_out[j].astype(BF16)
            lb = lb_all[i][None, :]
            og = hgrn_out_norm[j][None, :]
            heads = w_out.shape[0] // HGRN_DK
            qc, vc, gfc, gbc, gatec = _hgrn_proj(ctx, mod_c, ng, w_in, lb)
            q, v, gf, gb, gate = _hgrn_proj(x, mod, ng, w_in, lb)
            s0 = jnp.zeros((batch, heads, HGRN_DK, HGRN_DK), F32)
            oc_f, oc_b, s_f, s_b = _hgrn_scan(qc, vc, gfc, gbc, s0, s0)
            o_f, o_b, _, _ = _hgrn_scan(q, v, gf, gb, s_f, s_b)
            x = _hgrn_merge(x, o_f, o_b, gate, og, w_out, mod)
            if not last:
                ctx = _hgrn_merge(ctx, oc_f, oc_b, gatec, og, w_out, mod_c)
        else:
            w_in, wq, wk, wv, w_out = _mla_weights(mla_w_in[j], mla_w_qb[j], mla_w_kvb[j], mla_w_out[j])
            qg, kvg = mla_q_norm[j][None, :], mla_kv_norm[j][None, :]
            n_ctx = ctx.shape[1]
            q, k, v = _mla_proj(x, mod, ng, w_in, qg, kvg, wq, wk, wv, cos_m, slo_m, shi_m, True)
            qc, kc, vc = _mla_proj(ctx, mod_c, ng, w_in, qg, kvg, wq, wk, wv,
                                   cos_m[:n_ctx], slo_m[:n_ctx], shi_m[:n_ctx], False)
            o = _attention(q, [(k, v), (kc, vc)], 1, True, 512)
            x = _merge(x, o, w_out, mod)
            if not last:
                oc = _attention(qc, [(kc, vc)], 1, True, 512)
                ctx = _merge(ctx, oc, w_out, mod_c)

        fg = norm_ffn[i][None, :]
        f_in = ffn_w_in[i].astype(BF16)
        f_out = ffn_w_out[i].astype(BF16)
        f_cw = ffn_conv_w[i]
        f_cb = ffn_conv_b[i][None, :]
        x = _conv_ffn(x, mod, fg, f_in, f_cw, f_cb, f_out, final_norm[None, :], last)
        if not last:
            ctx = _conv_ffn(ctx, mod_c, fg, f_in, f_cw, f_cb, f_out, final_norm[None, :], False)

    return x
```
